```python
import jax, jax.numpy as jnp
from jax import lax
import numpy as np

D_MODEL = 4096
BATCH = 4
SEQ = 2048
DEPTH = 2
DEC_BATCH = 128
DEC_SEQ = 1
PAST_LEN = 16384
PAGE_SIZE = 128

N_EVEN = (DEPTH + 1) // 2
N_ODD = DEPTH // 2
MIX_WIDTH = D_MODEL
CONV_WIDTH = 4
GDN_HEAD_DIM = 128
GDN_WIDTH = MIX_WIDTH // 2
GDN_HEADS = GDN_WIDTH // GDN_HEAD_DIM
GDN_CHUNK = 64
LRU_WIDTH = MIX_WIDTH - GDN_WIDTH
LRU_BLOCKS = 16
LRU_BLOCK_DIM = LRU_WIDTH // LRU_BLOCKS
LRU_C = 8.0
HGRN_EXPAND = 128
HGRN_HEADS = MIX_WIDTH // HGRN_EXPAND
HGRN_KEY_DIM = HGRN_EXPAND
HGRN_VAL_DIM = MIX_WIDTH // HGRN_HEADS
HGRN_KEY_WIDTH = HGRN_HEADS * HGRN_KEY_DIM
HGRN_CHUNK = 32
MEM_TOKENS = 256
MEM_HEADS = 4
MEM_HEAD_DIM = 128
MEM_WIDTH = MEM_HEADS * MEM_HEAD_DIM
D_FF = 4 * D_MODEL
EPS = 1e-6
AB_IN = 4 * GDN_WIDTH + 2 * GDN_HEADS + 2 * LRU_WIDTH
AB_SPLITS = (3 * GDN_WIDTH, 4 * GDN_WIDTH, 4 * GDN_WIDTH + GDN_HEADS, 4 * GDN_WIDTH + 2 * GDN_HEADS,
             4 * GDN_WIDTH + 2 * GDN_HEADS + LRU_WIDTH)
C_IN = 2 * HGRN_KEY_WIDTH + 2 * MIX_WIDTH
C_SPLITS = (HGRN_KEY_WIDTH, 2 * HGRN_KEY_WIDTH, 2 * HGRN_KEY_WIDTH + MIX_WIDTH)

kernel_name = 'hybrid_gdn_rglru_hgrn2_memory_decoder_step'


def rms_norm(x, w):
    xf = x.astype(jnp.float32)
    y = xf * lax.rsqrt(jnp.mean(xf * xf, axis=-1, keepdims=True) + EPS)
    return (y * w.astype(jnp.float32)).astype(x.dtype)


def l2_normalize(x):
    return x * lax.rsqrt(jnp.sum(x * x, axis=-1, keepdims=True) + EPS)


def causal_conv(x, prev, w, b=None):
    L = x.shape[1]
    xp = jnp.concatenate([prev.astype(x.dtype), x], axis=1)
    y = xp[:, 0:L] * w[0]
    for tap in range(1, CONV_WIDTH):
        y = y + xp[:, tap:tap + L] * w[tap]
    if b is not None:
        y = y + b
    return y, xp[:, L:]


def _to_chunks(t, c):
    b, L = t.shape[:2]
    t = t.reshape(b, L // c, c, *t.shape[2:])
    return jnp.moveaxis(jnp.moveaxis(t, 1, 0), 2, 3)


def _from_chunks(t):
    n, b, h, c = t.shape[:4]
    t = jnp.moveaxis(jnp.moveaxis(t, 3, 2), 0, 1)
    return t.reshape(b, n * c, h, *t.shape[4:])


def gdn_chunked(q, k, v, g, beta, s0):
    C = GDN_CHUNK
    q, k, v = _to_chunks(q, C), _to_chunks(k, C), _to_chunks(v, C)
    g, beta = _to_chunks(g, C), _to_chunks(beta, C)
    gc = jnp.cumsum(g, axis=-1)
    causal = jnp.tril(jnp.ones((C, C), dtype=bool))
    strict = jnp.tril(jnp.ones((C, C), dtype=bool), k=-1)
    decay = jnp.exp(jnp.where(causal, gc[..., :, None] - gc[..., None, :], -jnp.inf))
    k_beta = k * beta[..., None]
    m = jnp.where(strict, jnp.einsum('nbhik,nbhjk->nbhij', k_beta, k) * decay, 0.0)
    eye = jnp.eye(C, dtype=m.dtype)
    dv = v.shape[-1]
    rhs = jnp.concatenate([v * beta[..., None], k_beta * jnp.exp(gc)[..., None]], axis=-1)
    sol = lax.linalg.triangular_solve(m + eye, rhs, left_side=True, lower=True, unit_diagonal=True)
    u, w = sol[..., :dv], sol[..., dv:]
    attn = jnp.einsum('nbhik,nbhjk->nbhij', q, k) * decay
    q_dec = q * jnp.exp(gc)[..., None]
    k_dec = k * jnp.exp(gc[..., -1:] - gc)[..., None]
    g_last = jnp.exp(gc[..., -1])

    def step(s, inp):
        u_n, w_n, a_n, qd_n, kd_n, gl_n = inp
        v_new = u_n - jnp.einsum('bhck,bhkv->bhcv', w_n, s)
        o_n = jnp.einsum('bhck,bhkv->bhcv', qd_n, s) + jnp.einsum('bhij,bhjv->bhiv', a_n, v_new)
        s = s * gl_n[..., None, None] + jnp.einsum('bhck,bhcv->bhkv', kd_n, v_new)
        return s, o_n

    s, o = lax.scan(step, s0, (u, w, attn, q_dec, k_dec, g_last))
    return _from_chunks(o), s


def gdn_recurrent(q, k, v, g, beta, s0):
    def step(s, inp):
        q_t, k_t, v_t, g_t, b_t = inp
        s = s * jnp.exp(g_t)[..., None, None]
        v_new = b_t[..., None] * (v_t - jnp.einsum('bhk,bhkv->bhv', k_t, s))
        s = s + k_t[..., :, None] * v_new[..., None, :]
        return s, jnp.einsum('bhk,bhkv->bhv', q_t, s)

    xs = (jnp.swapaxes(q, 0, 1), jnp.swapaxes(k, 0, 1), jnp.swapaxes(v, 0, 1),
          jnp.swapaxes(g, 0, 1), jnp.swapaxes(beta, 0, 1))
    s, o = lax.scan(step, s0, xs)
    return jnp.swapaxes(o, 0, 1), s


def hgrn_chunked(q, k, v, log_f, s0):
    C = HGRN_CHUNK
    q, k, v = _to_chunks(q, C), _to_chunks(k, C), _to_chunks(v, C)
    gc = jnp.cumsum(_to_chunks(log_f, C), axis=3)
    causal = jnp.tril(jnp.ones((C, C), dtype=bool))[:, :, None]

    def step(s, inp):
        q_n, k_n, v_n, g_n = inp
        rel = jnp.exp(jnp.where(causal, g_n[:, :, :, None, :] - g_n[:, :, None, :, :], -jnp.inf))
        a_n = jnp.einsum('bhik,bhjk,bhijk->bhij', q_n, k_n, rel)
        o_n = jnp.einsum('bhik,bhkv->bhiv', q_n * jnp.exp(g_n), s) + jnp.einsum('bhij,bhjv->bhiv', a_n, v_n)
        gl = g_n[:, :, -1]
        s = s * jnp.exp(gl)[..., None] + jnp.einsum('bhck,bhcv->bhkv', k_n * jnp.exp(gl[:, :, None] - g_n), v_n)
        return s, o_n

    s, o = lax.scan(step, s0, (q, k, v, gc))
    return _from_chunks(o), s


def hgrn_recurrent(q, k, v, log_f, s0):
    def step(s, inp):
        q_t, k_t, v_t, lf_t = inp
        s = s * jnp.exp(lf_t)[..., None] + k_t[..., :, None] * v_t[..., None, :]
        return s, jnp.einsum('bhk,bhkv->bhv', q_t, s)

    xs = (jnp.swapaxes(q, 0, 1), jnp.swapaxes(k, 0, 1), jnp.swapaxes(v, 0, 1), jnp.swapaxes(log_f, 0, 1))
    s, o = lax.scan(step, s0, xs)
    return jnp.swapaxes(o, 0, 1), s


def _linear_combine(earlier, later):
    a1, b1 = earlier
    a2, b2 = later
    return a1 * a2, a2 * b1 + b2


def rg_lru(x, h0, w_a, b_a, w_i, b_i, lam, reset):
    bsz, L, W = x.shape
    xb = x.reshape(bsz, L, LRU_BLOCKS, LRU_BLOCK_DIM)
    gate_a = jax.nn.sigmoid(jnp.einsum('blni,nij->blnj', xb, w_a).reshape(bsz, L, W) + b_a)
    gate_i = jax.nn.sigmoid(jnp.einsum('blni,nij->blnj', xb, w_i).reshape(bsz, L, W) + b_i)
    log_a = -LRU_C * gate_a * jax.nn.softplus(-lam)
    a = jnp.exp(log_a)
    mult = jnp.where(reset[None, :, None], 1.0, jnp.sqrt(-jnp.expm1(2.0 * log_a)))
    b = mult * gate_i * x
    b = b.at[:, 0].add(a[:, 0] * h0)
    _, h = lax.associative_scan(_linear_combine, (a, b), axis=1)
    return h, h[:, -1]


def ab_mixer(h, w_in, w_out, conv_w, a_log, dt_bias, norm_w, lconv_w, lconv_b, w_a, b_a, w_i, b_i, lam,
             conv_prev, s0, lconv_prev, h0, reset, prompt):
    bsz, L, _ = h.shape
    f32 = jnp.float32
    qkv, z, b_raw, a_raw, xl, yl = jnp.split(h @ w_in, AB_SPLITS, axis=-1)
    qkv, conv_new = causal_conv(qkv, conv_prev, conv_w)
    qkv = jax.nn.silu(qkv).astype(f32).reshape(bsz, L, 3 * GDN_HEADS, GDN_HEAD_DIM)
    q, k, v = jnp.split(qkv, 3, axis=2)
    q = l2_normalize(q) * GDN_HEAD_DIM ** -0.5
    k = l2_normalize(k)
    beta = jax.nn.sigmoid(b_raw.astype(f32))
    g = -jnp.exp(a_log.astype(f32)) * jax.nn.softplus(a_raw.astype(f32) + dt_bias.astype(f32))
    gdn_fn = gdn_chunked if prompt else gdn_recurrent
    o, s_new = gdn_fn(q, k, v, g, beta, s0.astype(f32))
    gate = jax.nn.silu(z.astype(f32)).reshape(bsz, L, GDN_HEADS, GDN_HEAD_DIM)
    o_a = (rms_norm(o, norm_w) * gate).reshape(bsz, L, GDN_WIDTH)
    xl, lconv_new = causal_conv(xl, lconv_prev, lconv_w, lconv_b)
    hl, h_new = rg_lru(xl.astype(f32), h0.astype(f32), w_a, b_a, w_i, b_i, lam.astype(f32), reset)
    o_b = hl * jax.nn.gelu(yl.astype(f32), approximate=True)
    out = jnp.concatenate([o_a, o_b], axis=-1).astype(h.dtype) @ w_out
    return out, conv_new, s_new, lconv_new, h_new


def c_mixer(h, w_in, w_out, lower_bound, norm_w, s0, prompt):
    bsz, L, _ = h.shape
    f32 = jnp.float32
    q, f_raw, i, gz = jnp.split(h @ w_in, C_SPLITS, axis=-1)
    kshape = (bsz, L, HGRN_HEADS, HGRN_KEY_DIM)
    vshape = (bsz, L, HGRN_HEADS, HGRN_VAL_DIM)
    q = jax.nn.silu(q.astype(f32)).reshape(kshape) * HGRN_KEY_DIM ** -0.5
    lb = lower_bound.astype(f32)
    f = (lb + (1.0 - lb) * jax.nn.sigmoid(f_raw.astype(f32))).reshape(kshape)
    log_f = jnp.log(f)
    k = 1.0 - f
    v = i.astype(f32).reshape(vshape)
    hgrn_fn = hgrn_chunked if prompt else hgrn_recurrent
    o, s_new = hgrn_fn(q, k, v, log_f, s0.astype(f32))
    o = rms_norm(o, norm_w) * jax.nn.silu(gz.astype(f32)).reshape(vshape)
    return o.reshape(bsz, L, MIX_WIDTH).astype(h.dtype) @ w_out, s_new


def mem_project(mem, norm_w, w_k, w_v):
    m = rms_norm(mem, norm_w)
    shp = (mem.shape[0], mem.shape[1], MEM_HEADS, MEM_HEAD_DIM)
    return (m @ w_k).reshape(shp), (m @ w_v).reshape(shp)


def mem_attend(h, mk, mv, w_q, w_o):
    bsz, L, _ = h.shape
    q = (h @ w_q).reshape(bsz, L, MEM_HEADS, MEM_HEAD_DIM)
    s = jnp.einsum('blhd,bmhd->bhlm', q, mk.astype(q.dtype)).astype(jnp.float32) * MEM_HEAD_DIM ** -0.5
    p = jax.nn.softmax(s, axis=-1).astype(q.dtype)
    o = jnp.einsum('bhlm,bmhd->blhd', p, mv.astype(q.dtype)).reshape(bsz, L, MEM_WIDTH)
    return o @ w_o


def trunk(x, mem_k, mem_v, gdn_conv, gdn_state, lru_conv, lru_state, hgrn_state, reset, prompt, params):
    (norm_mix, norm_mem, norm_ffn, norm_final, ab_w_in, ab_w_out, gdn_conv_w, gdn_a_log, gdn_dt_bias,
     gdn_norm_w, lru_conv_w, lru_conv_b, lru_w_a, lru_b_a, lru_w_i, lru_b_i, lru_lam, c_w_in, c_w_out,
     hgrn_lb_raw, hgrn_norm_w, mem_w_q, mem_w_o, ffn_w_up, ffn_w_down) = params
    lb_w = jax.nn.softmax(hgrn_lb_raw.astype(jnp.float32), axis=0)
    lower_bounds = jnp.cumsum(lb_w, axis=0) - lb_w[0]
    n_gdn_conv, n_gdn, n_lru_conv, n_lru, n_hgrn = [], [], [], [], []
    for l in range(DEPTH):
        h = rms_norm(x, norm_mix[l])
        if l % 2 == 0:
            e = l // 2
            out, c1, s1, c2, s2 = ab_mixer(
                h, ab_w_in[e], ab_w_out[e], gdn_conv_w[e], gdn_a_log[e], gdn_dt_bias[e], gdn_norm_w[e],
                lru_conv_w[e], lru_conv_b[e], lru_w_a[e], lru_b_a[e], lru_w_i[e], lru_b_i[e], lru_lam[e],
                gdn_conv[e], gdn_state[e], lru_conv[e], lru_state[e], reset, prompt)
            n_gdn_conv.append(c1)
            n_gdn.append(s1)
            n_lru_conv.append(c2)
            n_lru.append(s2)
        else:
            o = l // 2
            out, s3 = c_mixer(h, c_w_in[o], c_w_out[o], lower_bounds[l], hgrn_norm_w[o], hgrn_state[o], prompt)
            n_hgrn.append(s3)
        x = x + out
        x = x + mem_attend(rms_norm(x, norm_mem[l]), mem_k[l], mem_v[l], mem_w_q[l], mem_w_o[l])
        hf = rms_norm(x, norm_ffn[l])
        x = x + jnp.square(jax.nn.relu(hf @ ffn_w_up[l])) @ ffn_w_down[l]
    y = rms_norm(x, norm_final)
    return (y, jnp.stack(n_gdn_conv), jnp.stack(n_gdn), jnp.stack(n_lru_conv), jnp.stack(n_lru),
            jnp.stack(n_hgrn))


def _normal(k, shape, scale):
    return jax.random.normal(k, shape, jnp.float32) * scale


def setup_inputs(seed: int = 0) -> dict:
    key = jax.random.key(seed)
    ks = jax.random.split(key, 40)
    f32 = jnp.float32
    gain = lambda k, shape: 1.0 + _normal(k, shape, 0.02)
    dt = jnp.exp(jax.random.uniform(ks[19], (N_EVEN, GDN_HEADS), f32, np.log(1e-3), np.log(1e-1)))
    a0 = jax.random.uniform(ks[27], (N_EVEN, LRU_WIDTH), f32, 0.9, 0.999)
    sig = a0 ** (1.0 / LRU_C)
    return {
        'x_prompt': _normal(ks[0], (BATCH, SEQ, D_MODEL), 1.0),
        'x_sample': _normal(ks[1], (DEC_BATCH, DEC_SEQ, D_MODEL), 1.0),
        'cache_mem_k': _normal(ks[2], (DEPTH, DEC_BATCH, MEM_TOKENS, MEM_HEADS, MEM_HEAD_DIM), 1.0),
        'cache_mem_v': _normal(ks[3], (DEPTH, DEC_BATCH, MEM_TOKENS, MEM_HEADS, MEM_HEAD_DIM), 1.0),
        'state_gdn_conv': _normal(ks[4], (N_EVEN, DEC_BATCH, CONV_WIDTH - 1, 3 * GDN_WIDTH), 1.0),
        'state_gdn': _normal(ks[5], (N_EVEN, DEC_BATCH, GDN_HEADS, GDN_HEAD_DIM, GDN_HEAD_DIM), 0.1),
        'state_lru_conv': _normal(ks[6], (N_EVEN, DEC_BATCH, CONV_WIDTH - 1, LRU_WIDTH), 1.0),
        'state_lru': _normal(ks[7], (N_EVEN, DEC_BATCH, LRU_WIDTH), 0.5),
        'state_hgrn': _normal(ks[8], (N_ODD, DEC_BATCH, HGRN_HEADS, HGRN_KEY_DIM, HGRN_VAL_DIM), 0.5),
        'mem_prompt': _normal(ks[9], (BATCH, MEM_TOKENS, D_MODEL), 1.0),
        'norm_mix': gain(ks[10], (DEPTH, D_MODEL)),
        'norm_mem': gain(ks[11], (DEPTH, D_MODEL)),
        'norm_mem_kv': gain(ks[12], (DEPTH, D_MODEL)),
        'norm_ffn': gain(ks[13], (DEPTH, D_MODEL)),
        'norm_final': gain(ks[14], (D_MODEL,)),
        'ab_w_in': _normal(ks[15], (N_EVEN, D_MODEL, AB_IN), D_MODEL ** -0.5),
        'ab_w_out': _normal(ks[16], (N_EVEN, MIX_WIDTH, D_MODEL), MIX_WIDTH ** -0.5),
        'gdn_conv_w': _normal(ks[17], (N_EVEN, CONV_WIDTH, 3 * GDN_WIDTH), CONV_WIDTH ** -0.5),
        'gdn_a_log': jnp.log(jax.random.uniform(ks[18], (N_EVEN, GDN_HEADS), f32, 1.0, 16.0)),
        'gdn_dt_bias': dt + jnp.log(-jnp.expm1(-dt)),
        'gdn_norm_w': gain(ks[20], (N_EVEN, GDN_HEAD_DIM)),
        'lru_conv_w': _normal(ks[21], (N_EVEN, CONV_WIDTH, LRU_WIDTH), CONV_WIDTH ** -0.5),
        'lru_conv_b': _normal(ks[22], (N_EVEN, LRU_WIDTH), 0.01),
        'lru_w_a': _normal(ks[23], (N_EVEN, LRU_BLOCKS, LRU_BLOCK_DIM, LRU_BLOCK_DIM), LRU_BLOCK_DIM ** -0.5),
        'lru_b_a': _normal(ks[24], (N_EVEN, LRU_WIDTH), 0.01),
        'lru_w_i': _normal(ks[25], (N_EVEN, LRU_BLOCKS, LRU_BLOCK_DIM, LRU_BLOCK_DIM), LRU_BLOCK_DIM ** -0.5),
        'lru_b_i': _normal(ks[26], (N_EVEN, LRU_WIDTH), 0.01),
        'lru_lam': jnp.log(sig) - jnp.log1p(-sig),
        'c_w_in': _normal(ks[28], (N_ODD, D_MODEL, C_IN), D_MODEL ** -0.5),
        'c_w_out': _normal(ks[29], (N_ODD, MIX_WIDTH, D_MODEL), MIX_WIDTH ** -0.5),
        'hgrn_lb_raw': gain(ks[30], (DEPTH, HGRN_KEY_WIDTH)),
        'hgrn_norm_w': gain(ks[31], (N_ODD, HGRN_VAL_DIM)),
        'mem_w_q': _normal(ks[32], (DEPTH, D_MODEL, MEM_WIDTH), D_MODEL ** -0.5),
        'mem_w_k': _normal(ks[33], (DEPTH, D_MODEL, MEM_WIDTH), D_MODEL ** -0.5),
        'mem_w_v': _normal(ks[34], (DEPTH, D_MODEL, MEM_WIDTH), D_MODEL ** -0.5),
        'mem_w_o': _normal(ks[35], (DEPTH, MEM_WIDTH, D_MODEL), MEM_WIDTH ** -0.5),
        'ffn_w_up': _normal(ks[36], (DEPTH, D_MODEL, D_FF), D_MODEL ** -0.5),
        'ffn_w_down': _normal(ks[37], (DEPTH, D_FF, D_MODEL), D_FF ** -0.5),
    }


def reference(x_prompt, x_sample, cache_mem_k, cache_mem_v, state_gdn_conv, state_gdn, state_lru_conv,
              state_lru, state_hgrn, mem_prompt, norm_mix, norm_mem, norm_mem_kv, norm_ffn, norm_final,
              ab_w_in, ab_w_out, gdn_conv_w, gdn_a_log, gdn_dt_bias, gdn_norm_w, lru_conv_w, lru_conv_b,
              lru_w_a, lru_b_a, lru_w_i, lru_b_i, lru_lam, c_w_in, c_w_out, hgrn_lb_raw, hgrn_norm_w,
              mem_w_q, mem_w_k, mem_w_v, mem_w_o, ffn_w_up, ffn_w_down):
    params = (norm_mix, norm_mem, norm_ffn, norm_final, ab_w_in, ab_w_out, gdn_conv_w, gdn_a_log, gdn_dt_bias,
              gdn_norm_w, lru_conv_w, lru_conv_b, lru_w_a, lru_b_a, lru_w_i, lru_b_i, lru_lam, c_w_in, c_w_out,
              hgrn_lb_raw, hgrn_norm_w, mem_w_q, mem_w_o, ffn_w_up, ffn_w_down)
    bp, lp = x_prompt.shape[0], x_prompt.shape[1]
    adt = x_prompt.dtype
    kv_pairs = [mem_project(mem_prompt, norm_mem_kv[l], mem_w_k[l], mem_w_v[l]) for l in range(DEPTH)]
    p_mem_k = jnp.stack([kv[0] for kv in kv_pairs])
    p_mem_v = jnp.stack([kv[1] for kv in kv_pairs])
    z_gdn_conv = jnp.zeros((N_EVEN, bp, CONV_WIDTH - 1, 3 * GDN_WIDTH), adt)
    z_gdn = jnp.zeros((N_EVEN, bp, GDN_HEADS, GDN_HEAD_DIM, GDN_HEAD_DIM), jnp.float32)
    z_lru_conv = jnp.zeros((N_EVEN, bp, CONV_WIDTH - 1, LRU_WIDTH), adt)
    z_lru = jnp.zeros((N_EVEN, bp, LRU_WIDTH), jnp.float32)
    z_hgrn = jnp.zeros((N_ODD, bp, HGRN_HEADS, HGRN_KEY_DIM, HGRN_VAL_DIM), jnp.float32)
    reset_p = jnp.arange(lp) == 0
    y_prompt, p_gdn_conv, p_gdn, p_lru_conv, p_lru, p_hgrn = trunk(
        x_prompt, p_mem_k, p_mem_v, z_gdn_conv, z_gdn, z_lru_conv, z_lru, z_hgrn, reset_p, True, params)
    reset_s = (PAST_LEN + jnp.arange(x_sample.shape[1])) == 0
    y_sample, s_gdn_conv, s_gdn, s_lru_conv, s_lru, s_hgrn = trunk(
        x_sample, cache_mem_k, cache_mem_v, state_gdn_conv, state_gdn, state_lru_conv, state_lru, state_hgrn,
        reset_s, False, params)
    return (y_prompt, y_sample, p_mem_k, p_mem_v, p_gdn_conv, p_gdn, p_lru_conv, p_lru, p_hgrn,
            s_gdn_conv, s_gdn, s_lru_conv, s_lru, s_hgrn)
```

```python
import functools
import math

import jax
import jax.numpy as jnp
from jax import lax
from jax.experimental import pallas as pl
from jax.experimental.pallas import tpu as pltpu

F32 = jnp.float32
BF16 = jnp.bfloat16
EPS = 1e-6
LANES = 128
CONV_WIDTH = 4
LRU_C = 8.0
VMEM_LIMIT_BYTES = 56 * 1024 * 1024
CHUNK = 128


def _params(sem):
    return pltpu.CompilerParams(dimension_semantics=sem, vmem_limit_bytes=VMEM_LIMIT_BYTES)


def _sigmoid(x):
    return jax.nn.sigmoid(x)


def _silu(x):
    return x * _sigmoid(x)


def _softplus(x):
    return jnp.maximum(x, 0.0) + jnp.log1p(jnp.exp(-jnp.abs(x)))


def _neg_expm1(x):
    t = jnp.tanh(0.5 * x)
    return -2.0 * t / (1.0 - t)


def _gelu_tanh(x):
    c = math.sqrt(2.0 / math.pi)
    return 0.5 * x * (1.0 + jnp.tanh(c * (x + 0.044715 * (x * x * x))))


def _rms(x, w):
    return x * lax.rsqrt(jnp.mean(x * x, axis=-1, keepdims=True) + EPS) * w


def _l2n(x):
    return x * lax.rsqrt(jnp.sum(x * x, axis=-1, keepdims=True) + EPS)


def _dot(a, b):
    return jnp.dot(a.astype(BF16), b.astype(BF16), preferred_element_type=F32)


def _dot_nt(a, b):
    return lax.dot_general(a.astype(BF16), b.astype(BF16), (((1,), (1,)), ((), ())),
                           preferred_element_type=F32)


def _dot_tn(a, b):
    return lax.dot_general(a.astype(BF16), b.astype(BF16), (((0,), (0,)), ((), ())),
                           preferred_element_type=F32)


def _split2(a):
    hi = a.astype(BF16)
    lo = (a - hi.astype(F32)).astype(BF16)
    return hi, lo


def _dot_hp(a, b):
    ah, al = _split2(a)
    bh, bl = _split2(b)
    d = functools.partial(jnp.dot, preferred_element_type=F32)
    return (d(ah, bh) + d(ah, bl)) + d(al, bh)


def _dot_exact_lhs(t, b):
    b1 = b.astype(BF16)
    r1 = b - b1.astype(F32)
    b2 = r1.astype(BF16)
    b3 = (r1 - b2.astype(F32)).astype(BF16)
    d = functools.partial(jnp.dot, preferred_element_type=F32)
    return (d(t, b1) + d(t, b2)) + d(t, b3)


def _iota2(n, axis):
    return lax.broadcasted_iota(jnp.int32, (n, n), axis)


def _col_from_row(row):
    n = row.shape[-1]
    return jnp.broadcast_to(row, (n, n)).T


def _norm_matmul_kernel(x_ref, g_ref, w_ref, o_ref, xn_ref):
    @pl.when(pl.program_id(1) == 0)
    def _():
        xn_ref[...] = _rms(x_ref[...], g_ref[...]).astype(BF16)

    o_ref[...] = jnp.dot(xn_ref[...], w_ref[...], preferred_element_type=F32)


def _norm_matmul(x, g, w, tm, tn):
    m, k = x.shape
    n = w.shape[1]
    tm = min(tm, m)
    tn = min(tn, n)
    return pl.pallas_call(
        _norm_matmul_kernel,
        grid=(m // tm, n // tn),
        in_specs=[pl.BlockSpec((tm, k), lambda i, j: (i, 0)),
                  pl.BlockSpec((1, k), lambda i, j: (0, 0)),
                  pl.BlockSpec((k, tn), lambda i, j: (0, j))],
        out_specs=pl.BlockSpec((tm, tn), lambda i, j: (i, j)),
        out_shape=jax.ShapeDtypeStruct((m, n), F32),
        scratch_shapes=[pltpu.VMEM((tm, k), BF16)],
        compiler_params=_params(("arbitrary", "arbitrary")),
    )(x, g.reshape(1, k), w)


def _matmul2_res_kernel(a1_ref, a2_ref, w1_ref, w2_ref, r_ref, o_ref):
    acc = jnp.dot(a1_ref[...], w1_ref[...], preferred_element_type=F32)
    acc = acc + jnp.dot(a2_ref[...], w2_ref[...], preferred_element_type=F32)
    o_ref[...] = r_ref[...] + acc


def _matmul2_res(a1, a2, w, res, tm, tn):
    m, kh = a1.shape
    n = w.shape[1]
    tm = min(tm, m)
    tn = min(tn, n)
    return pl.pallas_call(
        _matmul2_res_kernel,
        grid=(m // tm, n // tn),
        in_specs=[pl.BlockSpec((tm, kh), lambda i, j: (i, 0)),
                  pl.BlockSpec((tm, kh), lambda i, j: (i, 0)),
                  pl.BlockSpec((kh, tn), lambda i, j: (0, j)),
                  pl.BlockSpec((kh, tn), lambda i, j: (1, j)),
                  pl.BlockSpec((tm, tn), lambda i, j: (i, j))],
        out_specs=pl.BlockSpec((tm, tn), lambda i, j: (i, j)),
        out_shape=jax.ShapeDtypeStruct((m, n), F32),
        compiler_params=_params(("arbitrary", "arbitrary")),
    )(a1, a2, w, w, res)


def _matmul_res_kernel(a_ref, w_ref, r_ref, o_ref):
    o_ref[...] = r_ref[...] + jnp.dot(a_ref[...], w_ref[...], preferred_element_type=F32)


def _matmul_res(a, w, res, tm, tn):
    m, k = a.shape
    n = w.shape[1]
    tm = min(tm, m)
    tn = min(tn, n)
    return pl.pallas_call(
        _matmul_res_kernel,
        grid=(m // tm, n // tn),
        in_specs=[pl.BlockSpec((tm, k), lambda i, j: (i, 0)),
                  pl.BlockSpec((k, tn), lambda i, j: (0, j)),
                  pl.BlockSpec((tm, tn), lambda i, j: (i, j))],
        out_specs=pl.BlockSpec((tm, tn), lambda i, j: (i, j)),
        out_shape=jax.ShapeDtypeStruct((m, n), F32),
        compiler_params=_params(("arbitrary", "arbitrary")),
    )(a, w, res)


def _ffn_kernel(x_ref, g_ref, wu_ref, wd_ref, gf_ref, o_ref, xn_ref, *, final_norm):
    f = pl.program_id(1)

    @pl.when(f == 0)
    def _():
        x = x_ref[...]
        xn_ref[...] = _rms(x, g_ref[...]).astype(BF16)
        o_ref[...] = x

    h = jnp.dot(xn_ref[...], wu_ref[...], preferred_element_type=F32)
    h = jnp.square(jnp.maximum(h, 0.0)).astype(BF16)
    o_ref[...] += jnp.dot(h, wd_ref[...], preferred_element_type=F32)

    if final_norm:
        @pl.when(f == pl.num_programs(1) - 1)
        def _():
            o_ref[...] = _rms(o_ref[...], gf_ref[...])


def _ffn(x, g, w_up, w_down, g_final, final_norm, tm, tf):
    m, d = x.shape
    dff = w_up.shape[1]
    tm = min(tm, m)
    return pl.pallas_call(
        functools.partial(_ffn_kernel, final_norm=final_norm),
        grid=(m // tm, dff // tf),
        in_specs=[pl.BlockSpec((tm, d), lambda i, f: (i, 0), pipeline_mode=pl.Buffered(1)),
                  pl.BlockSpec((1, d), lambda i, f: (0, 0)),
                  pl.BlockSpec((d, tf), lambda i, f: (0, f)),
                  pl.BlockSpec((tf, d), lambda i, f: (f, 0)),
                  pl.BlockSpec((1, d), lambda i, f: (0, 0))],
        out_specs=pl.BlockSpec((tm, d), lambda i, f: (i, 0)),
        out_shape=jax.ShapeDtypeStruct((m, d), F32),
        scratch_shapes=[pltpu.VMEM((tm, d), BF16)],
        compiler_params=_params(("arbitrary", "arbitrary")),
    )(x, g.reshape(1, d), w_up, w_down, g_final.reshape(1, d))


def _mem_attn_prompt_kernel(x_ref, g_ref, wq_ref, k_ref, v_ref, wo_ref, o_ref, *, heads):
    x = x_ref[...]
    xn = _rms(x, g_ref[...]).astype(BF16)
    q = jnp.dot(xn, wq_ref[...], preferred_element_type=F32)
    k = k_ref[0].astype(BF16)
    v = v_ref[0].astype(BF16)
    scale = LANES ** -0.5
    outs = []
    for h in range(heads):
        sl = slice(h * LANES, (h + 1) * LANES)
        s = _dot_nt(q[:, sl], k[:, sl]) * scale
        e = jnp.exp(s - jnp.max(s, axis=-1, keepdims=True))
        p = e / jnp.sum(e, axis=-1, keepdims=True)
        outs.append(_dot(p, v[:, sl]))
    o = jnp.concatenate(outs, axis=-1).astype(BF16)
    o_ref[...] = x + jnp.dot(o, wo_ref[...], preferred_element_type=F32)


def _mem_attn_prompt(x, g, w_q, mem_k, mem_v, w_o, seq, tl):
    m, d = x.shape
    _, t, w = mem_k.shape
    nblk = seq // tl
    return pl.pallas_call(
        functools.partial(_mem_attn_prompt_kernel, heads=w // LANES),
        grid=(m // tl,),
        in_specs=[pl.BlockSpec((tl, d), lambda i: (i, 0)),
                  pl.BlockSpec((1, d), lambda i: (0, 0)),
                  pl.BlockSpec((d, w), lambda i: (0, 0)),
                  pl.BlockSpec((1, t, w), lambda i: (i // nblk, 0, 0)),
                  pl.BlockSpec((1, t, w), lambda i: (i // nblk, 0, 0)),
                  pl.BlockSpec((w, d), lambda i: (0, 0))],
        out_specs=pl.BlockSpec((tl, d), lambda i: (i, 0)),
        out_shape=jax.ShapeDtypeStruct((m, d), F32),
        compiler_params=_params(("arbitrary",)),
    )(x, g.reshape(1, d), w_q, mem_k, mem_v, w_o)


def _mem_attn_sample_kernel(q_ref, k_ref, v_ref, e_ref, et_ref, o_ref, *, tb):
    scale = LANES ** -0.5
    e = e_ref[...]
    et = et_ref[...]
    for b in range(tb):
        q = q_ref[b:b + 1, :]
        kq = k_ref[b] * q
        kh, kl = _split2(kq)
        s = (jnp.dot(kh, e, preferred_element_type=F32)
             + jnp.dot(kl, e, preferred_element_type=F32)) * scale
        ex = jnp.exp(s - jnp.max(s, axis=0, keepdims=True))
        p = ex / jnp.sum(ex, axis=0, keepdims=True)
        pw = jnp.dot(p.astype(BF16), et, preferred_element_type=F32)
        o_ref[b:b + 1, :] = jnp.sum(pw * v_ref[b], axis=0, keepdims=True).astype(BF16)


def _mem_attn_sample(q, mem_k, mem_v, tb):
    b, w = q.shape
    t = mem_k.shape[1]
    head_of_lane = jnp.arange(w) // LANES
    e = (head_of_lane[:, None] == jnp.arange(LANES)[None, :]).astype(BF16)
    return pl.pallas_call(
        functools.partial(_mem_attn_sample_kernel, tb=tb),
        grid=(b // tb,),
        in_specs=[pl.BlockSpec((tb, w), lambda i: (i, 0)),
                  pl.BlockSpec((tb, t, w), lambda i: (i, 0, 0)),
                  pl.BlockSpec((tb, t, w), lambda i: (i, 0, 0)),
                  pl.BlockSpec((w, LANES), lambda i: (0, 0)),
                  pl.BlockSpec((LANES, w), lambda i: (0, 0))],
        out_specs=pl.BlockSpec((tb, w), lambda i: (i, 0)),
        out_shape=jax.ShapeDtypeStruct((b, w), BF16),
        compiler_params=_params(("arbitrary",)),
    )(q, mem_k, mem_v, e, e.T)


def _conv_block(x_ref, buf, w):
    cb = x_ref.shape[0]
    buf[8:8 + cb, :] = x_ref[...]
    y = buf[5:5 + cb, :] * w[0:1]
    y = y + buf[6:6 + cb, :] * w[1:2]
    y = y + buf[7:7 + cb, :] * w[2:3]
    y = y + buf[8:8 + cb, :] * w[3:4]
    buf[0:8, :] = buf[cb:cb + 8, :]
    return y


def _gdn_gates(ba, alog, dtb, h, heads):
    lane = lax.broadcasted_iota(jnp.int32, ba.shape, 1)
    beta_all = _sigmoid(ba)
    g_all = -jnp.exp(alog) * _softplus(ba + dtb)
    beta = jnp.sum(jnp.where(lane == h, beta_all, 0.0), axis=1, keepdims=True)
    g = jnp.sum(jnp.where(lane == heads + h, g_all, 0.0), axis=1, keepdims=True)
    return beta, g


def _gdn_prompt_kernel(q_ref, k_ref, v_ref, z_ref, ba_ref, cwq_ref, cwk_ref, cwv_ref, alog_ref, dtb_ref,
                       nw_ref, o_ref, s_ref, qbuf, kbuf, vbuf, s_scr, *, heads):
    h = pl.program_id(1)
    blk = pl.program_id(2)
    cb = q_ref.shape[0]
    c = CHUNK

    @pl.when(blk == 0)
    def _():
        s_scr[...] = jnp.zeros_like(s_scr)
        qbuf[0:8, :] = jnp.zeros((8, LANES), F32)
        kbuf[0:8, :] = jnp.zeros((8, LANES), F32)
        vbuf[0:8, :] = jnp.zeros((8, LANES), F32)

    q = _l2n(_silu(_conv_block(q_ref, qbuf, cwq_ref[...]))) * (LANES ** -0.5)
    k = _l2n(_silu(_conv_block(k_ref, kbuf, cwk_ref[...])))
    v = _silu(_conv_block(v_ref, vbuf, cwv_ref[...]))
    beta, g = _gdn_gates(ba_ref[...], alog_ref[...], dtb_ref[...], h, heads)

    row = _iota2(c, 0)
    col = _iota2(c, 1)
    causal = col <= row
    strict = col < row
    tri = causal.astype(BF16)
    eye = (col == row).astype(F32)
    nw = nw_ref[...]

    for ci in range(cb // c):
        sl = slice(ci * c, (ci + 1) * c)
        qc, kc, vc = q[sl], k[sl], v[sl]
        bc = beta[sl]
        gc = _dot_exact_lhs(tri, jnp.broadcast_to(g[sl], (c, c)))
        decay = jnp.exp(jnp.where(causal, gc - gc.T, -jnp.inf))
        kb = kc * bc
        x = -jnp.where(strict, _dot_nt(kb, kc) * decay, 0.0)
        t = eye + x
        for _ in range(int(math.log2(c)) - 1):
            x = _dot_hp(x, x)
            t = t + _dot_hp(t, x)
        eg = jnp.exp(gc)
        sol = _dot(t, jnp.concatenate([vc * bc, kb * eg], axis=1))
        u, w = sol[:, :LANES], sol[:, LANES:]
        attn = jnp.where(causal, _dot_nt(qc, kc) * decay, 0.0)
        g_last = gc[c - 1:c, :]
        kd = kc * jnp.exp(g_last - gc)
        s = s_scr[...]
        v_new = u - _dot(w, s)
        o = _dot(qc * eg, s) + _dot(attn, v_new)
        s_scr[...] = s * jnp.exp(g_last) + _dot_tn(kd, v_new)
        o_ref[sl, :] = (_rms(o, nw) * _silu(z_ref[sl, :])).astype(BF16)

    @pl.when(blk == pl.num_programs(2) - 1)
    def _():
        s_ref[0, 0] = s_scr[...]


def _gdn_prompt(proj, ba, conv_w, alog_pad, dtb_pad, norm_w, batch, seq, heads, cb):
    m = proj.shape[0]
    nblk = seq // cb
    rows = lambda b, h, c: b * nblk + c
    head_blk = lambda off: pl.BlockSpec((cb, LANES), lambda b, h, c: (rows(b, h, c), off + h))
    cw_blk = lambda off: pl.BlockSpec((CONV_WIDTH, LANES), lambda b, h, c: (0, off + h))
    row128 = pl.BlockSpec((1, LANES), lambda b, h, c: (0, 0))
    return pl.pallas_call(
        functools.partial(_gdn_prompt_kernel, heads=heads),
        grid=(batch, heads, nblk),
        in_specs=[head_blk(0), head_blk(heads), head_blk(2 * heads), head_blk(3 * heads),
                  pl.BlockSpec((cb, LANES), lambda b, h, c: (rows(b, h, c), 0)),
                  cw_blk(0), cw_blk(heads), cw_blk(2 * heads), row128, row128, row128],
        out_specs=[pl.BlockSpec((cb, LANES), lambda b, h, c: (rows(b, h, c), h)),
                   pl.BlockSpec((1, 1, LANES, LANES), lambda b, h, c: (b, h, 0, 0))],
        out_shape=[jax.ShapeDtypeStruct((m, heads * LANES), BF16),
                   jax.ShapeDtypeStruct((batch, heads, LANES, LANES), F32)],
        scratch_shapes=[pltpu.VMEM((cb + 8, LANES), F32)] * 3 + [pltpu.VMEM((LANES, LANES), F32)],
        compiler_params=_params(("arbitrary", "arbitrary", "arbitrary")),
    )(proj, proj, proj, proj, ba, conv_w, conv_w, conv_w, alog_pad, dtb_pad, norm_w.reshape(1, LANES))


def _gdn_step_kernel(q_ref, k_ref, v_ref, z_ref, ba_ref, cq_ref, ck_ref, cv_ref, cwq_ref, cwk_ref, cwv_ref,
                     alog_ref, dtb_ref, nw_ref, s_ref, o_ref, so_ref, *, heads):
    h = pl.program_id(1)
    tb = q_ref.shape[0]

    def conv(x_ref, c_ref, w_ref):
        w = w_ref[...]
        y = c_ref[0] * w[0:1]
        y = y + c_ref[1] * w[1:2]
        y = y + c_ref[2] * w[2:3]
        return y + x_ref[...] * w[3:4]

    q = _l2n(_silu(conv(q_ref, cq_ref, cwq_ref))) * (LANES ** -0.5)
    k = _l2n(_silu(conv(k_ref, ck_ref, cwk_ref)))
    v = _silu(conv(v_ref, cv_ref, cwv_ref))
    beta, g = _gdn_gates(ba_ref[...], alog_ref[...], dtb_ref[...], h, heads)
    eg = jnp.exp(g)
    outs = []
    for b in range(tb):
        r = slice(b, b + 1)
        kcol = _col_from_row(k[r])
        qcol = _col_from_row(q[r])
        s = s_ref[b, 0] * eg[r]
        v_new = beta[r] * (v[r] - jnp.sum(kcol * s, axis=0, keepdims=True))
        s = s + kcol * v_new
        so_ref[b, 0] = s
        outs.append(jnp.sum(qcol * s, axis=0, keepdims=True))
    o = jnp.concatenate(outs, axis=0)
    o_ref[...] = (_rms(o, nw_ref[...]) * _silu(z_ref[...])).astype(BF16)


def _gdn_step(proj, ba, conv_state_t, conv_w, alog_pad, dtb_pad, norm_w, state, heads, tb):
    bsz = proj.shape[0]
    head_blk = lambda off: pl.BlockSpec((tb, LANES), lambda i, h: (i, off + h))
    cs_blk = lambda off: pl.BlockSpec((CONV_WIDTH - 1, tb, LANES), lambda i, h: (0, i, off + h))
    cw_blk = lambda off: pl.BlockSpec((CONV_WIDTH, LANES), lambda i, h: (0, off + h))
    row128 = pl.BlockSpec((1, LANES), lambda i, h: (0, 0))
    st_blk = pl.BlockSpec((tb, 1, LANES, LANES), lambda i, h: (i, h, 0, 0))
    return pl.pallas_call(
        functools.partial(_gdn_step_kernel, heads=heads),
        grid=(bsz // tb, heads),
        in_specs=[head_blk(0), head_blk(heads), head_blk(2 * heads), head_blk(3 * heads),
                  pl.BlockSpec((tb, LANES), lambda i, h: (i, 0)),
                  cs_blk(0), cs_blk(heads), cs_blk(2 * heads),
                  cw_blk(0), cw_blk(heads), cw_blk(2 * heads), row128, row128, row128, st_blk],
        out_specs=[pl.BlockSpec((tb, LANES), lambda i, h: (i, h)), st_blk],
        out_shape=[jax.ShapeDtypeStruct((bsz, heads * LANES), BF16),
                   jax.ShapeDtypeStruct(state.shape, F32)],
        compiler_params=_params(("arbitrary", "arbitrary")),
    )(proj, proj, proj, proj, ba, conv_state_t, conv_state_t, conv_state_t, conv_w, conv_w, conv_w,
      alog_pad, dtb_pad, norm_w.reshape(1, LANES), state)


def _lru_gates(x, wa_ref, wi_ref, ba, bi, lam):
    nb = wa_ref.shape[0]
    ga, gi = [], []
    for s in range(nb):
        xs = x[:, s * LANES:(s + 1) * LANES].astype(BF16)
        ga.append(jnp.dot(xs, wa_ref[s], preferred_element_type=F32))
        gi.append(jnp.dot(xs, wi_ref[s], preferred_element_type=F32))
    gate_a = _sigmoid(jnp.concatenate(ga, axis=-1) + ba)
    gate_i = _sigmoid(jnp.concatenate(gi, axis=-1) + bi)
    log_a = -LRU_C * gate_a * _softplus(-lam)
    return log_a, gate_i


def _lru_prompt_kernel(xl_ref, yl_ref, cw_ref, cb_ref, wa_ref, wi_ref, ba_ref, bi_ref, lam_ref,
                       o_ref, hl_ref, xbuf, abuf, bbuf, h_scr):
    blk = pl.program_id(1)
    cb = xl_ref.shape[0]
    width = xl_ref.shape[1]

    @pl.when(blk == 0)
    def _():
        h_scr[...] = jnp.zeros_like(h_scr)
        xbuf[0:8, :] = jnp.zeros((8, width), F32)

    x = _conv_block(xl_ref, xbuf, cw_ref[...]) + cb_ref[...]
    log_a, gate_i = _lru_gates(x, wa_ref, wi_ref, ba_ref[...], bi_ref[...], lam_ref[...])
    mult = jnp.sqrt(_neg_expm1(2.0 * log_a))
    first = (lax.broadcasted_iota(jnp.int32, (cb, 1), 0) == 0) & (blk == 0)
    mult = jnp.where(first, 1.0, mult)
    abuf[...] = jnp.exp(log_a)
    bbuf[...] = mult * gate_i * x

    def step(t, h):
        h = abuf[pl.ds(t, 1), :] * h + bbuf[pl.ds(t, 1), :]
        bbuf[pl.ds(t, 1), :] = h
        return h

    h_last = lax.fori_loop(0, cb, step, h_scr[...], unroll=8)
    h_scr[...] = h_last
    o_ref[...] = (bbuf[...] * _gelu_tanh(yl_ref[...])).astype(BF16)

    @pl.when(blk == pl.num_programs(1) - 1)
    def _():
        hl_ref[0] = h_last


def _lru_prompt(xy, conv_w, conv_b, w_a, w_i, b_a, b_i, lam, batch, seq, cb):
    m = xy.shape[0]
    width = xy.shape[1] // 2
    nblk = seq // cb
    nb = w_a.shape[0]
    vec = pl.BlockSpec((1, width), lambda b, c: (0, 0))
    wblk = pl.BlockSpec((nb, LANES, LANES), lambda b, c: (0, 0, 0))
    out, h_last = pl.pallas_call(
        _lru_prompt_kernel,
        grid=(batch, nblk),
        in_specs=[pl.BlockSpec((cb, width), lambda b, c: (b * nblk + c, 0)),
                  pl.BlockSpec((cb, width), lambda b, c: (b * nblk + c, 1)),
                  pl.BlockSpec((CONV_WIDTH, width), lambda b, c: (0, 0)),
                  vec, wblk, wblk, vec, vec, vec],
        out_specs=[pl.BlockSpec((cb, width), lambda b, c: (b * nblk + c, 0)),
                   pl.BlockSpec((1, 1, width), lambda b, c: (b, 0, 0))],
        out_shape=[jax.ShapeDtypeStruct((m, width), BF16),
                   jax.ShapeDtypeStruct((batch, 1, width), F32)],
        scratch_shapes=[pltpu.VMEM((cb + 8, width), F32), pltpu.VMEM((cb, width), F32),
                        pltpu.VMEM((cb, width), F32), pltpu.VMEM((1, width), F32)],
        compiler_params=_params(("arbitrary", "arbitrary")),
    )(xy, xy, conv_w, conv_b.reshape(1, width), w_a, w_i, b_a.reshape(1, width), b_i.reshape(1, width),
      lam.reshape(1, width))
    return out, h_last.reshape(batch, width)


def _lru_step_kernel(xl_ref, yl_ref, cs_ref, h0_ref, cw_ref, cb_ref, wa_ref, wi_ref, ba_ref, bi_ref, lam_ref,
                     o_ref, h_ref, *, reset):
    w = cw_ref[...]
    x = cs_ref[0] * w[0:1]
    x = x + cs_ref[1] * w[1:2]
    x = x + cs_ref[2] * w[2:3]
    x = x + xl_ref[...] * w[3:4] + cb_ref[...]
    log_a, gate_i = _lru_gates(x, wa_ref, wi_ref, ba_ref[...], bi_ref[...], lam_ref[...])
    mult = 1.0 if reset else jnp.sqrt(_neg_expm1(2.0 * log_a))
    h = jnp.exp(log_a) * h0_ref[...] + mult * gate_i * x
    h_ref[...] = h
    o_ref[...] = (h * _gelu_tanh(yl_ref[...])).astype(BF16)


def _lru_step(xy, conv_state_t, h0, conv_w, conv_b, w_a, w_i, b_a, b_i, lam, reset):
    bsz = xy.shape[0]
    width = xy.shape[1] // 2
    nb = w_a.shape[0]
    vec = pl.BlockSpec((1, width), lambda i: (0, 0))
    wblk = pl.BlockSpec((nb, LANES, LANES), lambda i: (0, 0, 0))
    full = pl.BlockSpec((bsz, width), lambda i: (0, 0))
    return pl.pallas_call(
        functools.partial(_lru_step_kernel, reset=reset),
        grid=(1,),
        in_specs=[full, pl.BlockSpec((bsz, width), lambda i: (0, 1)),
                  pl.BlockSpec((CONV_WIDTH - 1, bsz, width), lambda i: (0, 0, 0)), full,
                  pl.BlockSpec((CONV_WIDTH, width), lambda i: (0, 0)), vec, wblk, wblk, vec, vec, vec],
        out_specs=[full, full],
        out_shape=[jax.ShapeDtypeStruct((bsz, width), BF16), jax.ShapeDtypeStruct((bsz, width), F32)],
        compiler_params=_params(("arbitrary",)),
    )(xy, xy, conv_state_t, h0, conv_w, conv_b.reshape(1, width), w_a, w_i, b_a.reshape(1, width),
      b_i.reshape(1, width), lam.reshape(1, width))


def _hgrn_lower_bound(lb_ref, layer):
    depth = lb_ref.shape[0]
    raw = [lb_ref[l, 0] for l in range(depth)]
    mx = raw[0]
    for r in raw[1:]:
        mx = jnp.maximum(mx, r)
    ex = [jnp.exp(r - mx) for r in raw]
    tot = ex[0]
    for e in ex[1:]:
        tot = tot + e
    wts = [e / tot for e in ex]
    cum = wts[0]
    for w in wts[1:layer + 1]:
        cum = cum + w
    return cum - wts[0]


def _hgrn_inputs(q_raw, f_raw, lb):
    q = _silu(q_raw) * (LANES ** -0.5)
    f = lb + (1.0 - lb) * _sigmoid(f_raw)
    return q, f


def _hgrn_prompt_kernel(q_ref, f_ref, i_ref, gz_ref, lb_ref, nw_ref, o_ref, s_ref, s_scr, *, layer):
    blk = pl.program_id(2)
    cb = q_ref.shape[0]
    c = CHUNK
    sub = 8

    @pl.when(blk == 0)
    def _():
        s_scr[...] = jnp.zeros_like(s_scr)

    lb = _hgrn_lower_bound(lb_ref, layer)
    row = _iota2(c, 0)
    col = _iota2(c, 1)
    tri = (col <= row).astype(BF16)
    nw = nw_ref[...]
    il = lax.broadcasted_iota(jnp.int32, (c // sub, sub, LANES), 1)

    for ci in range(cb // c):
        sl = slice(ci * c, (ci + 1) * c)
        q, f = _hgrn_inputs(q_ref[sl, :], f_ref[sl, :], lb)
        k = 1.0 - f
        v = i_ref[sl, :]
        gcs = _dot_exact_lhs(tri, jnp.log(f))
        s = s_scr[...]
        o = _dot(q * jnp.exp(gcs), s)

        a_off = jnp.zeros((c, c), F32)
        half = c // 2
        while half >= sub:
            blk2 = 2 * half
            g3 = gcs.reshape(c // blk2, blk2, LANES)
            ref3 = jnp.broadcast_to(g3[:, half - 1:half, :], g3.shape).reshape(c, LANES)
            second = (row[:, 0:1] % blk2) >= half
            qt = jnp.where(second, q * jnp.exp(jnp.where(second, gcs - ref3, 0.0)), 0.0)
            kt = jnp.where(second, 0.0, k * jnp.exp(jnp.where(second, 0.0, ref3 - gcs)))
            same = (row // blk2) == (col // blk2)
            a_off = a_off + jnp.where(same, _dot_nt(qt, kt), 0.0)
            half //= 2
        o = o + _dot(a_off, v)

        g3 = gcs.reshape(c // sub, sub, LANES)
        q3 = q.reshape(c // sub, sub, LANES)
        k3 = k.reshape(c // sub, sub, LANES)
        v3 = v.reshape(c // sub, sub, LANES)
        od = jnp.zeros((c // sub, sub, LANES), F32)
        for j in range(sub):
            gj = g3[:, j:j + 1, :]
            d = jnp.exp(jnp.where(il >= j, g3 - gj, -jnp.inf))
            sj = jnp.sum(q3 * k3[:, j:j + 1, :] * d, axis=-1, keepdims=True)
            od = od + sj * v3[:, j:j + 1, :]
        o = o + od.reshape(c, LANES)

        g_last = gcs[c - 1:c, :]
        kd = k * jnp.exp(g_last - gcs)
        s_scr[...] = s * _col_from_row(jnp.exp(g_last)) + _dot_tn(kd, v)
        o_ref[sl, :] = (_rms(o, nw) * _silu(gz_ref[sl, :])).astype(BF16)

    @pl.when(blk == pl.num_programs(2) - 1)
    def _():
        s_ref[0, 0] = s_scr[...]


def _hgrn_prompt(proj, lb_raw, norm_w, layer, batch, seq, heads, cb):
    m = proj.shape[0]
    nblk = seq // cb
    depth = lb_raw.shape[0]
    head_blk = lambda off: pl.BlockSpec((cb, LANES), lambda b, h, c: (b * nblk + c, off + h))
    return pl.pallas_call(
        functools.partial(_hgrn_prompt_kernel, layer=layer),
        grid=(batch, heads, nblk),
        in_specs=[head_blk(0), head_blk(heads), head_blk(2 * heads), head_blk(3 * heads),
                  pl.BlockSpec((depth, 1, 1, LANES), lambda b, h, c: (0, h, 0, 0)),
                  pl.BlockSpec((1, LANES), lambda b, h, c: (0, 0))],
        out_specs=[pl.BlockSpec((cb, LANES), lambda b, h, c: (b * nblk + c, h)),
                   pl.BlockSpec((1, 1, LANES, LANES), lambda b, h, c: (b, h, 0, 0))],
        out_shape=[jax.ShapeDtypeStruct((m, heads * LANES), BF16),
                   jax.ShapeDtypeStruct((batch, heads, LANES, LANES), F32)],
        scratch_shapes=[pltpu.VMEM((LANES, LANES), F32)],
        compiler_params=_params(("arbitrary", "arbitrary", "arbitrary")),
    )(proj, proj, proj, proj, lb_raw.reshape(depth, heads, 1, LANES), norm_w.reshape(1, LANES))


def _hgrn_step_kernel(q_ref, f_ref, i_ref, gz_ref, lb_ref, nw_ref, s_ref, o_ref, so_ref, *, layer):
    tb = q_ref.shape[0]
    lb = _hgrn_lower_bound(lb_ref, layer)
    q, f = _hgrn_inputs(q_ref[...], f_ref[...], lb)
    k = 1.0 - f
    v = i_ref[...]
    outs = []
    for b in range(tb):
        r = slice(b, b + 1)
        s = s_ref[b, 0] * _col_from_row(f[r]) + _col_from_row(k[r]) * v[r]
        so_ref[b, 0] = s
        outs.append(jnp.sum(_col_from_row(q[r]) * s, axis=0, keepdims=True))
    o = jnp.concatenate(outs, axis=0)
    o_ref[...] = (_rms(o, nw_ref[...]) * _silu(gz_ref[...])).astype(BF16)


def _hgrn_step(proj, lb_raw, norm_w, state, layer, heads, tb):
    bsz = proj.shape[0]
    depth = lb_raw.shape[0]
    head_blk = lambda off: pl.BlockSpec((tb, LANES), lambda i, h: (i, off + h))
    st_blk = pl.BlockSpec((tb, 1, LANES, LANES), lambda i, h: (i, h, 0, 0))
    return pl.pallas_call(
        functools.partial(_hgrn_step_kernel, layer=layer),
        grid=(bsz // tb, heads),
        in_specs=[head_blk(0), head_blk(heads), head_blk(2 * heads), head_blk(3 * heads),
                  pl.BlockSpec((depth, 1, 1, LANES), lambda i, h: (0, h, 0, 0)),
                  pl.BlockSpec((1, LANES), lambda i, h: (0, 0)), st_blk],
        out_specs=[pl.BlockSpec((tb, LANES), lambda i, h: (i, h)), st_blk],
        out_shape=[jax.ShapeDtypeStruct((bsz, heads * LANES), BF16),
                   jax.ShapeDtypeStruct(state.shape, F32)],
        compiler_params=_params(("arbitrary", "arbitrary")),
    )(proj, proj, proj, proj, lb_raw.reshape(depth, heads, 1, LANES), norm_w.reshape(1, LANES), state)


def _trunk(x, mem_k, mem_v, states, wts, batch, seq, prompt):
    depth = wts["norm_mix"].shape[0]
    gdn_w = wts["gdn_conv_w"].shape[-1] // 3
    gdn_heads = gdn_w // LANES
    hgrn_heads = wts["hgrn_lb_raw"].shape[1] // LANES
    m = x.shape[0]
    tm = 512
    new = {"gdn_conv": [], "gdn": [], "lru_conv": [], "lru": [], "hgrn": []}
    for l in range(depth):
        if l % 2 == 0:
            e = l // 2
            proj = _norm_matmul(x, wts["norm_mix"][l], wts["ab_main"][e], tm, 1024)
            ba = _norm_matmul(x, wts["norm_mix"][l], wts["ab_ba"][e], tm, LANES)
            xy = _norm_matmul(x, wts["norm_mix"][l], wts["ab_lru"][e], tm, 1024)
            lru_w = xy.shape[1] // 2
            gargs = (wts["gdn_conv_w"][e], wts["alog_pad"][e], wts["dtb_pad"][e], wts["gdn_norm_w"][e])
            largs = (wts["lru_conv_w"][e], wts["lru_conv_b"][e], wts["lru_w_a"][e], wts["lru_w_i"][e],
                     wts["lru_b_a"][e], wts["lru_b_i"][e], wts["lru_lam"][e])
            if prompt:
                o_a, s_new = _gdn_prompt(proj, ba, *gargs, batch, seq, gdn_heads, 512)
                o_b, h_new = _lru_prompt(xy, *largs, batch, seq, 256)
                tail = lambda t, w: t[:, :w].reshape(batch, seq, w)[:, seq - (CONV_WIDTH - 1):]
                new["gdn_conv"].append(tail(proj, 3 * gdn_w))
                new["lru_conv"].append(tail(xy, lru_w))
            else:
                gc_state, g_state, lc_state, l_state = (states[n][e] for n in
                                                        ("gdn_conv", "gdn", "lru_conv", "lru"))
                o_a, s_new = _gdn_step(proj, ba, jnp.swapaxes(gc_state, 0, 1), *gargs, g_state, gdn_heads, 8)
                o_b, h_new = _lru_step(xy, jnp.swapaxes(lc_state, 0, 1), l_state, *largs, reset=False)
                new["gdn_conv"].append(jnp.concatenate([gc_state[:, 1:], proj[:, None, :3 * gdn_w]], axis=1))
                new["lru_conv"].append(jnp.concatenate([lc_state[:, 1:], xy[:, None, :lru_w]], axis=1))
            new["gdn"].append(s_new)
            new["lru"].append(h_new)
            x = _matmul2_res(o_a, o_b, wts["ab_w_out"][e], x, tm, 1024)
        else:
            o_idx = l // 2
            proj = _norm_matmul(x, wts["norm_mix"][l], wts["c_w_in"][o_idx], tm, 1024)
            if prompt:
                o_c, s_new = _hgrn_prompt(proj, wts["hgrn_lb_raw"], wts["hgrn_norm_w"][o_idx], l,
                                          batch, seq, hgrn_heads, 512)
            else:
                o_c, s_new = _hgrn_step(proj, wts["hgrn_lb_raw"], wts["hgrn_norm_w"][o_idx],
                                        states["hgrn"][o_idx], l, hgrn_heads, 8)
            new["hgrn"].append(s_new)
            x = _matmul_res(o_c, wts["c_w_out"][o_idx], x, tm, 1024)
        mk = mem_k[l].reshape(mem_k.shape[1], mem_k.shape[2], -1)
        mv = mem_v[l].reshape(mk.shape)
        if prompt:
            x = _mem_attn_prompt(x, wts["norm_mem"][l], wts["mem_w_q"][l], mk, mv, wts["mem_w_o"][l], seq, 256)
        else:
            q = _norm_matmul(x, wts["norm_mem"][l], wts["mem_w_q"][l], tm, 512)
            o = _mem_attn_sample(q, mk, mv, 8)
            x = _matmul_res(o, wts["mem_w_o"][l], x, tm, 1024)
        last = l == depth - 1
        x = _ffn(x, wts["norm_ffn"][l], wts["ffn_w_up"][l], wts["ffn_w_down"][l], wts["norm_final"], last,
                 tm, 512)
    return x, {n: jnp.stack(v) for n, v in new.items()}


def kernel(x_prompt, x_sample, cache_mem_k, cache_mem_v, state_gdn_conv, state_gdn, state_lru_conv, state_lru, state_hgrn, mem_prompt, norm_mix, norm_mem, norm_mem_kv, norm_ffn, norm_final, ab_w_in, ab_w_out, gdn_conv_w, gdn_a_log, gdn_dt_bias, gdn_norm_w, lru_conv_w, lru_conv_b, lru_w_a, lru_b_a, lru_w_i, lru_b_i, lru_lam, c_w_in, c_w_out, hgrn_lb_raw, hgrn_norm_w, mem_w_q, mem_w_k, mem_w_v, mem_w_o, ffn_w_up, ffn_w_down):
    bp, lp, d = x_prompt.shape
    bs, ls, _ = x_sample.shape
    assert ls == 1, "the sample group advances one token per call"
    depth = norm_mix.shape[0]
    gdn_heads = gdn_a_log.shape[1]
    gdn_w = gdn_heads * LANES
    mem_tokens, mem_heads, mem_hd = cache_mem_k.shape[2:]
    n_ba = 2 * gdn_heads
    assert n_ba <= LANES

    pad_ba = lambda a: jnp.pad(a, ((0, 0), (gdn_heads, LANES - n_ba)))[:, None, :]
    wts = dict(
        norm_mix=norm_mix, norm_mem=norm_mem, norm_ffn=norm_ffn, norm_final=norm_final,
        ab_main=ab_w_in[:, :, :4 * gdn_w].astype(BF16),
        ab_ba=jnp.pad(ab_w_in[:, :, 4 * gdn_w:4 * gdn_w + n_ba], ((0, 0), (0, 0), (0, LANES - n_ba))).astype(BF16),
        ab_lru=ab_w_in[:, :, 4 * gdn_w + n_ba:].astype(BF16),
        ab_w_out=ab_w_out.astype(BF16),
        gdn_conv_w=gdn_conv_w, alog_pad=pad_ba(gdn_a_log), dtb_pad=pad_ba(gdn_dt_bias), gdn_norm_w=gdn_norm_w,
        lru_conv_w=lru_conv_w, lru_conv_b=lru_conv_b, lru_w_a=lru_w_a.astype(BF16), lru_w_i=lru_w_i.astype(BF16),
        lru_b_a=lru_b_a, lru_b_i=lru_b_i, lru_lam=lru_lam,
        c_w_in=c_w_in.astype(BF16), c_w_out=c_w_out.astype(BF16),
        hgrn_lb_raw=hgrn_lb_raw, hgrn_norm_w=hgrn_norm_w,
        mem_w_q=mem_w_q.astype(BF16), mem_w_o=mem_w_o.astype(BF16),
        ffn_w_up=ffn_w_up.astype(BF16), ffn_w_down=ffn_w_down.astype(BF16),
    )

    mem_rows = mem_prompt.reshape(bp * mem_tokens, d)
    w_kv = jnp.concatenate([mem_w_k, mem_w_v], axis=-1).astype(BF16)
    mem_w = mem_heads * mem_hd
    kv = [_norm_matmul(mem_rows, norm_mem_kv[l], w_kv[l], 512, 1024) for l in range(depth)]
    shp = (bp, mem_tokens, mem_heads, mem_hd)
    p_mem_k = jnp.stack([t[:, :mem_w].reshape(shp) for t in kv])
    p_mem_v = jnp.stack([t[:, mem_w:].reshape(shp) for t in kv])

    y_p, new_p = _trunk(x_prompt.reshape(bp * lp, d), p_mem_k, p_mem_v, None, wts, bp, lp, True)
    states = dict(gdn_conv=state_gdn_conv, gdn=state_gdn, lru_conv=state_lru_conv, lru=state_lru,
                  hgrn=state_hgrn)
    y_s, new_s = _trunk(x_sample.reshape(bs * ls, d), cache_mem_k, cache_mem_v, states, wts, bs, ls, False)

    order = ("gdn_conv", "gdn", "lru_conv", "lru", "hgrn")
    return (y_p.reshape(bp, lp, d), y_s.reshape(bs, ls, d), p_mem_k, p_mem_v,
            *(new_p[n] for n in order), *(new_s[n] for n in order))
```

```python
import functools
import math

import jax
import jax.numpy as jnp
from jax import lax
from jax.experimental import pallas as pl
from jax.experimental.pallas import tpu as pltpu

F32 = jnp.float32
BF16 = jnp.bfloat16
EPS = 1e-6
LANES = 128
CONV_WIDTH = 4
LRU_C = 8.0
VMEM_LIMIT_BYTES = 56 * 1024 * 1024
CHUNK = 128


def _params(sem):
    return pltpu.CompilerParams(dimension_semantics=sem, vmem_limit_bytes=VMEM_LIMIT_BYTES)


def _sigmoid(x):
    return jax.nn.sigmoid(x)


def _silu(x):
    return x * _sigmoid(x)


def _softplus(x):
    return jnp.maximum(x, 0.0) + jnp.log1p(jnp.exp(-jnp.abs(x)))


def _neg_expm1(x):
    t = jnp.tanh(0.5 * x)
    return -2.0 * t / (1.0 - t)


def _gelu_tanh(x):
    c = math.sqrt(2.0 / math.pi)
    return 0.5 * x * (1.0 + jnp.tanh(c * (x + 0.044715 * (x * x * x))))


def _rms(x, w):
    return x * lax.rsqrt(jnp.mean(x * x, axis=-1, keepdims=True) + EPS) * w


def _l2n(x):
    return x * lax.rsqrt(jnp.sum(x * x, axis=-1, keepdims=True) + EPS)


def _dot(a, b):
    return jnp.dot(a.astype(BF16), b.astype(BF16), preferred_element_type=F32)


def _dot_nt(a, b):
    return lax.dot_general(a.astype(BF16), b.astype(BF16), (((1,), (1,)), ((), ())),
                           preferred_element_type=F32)


def _dot_tn(a, b):
    return lax.dot_general(a.astype(BF16), b.astype(BF16), (((0,), (0,)), ((), ())),
                           preferred_element_type=F32)


def _split2(a):
    hi = a.astype(BF16)
    lo = (a - hi.astype(F32)).astype(BF16)
    return hi, lo


def _dot_hp(a_parts, b_parts):
    ah, al = a_parts
    bh, bl = b_parts
    n = bh.shape[1]
    d = functools.partial(jnp.dot, preferred_element_type=F32)
    r = d(ah, jnp.concatenate([bh, bl], axis=1))
    return (r[:, :n] + r[:, n:]) + d(al, bh)


def _dot_exact_lhs(t, b):
    b1 = b.astype(BF16)
    r1 = b - b1.astype(F32)
    b2 = r1.astype(BF16)
    b3 = (r1 - b2.astype(F32)).astype(BF16)
    d = functools.partial(jnp.dot, preferred_element_type=F32)
    return (d(t, b1) + d(t, b2)) + d(t, b3)


def _iota2(n, axis):
    return lax.broadcasted_iota(jnp.int32, (n, n), axis)


def _col_from_row(row):
    n = row.shape[-1]
    return jnp.broadcast_to(row, (n, n)).T


def _norm_matmul_kernel(x_ref, g_ref, w_ref, o_ref, xn_ref):
    @pl.when(pl.program_id(1) == 0)
    def _():
        xn_ref[...] = _rms(x_ref[...], g_ref[...]).astype(BF16)

    o_ref[...] = jnp.dot(xn_ref[...], w_ref[...], preferred_element_type=F32)


def _norm_matmul(x, g, w, layer, n0, n, tm, tn):
    m, k = x.shape
    tm = min(tm, m)
    tn = min(tn, n)
    j0 = n0 // tn
    return pl.pallas_call(
        _norm_matmul_kernel,
        grid=(m // tm, n // tn),
        in_specs=[pl.BlockSpec((tm, k), lambda i, j: (i, 0)),
                  pl.BlockSpec((1, k), lambda i, j: (0, 0)),
                  pl.BlockSpec((None, k, tn), lambda i, j: (layer, 0, j0 + j))],
        out_specs=pl.BlockSpec((tm, tn), lambda i, j: (i, j)),
        out_shape=jax.ShapeDtypeStruct((m, n), F32),
        scratch_shapes=[pltpu.VMEM((tm, k), BF16)],
        compiler_params=_params(("arbitrary", "arbitrary")),
    )(x, g.reshape(1, k), w)


def _matmul2_res_kernel(a1_ref, a2_ref, w1_ref, w2_ref, r_ref, o_ref):
    acc = jnp.dot(a1_ref[...], w1_ref[...], preferred_element_type=F32)
    acc = acc + jnp.dot(a2_ref[...], w2_ref[...], preferred_element_type=F32)
    o_ref[...] = r_ref[...] + acc


def _matmul2_res(a1, a2, w, layer, res, tm, tn):
    m, kh = a1.shape
    n = w.shape[2]
    tm = min(tm, m)
    tn = min(tn, n)
    return pl.pallas_call(
        _matmul2_res_kernel,
        grid=(m // tm, n // tn),
        in_specs=[pl.BlockSpec((tm, kh), lambda i, j: (i, 0)),
                  pl.BlockSpec((tm, kh), lambda i, j: (i, 0)),
                  pl.BlockSpec((None, kh, tn), lambda i, j: (layer, 0, j)),
                  pl.BlockSpec((None, kh, tn), lambda i, j: (layer, 1, j)),
                  pl.BlockSpec((tm, tn), lambda i, j: (i, j))],
        out_specs=pl.BlockSpec((tm, tn), lambda i, j: (i, j)),
        out_shape=jax.ShapeDtypeStruct((m, n), F32),
        compiler_params=_params(("arbitrary", "arbitrary")),
    )(a1, a2, w, w, res)


def _matmul_res_kernel(a_ref, w_ref, r_ref, o_ref):
    o_ref[...] = r_ref[...] + jnp.dot(a_ref[...], w_ref[...], preferred_element_type=F32)


def _matmul_res(a, w, layer, res, tm, tn):
    m, k = a.shape
    n = w.shape[2]
    tm = min(tm, m)
    tn = min(tn, n)
    return pl.pallas_call(
        _matmul_res_kernel,
        grid=(m // tm, n // tn),
        in_specs=[pl.BlockSpec((tm, k), lambda i, j: (i, 0)),
                  pl.BlockSpec((None, k, tn), lambda i, j: (layer, 0, j)),
                  pl.BlockSpec((tm, tn), lambda i, j: (i, j))],
        out_specs=pl.BlockSpec((tm, tn), lambda i, j: (i, j)),
        out_shape=jax.ShapeDtypeStruct((m, n), F32),
        compiler_params=_params(("arbitrary", "arbitrary")),
    )(a, w, res)


def _ffn_kernel(x_ref, g_ref, wu_ref, wd_ref, gf_ref, o_ref, xn_ref, *, final_norm):
    f = pl.program_id(1)

    @pl.when(f == 0)
    def _():
        x = x_ref[...]
        xn_ref[...] = _rms(x, g_ref[...]).astype(BF16)
        o_ref[...] = x

    h = jnp.dot(xn_ref[...], wu_ref[...], preferred_element_type=F32)
    h = jnp.square(jnp.maximum(h, 0.0)).astype(BF16)
    o_ref[...] += jnp.dot(h, wd_ref[...], preferred_element_type=F32)

    if final_norm:
        @pl.when(f == pl.num_programs(1) - 1)
        def _():
            o_ref[...] = _rms(o_ref[...], gf_ref[...])


def _ffn(x, g, w_up, w_down, layer, g_final, final_norm, tm, tf):
    m, d = x.shape
    dff = w_up.shape[2]
    tm = min(tm, m)
    return pl.pallas_call(
        functools.partial(_ffn_kernel, final_norm=final_norm),
        grid=(m // tm, dff // tf),
        in_specs=[pl.BlockSpec((tm, d), lambda i, f: (i, 0), pipeline_mode=pl.Buffered(1)),
                  pl.BlockSpec((1, d), lambda i, f: (0, 0)),
                  pl.BlockSpec((None, d, tf), lambda i, f: (layer, 0, f)),
                  pl.BlockSpec((None, tf, d), lambda i, f: (layer, f, 0)),
                  pl.BlockSpec((1, d), lambda i, f: (0, 0))],
        out_specs=pl.BlockSpec((tm, d), lambda i, f: (i, 0)),
        out_shape=jax.ShapeDtypeStruct((m, d), F32),
        scratch_shapes=[pltpu.VMEM((tm, d), BF16)],
        compiler_params=_params(("arbitrary", "arbitrary")),
    )(x, g.reshape(1, d), w_up, w_down, g_final.reshape(1, d))


def _mem_attn_prompt_kernel(x_ref, g_ref, wq_ref, k_ref, v_ref, wo_ref, o_ref, *, heads):
    x = x_ref[...]
    xn = _rms(x, g_ref[...]).astype(BF16)
    q = jnp.dot(xn, wq_ref[...], preferred_element_type=F32)
    k = k_ref[0].astype(BF16)
    v = v_ref[0].astype(BF16)
    scale = LANES ** -0.5
    outs = []
    for h in range(heads):
        sl = slice(h * LANES, (h + 1) * LANES)
        s = _dot_nt(q[:, sl], k[:, sl]) * scale
        e = jnp.exp(s - jnp.max(s, axis=-1, keepdims=True))
        p = e / jnp.sum(e, axis=-1, keepdims=True)
        outs.append(_dot(p, v[:, sl]))
    o = jnp.concatenate(outs, axis=-1).astype(BF16)
    o_ref[...] = x + jnp.dot(o, wo_ref[...], preferred_element_type=F32)


def _mem_attn_prompt(x, g, w_q, mem_k, mem_v, w_o, layer, seq, tl):
    m, d = x.shape
    _, _, t, w = mem_k.shape
    nblk = seq // tl
    return pl.pallas_call(
        functools.partial(_mem_attn_prompt_kernel, heads=w // LANES),
        grid=(m // tl,),
        in_specs=[pl.BlockSpec((tl, d), lambda i: (i, 0)),
                  pl.BlockSpec((1, d), lambda i: (0, 0)),
                  pl.BlockSpec((None, d, w), lambda i: (layer, 0, 0)),
                  pl.BlockSpec((None, 1, t, w), lambda i: (layer, i // nblk, 0, 0)),
                  pl.BlockSpec((None, 1, t, w), lambda i: (layer, i // nblk, 0, 0)),
                  pl.BlockSpec((None, w, d), lambda i: (layer, 0, 0))],
        out_specs=pl.BlockSpec((tl, d), lambda i: (i, 0)),
        out_shape=jax.ShapeDtypeStruct((m, d), F32),
        compiler_params=_params(("arbitrary",)),
    )(x, g.reshape(1, d), w_q, mem_k, mem_v, w_o)


def _mem_attn_sample_kernel(q_ref, k_ref, v_ref, e_ref, et_ref, o_ref, *, tb):
    scale = LANES ** -0.5
    e = e_ref[...]
    et = et_ref[...]
    for b in range(tb):
        q = q_ref[b:b + 1, :]
        kq = k_ref[b] * q
        kh, kl = _split2(kq)
        s = (jnp.dot(kh, e, preferred_element_type=F32)
             + jnp.dot(kl, e, preferred_element_type=F32)) * scale
        ex = jnp.exp(s - jnp.max(s, axis=0, keepdims=True))
        p = ex / jnp.sum(ex, axis=0, keepdims=True)
        pw = jnp.dot(p.astype(BF16), et, preferred_element_type=F32)
        o_ref[b:b + 1, :] = jnp.sum(pw * v_ref[b], axis=0, keepdims=True).astype(BF16)


def _mem_attn_sample(q, mem_k, mem_v, layer, tb):
    b, w = q.shape
    t = mem_k.shape[2]
    head_of_lane = jnp.arange(w) // LANES
    e = (head_of_lane[:, None] == jnp.arange(LANES)[None, :]).astype(BF16)
    return pl.pallas_call(
        functools.partial(_mem_attn_sample_kernel, tb=tb),
        grid=(b // tb,),
        in_specs=[pl.BlockSpec((tb, w), lambda i: (i, 0)),
                  pl.BlockSpec((None, tb, t, w), lambda i: (layer, i, 0, 0)),
                  pl.BlockSpec((None, tb, t, w), lambda i: (layer, i, 0, 0)),
                  pl.BlockSpec((w, LANES), lambda i: (0, 0)),
                  pl.BlockSpec((LANES, w), lambda i: (0, 0))],
        out_specs=pl.BlockSpec((tb, w), lambda i: (i, 0)),
        out_shape=jax.ShapeDtypeStruct((b, w), BF16),
        compiler_params=_params(("arbitrary",)),
    )(q, mem_k, mem_v, e, e.T)


def _conv_block(x_ref, buf, w):
    cb = x_ref.shape[0]
    buf[8:8 + cb, :] = x_ref[...]
    y = buf[5:5 + cb, :] * w[0:1]
    y = y + buf[6:6 + cb, :] * w[1:2]
    y = y + buf[7:7 + cb, :] * w[2:3]
    y = y + buf[8:8 + cb, :] * w[3:4]
    buf[0:8, :] = buf[cb:cb + 8, :]
    return y


def _gdn_gates(ba, alog, dtb, h, heads):
    lane = lax.broadcasted_iota(jnp.int32, ba.shape, 1)
    beta_all = _sigmoid(ba)
    g_all = -jnp.exp(alog) * _softplus(ba + dtb)
    beta = jnp.sum(jnp.where(lane == h, beta_all, 0.0), axis=1, keepdims=True)
    g = jnp.sum(jnp.where(lane == heads + h, g_all, 0.0), axis=1, keepdims=True)
    return beta, g


def _gdn_prompt_kernel(q_ref, k_ref, v_ref, z_ref, ba_ref, cwq_ref, cwk_ref, cwv_ref, alog_ref, dtb_ref,
                       nw_ref, o_ref, s_ref, qbuf, kbuf, vbuf, s_scr, *, heads):
    h = pl.program_id(1)
    blk = pl.program_id(2)
    cb = q_ref.shape[0]
    c = CHUNK

    @pl.when(blk == 0)
    def _():
        s_scr[...] = jnp.zeros_like(s_scr)
        qbuf[0:8, :] = jnp.zeros((8, LANES), F32)
        kbuf[0:8, :] = jnp.zeros((8, LANES), F32)
        vbuf[0:8, :] = jnp.zeros((8, LANES), F32)

    q = _l2n(_silu(_conv_block(q_ref, qbuf, cwq_ref[...]))) * (LANES ** -0.5)
    k = _l2n(_silu(_conv_block(k_ref, kbuf, cwk_ref[...])))
    v = _silu(_conv_block(v_ref, vbuf, cwv_ref[...]))
    beta, g = _gdn_gates(ba_ref[...], alog_ref[...], dtb_ref[...], h, heads)

    row = _iota2(c, 0)
    col = _iota2(c, 1)
    causal = col <= row
    strict = col < row
    tri = causal.astype(BF16)
    eye = (col == row).astype(F32)
    nw = nw_ref[...]

    chunks = [slice(ci * c, (ci + 1) * c) for ci in range(cb // c)]
    gcs = [_dot_exact_lhs(tri, jnp.broadcast_to(g[sl], (c, c))) for sl in chunks]
    decays = [jnp.exp(jnp.where(causal, gc - gc.T, -jnp.inf)) for gc in gcs]
    kbs = [k[sl] * beta[sl] for sl in chunks]
    ms = [jnp.where(strict, _dot_nt(kb, k[sl]) * dec, 0.0) for kb, sl, dec in zip(kbs, chunks, decays)]
    xs = [_split2(-m) for m in ms]
    ts = [eye - m for m in ms]
    for _ in range(int(math.log2(c)) - 1):
        xs = [_split2(_dot_hp(xp, xp)) for xp in xs]
        ts = [t + _dot_hp(_split2(t), xp) for t, xp in zip(ts, xs)]
    trans, adds, qeffs, outs0 = [], [], [], []
    for sl, gc, dec, kb, t in zip(chunks, gcs, decays, kbs, ts):
        eg = jnp.exp(gc)
        sol = _dot(t, jnp.concatenate([v[sl] * beta[sl], kb * eg], axis=1))
        u, w = sol[:, :LANES], sol[:, LANES:]
        attn = jnp.where(causal, _dot_nt(q[sl], k[sl]) * dec, 0.0)
        g_last = gc[c - 1:c, :]
        kd = k[sl] * jnp.exp(g_last - gc)
        trans.append(eye * jnp.exp(g_last) - _dot_tn(kd, w))
        adds.append(_dot_tn(kd, u))
        qeffs.append(q[sl] * eg - _dot(attn, w))
        outs0.append(_dot(attn, u))

    s = s_scr[...]
    outs = []
    for a, b, qe, o0 in zip(trans, adds, qeffs, outs0):
        outs.append(o0 + _dot(qe, s))
        s = _dot(a, s) + b
    s_scr[...] = s
    for sl, o in zip(chunks, outs):
        o_ref[sl, :] = (_rms(o, nw) * _silu(z_ref[sl, :])).astype(BF16)

    @pl.when(blk == pl.num_programs(2) - 1)
    def _():
        s_ref[0, 0] = s_scr[...]


def _gdn_prompt(proj, ba, conv_w, alog_pad, dtb_pad, norm_w, batch, seq, heads, cb):
    m = proj.shape[0]
    nblk = seq // cb
    rows = lambda b, h, c: b * nblk + c
    head_blk = lambda off: pl.BlockSpec((cb, LANES), lambda b, h, c: (rows(b, h, c), off + h))
    cw_blk = lambda off: pl.BlockSpec((CONV_WIDTH, LANES), lambda b, h, c: (0, off + h))
    row128 = pl.BlockSpec((1, LANES), lambda b, h, c: (0, 0))
    return pl.pallas_call(
        functools.partial(_gdn_prompt_kernel, heads=heads),
        grid=(batch, heads, nblk),
        in_specs=[head_blk(0), head_blk(heads), head_blk(2 * heads), head_blk(3 * heads),
                  pl.BlockSpec((cb, LANES), lambda b, h, c: (rows(b, h, c), 0)),
                  cw_blk(0), cw_blk(heads), cw_blk(2 * heads), row128, row128, row128],
        out_specs=[pl.BlockSpec((cb, LANES), lambda b, h, c: (rows(b, h, c), h)),
                   pl.BlockSpec((1, 1, LANES, LANES), lambda b, h, c: (b, h, 0, 0))],
        out_shape=[jax.ShapeDtypeStruct((m, heads * LANES), BF16),
                   jax.ShapeDtypeStruct((batch, heads, LANES, LANES), F32)],
        scratch_shapes=[pltpu.VMEM((cb + 8, LANES), F32)] * 3 + [pltpu.VMEM((LANES, LANES), F32)],
        compiler_params=_params(("arbitrary", "arbitrary", "arbitrary")),
    )(proj, proj, proj, proj, ba, conv_w, conv_w, conv_w, alog_pad, dtb_pad, norm_w.reshape(1, LANES))


def _gdn_step_kernel(q_ref, k_ref, v_ref, z_ref, ba_ref, cq_ref, ck_ref, cv_ref, cwq_ref, cwk_ref, cwv_ref,
                     alog_ref, dtb_ref, nw_ref, s_ref, o_ref, so_ref, *, heads):
    h = pl.program_id(1)
    tb = q_ref.shape[0]

    def conv(x_ref, c_ref, w_ref):
        w = w_ref[...]
        y = c_ref[0] * w[0:1]
        y = y + c_ref[1] * w[1:2]
        y = y + c_ref[2] * w[2:3]
        return y + x_ref[...] * w[3:4]

    q = _l2n(_silu(conv(q_ref, cq_ref, cwq_ref))) * (LANES ** -0.5)
    k = _l2n(_silu(conv(k_ref, ck_ref, cwk_ref)))
    v = _silu(conv(v_ref, cv_ref, cwv_ref))
    beta, g = _gdn_gates(ba_ref[...], alog_ref[...], dtb_ref[...], h, heads)
    eg = jnp.exp(g)
    outs = []
    for b in range(tb):
        r = slice(b, b + 1)
        kcol = _col_from_row(k[r])
        qcol = _col_from_row(q[r])
        s = s_ref[b, 0] * eg[r]
        v_new = beta[r] * (v[r] - jnp.sum(kcol * s, axis=0, keepdims=True))
        s = s + kcol * v_new
        so_ref[b, 0] = s
        outs.append(jnp.sum(qcol * s, axis=0, keepdims=True))
    o = jnp.concatenate(outs, axis=0)
    o_ref[...] = (_rms(o, nw_ref[...]) * _silu(z_ref[...])).astype(BF16)


def _gdn_step(proj, ba, conv_state_t, conv_w, alog_pad, dtb_pad, norm_w, state, layer, heads, tb):
    bsz = proj.shape[0]
    head_blk = lambda off: pl.BlockSpec((tb, LANES), lambda i, h: (i, off + h))
    cs_blk = lambda off: pl.BlockSpec((CONV_WIDTH - 1, tb, LANES), lambda i, h: (0, i, off + h))
    cw_blk = lambda off: pl.BlockSpec((CONV_WIDTH, LANES), lambda i, h: (0, off + h))
    row128 = pl.BlockSpec((1, LANES), lambda i, h: (0, 0))
    st_in = pl.BlockSpec((None, tb, 1, LANES, LANES), lambda i, h: (layer, i, h, 0, 0))
    st_out = pl.BlockSpec((tb, 1, LANES, LANES), lambda i, h: (i, h, 0, 0))
    return pl.pallas_call(
        functools.partial(_gdn_step_kernel, heads=heads),
        grid=(bsz // tb, heads),
        in_specs=[head_blk(0), head_blk(heads), head_blk(2 * heads), head_blk(3 * heads),
                  pl.BlockSpec((tb, LANES), lambda i, h: (i, 0)),
                  cs_blk(0), cs_blk(heads), cs_blk(2 * heads),
                  cw_blk(0), cw_blk(heads), cw_blk(2 * heads), row128, row128, row128, st_in],
        out_specs=[pl.BlockSpec((tb, LANES), lambda i, h: (i, h)), st_out],
        out_shape=[jax.ShapeDtypeStruct((bsz, heads * LANES), BF16),
                   jax.ShapeDtypeStruct(state.shape[1:], F32)],
        compiler_params=_params(("arbitrary", "arbitrary")),
    )(proj, proj, proj, proj, ba, conv_state_t, conv_state_t, conv_state_t, conv_w, conv_w, conv_w,
      alog_pad, dtb_pad, norm_w.reshape(1, LANES), state)


def _lru_gates(x, wa_ref, wi_ref, ba, bi, lam):
    nb = wa_ref.shape[0]
    ga, gi = [], []
    for s in range(nb):
        xs = x[:, s * LANES:(s + 1) * LANES].astype(BF16)
        ga.append(jnp.dot(xs, wa_ref[s], preferred_element_type=F32))
        gi.append(jnp.dot(xs, wi_ref[s], preferred_element_type=F32))
    gate_a = _sigmoid(jnp.concatenate(ga, axis=-1) + ba)
    gate_i = _sigmoid(jnp.concatenate(gi, axis=-1) + bi)
    log_a = -LRU_C * gate_a * _softplus(-lam)
    return log_a, gate_i


def _lru_prompt_kernel(xl_ref, yl_ref, cw_ref, cb_ref, wa_ref, wi_ref, ba_ref, bi_ref, lam_ref,
                       o_ref, hl_ref, xbuf, abuf, bbuf, h_scr):
    blk = pl.program_id(1)
    cb = xl_ref.shape[0]
    width = xl_ref.shape[1]

    @pl.when(blk == 0)
    def _():
        h_scr[...] = jnp.zeros_like(h_scr)
        xbuf[0:8, :] = jnp.zeros((8, width), F32)

    x = _conv_block(xl_ref, xbuf, cw_ref[...]) + cb_ref[...]
    log_a, gate_i = _lru_gates(x, wa_ref, wi_ref, ba_ref[...], bi_ref[...], lam_ref[...])
    mult = jnp.sqrt(_neg_expm1(2.0 * log_a))
    first = (lax.broadcasted_iota(jnp.int32, (cb, 1), 0) == 0) & (blk == 0)
    mult = jnp.where(first, 1.0, mult)
    abuf[...] = jnp.exp(log_a)
    bbuf[...] = mult * gate_i * x

    def step(t, h):
        h = abuf[pl.ds(t, 1), :] * h + bbuf[pl.ds(t, 1), :]
        bbuf[pl.ds(t, 1), :] = h
        return h

    h_last = lax.fori_loop(0, cb, step, h_scr[...], unroll=8)
    h_scr[...] = h_last
    o_ref[...] = (bbuf[...] * _gelu_tanh(yl_ref[...])).astype(BF16)

    @pl.when(blk == pl.num_programs(1) - 1)
    def _():
        hl_ref[0] = h_last


def _lru_prompt(xy, conv_w, conv_b, w_a, w_i, b_a, b_i, lam, batch, seq, cb):
    m = xy.shape[0]
    width = xy.shape[1] // 2
    nblk = seq // cb
    nb = w_a.shape[0]
    vec = pl.BlockSpec((1, width), lambda b, c: (0, 0))
    wblk = pl.BlockSpec((nb, LANES, LANES), lambda b, c: (0, 0, 0))
    out, h_last = pl.pallas_call(
        _lru_prompt_kernel,
        grid=(batch, nblk),
        in_specs=[pl.BlockSpec((cb, width), lambda b, c: (b * nblk + c, 0)),
                  pl.BlockSpec((cb, width), lambda b, c: (b * nblk + c, 1)),
                  pl.BlockSpec((CONV_WIDTH, width), lambda b, c: (0, 0)),
                  vec, wblk, wblk, vec, vec, vec],
        out_specs=[pl.BlockSpec((cb, width), lambda b, c: (b * nblk + c, 0)),
                   pl.BlockSpec((1, 1, width), lambda b, c: (b, 0, 0))],
        out_shape=[jax.ShapeDtypeStruct((m, width), BF16),
                   jax.ShapeDtypeStruct((batch, 1, width), F32)],
        scratch_shapes=[pltpu.VMEM((cb + 8, width), F32), pltpu.VMEM((cb, width), F32),
                        pltpu.VMEM((cb, width), F32), pltpu.VMEM((1, width), F32)],
        compiler_params=_params(("arbitrary", "arbitrary")),
    )(xy, xy, conv_w, conv_b.reshape(1, width), w_a, w_i, b_a.reshape(1, width), b_i.reshape(1, width),
      lam.reshape(1, width))
    return out, h_last.reshape(batch, width)


def _lru_step_kernel(xl_ref, yl_ref, cs_ref, h0_ref, cw_ref, cb_ref, wa_ref, wi_ref, ba_ref, bi_ref, lam_ref,
                     o_ref, h_ref, *, reset):
    w = cw_ref[...]
    x = cs_ref[0] * w[0:1]
    x = x + cs_ref[1] * w[1:2]
    x = x + cs_ref[2] * w[2:3]
    x = x + xl_ref[...] * w[3:4] + cb_ref[...]
    log_a, gate_i = _lru_gates(x, wa_ref, wi_ref, ba_ref[...], bi_ref[...], lam_ref[...])
    mult = 1.0 if reset else jnp.sqrt(_neg_expm1(2.0 * log_a))
    h = jnp.exp(log_a) * h0_ref[...] + mult * gate_i * x
    h_ref[...] = h
    o_ref[...] = (h * _gelu_tanh(yl_ref[...])).astype(BF16)


def _lru_step(xy, conv_state_t, h0, conv_w, conv_b, w_a, w_i, b_a, b_i, lam, reset):
    bsz = xy.shape[0]
    width = xy.shape[1] // 2
    nb = w_a.shape[0]
    vec = pl.BlockSpec((1, width), lambda i: (0, 0))
    wblk = pl.BlockSpec((nb, LANES, LANES), lambda i: (0, 0, 0))
    full = pl.BlockSpec((bsz, width), lambda i: (0, 0))
    return pl.pallas_call(
        functools.partial(_lru_step_kernel, reset=reset),
        grid=(1,),
        in_specs=[full, pl.BlockSpec((bsz, width), lambda i: (0, 1)),
                  pl.BlockSpec((CONV_WIDTH - 1, bsz, width), lambda i: (0, 0, 0)), full,
                  pl.BlockSpec((CONV_WIDTH, width), lambda i: (0, 0)), vec, wblk, wblk, vec, vec, vec],
        out_specs=[full, full],
        out_shape=[jax.ShapeDtypeStruct((bsz, width), BF16), jax.ShapeDtypeStruct((bsz, width), F32)],
        compiler_params=_params(("arbitrary",)),
    )(xy, xy, conv_state_t, h0, conv_w, conv_b.reshape(1, width), w_a, w_i, b_a.reshape(1, width),
      b_i.reshape(1, width), lam.reshape(1, width))


def _hgrn_lower_bound(lb_ref, layer):
    depth = lb_ref.shape[0]
    raw = [lb_ref[l, 0] for l in range(depth)]
    mx = raw[0]
    for r in raw[1:]:
        mx = jnp.maximum(mx, r)
    ex = [jnp.exp(r - mx) for r in raw]
    tot = ex[0]
    for e in ex[1:]:
        tot = tot + e
    wts = [e / tot for e in ex]
    cum = wts[0]
    for w in wts[1:layer + 1]:
        cum = cum + w
    return cum - wts[0]


def _hgrn_inputs(q_raw, f_raw, lb):
    q = _silu(q_raw) * (LANES ** -0.5)
    f = lb + (1.0 - lb) * _sigmoid(f_raw)
    return q, f


def _hgrn_prompt_kernel(q_ref, f_ref, i_ref, gz_ref, lb_ref, nw_ref, o_ref, s_ref, s_scr, *, layer):
    blk = pl.program_id(2)
    cb = q_ref.shape[0]
    c = CHUNK
    sub = 8

    @pl.when(blk == 0)
    def _():
        s_scr[...] = jnp.zeros_like(s_scr)

    lb = _hgrn_lower_bound(lb_ref, layer)
    row = _iota2(c, 0)
    col = _iota2(c, 1)
    tri = (col <= row).astype(BF16)
    nw = nw_ref[...]
    il = lax.broadcasted_iota(jnp.int32, (c // sub, sub, LANES), 1)

    chunks = [slice(ci * c, (ci + 1) * c) for ci in range(cb // c)]
    intra, qdec, sdec, sadd = [], [], [], []
    for sl in chunks:
        q, f = _hgrn_inputs(q_ref[sl, :], f_ref[sl, :], lb)
        k = 1.0 - f
        v = i_ref[sl, :]
        gcs = _dot_exact_lhs(tri, jnp.log(f))

        a_off = None
        half = c // 2
        while half >= sub:
            blk2 = 2 * half
            g3 = gcs.reshape(c // blk2, blk2, LANES)
            ref3 = jnp.broadcast_to(g3[:, half - 1:half, :], g3.shape).reshape(c, LANES)
            second = (row[:, 0:1] % blk2) >= half
            qt = q * jnp.exp(jnp.where(second, gcs - ref3, -jnp.inf))
            kt = k * jnp.exp(jnp.where(second, -jnp.inf, ref3 - gcs))
            p = _dot_nt(qt, kt)
            if blk2 < c:
                p = jnp.where((row // blk2) == (col // blk2), p, 0.0)
            a_off = p if a_off is None else a_off + p
            half //= 2
        o = _dot(a_off, v)

        g3 = gcs.reshape(c // sub, sub, LANES)
        q3 = q.reshape(c // sub, sub, LANES)
        k3 = k.reshape(c // sub, sub, LANES)
        v3 = v.reshape(c // sub, sub, LANES)
        od = jnp.zeros((c // sub, sub, LANES), F32)
        for j in range(sub):
            gj = g3[:, j:j + 1, :]
            d = jnp.exp(jnp.where(il >= j, g3 - gj, -jnp.inf))
            sj = jnp.sum(q3 * k3[:, j:j + 1, :] * d, axis=-1, keepdims=True)
            od = od + sj * v3[:, j:j + 1, :]
        intra.append(o + od.reshape(c, LANES))

        g_last = gcs[c - 1:c, :]
        qdec.append(q * jnp.exp(gcs))
        sdec.append(_col_from_row(jnp.exp(g_last)))
        sadd.append(_dot_tn(k * jnp.exp(g_last - gcs), v))

    s = s_scr[...]
    outs = []
    for o, qd, dec, add in zip(intra, qdec, sdec, sadd):
        outs.append(o + _dot(qd, s))
        s = s * dec + add
    s_scr[...] = s
    for sl, o in zip(chunks, outs):
        o_ref[sl, :] = (_rms(o, nw) * _silu(gz_ref[sl, :])).astype(BF16)

    @pl.when(blk == pl.num_programs(2) - 1)
    def _():
        s_ref[0, 0] = s_scr[...]


def _hgrn_prompt(proj, lb_raw, norm_w, layer, batch, seq, heads, cb):
    m = proj.shape[0]
    nblk = seq // cb
    depth = lb_raw.shape[0]
    head_blk = lambda off: pl.BlockSpec((cb, LANES), lambda b, h, c: (b * nblk + c, off + h))
    return pl.pallas_call(
        functools.partial(_hgrn_prompt_kernel, layer=layer),
        grid=(batch, heads, nblk),
        in_specs=[head_blk(0), head_blk(heads), head_blk(2 * heads), head_blk(3 * heads),
                  pl.BlockSpec((depth, 1, 1, LANES), lambda b, h, c: (0, h, 0, 0)),
                  pl.BlockSpec((1, LANES), lambda b, h, c: (0, 0))],
        out_specs=[pl.BlockSpec((cb, LANES), lambda b, h, c: (b * nblk + c, h)),
                   pl.BlockSpec((1, 1, LANES, LANES), lambda b, h, c: (b, h, 0, 0))],
        out_shape=[jax.ShapeDtypeStruct((m, heads * LANES), BF16),
                   jax.ShapeDtypeStruct((batch, heads, LANES, LANES), F32)],
        scratch_shapes=[pltpu.VMEM((LANES, LANES), F32)],
        compiler_params=_params(("arbitrary", "arbitrary", "arbitrary")),
    )(proj, proj, proj, proj, lb_raw.reshape(depth, heads, 1, LANES), norm_w.reshape(1, LANES))


def _hgrn_step_kernel(q_ref, f_ref, i_ref, gz_ref, lb_ref, nw_ref, s_ref, o_ref, so_ref, *, layer):
    tb = q_ref.shape[0]
    lb = _hgrn_lower_bound(lb_ref, layer)
    q, f = _hgrn_inputs(q_ref[...], f_ref[...], lb)
    k = 1.0 - f
    v = i_ref[...]
    outs = []
    for b in range(tb):
        r = slice(b, b + 1)
        s = s_ref[b, 0] * _col_from_row(f[r]) + _col_from_row(k[r]) * v[r]
        so_ref[b, 0] = s
        outs.append(jnp.sum(_col_from_row(q[r]) * s, axis=0, keepdims=True))
    o = jnp.concatenate(outs, axis=0)
    o_ref[...] = (_rms(o, nw_ref[...]) * _silu(gz_ref[...])).astype(BF16)


def _hgrn_step(proj, lb_raw, norm_w, state, state_idx, layer, heads, tb):
    bsz = proj.shape[0]
    depth = lb_raw.shape[0]
    head_blk = lambda off: pl.BlockSpec((tb, LANES), lambda i, h: (i, off + h))
    st_in = pl.BlockSpec((None, tb, 1, LANES, LANES), lambda i, h: (state_idx, i, h, 0, 0))
    st_out = pl.BlockSpec((tb, 1, LANES, LANES), lambda i, h: (i, h, 0, 0))
    return pl.pallas_call(
        functools.partial(_hgrn_step_kernel, layer=layer),
        grid=(bsz // tb, heads),
        in_specs=[head_blk(0), head_blk(heads), head_blk(2 * heads), head_blk(3 * heads),
                  pl.BlockSpec((depth, 1, 1, LANES), lambda i, h: (0, h, 0, 0)),
                  pl.BlockSpec((1, LANES), lambda i, h: (0, 0)), st_in],
        out_specs=[pl.BlockSpec((tb, LANES), lambda i, h: (i, h)), st_out],
        out_shape=[jax.ShapeDtypeStruct((bsz, heads * LANES), BF16),
                   jax.ShapeDtypeStruct(state.shape[1:], F32)],
        compiler_params=_params(("arbitrary", "arbitrary")),
    )(proj, proj, proj, proj, lb_raw.reshape(depth, heads, 1, LANES), norm_w.reshape(1, LANES), state)


TM = 512
TN = 1024
TF = 512


def _trunk(x, mem_k, mem_v, states, wts, batch, seq, prompt):
    depth = wts["norm_mix"].shape[0]
    gdn_w = wts["gdn_conv_w"].shape[-1] // 3
    gdn_heads = gdn_w // LANES
    hgrn_heads = wts["hgrn_lb_raw"].shape[1] // LANES
    lru_w = wts["lru_conv_w"].shape[-1]
    new = {"gdn_conv": [], "gdn": [], "lru_conv": [], "lru": [], "hgrn": []}
    for l in range(depth):
        g_mix = wts["norm_mix"][l]
        if l % 2 == 0:
            e = l // 2
            proj = _norm_matmul(x, g_mix, wts["ab_head"], e, 0, 4 * gdn_w, TM, TN)
            ba = _norm_matmul(x, g_mix, wts["ab_head"], e, 4 * gdn_w, LANES, TM, LANES)
            xy = _norm_matmul(x, g_mix, wts["ab_lru"], e, 0, 2 * lru_w, TM, TN)
            gargs = (wts["gdn_conv_w"][e], wts["alog_pad"][e], wts["dtb_pad"][e], wts["gdn_norm_w"][e])
            largs = (wts["lru_conv_w"][e], wts["lru_conv_b"][e], wts["lru_w_a"][e], wts["lru_w_i"][e],
                     wts["lru_b_a"][e], wts["lru_b_i"][e], wts["lru_lam"][e])
            if prompt:
                o_a, s_new = _gdn_prompt(proj, ba, *gargs, batch, seq, gdn_heads, 512)
                o_b, h_new = _lru_prompt(xy, *largs, batch, seq, 256)
                tail = lambda t, w: t.reshape(batch, seq, -1)[:, seq - (CONV_WIDTH - 1):, :w]
                new["gdn_conv"].append(tail(proj, 3 * gdn_w))
                new["lru_conv"].append(tail(xy, lru_w))
            else:
                gc_state, lc_state = states["gdn_conv"][e], states["lru_conv"][e]
                o_a, s_new = _gdn_step(proj, ba, jnp.swapaxes(gc_state, 0, 1), *gargs, states["gdn"], e,
                                       gdn_heads, 32)
                o_b, h_new = _lru_step(xy, jnp.swapaxes(lc_state, 0, 1), states["lru"][e], *largs, reset=False)
                new["gdn_conv"].append(jnp.concatenate([gc_state[:, 1:], proj[:, None, :3 * gdn_w]], axis=1))
                new["lru_conv"].append(jnp.concatenate([lc_state[:, 1:], xy[:, None, :lru_w]], axis=1))
            new["gdn"].append(s_new)
            new["lru"].append(h_new)
            x = _matmul2_res(o_a, o_b, wts["ab_w_out"], e, x, TM, TN)
        else:
            o_idx = l // 2
            proj = _norm_matmul(x, g_mix, wts["c_w_in"], o_idx, 0, wts["c_w_in"].shape[2], TM, TN)
            hargs = (wts["hgrn_lb_raw"], wts["hgrn_norm_w"][o_idx])
            if prompt:
                o_c, s_new = _hgrn_prompt(proj, *hargs, l, batch, seq, hgrn_heads, 512)
            else:
                o_c, s_new = _hgrn_step(proj, *hargs, states["hgrn"], o_idx, l, hgrn_heads, 32)
            new["hgrn"].append(s_new)
            x = _matmul_res(o_c, wts["c_w_out"], o_idx, x, TM, TN)
        if prompt:
            x = _mem_attn_prompt(x, wts["norm_mem"][l], wts["mem_w_q"], mem_k, mem_v, wts["mem_w_o"], l, seq, 256)
        else:
            q = _norm_matmul(x, wts["norm_mem"][l], wts["mem_w_q"], l, 0, wts["mem_w_q"].shape[2], TM, TN)
            o = _mem_attn_sample(q, mem_k, mem_v, l, 8)
            x = _matmul_res(o, wts["mem_w_o"], l, x, TM, TN)
        x = _ffn(x, wts["norm_ffn"][l], wts["ffn_w_up"], wts["ffn_w_down"], l, wts["norm_final"],
                 l == depth - 1, TM, TF)
    return x, {n: jnp.stack(v) for n, v in new.items()}


def kernel(x_prompt, x_sample, cache_mem_k, cache_mem_v, state_gdn_conv, state_gdn, state_lru_conv, state_lru, state_hgrn, mem_prompt, norm_mix, norm_mem, norm_mem_kv, norm_ffn, norm_final, ab_w_in, ab_w_out, gdn_conv_w, gdn_a_log, gdn_dt_bias, gdn_norm_w, lru_conv_w, lru_conv_b, lru_w_a, lru_b_a, lru_w_i, lru_b_i, lru_lam, c_w_in, c_w_out, hgrn_lb_raw, hgrn_norm_w, mem_w_q, mem_w_k, mem_w_v, mem_w_o, ffn_w_up, ffn_w_down):
    bp, lp, d = x_prompt.shape
    bs, ls, _ = x_sample.shape
    assert ls == 1, "the sample group advances one token per call"
    depth = norm_mix.shape[0]
    gdn_heads = gdn_a_log.shape[1]
    gdn_w = gdn_heads * LANES
    mem_tokens, mem_heads, mem_hd = cache_mem_k.shape[2:]
    mem_w = mem_heads * mem_hd
    n_ba = 2 * gdn_heads
    assert n_ba <= LANES

    pad_ba = lambda a: jnp.pad(a, ((0, 0), (gdn_heads, LANES - n_ba)))[:, None, :]
    wts = dict(
        norm_mix=norm_mix, norm_mem=norm_mem, norm_ffn=norm_ffn, norm_final=norm_final,
        ab_head=ab_w_in[:, :, :4 * gdn_w + LANES].astype(BF16),
        ab_lru=ab_w_in[:, :, 4 * gdn_w + n_ba:].astype(BF16),
        ab_w_out=ab_w_out.astype(BF16),
        gdn_conv_w=gdn_conv_w, alog_pad=pad_ba(gdn_a_log), dtb_pad=pad_ba(gdn_dt_bias), gdn_norm_w=gdn_norm_w,
        lru_conv_w=lru_conv_w, lru_conv_b=lru_conv_b, lru_w_a=lru_w_a.astype(BF16), lru_w_i=lru_w_i.astype(BF16),
        lru_b_a=lru_b_a, lru_b_i=lru_b_i, lru_lam=lru_lam,
        c_w_in=c_w_in.astype(BF16), c_w_out=c_w_out.astype(BF16),
        hgrn_lb_raw=hgrn_lb_raw, hgrn_norm_w=hgrn_norm_w,
        mem_w_q=mem_w_q.astype(BF16), mem_w_o=mem_w_o.astype(BF16),
        ffn_w_up=ffn_w_up.astype(BF16), ffn_w_down=ffn_w_down.astype(BF16),
    )

    mem_rows = mem_prompt.reshape(bp * mem_tokens, d)
    w_kv = jnp.concatenate([mem_w_k, mem_w_v], axis=-1).astype(BF16)
    kv = jnp.stack([_norm_matmul(mem_rows, norm_mem_kv[l], w_kv, l, 0, 2 * mem_w, TM, TN) for l in range(depth)])
    p_mem_k = kv[:, :, :mem_w].reshape(depth, bp, mem_tokens, mem_w)
    p_mem_v = kv[:, :, mem_w:].reshape(depth, bp, mem_tokens, mem_w)

    y_p, new_p = _trunk(x_prompt.reshape(bp * lp, d), p_mem_k, p_mem_v, None, wts, bp, lp, True)
    states = dict(gdn_conv=state_gdn_conv, gdn=state_gdn, lru_conv=state_lru_conv, lru=state_lru,
                  hgrn=state_hgrn)
    mem4 = lambda t: t.reshape(depth, bs, mem_tokens, mem_w)
    y_s, new_s = _trunk(x_sample.reshape(bs * ls, d), mem4(cache_mem_k), mem4(cache_mem_v), states, wts,
                        bs, ls, False)

    order = ("gdn_conv", "gdn", "lru_conv", "lru", "hgrn")
    mem5 = lambda t: t.reshape(depth, bp, mem_tokens, mem_heads, mem_hd)
    return (y_p.reshape(bp, lp, d), y_s.reshape(bs, ls, d), mem5(p_mem_k), mem5(p_mem_v),
            *(new_p[n] for n in order), *(new_s[n] for n in order))
```

```python
import functools
import math

import jax
import jax.numpy as jnp
import numpy as np
from jax import lax
from jax.experimental import pallas as pl
from jax.experimental.pallas import tpu as pltpu

F32 = jnp.float32
BF16 = jnp.bfloat16
EPS = 1e-6
LANES = 128
CONV_WIDTH = 4
LRU_C = 8.0
VMEM_LIMIT_BYTES = 56 * 1024 * 1024
CHUNK = 128


def _params(sem):
    return pltpu.CompilerParams(dimension_semantics=sem, vmem_limit_bytes=VMEM_LIMIT_BYTES)


def _sigmoid(x):
    return jax.nn.sigmoid(x)


def _silu(x):
    return x * _sigmoid(x)


def _softplus(x):
    return jnp.maximum(x, 0.0) + jnp.log1p(jnp.exp(-jnp.abs(x)))


def _neg_expm1(x):
    t = jnp.tanh(0.5 * x)
    return -2.0 * t / (1.0 - t)


def _gelu_tanh(x):
    c = math.sqrt(2.0 / math.pi)
    return 0.5 * x * (1.0 + jnp.tanh(c * (x + 0.044715 * (x * x * x))))


def _rms(x, w):
    return x * lax.rsqrt(jnp.mean(x * x, axis=-1, keepdims=True) + EPS) * w


def _l2n(x):
    return x * lax.rsqrt(jnp.sum(x * x, axis=-1, keepdims=True) + EPS)


def _dot(a, b):
    return jnp.dot(a.astype(BF16), b.astype(BF16), preferred_element_type=F32)


def _dot_nt(a, b):
    return lax.dot_general(a.astype(BF16), b.astype(BF16), (((1,), (1,)), ((), ())),
                           preferred_element_type=F32)


def _dot_tn(a, b):
    return lax.dot_general(a.astype(BF16), b.astype(BF16), (((0,), (0,)), ((), ())),
                           preferred_element_type=F32)


def _split2(a):
    hi = a.astype(BF16)
    lo = (a - hi.astype(F32)).astype(BF16)
    return hi, lo


def _dot_hp(a_parts, b_parts):
    ah, al = a_parts
    bh, bl = b_parts
    n = bh.shape[1]
    d = functools.partial(jnp.dot, preferred_element_type=F32)
    r = d(ah, jnp.concatenate([bh, bl], axis=1))
    return (r[:, :n] + r[:, n:]) + d(al, bh)


def _dot_exact_lhs(t, b):
    b1 = b.astype(BF16)
    r1 = b - b1.astype(F32)
    b2 = r1.astype(BF16)
    b3 = (r1 - b2.astype(F32)).astype(BF16)
    d = functools.partial(jnp.dot, preferred_element_type=F32)
    return (d(t, b1) + d(t, b2)) + d(t, b3)


def _iota2(n, axis):
    return lax.broadcasted_iota(jnp.int32, (n, n), axis)


def _col_from_row(row):
    n = row.shape[-1]
    return jnp.broadcast_to(row, (n, n)).T


def _norm_matmul_kernel(x_ref, g_ref, w_ref, o_ref, xn_ref):
    @pl.when(pl.program_id(1) == 0)
    def _():
        xn_ref[...] = _rms(x_ref[...], g_ref[...]).astype(BF16)

    o_ref[...] = jnp.dot(xn_ref[...], w_ref[...], preferred_element_type=F32)


def _norm_matmul(x, g, w, layer, n0, n, tm, tn):
    m, k = x.shape
    tm = min(tm, m)
    tn = min(tn, n)
    j0 = n0 // tn
    return pl.pallas_call(
        _norm_matmul_kernel,
        grid=(m // tm, n // tn),
        in_specs=[pl.BlockSpec((tm, k), lambda i, j: (i, 0)),
                  pl.BlockSpec((1, k), lambda i, j: (0, 0)),
                  pl.BlockSpec((None, k, tn), lambda i, j: (layer, 0, j0 + j))],
        out_specs=pl.BlockSpec((tm, tn), lambda i, j: (i, j)),
        out_shape=jax.ShapeDtypeStruct((m, n), F32),
        scratch_shapes=[pltpu.VMEM((tm, k), BF16)],
        compiler_params=_params(("arbitrary", "arbitrary")),
    )(x, g.reshape(1, k), w)


def _matmul2_res_kernel(a1_ref, a2_ref, w1_ref, w2_ref, r_ref, o_ref):
    acc = jnp.dot(a1_ref[...], w1_ref[...], preferred_element_type=F32)
    acc = acc + jnp.dot(a2_ref[...], w2_ref[...], preferred_element_type=F32)
    o_ref[...] = r_ref[...] + acc


def _matmul2_res(a1, a2, w, layer, res, tm, tn):
    m, kh = a1.shape
    n = w.shape[2]
    tm = min(tm, m)
    tn = min(tn, n)
    return pl.pallas_call(
        _matmul2_res_kernel,
        grid=(m // tm, n // tn),
        in_specs=[pl.BlockSpec((tm, kh), lambda i, j: (i, 0)),
                  pl.BlockSpec((tm, kh), lambda i, j: (i, 0)),
                  pl.BlockSpec((None, kh, tn), lambda i, j: (layer, 0, j)),
                  pl.BlockSpec((None, kh, tn), lambda i, j: (layer, 1, j)),
                  pl.BlockSpec((tm, tn), lambda i, j: (i, j))],
        out_specs=pl.BlockSpec((tm, tn), lambda i, j: (i, j)),
        out_shape=jax.ShapeDtypeStruct((m, n), F32),
        compiler_params=_params(("arbitrary", "arbitrary")),
    )(a1, a2, w, w, res)


def _matmul_res_kernel(a_ref, w_ref, r_ref, o_ref):
    o_ref[...] = r_ref[...] + jnp.dot(a_ref[...], w_ref[...], preferred_element_type=F32)


def _matmul_res(a, w, layer, res, tm, tn):
    m, k = a.shape
    n = w.shape[2]
    tm = min(tm, m)
    tn = min(tn, n)
    return pl.pallas_call(
        _matmul_res_kernel,
        grid=(m // tm, n // tn),
        in_specs=[pl.BlockSpec((tm, k), lambda i, j: (i, 0)),
                  pl.BlockSpec((None, k, tn), lambda i, j: (layer, 0, j)),
                  pl.BlockSpec((tm, tn), lambda i, j: (i, j))],
        out_specs=pl.BlockSpec((tm, tn), lambda i, j: (i, j)),
        out_shape=jax.ShapeDtypeStruct((m, n), F32),
        compiler_params=_params(("arbitrary", "arbitrary")),
    )(a, w, res)


def _ffn_kernel(x_ref, g_ref, wu_ref, wd_ref, gf_ref, o_ref, xn_ref, *, final_norm):
    f = pl.program_id(1)

    @pl.when(f == 0)
    def _():
        x = x_ref[...]
        xn_ref[...] = _rms(x, g_ref[...]).astype(BF16)
        o_ref[...] = x

    h = jnp.dot(xn_ref[...], wu_ref[...], preferred_element_type=F32)
    h = jnp.square(jnp.maximum(h, 0.0)).astype(BF16)
    o_ref[...] += jnp.dot(h, wd_ref[...], preferred_element_type=F32)

    if final_norm:
        @pl.when(f == pl.num_programs(1) - 1)
        def _():
            o_ref[...] = _rms(o_ref[...], gf_ref[...])


def _ffn(x, g, w_up, w_down, layer, g_final, final_norm, tm, tf):
    m, d = x.shape
    dff = w_up.shape[2]
    tm = min(tm, m)
    return pl.pallas_call(
        functools.partial(_ffn_kernel, final_norm=final_norm),
        grid=(m // tm, dff // tf),
        in_specs=[pl.BlockSpec((tm, d), lambda i, f: (i, 0), pipeline_mode=pl.Buffered(1)),
                  pl.BlockSpec((1, d), lambda i, f: (0, 0)),
                  pl.BlockSpec((None, d, tf), lambda i, f: (layer, 0, f)),
                  pl.BlockSpec((None, tf, d), lambda i, f: (layer, f, 0)),
                  pl.BlockSpec((1, d), lambda i, f: (0, 0))],
        out_specs=pl.BlockSpec((tm, d), lambda i, f: (i, 0)),
        out_shape=jax.ShapeDtypeStruct((m, d), F32),
        scratch_shapes=[pltpu.VMEM((tm, d), BF16)],
        compiler_params=_params(("arbitrary", "arbitrary")),
    )(x, g.reshape(1, d), w_up, w_down, g_final.reshape(1, d))


def _mem_attn_prompt_kernel(x_ref, g_ref, wq_ref, k_ref, v_ref, wo_ref, o_ref, *, heads):
    x = x_ref[...]
    xn = _rms(x, g_ref[...]).astype(BF16)
    q = jnp.dot(xn, wq_ref[...], preferred_element_type=F32)
    k = k_ref[0].astype(BF16)
    v = v_ref[0].astype(BF16)
    scale = LANES ** -0.5
    outs = []
    for h in range(heads):
        sl = slice(h * LANES, (h + 1) * LANES)
        s = _dot_nt(q[:, sl], k[:, sl]) * scale
        e = jnp.exp(s - jnp.max(s, axis=-1, keepdims=True))
        p = e / jnp.sum(e, axis=-1, keepdims=True)
        outs.append(_dot(p, v[:, sl]))
    o = jnp.concatenate(outs, axis=-1).astype(BF16)
    o_ref[...] = x + jnp.dot(o, wo_ref[...], preferred_element_type=F32)


def _mem_attn_prompt(x, g, w_q, mem_k, mem_v, w_o, layer, seq, tl):
    m, d = x.shape
    _, _, t, w = mem_k.shape
    nblk = seq // tl
    return pl.pallas_call(
        functools.partial(_mem_attn_prompt_kernel, heads=w // LANES),
        grid=(m // tl,),
        in_specs=[pl.BlockSpec((tl, d), lambda i: (i, 0)),
                  pl.BlockSpec((1, d), lambda i: (0, 0)),
                  pl.BlockSpec((None, d, w), lambda i: (layer, 0, 0)),
                  pl.BlockSpec((None, 1, t, w), lambda i: (layer, i // nblk, 0, 0)),
                  pl.BlockSpec((None, 1, t, w), lambda i: (layer, i // nblk, 0, 0)),
                  pl.BlockSpec((None, w, d), lambda i: (layer, 0, 0))],
        out_specs=pl.BlockSpec((tl, d), lambda i: (i, 0)),
        out_shape=jax.ShapeDtypeStruct((m, d), F32),
        compiler_params=_params(("arbitrary",)),
    )(x, g.reshape(1, d), w_q, mem_k, mem_v, w_o)


def _mem_attn_sample_kernel(q_ref, k_ref, v_ref, o_ref, *, tb, heads):
    scale = LANES ** -0.5
    for b in range(tb):
        q = jnp.concatenate([q_ref[b:b + 1, h * LANES:(h + 1) * LANES] for h in range(heads)], axis=0)
        s = jnp.sum(k_ref[b] * (q * scale)[None], axis=-1, keepdims=True)
        e = jnp.exp(s - jnp.max(s, axis=0, keepdims=True))
        o_ref[b] = jnp.sum(e * v_ref[b], axis=0) / jnp.sum(e, axis=0)


def _mem_attn_sample(q, mem_k, mem_v, layer, tb):
    b = q.shape[0]
    _, _, t, heads, hd = mem_k.shape
    kv_blk = pl.BlockSpec((None, tb, t, heads, hd), lambda i: (layer, i, 0, 0, 0))
    return pl.pallas_call(
        functools.partial(_mem_attn_sample_kernel, tb=tb, heads=heads),
        grid=(b // tb,),
        in_specs=[pl.BlockSpec((tb, heads * hd), lambda i: (i, 0)), kv_blk, kv_blk],
        out_specs=pl.BlockSpec((tb, heads, hd), lambda i: (i, 0, 0)),
        out_shape=jax.ShapeDtypeStruct((b, heads, hd), F32),
        compiler_params=_params(("arbitrary",)),
    )(q, mem_k, mem_v)


def _conv_block(x_ref, buf, w):
    cb = x_ref.shape[0]
    buf[8:8 + cb, :] = x_ref[...]
    y = buf[5:5 + cb, :] * w[0:1]
    y = y + buf[6:6 + cb, :] * w[1:2]
    y = y + buf[7:7 + cb, :] * w[2:3]
    y = y + buf[8:8 + cb, :] * w[3:4]
    buf[0:8, :] = buf[cb:cb + 8, :]
    return y


def _gdn_gates(ba, alog, dtb, h, heads):
    lane = lax.broadcasted_iota(jnp.int32, ba.shape, 1)
    beta_all = _sigmoid(ba)
    g_all = -jnp.exp(alog) * _softplus(ba + dtb)
    beta = jnp.sum(jnp.where(lane == h, beta_all, 0.0), axis=1, keepdims=True)
    g = jnp.sum(jnp.where(lane == heads + h, g_all, 0.0), axis=1, keepdims=True)
    return beta, g


def _gdn_prompt_kernel(q_ref, k_ref, v_ref, z_ref, ba_ref, cwq_ref, cwk_ref, cwv_ref, alog_ref, dtb_ref,
                       nw_ref, o_ref, s_ref, qbuf, kbuf, vbuf, s_scr, *, heads):
    h = pl.program_id(1)
    blk = pl.program_id(2)
    cb = q_ref.shape[0]
    c = CHUNK

    @pl.when(blk == 0)
    def _():
        s_scr[...] = jnp.zeros_like(s_scr)
        qbuf[0:8, :] = jnp.zeros((8, LANES), F32)
        kbuf[0:8, :] = jnp.zeros((8, LANES), F32)
        vbuf[0:8, :] = jnp.zeros((8, LANES), F32)

    q = _l2n(_silu(_conv_block(q_ref, qbuf, cwq_ref[...]))) * (LANES ** -0.5)
    k = _l2n(_silu(_conv_block(k_ref, kbuf, cwk_ref[...])))
    v = _silu(_conv_block(v_ref, vbuf, cwv_ref[...]))
    beta, g = _gdn_gates(ba_ref[...], alog_ref[...], dtb_ref[...], h, heads)

    row = _iota2(c, 0)
    col = _iota2(c, 1)
    causal = col <= row
    strict = col < row
    tri = causal.astype(BF16)
    eye = (col == row).astype(F32)
    nw = nw_ref[...]

    chunks = [slice(ci * c, (ci + 1) * c) for ci in range(cb // c)]
    gcs = [_dot_exact_lhs(tri, jnp.broadcast_to(g[sl], (c, c))) for sl in chunks]
    decays = [jnp.exp(jnp.where(causal, gc - gc.T, -jnp.inf)) for gc in gcs]
    kbs = [k[sl] * beta[sl] for sl in chunks]
    ms = [jnp.where(strict, _dot_nt(kb, k[sl]) * dec, 0.0) for kb, sl, dec in zip(kbs, chunks, decays)]
    xs = [_split2(-m) for m in ms]
    ts = [eye - m for m in ms]
    for _ in range(int(math.log2(c)) - 1):
        xs = [_split2(_dot_hp(xp, xp)) for xp in xs]
        ts = [t + _dot_hp(_split2(t), xp) for t, xp in zip(ts, xs)]
    trans, adds, qeffs, outs0 = [], [], [], []
    for sl, gc, dec, kb, t in zip(chunks, gcs, decays, kbs, ts):
        eg = jnp.exp(gc)
        sol = _dot(t, jnp.concatenate([v[sl] * beta[sl], kb * eg], axis=1))
        u, w = sol[:, :LANES], sol[:, LANES:]
        attn = jnp.where(causal, _dot_nt(q[sl], k[sl]) * dec, 0.0)
        g_last = gc[c - 1:c, :]
        kd = k[sl] * jnp.exp(g_last - gc)
        trans.append(eye * jnp.exp(g_last) - _dot_tn(kd, w))
        adds.append(_dot_tn(kd, u))
        qeffs.append(q[sl] * eg - _dot(attn, w))
        outs0.append(_dot(attn, u))

    s = s_scr[...]
    outs = []
    for a, b, qe, o0 in zip(trans, adds, qeffs, outs0):
        outs.append(o0 + _dot(qe, s))
        s = _dot(a, s) + b
    s_scr[...] = s
    for sl, o in zip(chunks, outs):
        o_ref[sl, :] = (_rms(o, nw) * _silu(z_ref[sl, :])).astype(BF16)

    @pl.when(blk == pl.num_programs(2) - 1)
    def _():
        s_ref[0, 0] = s_scr[...]


def _gdn_prompt(proj, ba, conv_w, alog_pad, dtb_pad, norm_w, batch, seq, heads, cb):
    m = proj.shape[0]
    nblk = seq // cb
    rows = lambda b, h, c: b * nblk + c
    head_blk = lambda off: pl.BlockSpec((cb, LANES), lambda b, h, c: (rows(b, h, c), off + h))
    cw_blk = lambda off: pl.BlockSpec((CONV_WIDTH, LANES), lambda b, h, c: (0, off + h))
    row128 = pl.BlockSpec((1, LANES), lambda b, h, c: (0, 0))
    return pl.pallas_call(
        functools.partial(_gdn_prompt_kernel, heads=heads),
        grid=(batch, heads, nblk),
        in_specs=[head_blk(0), head_blk(heads), head_blk(2 * heads), head_blk(3 * heads),
                  pl.BlockSpec((cb, LANES), lambda b, h, c: (rows(b, h, c), 0)),
                  cw_blk(0), cw_blk(heads), cw_blk(2 * heads), row128, row128, row128],
        out_specs=[pl.BlockSpec((cb, LANES), lambda b, h, c: (rows(b, h, c), h)),
                   pl.BlockSpec((1, 1, LANES, LANES), lambda b, h, c: (b, h, 0, 0))],
        out_shape=[jax.ShapeDtypeStruct((m, heads * LANES), BF16),
                   jax.ShapeDtypeStruct((batch, heads, LANES, LANES), F32)],
        scratch_shapes=[pltpu.VMEM((cb + 8, LANES), F32)] * 3 + [pltpu.VMEM((LANES, LANES), F32)],
        compiler_params=_params(("arbitrary", "arbitrary", "arbitrary")),
    )(proj, proj, proj, proj, ba, conv_w, conv_w, conv_w, alog_pad, dtb_pad, norm_w.reshape(1, LANES))


def _gdn_step_kernel(q_ref, k_ref, v_ref, z_ref, ba_ref, cq_ref, ck_ref, cv_ref, cwq_ref, cwk_ref, cwv_ref,
                     alog_ref, dtb_ref, nw_ref, s_ref, o_ref, so_ref, *, heads):
    h = pl.program_id(1)
    tb = q_ref.shape[0]

    def conv(x_ref, c_ref, w_ref):
        w = w_ref[...]
        y = c_ref[0] * w[0:1]
        y = y + c_ref[1] * w[1:2]
        y = y + c_ref[2] * w[2:3]
        return y + x_ref[...] * w[3:4]

    q = _l2n(_silu(conv(q_ref, cq_ref, cwq_ref))) * (LANES ** -0.5)
    k = _l2n(_silu(conv(k_ref, ck_ref, cwk_ref)))
    v = _silu(conv(v_ref, cv_ref, cwv_ref))
    beta, g = _gdn_gates(ba_ref[...], alog_ref[...], dtb_ref[...], h, heads)
    eg = jnp.exp(g)
    outs = []
    group = 4
    for b0 in range(0, tb, group):
        rows = range(b0, min(b0 + group, tb))
        kcols = [_col_from_row(k[b:b + 1]) for b in rows]
        qcols = [_col_from_row(q[b:b + 1]) for b in rows]
        for b, kcol, qcol in zip(rows, kcols, qcols):
            r = slice(b, b + 1)
            s = s_ref[b, 0] * eg[r]
            v_new = beta[r] * (v[r] - jnp.sum(kcol * s, axis=0, keepdims=True))
            s = s + kcol * v_new
            so_ref[b, 0] = s
            outs.append(jnp.sum(qcol * s, axis=0, keepdims=True))
    o = jnp.concatenate(outs, axis=0)
    o_ref[...] = (_rms(o, nw_ref[...]) * _silu(z_ref[...])).astype(BF16)


def _gdn_step(proj, ba, conv_state_t, conv_w, alog_pad, dtb_pad, norm_w, state, layer, heads, tb):
    bsz = proj.shape[0]
    head_blk = lambda off: pl.BlockSpec((tb, LANES), lambda i, h: (i, off + h))
    cs_blk = lambda off: pl.BlockSpec((CONV_WIDTH - 1, tb, LANES), lambda i, h: (0, i, off + h))
    cw_blk = lambda off: pl.BlockSpec((CONV_WIDTH, LANES), lambda i, h: (0, off + h))
    row128 = pl.BlockSpec((1, LANES), lambda i, h: (0, 0))
    st_in = pl.BlockSpec((None, tb, 1, LANES, LANES), lambda i, h: (layer, i, h, 0, 0))
    st_out = pl.BlockSpec((tb, 1, LANES, LANES), lambda i, h: (i, h, 0, 0))
    return pl.pallas_call(
        functools.partial(_gdn_step_kernel, heads=heads),
        grid=(bsz // tb, heads),
        in_specs=[head_blk(0), head_blk(heads), head_blk(2 * heads), head_blk(3 * heads),
                  pl.BlockSpec((tb, LANES), lambda i, h: (i, 0)),
                  cs_blk(0), cs_blk(heads), cs_blk(2 * heads),
                  cw_blk(0), cw_blk(heads), cw_blk(2 * heads), row128, row128, row128, st_in],
        out_specs=[pl.BlockSpec((tb, LANES), lambda i, h: (i, h)), st_out],
        out_shape=[jax.ShapeDtypeStruct((bsz, heads * LANES), BF16),
                   jax.ShapeDtypeStruct(state.shape[1:], F32)],
        compiler_params=_params(("arbitrary", "arbitrary")),
    )(proj, proj, proj, proj, ba, conv_state_t, conv_state_t, conv_state_t, conv_w, conv_w, conv_w,
      alog_pad, dtb_pad, norm_w.reshape(1, LANES), state)


def _lru_gates(x, wa_ref, wi_ref, ba, bi, lam):
    nb = wa_ref.shape[0]
    ga, gi = [], []
    for s in range(nb):
        xs = x[:, s * LANES:(s + 1) * LANES].astype(BF16)
        ga.append(jnp.dot(xs, wa_ref[s], preferred_element_type=F32))
        gi.append(jnp.dot(xs, wi_ref[s], preferred_element_type=F32))
    gate_a = _sigmoid(jnp.concatenate(ga, axis=-1) + ba)
    gate_i = _sigmoid(jnp.concatenate(gi, axis=-1) + bi)
    log_a = -LRU_C * gate_a * _softplus(-lam)
    return log_a, gate_i


def _lru_prompt_kernel(xl_ref, yl_ref, cw_ref, cb_ref, wa_ref, wi_ref, ba_ref, bi_ref, lam_ref,
                       o_ref, hl_ref, xbuf, abuf, bbuf, h_scr):
    blk = pl.program_id(1)
    cb = xl_ref.shape[0]
    width = xl_ref.shape[1]

    @pl.when(blk == 0)
    def _():
        h_scr[...] = jnp.zeros_like(h_scr)
        xbuf[0:8, :] = jnp.zeros((8, width), F32)

    x = _conv_block(xl_ref, xbuf, cw_ref[...]) + cb_ref[...]
    log_a, gate_i = _lru_gates(x, wa_ref, wi_ref, ba_ref[...], bi_ref[...], lam_ref[...])
    mult = jnp.sqrt(_neg_expm1(2.0 * log_a))
    first = (lax.broadcasted_iota(jnp.int32, (cb, 1), 0) == 0) & (blk == 0)
    mult = jnp.where(first, 1.0, mult)
    abuf[...] = jnp.exp(log_a)
    bbuf[...] = mult * gate_i * x

    def step(t, h):
        h = abuf[pl.ds(t, 1), :] * h + bbuf[pl.ds(t, 1), :]
        bbuf[pl.ds(t, 1), :] = h
        return h

    h_last = lax.fori_loop(0, cb, step, h_scr[...], unroll=8)
    h_scr[...] = h_last
    o_ref[...] = (bbuf[...] * _gelu_tanh(yl_ref[...])).astype(BF16)

    @pl.when(blk == pl.num_programs(1) - 1)
    def _():
        hl_ref[0] = h_last


def _lru_prompt(xy, conv_w, conv_b, w_a, w_i, b_a, b_i, lam, batch, seq, cb):
    m = xy.shape[0]
    width = xy.shape[1] // 2
    nblk = seq // cb
    nb = w_a.shape[0]
    vec = pl.BlockSpec((1, width), lambda b, c: (0, 0))
    wblk = pl.BlockSpec((nb, LANES, LANES), lambda b, c: (0, 0, 0))
    out, h_last = pl.pallas_call(
        _lru_prompt_kernel,
        grid=(batch, nblk),
        in_specs=[pl.BlockSpec((cb, width), lambda b, c: (b * nblk + c, 0)),
                  pl.BlockSpec((cb, width), lambda b, c: (b * nblk + c, 1)),
                  pl.BlockSpec((CONV_WIDTH, width), lambda b, c: (0, 0)),
                  vec, wblk, wblk, vec, vec, vec],
        out_specs=[pl.BlockSpec((cb, width), lambda b, c: (b * nblk + c, 0)),
                   pl.BlockSpec((1, 1, width), lambda b, c: (b, 0, 0))],
        out_shape=[jax.ShapeDtypeStruct((m, width), BF16),
                   jax.ShapeDtypeStruct((batch, 1, width), F32)],
        scratch_shapes=[pltpu.VMEM((cb + 8, width), F32), pltpu.VMEM((cb, width), F32),
                        pltpu.VMEM((cb, width), F32), pltpu.VMEM((1, width), F32)],
        compiler_params=_params(("arbitrary", "arbitrary")),
    )(xy, xy, conv_w, conv_b.reshape(1, width), w_a, w_i, b_a.reshape(1, width), b_i.reshape(1, width),
      lam.reshape(1, width))
    return out, h_last.reshape(batch, width)


def _lru_step_kernel(xl_ref, yl_ref, cs_ref, h0_ref, cw_ref, cb_ref, wa_ref, wi_ref, ba_ref, bi_ref, lam_ref,
                     o_ref, h_ref, *, reset):
    w = cw_ref[...]
    x = cs_ref[0] * w[0:1]
    x = x + cs_ref[1] * w[1:2]
    x = x + cs_ref[2] * w[2:3]
    x = x + xl_ref[...] * w[3:4] + cb_ref[...]
    log_a, gate_i = _lru_gates(x, wa_ref, wi_ref, ba_ref[...], bi_ref[...], lam_ref[...])
    mult = 1.0 if reset else jnp.sqrt(_neg_expm1(2.0 * log_a))
    h = jnp.exp(log_a) * h0_ref[...] + mult * gate_i * x
    h_ref[...] = h
    o_ref[...] = (h * _gelu_tanh(yl_ref[...])).astype(BF16)


def _lru_step(xy, conv_state_t, h0, conv_w, conv_b, w_a, w_i, b_a, b_i, lam, reset):
    bsz = xy.shape[0]
    width = xy.shape[1] // 2
    nb = w_a.shape[0]
    vec = pl.BlockSpec((1, width), lambda i: (0, 0))
    wblk = pl.BlockSpec((nb, LANES, LANES), lambda i: (0, 0, 0))
    full = pl.BlockSpec((bsz, width), lambda i: (0, 0))
    return pl.pallas_call(
        functools.partial(_lru_step_kernel, reset=reset),
        grid=(1,),
        in_specs=[full, pl.BlockSpec((bsz, width), lambda i: (0, 1)),
                  pl.BlockSpec((CONV_WIDTH - 1, bsz, width), lambda i: (0, 0, 0)), full,
                  pl.BlockSpec((CONV_WIDTH, width), lambda i: (0, 0)), vec, wblk, wblk, vec, vec, vec],
        out_specs=[full, full],
        out_shape=[jax.ShapeDtypeStruct((bsz, width), BF16), jax.ShapeDtypeStruct((bsz, width), F32)],
        compiler_params=_params(("arbitrary",)),
    )(xy, xy, conv_state_t, h0, conv_w, conv_b.reshape(1, width), w_a, w_i, b_a.reshape(1, width),
      b_i.reshape(1, width), lam.reshape(1, width))


def _hgrn_lower_bound(lb_ref, layer):
    depth = lb_ref.shape[0]
    raw = [lb_ref[l, 0] for l in range(depth)]
    mx = raw[0]
    for r in raw[1:]:
        mx = jnp.maximum(mx, r)
    ex = [jnp.exp(r - mx) for r in raw]
    tot = ex[0]
    for e in ex[1:]:
        tot = tot + e
    wts = [e / tot for e in ex]
    cum = wts[0]
    for w in wts[1:layer + 1]:
        cum = cum + w
    return cum - wts[0]


def _hgrn_inputs(q_raw, f_raw, lb):
    q = _silu(q_raw) * (LANES ** -0.5)
    f = lb + (1.0 - lb) * _sigmoid(f_raw)
    return q, f


def _level_ref(g, half):
    c = g.shape[0]
    sub = 8
    if 2 * half >= sub:
        g3 = g.reshape(c // (2 * half), 2 * half, LANES)
        return jnp.broadcast_to(g3[:, half - 1:half, :], g3.shape).reshape(c, LANES)
    g3 = g.reshape(c // sub, sub, LANES)
    rin = lax.broadcasted_iota(jnp.int32, g3.shape, 1)
    out = jnp.broadcast_to(g3[:, sub - half - 1:sub - half, :], g3.shape)
    for start in range(sub - 4 * half, -1, -2 * half):
        out = jnp.where(rin < start + 2 * half, jnp.broadcast_to(g3[:, start + half - 1:start + half, :], g3.shape),
                        out)
    return out.reshape(c, LANES)


def _hgrn_level_table(c):
    i = np.arange(c)[:, None]
    j = np.arange(c)[None, :]
    nlev = int(math.log2(c))
    top_bit = sum(((i ^ j) >> b > 0).astype(np.int32) for b in range(1, nlev))
    return jnp.asarray(np.where(j < i, nlev - 1 - top_bit, np.where(j == i, nlev, -1)), jnp.int32)


def _hgrn_prompt_kernel(q_ref, f_ref, i_ref, gz_ref, lb_ref, nw_ref, lvl_ref, o_ref, s_ref, s_scr, *, layer):
    blk = pl.program_id(2)
    cb = q_ref.shape[0]
    c = CHUNK

    @pl.when(blk == 0)
    def _():
        s_scr[...] = jnp.zeros_like(s_scr)

    lb = _hgrn_lower_bound(lb_ref, layer)
    tri = (_iota2(c, 1) <= _iota2(c, 0)).astype(BF16)
    nw = nw_ref[...]
    lvl = lvl_ref[...]
    halves = [c >> (i + 1) for i in range(int(math.log2(c)))]

    chunks = [slice(ci * c, (ci + 1) * c) for ci in range(cb // c)]
    qf = [_hgrn_inputs(q_ref[sl, :], f_ref[sl, :], lb) for sl in chunks]
    qs = [q for q, _ in qf]
    ks = [1.0 - f for _, f in qf]
    vs = [i_ref[sl, :] for sl in chunks]
    gs = [_dot_exact_lhs(tri, jnp.log(f)) for _, f in qf]

    amats = [_dot_nt(q, k) for q, k in zip(qs, ks)]
    for li, half in enumerate(halves):
        es = [jnp.exp(-jnp.abs(g - _level_ref(g, half))) for g in gs]
        ps = [_dot_nt(q * e, k * e) for q, k, e in zip(qs, ks, es)]
        amats = [jnp.where(lvl == li, p, a) for p, a in zip(ps, amats)]
    intra = [_dot(jnp.where(lvl >= 0, a, 0.0), v) for a, v in zip(amats, vs)]

    g_lasts = [g[c - 1:c, :] for g in gs]
    qdec = [q * jnp.exp(g) for q, g in zip(qs, gs)]
    sdec = [_col_from_row(jnp.exp(gl)) for gl in g_lasts]
    sadd = [_dot_tn(k * jnp.exp(gl - g), v) for k, gl, g, v in zip(ks, g_lasts, gs, vs)]

    s = s_scr[...]
    outs = []
    for o, qd, dec, add in zip(intra, qdec, sdec, sadd):
        outs.append(o + _dot(qd, s))
        s = s * dec + add
    s_scr[...] = s
    for sl, o in zip(chunks, outs):
        o_ref[sl, :] = (_rms(o, nw) * _silu(gz_ref[sl, :])).astype(BF16)

    @pl.when(blk == pl.num_programs(2) - 1)
    def _():
        s_ref[0, 0] = s_scr[...]


def _hgrn_prompt(proj, lb_raw, norm_w, layer, batch, seq, heads, cb):
    m = proj.shape[0]
    nblk = seq // cb
    depth = lb_raw.shape[0]
    head_blk = lambda off: pl.BlockSpec((cb, LANES), lambda b, h, c: (b * nblk + c, off + h))
    return pl.pallas_call(
        functools.partial(_hgrn_prompt_kernel, layer=layer),
        grid=(batch, heads, nblk),
        in_specs=[head_blk(0), head_blk(heads), head_blk(2 * heads), head_blk(3 * heads),
                  pl.BlockSpec((depth, 1, 1, LANES), lambda b, h, c: (0, h, 0, 0)),
                  pl.BlockSpec((1, LANES), lambda b, h, c: (0, 0)),
                  pl.BlockSpec((CHUNK, CHUNK), lambda b, h, c: (0, 0))],
        out_specs=[pl.BlockSpec((cb, LANES), lambda b, h, c: (b * nblk + c, h)),
                   pl.BlockSpec((1, 1, LANES, LANES), lambda b, h, c: (b, h, 0, 0))],
        out_shape=[jax.ShapeDtypeStruct((m, heads * LANES), BF16),
                   jax.ShapeDtypeStruct((batch, heads, LANES, LANES), F32)],
        scratch_shapes=[pltpu.VMEM((LANES, LANES), F32)],
        compiler_params=_params(("arbitrary", "arbitrary", "arbitrary")),
    )(proj, proj, proj, proj, lb_raw.reshape(depth, heads, 1, LANES), norm_w.reshape(1, LANES),
      _hgrn_level_table(CHUNK))


def _hgrn_step_kernel(q_ref, f_ref, i_ref, gz_ref, lb_ref, nw_ref, s_ref, o_ref, so_ref, *, layer):
    tb = q_ref.shape[0]
    lb = _hgrn_lower_bound(lb_ref, layer)
    q, f = _hgrn_inputs(q_ref[...], f_ref[...], lb)
    v = i_ref[...]
    outs = []
    group = 4
    for b0 in range(0, tb, group):
        rows = range(b0, min(b0 + group, tb))
        fcols = [_col_from_row(f[b:b + 1]) for b in rows]
        qcols = [_col_from_row(q[b:b + 1]) for b in rows]
        for b, fcol, qcol in zip(rows, fcols, qcols):
            s = fcol * (s_ref[b, 0] - v[b:b + 1]) + v[b:b + 1]
            so_ref[b, 0] = s
            outs.append(jnp.sum(qcol * s, axis=0, keepdims=True))
    o = jnp.concatenate(outs, axis=0)
    o_ref[...] = (_rms(o, nw_ref[...]) * _silu(gz_ref[...])).astype(BF16)


def _hgrn_step(proj, lb_raw, norm_w, state, state_idx, layer, heads, tb):
    bsz = proj.shape[0]
    depth = lb_raw.shape[0]
    head_blk = lambda off: pl.BlockSpec((tb, LANES), lambda i, h: (i, off + h))
    st_in = pl.BlockSpec((None, tb, 1, LANES, LANES), lambda i, h: (state_idx, i, h, 0, 0))
    st_out = pl.BlockSpec((tb, 1, LANES, LANES), lambda i, h: (i, h, 0, 0))
    return pl.pallas_call(
        functools.partial(_hgrn_step_kernel, layer=layer),
        grid=(bsz // tb, heads),
        in_specs=[head_blk(0), head_blk(heads), head_blk(2 * heads), head_blk(3 * heads),
                  pl.BlockSpec((depth, 1, 1, LANES), lambda i, h: (0, h, 0, 0)),
                  pl.BlockSpec((1, LANES), lambda i, h: (0, 0)), st_in],
        out_specs=[pl.BlockSpec((tb, LANES), lambda i, h: (i, h)), st_out],
        out_shape=[jax.ShapeDtypeStruct((bsz, heads * LANES), BF16),
                   jax.ShapeDtypeStruct(state.shape[1:], F32)],
        compiler_params=_params(("arbitrary", "arbitrary")),
    )(proj, proj, proj, proj, lb_raw.reshape(depth, heads, 1, LANES), norm_w.reshape(1, LANES), state)


TM = 512
TN = 1024
TF = 512


def _trunk(x, mem_k, mem_v, states, wts, batch, seq, prompt):
    depth = wts["norm_mix"].shape[0]
    gdn_w = wts["gdn_conv_w"].shape[-1] // 3
    gdn_heads = gdn_w // LANES
    hgrn_heads = wts["hgrn_lb_raw"].shape[1] // LANES
    lru_w = wts["lru_conv_w"].shape[-1]
    new = {"gdn_conv": [], "gdn": [], "lru_conv": [], "lru": [], "hgrn": []}
    for l in range(depth):
        g_mix = wts["norm_mix"][l]
        if l % 2 == 0:
            e = l // 2
            proj = _norm_matmul(x, g_mix, wts["ab_head"], e, 0, 4 * gdn_w, TM, TN)
            ba = _norm_matmul(x, g_mix, wts["ab_head"], e, 4 * gdn_w, LANES, TM, LANES)
            xy = _norm_matmul(x, g_mix, wts["ab_lru"], e, 0, 2 * lru_w, TM, TN)
            gargs = (wts["gdn_conv_w"][e], wts["alog_pad"][e], wts["dtb_pad"][e], wts["gdn_norm_w"][e])
            largs = (wts["lru_conv_w"][e], wts["lru_conv_b"][e], wts["lru_w_a"][e], wts["lru_w_i"][e],
                     wts["lru_b_a"][e], wts["lru_b_i"][e], wts["lru_lam"][e])
            if prompt:
                o_a, s_new = _gdn_prompt(proj, ba, *gargs, batch, seq, gdn_heads, 512)
                o_b, h_new = _lru_prompt(xy, *largs, batch, seq, 256)
                tail = lambda t, w: t.reshape(batch, seq, -1)[:, seq - (CONV_WIDTH - 1):, :w]
                new["gdn_conv"].append(tail(proj, 3 * gdn_w))
                new["lru_conv"].append(tail(xy, lru_w))
            else:
                gc_state, lc_state = states["gdn_conv"][e], states["lru_conv"][e]
                o_a, s_new = _gdn_step(proj, ba, jnp.swapaxes(gc_state, 0, 1), *gargs, states["gdn"], e,
                                       gdn_heads, 32)
                o_b, h_new = _lru_step(xy, jnp.swapaxes(lc_state, 0, 1), states["lru"][e], *largs, reset=False)
                new["gdn_conv"].append(jnp.concatenate([gc_state[:, 1:], proj[:, None, :3 * gdn_w]], axis=1))
                new["lru_conv"].append(jnp.concatenate([lc_state[:, 1:], xy[:, None, :lru_w]], axis=1))
            new["gdn"].append(s_new)
            new["lru"].append(h_new)
            x = _matmul2_res(o_a, o_b, wts["ab_w_out"], e, x, TM, TN)
        else:
            o_idx = l // 2
            proj = _norm_matmul(x, g_mix, wts["c_w_in"], o_idx, 0, wts["c_w_in"].shape[2], TM, TN)
            hargs = (wts["hgrn_lb_raw"], wts["hgrn_norm_w"][o_idx])
            if prompt:
                o_c, s_new = _hgrn_prompt(proj, *hargs, l, batch, seq, hgrn_heads, 512)
            else:
                o_c, s_new = _hgrn_step(proj, *hargs, states["hgrn"], o_idx, l, hgrn_heads, 32)
            new["hgrn"].append(s_new)
            x = _matmul_res(o_c, wts["c_w_out"], o_idx, x, TM, TN)
        if prompt:
            x = _mem_attn_prompt(x, wts["norm_mem"][l], wts["mem_w_q"], mem_k, mem_v, wts["mem_w_o"], l, seq, 256)
        else:
            q = _norm_matmul(x, wts["norm_mem"][l], wts["mem_w_q"], l, 0, wts["mem_w_q"].shape[2], TM, TN)
            o = _mem_attn_sample(q, mem_k, mem_v, l, 8)
            x = _matmul_res(o.reshape(batch, -1).astype(BF16), wts["mem_w_o"], l, x, TM, TN)
        x = _ffn(x, wts["norm_ffn"][l], wts["ffn_w_up"], wts["ffn_w_down"], l, wts["norm_final"],
                 l == depth - 1, TM, TF)
    return x, {n: jnp.stack(v) for n, v in new.items()}


def kernel(x_prompt, x_sample, cache_mem_k, cache_mem_v, state_gdn_conv, state_gdn, state_lru_conv, state_lru, state_hgrn, mem_prompt, norm_mix, norm_mem, norm_mem_kv, norm_ffn, norm_final, ab_w_in, ab_w_out, gdn_conv_w, gdn_a_log, gdn_dt_bias, gdn_norm_w, lru_conv_w, lru_conv_b, lru_w_a, lru_b_a, lru_w_i, lru_b_i, lru_lam, c_w_in, c_w_out, hgrn_lb_raw, hgrn_norm_w, mem_w_q, mem_w_k, mem_w_v, mem_w_o, ffn_w_up, ffn_w_down):
    bp, lp, d = x_prompt.shape
    bs, ls, _ = x_sample.shape
    assert ls == 1, "the sample group advances one token per call"
    depth = norm_mix.shape[0]
    gdn_heads = gdn_a_log.shape[1]
    gdn_w = gdn_heads * LANES
    mem_tokens, mem_heads, mem_hd = cache_mem_k.shape[2:]
    mem_w = mem_heads * mem_hd
    n_ba = 2 * gdn_heads
    assert n_ba <= LANES

    pad_ba = lambda a: jnp.pad(a, ((0, 0), (gdn_heads, LANES - n_ba)))[:, None, :]
    ab_bf = ab_w_in.astype(BF16)
    wts = dict(
        norm_mix=norm_mix, norm_mem=norm_mem, norm_ffn=norm_ffn, norm_final=norm_final,
        ab_head=ab_bf, ab_lru=ab_bf[:, :, 4 * gdn_w + n_ba:],
        ab_w_out=ab_w_out.astype(BF16),
        gdn_conv_w=gdn_conv_w, alog_pad=pad_ba(gdn_a_log), dtb_pad=pad_ba(gdn_dt_bias), gdn_norm_w=gdn_norm_w,
        lru_conv_w=lru_conv_w, lru_conv_b=lru_conv_b, lru_w_a=lru_w_a.astype(BF16), lru_w_i=lru_w_i.astype(BF16),
        lru_b_a=lru_b_a, lru_b_i=lru_b_i, lru_lam=lru_lam,
        c_w_in=c_w_in.astype(BF16), c_w_out=c_w_out.astype(BF16),
        hgrn_lb_raw=hgrn_lb_raw, hgrn_norm_w=hgrn_norm_w,
        mem_w_q=mem_w_q.astype(BF16), mem_w_o=mem_w_o.astype(BF16),
        ffn_w_up=ffn_w_up.astype(BF16), ffn_w_down=ffn_w_down.astype(BF16),
    )

    mem_rows = mem_prompt.reshape(bp * mem_tokens, d)
    w_kv = jnp.concatenate([mem_w_k, mem_w_v], axis=-1).astype(BF16)
    kv = jnp.stack([_norm_matmul(mem_rows, norm_mem_kv[l], w_kv, l, 0, 2 * mem_w, TM, TN) for l in range(depth)])
    p_mem_k = kv[:, :, :mem_w].reshape(depth, bp, mem_tokens, mem_w)
    p_mem_v = kv[:, :, mem_w:].reshape(depth, bp, mem_tokens, mem_w)

    y_p, new_p = _trunk(x_prompt.reshape(bp * lp, d), p_mem_k, p_mem_v, None, wts, bp, lp, True)
    states = dict(gdn_conv=state_gdn_conv, gdn=state_gdn, lru_conv=state_lru_conv, lru=state_lru,
                  hgrn=state_hgrn)
    y_s, new_s = _trunk(x_sample.reshape(bs * ls, d), cache_mem_k, cache_mem_v, states, wts, bs, ls, False)

    order = ("gdn_conv", "gdn", "lru_conv", "lru", "hgrn")
    mem5 = lambda t: t.reshape(depth, bp, mem_tokens, mem_heads, mem_hd)
    return (y_p.reshape(bp, lp, d), y_s.reshape(bs, ls, d), mem5(p_mem_k), mem5(p_mem_v),
            *(new_p[n] for n in order), *(new_s[n] for n in order))
```

```python
import functools
import math

import jax
import jax.numpy as jnp
import numpy as np
from jax import lax
from jax.experimental import pallas as pl
from jax.experimental.pallas import tpu as pltpu

F32 = jnp.float32
BF16 = jnp.bfloat16
EPS = 1e-6
LANES = 128
CONV_WIDTH = 4
LRU_C = 8.0
VMEM_LIMIT_BYTES = 56 * 1024 * 1024
CHUNK = 128
GDN_HEADS_PER_STEP = 2
HGRN_HEADS_PER_STEP = 2


def _params(sem):
    return pltpu.CompilerParams(dimension_semantics=sem, vmem_limit_bytes=VMEM_LIMIT_BYTES)


def _sigmoid(x):
    return jax.nn.sigmoid(x)


def _silu(x):
    return x * _sigmoid(x)


def _softplus(x):
    return jnp.maximum(x, 0.0) + jnp.log1p(jnp.exp(-jnp.abs(x)))


def _neg_expm1(x):
    t = jnp.tanh(0.5 * x)
    return -2.0 * t / (1.0 - t)


def _gelu_tanh(x):
    c = math.sqrt(2.0 / math.pi)
    return 0.5 * x * (1.0 + jnp.tanh(c * (x + 0.044715 * (x * x * x))))


def _rms(x, w):
    return x * lax.rsqrt(jnp.mean(x * x, axis=-1, keepdims=True) + EPS) * w


def _l2n(x):
    return x * lax.rsqrt(jnp.sum(x * x, axis=-1, keepdims=True) + EPS)


def _dot(a, b):
    return jnp.dot(a.astype(BF16), b.astype(BF16), preferred_element_type=F32)


def _dot_nt(a, b):
    return lax.dot_general(a.astype(BF16), b.astype(BF16), (((1,), (1,)), ((), ())),
                           preferred_element_type=F32)


def _dot_tn(a, b):
    return lax.dot_general(a.astype(BF16), b.astype(BF16), (((0,), (0,)), ((), ())),
                           preferred_element_type=F32)


def _split2(a):
    hi = a.astype(BF16)
    lo = (a - hi.astype(F32)).astype(BF16)
    return hi, lo


def _dot_hp(a_parts, b_parts):
    ah, al = a_parts
    bh, bl = b_parts
    return jnp.dot(jnp.concatenate([ah, ah, al], axis=1), jnp.concatenate([bh, bl, bh], axis=0),
                   preferred_element_type=F32)


def _dot_exact_lhs(t, b):
    b1 = b.astype(BF16)
    r1 = b - b1.astype(F32)
    b2 = r1.astype(BF16)
    b3 = (r1 - b2.astype(F32)).astype(BF16)
    return jnp.dot(jnp.concatenate([t, t, t], axis=1), jnp.concatenate([b1, b2, b3], axis=0),
                   preferred_element_type=F32)


def _iota2(n, axis):
    return lax.broadcasted_iota(jnp.int32, (n, n), axis)


def _col_from_row(row):
    n = row.shape[-1]
    return jnp.broadcast_to(row, (n, n)).T


def _norm_matmul_kernel(x_ref, g_ref, w_ref, o_ref, xn_ref):
    @pl.when(pl.program_id(1) == 0)
    def _():
        xn_ref[...] = _rms(x_ref[...], g_ref[...]).astype(BF16)

    o_ref[...] = jnp.dot(xn_ref[...], w_ref[...], preferred_element_type=F32)


def _norm_matmul(x, g, w, layer, n0, n, tm, tn):
    m, k = x.shape
    tm = min(tm, m)
    tn = min(tn, n)
    j0 = n0 // tn
    return pl.pallas_call(
        _norm_matmul_kernel,
        grid=(m // tm, n // tn),
        in_specs=[pl.BlockSpec((tm, k), lambda i, j: (i, 0)),
                  pl.BlockSpec((1, k), lambda i, j: (0, 0)),
                  pl.BlockSpec((None, k, tn), lambda i, j: (layer, 0, j0 + j))],
        out_specs=pl.BlockSpec((tm, tn), lambda i, j: (i, j)),
        out_shape=jax.ShapeDtypeStruct((m, n), F32),
        scratch_shapes=[pltpu.VMEM((tm, k), BF16)],
        compiler_params=_params(("arbitrary", "arbitrary")),
    )(x, g.reshape(1, k), w)


def _matmul2_res_kernel(a1_ref, a2_ref, w1_ref, w2_ref, r_ref, o_ref):
    acc = jnp.dot(a1_ref[...], w1_ref[...], preferred_element_type=F32)
    acc = acc + jnp.dot(a2_ref[...], w2_ref[...], preferred_element_type=F32)
    o_ref[...] = r_ref[...] + acc


def _matmul2_res(a1, a2, w, layer, res, tm, tn):
    m, kh = a1.shape
    n = w.shape[2]
    tm = min(tm, m)
    tn = min(tn, n)
    return pl.pallas_call(
        _matmul2_res_kernel,
        grid=(m // tm, n // tn),
        in_specs=[pl.BlockSpec((tm, kh), lambda i, j: (i, 0)),
                  pl.BlockSpec((tm, kh), lambda i, j: (i, 0)),
                  pl.BlockSpec((None, kh, tn), lambda i, j: (layer, 0, j)),
                  pl.BlockSpec((None, kh, tn), lambda i, j: (layer, 1, j)),
                  pl.BlockSpec((tm, tn), lambda i, j: (i, j))],
        out_specs=pl.BlockSpec((tm, tn), lambda i, j: (i, j)),
        out_shape=jax.ShapeDtypeStruct((m, n), F32),
        compiler_params=_params(("arbitrary", "arbitrary")),
    )(a1, a2, w, w, res)


def _matmul_res_kernel(a_ref, w_ref, r_ref, o_ref):
    o_ref[...] = r_ref[...] + jnp.dot(a_ref[...], w_ref[...], preferred_element_type=F32)


def _matmul_res(a, w, layer, res, tm, tn):
    m, k = a.shape
    n = w.shape[2]
    tm = min(tm, m)
    tn = min(tn, n)
    return pl.pallas_call(
        _matmul_res_kernel,
        grid=(m // tm, n // tn),
        in_specs=[pl.BlockSpec((tm, k), lambda i, j: (i, 0)),
                  pl.BlockSpec((None, k, tn), lambda i, j: (layer, 0, j)),
                  pl.BlockSpec((tm, tn), lambda i, j: (i, j))],
        out_specs=pl.BlockSpec((tm, tn), lambda i, j: (i, j)),
        out_shape=jax.ShapeDtypeStruct((m, n), F32),
        compiler_params=_params(("arbitrary", "arbitrary")),
    )(a, w, res)


def _ffn_kernel(x_ref, g_ref, wu_ref, wd_ref, gf_ref, o_ref, xn_ref, *, final_norm):
    f = pl.program_id(1)

    @pl.when(f == 0)
    def _():
        x = x_ref[...]
        xn_ref[...] = _rms(x, g_ref[...]).astype(BF16)
        o_ref[...] = x

    h = jnp.dot(xn_ref[...], wu_ref[...], preferred_element_type=F32)
    h = jnp.square(jnp.maximum(h, 0.0)).astype(BF16)
    o_ref[...] += jnp.dot(h, wd_ref[...], preferred_element_type=F32)

    if final_norm:
        @pl.when(f == pl.num_programs(1) - 1)
        def _():
            o_ref[...] = _rms(o_ref[...], gf_ref[...])


def _ffn(x, g, w_up, w_down, layer, g_final, final_norm, tm, tf):
    m, d = x.shape
    dff = w_up.shape[2]
    tm = min(tm, m)
    return pl.pallas_call(
        functools.partial(_ffn_kernel, final_norm=final_norm),
        grid=(m // tm, dff // tf),
        in_specs=[pl.BlockSpec((tm, d), lambda i, f: (i, 0), pipeline_mode=pl.Buffered(1)),
                  pl.BlockSpec((1, d), lambda i, f: (0, 0)),
                  pl.BlockSpec((None, d, tf), lambda i, f: (layer, 0, f)),
                  pl.BlockSpec((None, tf, d), lambda i, f: (layer, f, 0)),
                  pl.BlockSpec((1, d), lambda i, f: (0, 0))],
        out_specs=pl.BlockSpec((tm, d), lambda i, f: (i, 0)),
        out_shape=jax.ShapeDtypeStruct((m, d), F32),
        scratch_shapes=[pltpu.VMEM((tm, d), BF16)],
        compiler_params=_params(("arbitrary", "arbitrary")),
    )(x, g.reshape(1, d), w_up, w_down, g_final.reshape(1, d))


def _mem_attn_prompt_kernel(x_ref, g_ref, wq_ref, k_ref, v_ref, wo_ref, o_ref, *, heads):
    x = x_ref[...]
    xn = _rms(x, g_ref[...]).astype(BF16)
    q = jnp.dot(xn, wq_ref[...], preferred_element_type=F32)
    k = k_ref[0].astype(BF16)
    v = v_ref[0].astype(BF16)
    scale = LANES ** -0.5
    outs = []
    for h in range(heads):
        sl = slice(h * LANES, (h + 1) * LANES)
        s = _dot_nt(q[:, sl], k[:, sl]) * scale
        e = jnp.exp(s - jnp.max(s, axis=-1, keepdims=True))
        p = e / jnp.sum(e, axis=-1, keepdims=True)
        outs.append(_dot(p, v[:, sl]))
    o = jnp.concatenate(outs, axis=-1).astype(BF16)
    o_ref[...] = x + jnp.dot(o, wo_ref[...], preferred_element_type=F32)


def _mem_attn_prompt(x, g, w_q, mem_k, mem_v, w_o, layer, seq, tl):
    m, d = x.shape
    _, _, t, w = mem_k.shape
    nblk = seq // tl
    return pl.pallas_call(
        functools.partial(_mem_attn_prompt_kernel, heads=w // LANES),
        grid=(m // tl,),
        in_specs=[pl.BlockSpec((tl, d), lambda i: (i, 0)),
                  pl.BlockSpec((1, d), lambda i: (0, 0)),
                  pl.BlockSpec((None, d, w), lambda i: (layer, 0, 0)),
                  pl.BlockSpec((None, 1, t, w), lambda i: (layer, i // nblk, 0, 0)),
                  pl.BlockSpec((None, 1, t, w), lambda i: (layer, i // nblk, 0, 0)),
                  pl.BlockSpec((None, w, d), lambda i: (layer, 0, 0))],
        out_specs=pl.BlockSpec((tl, d), lambda i: (i, 0)),
        out_shape=jax.ShapeDtypeStruct((m, d), F32),
        compiler_params=_params(("arbitrary",)),
    )(x, g.reshape(1, d), w_q, mem_k, mem_v, w_o)


def _mem_attn_sample_kernel(q_ref, k_ref, v_ref, o_ref, *, tb, heads):
    scale = LANES ** -0.5
    for b in range(tb):
        q = jnp.concatenate([q_ref[b:b + 1, h * LANES:(h + 1) * LANES] for h in range(heads)], axis=0)
        s = jnp.sum(k_ref[b] * (q * scale)[None], axis=-1, keepdims=True)
        e = jnp.exp(s - jnp.max(s, axis=0, keepdims=True))
        o_ref[b] = jnp.sum(e * v_ref[b], axis=0) / jnp.sum(e, axis=0)


def _mem_attn_sample(q, mem_k, mem_v, layer, tb):
    b = q.shape[0]
    _, _, t, heads, hd = mem_k.shape
    kv_blk = pl.BlockSpec((None, tb, t, heads, hd), lambda i: (layer, i, 0, 0, 0))
    return pl.pallas_call(
        functools.partial(_mem_attn_sample_kernel, tb=tb, heads=heads),
        grid=(b // tb,),
        in_specs=[pl.BlockSpec((tb, heads * hd), lambda i: (i, 0)), kv_blk, kv_blk],
        out_specs=pl.BlockSpec((tb, heads, hd), lambda i: (i, 0, 0)),
        out_shape=jax.ShapeDtypeStruct((b, heads, hd), F32),
        compiler_params=_params(("arbitrary",)),
    )(q, mem_k, mem_v)


def _conv_block(x_ref, buf, w):
    cb = x_ref.shape[0]
    buf[8:8 + cb, :] = x_ref[...]
    y = buf[5:5 + cb, :] * w[0:1]
    y = y + buf[6:6 + cb, :] * w[1:2]
    y = y + buf[7:7 + cb, :] * w[2:3]
    y = y + buf[8:8 + cb, :] * w[3:4]
    buf[0:8, :] = buf[cb:cb + 8, :]
    return y


def _gdn_gates(ba, alog, dtb, h, heads):
    lane = lax.broadcasted_iota(jnp.int32, ba.shape, 1)
    beta_all = _sigmoid(ba)
    g_all = -jnp.exp(alog) * _softplus(ba + dtb)
    beta = jnp.sum(jnp.where(lane == h, beta_all, 0.0), axis=1, keepdims=True)
    g = jnp.sum(jnp.where(lane == heads + h, g_all, 0.0), axis=1, keepdims=True)
    return beta, g


def _gdn_prompt_kernel(q_ref, k_ref, v_ref, z_ref, ba_ref, cwq_ref, cwk_ref, cwv_ref, alog_ref, dtb_ref,
                       nw_ref, o_ref, s_ref, qbuf, kbuf, vbuf, s_scr, *, heads):
    hp = s_scr.shape[0]
    blk = pl.program_id(2)
    cb = q_ref.shape[0]
    c = CHUNK
    n_chunks = cb // c

    @pl.when(blk == 0)
    def _():
        s_scr[...] = jnp.zeros_like(s_scr)
        for buf in (qbuf, kbuf, vbuf):
            buf[0:8, :] = jnp.zeros((8, hp * LANES), F32)

    for x_ref, buf in ((q_ref, qbuf), (k_ref, kbuf), (v_ref, vbuf)):
        buf[8:8 + cb, :] = x_ref[...]
    conv_w = (cwq_ref[...], cwk_ref[...], cwv_ref[...])
    ba = ba_ref[...]
    gates = [_gdn_gates(ba, alog_ref[...], dtb_ref[...], pl.program_id(1) * hp + j, heads) for j in range(hp)]
    head_lanes = [slice(j * LANES, (j + 1) * LANES) for j in range(hp)]

    row = _iota2(c, 0)
    col = _iota2(c, 1)
    causal = col <= row
    strict = col < row
    tri = causal.astype(BF16)
    eye = (col == row).astype(F32)
    nw = nw_ref[...]

    def conv_rows(buf, w, r0):
        y = buf[r0 + 5:r0 + 5 + c, :] * w[0:1]
        y = y + buf[r0 + 6:r0 + 6 + c, :] * w[1:2]
        y = y + buf[r0 + 7:r0 + 7 + c, :] * w[2:3]
        return y + buf[r0 + 8:r0 + 8 + c, :] * w[3:4]

    conv = [[_silu(conv_rows(buf, w, ci * c)) for buf, w in zip((qbuf, kbuf, vbuf), conv_w)]
            for ci in range(n_chunks)]
    items = [(ci, j) for ci in range(n_chunks) for j in range(hp)]
    rows = lambda ci: slice(ci * c, (ci + 1) * c)
    qs = [_l2n(conv[ci][0][:, head_lanes[j]]) * (LANES ** -0.5) for ci, j in items]
    ks = [_l2n(conv[ci][1][:, head_lanes[j]]) for ci, j in items]
    vs = [conv[ci][2][:, head_lanes[j]] for ci, j in items]
    betas = [gates[j][0][rows(ci)] for ci, j in items]
    gcs = [_dot_exact_lhs(tri, jnp.broadcast_to(gates[j][1][rows(ci)], (c, c))) for ci, j in items]
    decays = [jnp.exp(jnp.where(causal, gc - gc.T, -jnp.inf)) for gc in gcs]
    kbs = [k * b for k, b in zip(ks, betas)]
    ms = [jnp.where(strict, _dot_nt(kb, k) * dec, 0.0) for kb, k, dec in zip(kbs, ks, decays)]
    xs = [_split2(-m) for m in ms]
    ts = [eye - m for m in ms]
    for _ in range(int(math.log2(c)) - 1):
        xs = [_split2(_dot_hp(xp, xp)) for xp in xs]
        ts = [t + _dot_hp(_split2(t), xp) for t, xp in zip(ts, xs)]
    egs = [jnp.exp(gc) for gc in gcs]
    uws = [_dot(t, jnp.concatenate([v * b, kb * eg], axis=1))
           for t, v, b, kb, eg in zip(ts, vs, betas, kbs, egs)]
    attns = [jnp.where(causal, _dot_nt(q, k) * dec, 0.0) for q, k, dec in zip(qs, ks, decays)]
    g_lasts = [gc[c - 1:c, :] for gc in gcs]
    kd_uws = [_dot_tn(k * jnp.exp(gl - gc), uw) for k, gl, gc, uw in zip(ks, g_lasts, gcs, uws)]
    at_uws = [_dot(attn, uw) for attn, uw in zip(attns, uws)]
    lhs = [jnp.concatenate([q * eg - at[:, LANES:], eye * jnp.exp(gl) - kd[:, LANES:]], axis=0)
           for q, eg, at, gl, kd in zip(qs, egs, at_uws, g_lasts, kd_uws)]

    states = [s_scr[j] for j in range(hp)]
    outs = []
    for it, (ci, j) in enumerate(items):
        both = _dot(lhs[it], states[j])
        outs.append(at_uws[it][:, :LANES] + both[:c])
        states[j] = both[c:] + kd_uws[it][:, :LANES]
    for j in range(hp):
        s_scr[j] = states[j]
    for buf in (qbuf, kbuf, vbuf):
        buf[0:8, :] = buf[cb:cb + 8, :]
    for (ci, j), o in zip(items, outs):
        o_ref[rows(ci), head_lanes[j]] = (_rms(o, nw) * _silu(z_ref[rows(ci), head_lanes[j]])).astype(BF16)

    @pl.when(blk == pl.num_programs(2) - 1)
    def _():
        s_ref[0] = s_scr[...]


def _gdn_prompt(proj, ba, conv_w, alog_pad, dtb_pad, norm_w, batch, seq, heads, cb):
    m = proj.shape[0]
    nblk = seq // cb
    hp = GDN_HEADS_PER_STEP
    assert heads % hp == 0
    groups = heads // hp
    rows = lambda b, h, c: b * nblk + c
    head_blk = lambda part: pl.BlockSpec((cb, hp * LANES), lambda b, h, c: (rows(b, h, c), part * groups + h))
    cw_blk = lambda part: pl.BlockSpec((CONV_WIDTH, hp * LANES), lambda b, h, c: (0, part * groups + h))
    row128 = pl.BlockSpec((1, LANES), lambda b, h, c: (0, 0))
    return pl.pallas_call(
        functools.partial(_gdn_prompt_kernel, heads=heads),
        grid=(batch, groups, nblk),
        in_specs=[head_blk(0), head_blk(1), head_blk(2), head_blk(3),
                  pl.BlockSpec((cb, LANES), lambda b, h, c: (rows(b, h, c), 0)),
                  cw_blk(0), cw_blk(1), cw_blk(2), row128, row128, row128],
        out_specs=[pl.BlockSpec((cb, hp * LANES), lambda b, h, c: (rows(b, h, c), h)),
                   pl.BlockSpec((1, hp, LANES, LANES), lambda b, h, c: (b, h, 0, 0))],
        out_shape=[jax.ShapeDtypeStruct((m, heads * LANES), BF16),
                   jax.ShapeDtypeStruct((batch, heads, LANES, LANES), F32)],
        scratch_shapes=[pltpu.VMEM((cb + 8, hp * LANES), F32)] * 3 + [pltpu.VMEM((hp, LANES, LANES), F32)],
        compiler_params=_params(("arbitrary", "arbitrary", "arbitrary")),
    )(proj, proj, proj, proj, ba, conv_w, conv_w, conv_w, alog_pad, dtb_pad, norm_w.reshape(1, LANES))


def _gdn_step_kernel(q_ref, k_ref, v_ref, z_ref, ba_ref, cq_ref, ck_ref, cv_ref, cwq_ref, cwk_ref, cwv_ref,
                     alog_ref, dtb_ref, nw_ref, s_ref, o_ref, so_ref, *, heads):
    h = pl.program_id(1)
    tb = q_ref.shape[0]

    def conv(x_ref, c_ref, w_ref):
        w = w_ref[...]
        y = c_ref[0] * w[0:1]
        y = y + c_ref[1] * w[1:2]
        y = y + c_ref[2] * w[2:3]
        return y + x_ref[...] * w[3:4]

    q = _l2n(_silu(conv(q_ref, cq_ref, cwq_ref))) * (LANES ** -0.5)
    k = _l2n(_silu(conv(k_ref, ck_ref, cwk_ref)))
    v = _silu(conv(v_ref, cv_ref, cwv_ref))
    beta, g = _gdn_gates(ba_ref[...], alog_ref[...], dtb_ref[...], h, heads)
    eg = jnp.exp(g)
    outs = []
    group = 4
    for b0 in range(0, tb, group):
        rows = range(b0, min(b0 + group, tb))
        kcols = [_col_from_row(k[b:b + 1]) for b in rows]
        qcols = [_col_from_row(q[b:b + 1]) for b in rows]
        for b, kcol, qcol in zip(rows, kcols, qcols):
            r = slice(b, b + 1)
            s = s_ref[b, 0] * eg[r]
            v_new = beta[r] * (v[r] - jnp.sum(kcol * s, axis=0, keepdims=True))
            s = s + kcol * v_new
            so_ref[b, 0] = s
            outs.append(jnp.sum(qcol * s, axis=0, keepdims=True))
    o = jnp.concatenate(outs, axis=0)
    o_ref[...] = (_rms(o, nw_ref[...]) * _silu(z_ref[...])).astype(BF16)


def _gdn_step(proj, ba, conv_state_t, conv_w, alog_pad, dtb_pad, norm_w, state, layer, heads, tb):
    bsz = proj.shape[0]
    head_blk = lambda off: pl.BlockSpec((tb, LANES), lambda i, h: (i, off + h))
    cs_blk = lambda off: pl.BlockSpec((CONV_WIDTH - 1, tb, LANES), lambda i, h: (0, i, off + h))
    cw_blk = lambda off: pl.BlockSpec((CONV_WIDTH, LANES), lambda i, h: (0, off + h))
    row128 = pl.BlockSpec((1, LANES), lambda i, h: (0, 0))
    st_in = pl.BlockSpec((None, tb, 1, LANES, LANES), lambda i, h: (layer, i, h, 0, 0))
    st_out = pl.BlockSpec((tb, 1, LANES, LANES), lambda i, h: (i, h, 0, 0))
    return pl.pallas_call(
        functools.partial(_gdn_step_kernel, heads=heads),
        grid=(bsz // tb, heads),
        in_specs=[head_blk(0), head_blk(heads), head_blk(2 * heads), head_blk(3 * heads),
                  pl.BlockSpec((tb, LANES), lambda i, h: (i, 0)),
                  cs_blk(0), cs_blk(heads), cs_blk(2 * heads),
                  cw_blk(0), cw_blk(heads), cw_blk(2 * heads), row128, row128, row128, st_in],
        out_specs=[pl.BlockSpec((tb, LANES), lambda i, h: (i, h)), st_out],
        out_shape=[jax.ShapeDtypeStruct((bsz, heads * LANES), BF16),
                   jax.ShapeDtypeStruct(state.shape[1:], F32)],
        compiler_params=_params(("arbitrary", "arbitrary")),
    )(proj, proj, proj, proj, ba, conv_state_t, conv_state_t, conv_state_t, conv_w, conv_w, conv_w,
      alog_pad, dtb_pad, norm_w.reshape(1, LANES), state)


def _lru_gates(x, wa_ref, wi_ref, ba, bi, lam):
    nb = wa_ref.shape[0]
    ga, gi = [], []
    for s in range(nb):
        xs = x[:, s * LANES:(s + 1) * LANES].astype(BF16)
        ga.append(jnp.dot(xs, wa_ref[s], preferred_element_type=F32))
        gi.append(jnp.dot(xs, wi_ref[s], preferred_element_type=F32))
    gate_a = _sigmoid(jnp.concatenate(ga, axis=-1) + ba)
    gate_i = _sigmoid(jnp.concatenate(gi, axis=-1) + bi)
    log_a = -LRU_C * gate_a * _softplus(-lam)
    return log_a, gate_i


def _lru_prompt_kernel(xl_ref, yl_ref, cw_ref, cb_ref, wa_ref, wi_ref, ba_ref, bi_ref, lam_ref,
                       o_ref, hl_ref, xbuf, abuf, bbuf, h_scr):
    blk = pl.program_id(1)
    cb = xl_ref.shape[0]
    width = xl_ref.shape[1]

    @pl.when(blk == 0)
    def _():
        h_scr[...] = jnp.zeros_like(h_scr)
        xbuf[0:8, :] = jnp.zeros((8, width), F32)

    x = _conv_block(xl_ref, xbuf, cw_ref[...]) + cb_ref[...]
    log_a, gate_i = _lru_gates(x, wa_ref, wi_ref, ba_ref[...], bi_ref[...], lam_ref[...])
    mult = jnp.sqrt(_neg_expm1(2.0 * log_a))
    first = (lax.broadcasted_iota(jnp.int32, (cb, 1), 0) == 0) & (blk == 0)
    mult = jnp.where(first, 1.0, mult)
    abuf[...] = jnp.exp(log_a)
    bbuf[...] = mult * gate_i * x

    def step(t, h):
        h = abuf[pl.ds(t, 1), :] * h + bbuf[pl.ds(t, 1), :]
        bbuf[pl.ds(t, 1), :] = h
        return h

    h_last = lax.fori_loop(0, cb, step, h_scr[...], unroll=8)
    h_scr[...] = h_last
    o_ref[...] = (bbuf[...] * _gelu_tanh(yl_ref[...])).astype(BF16)

    @pl.when(blk == pl.num_programs(1) - 1)
    def _():
        hl_ref[0] = h_last


def _lru_prompt(xy, conv_w, conv_b, w_a, w_i, b_a, b_i, lam, batch, seq, cb):
    m = xy.shape[0]
    width = xy.shape[1] // 2
    nblk = seq // cb
    nb = w_a.shape[0]
    vec = pl.BlockSpec((1, width), lambda b, c: (0, 0))
    wblk = pl.BlockSpec((nb, LANES, LANES), lambda b, c: (0, 0, 0))
    out, h_last = pl.pallas_call(
        _lru_prompt_kernel,
        grid=(batch, nblk),
        in_specs=[pl.BlockSpec((cb, width), lambda b, c: (b * nblk + c, 0)),
                  pl.BlockSpec((cb, width), lambda b, c: (b * nblk + c, 1)),
                  pl.BlockSpec((CONV_WIDTH, width), lambda b, c: (0, 0)),
                  vec, wblk, wblk, vec, vec, vec],
        out_specs=[pl.BlockSpec((cb, width), lambda b, c: (b * nblk + c, 0)),
                   pl.BlockSpec((1, 1, width), lambda b, c: (b, 0, 0))],
        out_shape=[jax.ShapeDtypeStruct((m, width), BF16),
                   jax.ShapeDtypeStruct((batch, 1, width), F32)],
        scratch_shapes=[pltpu.VMEM((cb + 8, width), F32), pltpu.VMEM((cb, width), F32),
                        pltpu.VMEM((cb, width), F32), pltpu.VMEM((1, width), F32)],
        compiler_params=_params(("arbitrary", "arbitrary")),
    )(xy, xy, conv_w, conv_b.reshape(1, width), w_a, w_i, b_a.reshape(1, width), b_i.reshape(1, width),
      lam.reshape(1, width))
    return out, h_last.reshape(batch, width)


def _lru_step_kernel(xl_ref, yl_ref, cs_ref, h0_ref, cw_ref, cb_ref, wa_ref, wi_ref, ba_ref, bi_ref, lam_ref,
                     o_ref, h_ref, *, reset):
    w = cw_ref[...]
    x = cs_ref[0] * w[0:1]
    x = x + cs_ref[1] * w[1:2]
    x = x + cs_ref[2] * w[2:3]
    x = x + xl_ref[...] * w[3:4] + cb_ref[...]
    log_a, gate_i = _lru_gates(x, wa_ref, wi_ref, ba_ref[...], bi_ref[...], lam_ref[...])
    mult = 1.0 if reset else jnp.sqrt(_neg_expm1(2.0 * log_a))
    h = jnp.exp(log_a) * h0_ref[...] + mult * gate_i * x
    h_ref[...] = h
    o_ref[...] = (h * _gelu_tanh(yl_ref[...])).astype(BF16)


def _lru_step(xy, conv_state_t, h0, conv_w, conv_b, w_a, w_i, b_a, b_i, lam, reset):
    bsz = xy.shape[0]
    width = xy.shape[1] // 2
    nb = w_a.shape[0]
    vec = pl.BlockSpec((1, width), lambda i: (0, 0))
    wblk = pl.BlockSpec((nb, LANES, LANES), lambda i: (0, 0, 0))
    full = pl.BlockSpec((bsz, width), lambda i: (0, 0))
    return pl.pallas_call(
        functools.partial(_lru_step_kernel, reset=reset),
        grid=(1,),
        in_specs=[full, pl.BlockSpec((bsz, width), lambda i: (0, 1)),
                  pl.BlockSpec((CONV_WIDTH - 1, bsz, width), lambda i: (0, 0, 0)), full,
                  pl.BlockSpec((CONV_WIDTH, width), lambda i: (0, 0)), vec, wblk, wblk, vec, vec, vec],
        out_specs=[full, full],
        out_shape=[jax.ShapeDtypeStruct((bsz, width), BF16), jax.ShapeDtypeStruct((bsz, width), F32)],
        compiler_params=_params(("arbitrary",)),
    )(xy, xy, conv_state_t, h0, conv_w, conv_b.reshape(1, width), w_a, w_i, b_a.reshape(1, width),
      b_i.reshape(1, width), lam.reshape(1, width))


def _hgrn_lower_bound(lb_ref, layer, j=0):
    depth = lb_ref.shape[0]
    raw = [lb_ref[l, j] for l in range(depth)]
    mx = raw[0]
    for r in raw[1:]:
        mx = jnp.maximum(mx, r)
    ex = [jnp.exp(r - mx) for r in raw]
    tot = ex[0]
    for e in ex[1:]:
        tot = tot + e
    wts = [e / tot for e in ex]
    cum = wts[0]
    for w in wts[1:layer + 1]:
        cum = cum + w
    return cum - wts[0]


def _hgrn_inputs(q_raw, f_raw, lb):
    q = _silu(q_raw) * (LANES ** -0.5)
    f = lb + (1.0 - lb) * _sigmoid(f_raw)
    return q, f


def _level_ref(g, half):
    c = g.shape[0]
    sub = 8
    if 2 * half >= sub:
        g3 = g.reshape(c // (2 * half), 2 * half, LANES)
        return jnp.broadcast_to(g3[:, half - 1:half, :], g3.shape).reshape(c, LANES)
    g3 = g.reshape(c // sub, sub, LANES)
    rin = lax.broadcasted_iota(jnp.int32, g3.shape, 1)
    out = jnp.broadcast_to(g3[:, sub - half - 1:sub - half, :], g3.shape)
    for start in range(sub - 4 * half, -1, -2 * half):
        out = jnp.where(rin < start + 2 * half, jnp.broadcast_to(g3[:, start + half - 1:start + half, :], g3.shape),
                        out)
    return out.reshape(c, LANES)


def _hgrn_level_table(c):
    i = np.arange(c)[:, None]
    j = np.arange(c)[None, :]
    nlev = int(math.log2(c))
    top_bit = sum(((i ^ j) >> b > 0).astype(np.int32) for b in range(1, nlev))
    return jnp.asarray(np.where(j < i, nlev - 1 - top_bit, np.where(j == i, nlev, -1)), jnp.int32)


def _hgrn_prompt_kernel(q_ref, f_ref, i_ref, gz_ref, lb_ref, nw_ref, lvl_ref, o_ref, s_ref, s_scr, *, layer):
    hp = s_scr.shape[0]
    blk = pl.program_id(2)
    cb = q_ref.shape[0]
    c = CHUNK

    @pl.when(blk == 0)
    def _():
        s_scr[...] = jnp.zeros_like(s_scr)

    lbs = [_hgrn_lower_bound(lb_ref, layer, j) for j in range(hp)]
    head_lanes = [slice(j * LANES, (j + 1) * LANES) for j in range(hp)]
    tri = (_iota2(c, 1) <= _iota2(c, 0)).astype(BF16)
    nw = nw_ref[...]
    lvl = lvl_ref[...]
    halves = [c >> (i + 1) for i in range(int(math.log2(c)))]

    items = [(slice(ci * c, (ci + 1) * c), j) for ci in range(cb // c) for j in range(hp)]
    qf = [_hgrn_inputs(q_ref[sl, head_lanes[j]], f_ref[sl, head_lanes[j]], lbs[j]) for sl, j in items]
    qs = [q for q, _ in qf]
    ks = [1.0 - f for _, f in qf]
    vs = [i_ref[sl, head_lanes[j]] for sl, j in items]
    gs = [_dot_exact_lhs(tri, jnp.log(f)) for _, f in qf]

    amats = [_dot_nt(q, k) for q, k in zip(qs, ks)]
    for li, half in enumerate(halves):
        es = [jnp.exp(-jnp.abs(g - _level_ref(g, half))) for g in gs]
        ps = [_dot_nt(q * e, k * e) for q, k, e in zip(qs, ks, es)]
        amats = [jnp.where(lvl == li, p, a) for p, a in zip(ps, amats)]
    intra = [_dot(jnp.where(lvl >= 0, a, 0.0), v) for a, v in zip(amats, vs)]

    g_lasts = [g[c - 1:c, :] for g in gs]
    qdec = [q * jnp.exp(g) for q, g in zip(qs, gs)]
    sdec = [_col_from_row(jnp.exp(gl)) for gl in g_lasts]
    sadd = [_dot_tn(k * jnp.exp(gl - g), v) for k, gl, g, v in zip(ks, g_lasts, gs, vs)]

    states = [s_scr[j] for j in range(hp)]
    outs = []
    for (sl, j), o, qd, dec, add in zip(items, intra, qdec, sdec, sadd):
        outs.append(o + _dot(qd, states[j]))
        states[j] = states[j] * dec + add
    for j in range(hp):
        s_scr[j] = states[j]
    for (sl, j), o in zip(items, outs):
        o_ref[sl, head_lanes[j]] = (_rms(o, nw) * _silu(gz_ref[sl, head_lanes[j]])).astype(BF16)

    @pl.when(blk == pl.num_programs(2) - 1)
    def _():
        s_ref[0] = s_scr[...]


def _hgrn_prompt(proj, lb_raw, norm_w, layer, batch, seq, heads, cb):
    m = proj.shape[0]
    nblk = seq // cb
    depth = lb_raw.shape[0]
    hp = HGRN_HEADS_PER_STEP
    assert heads % hp == 0
    groups = heads // hp
    head_blk = lambda part: pl.BlockSpec((cb, hp * LANES), lambda b, h, c: (b * nblk + c, part * groups + h))
    return pl.pallas_call(
        functools.partial(_hgrn_prompt_kernel, layer=layer),
        grid=(batch, groups, nblk),
        in_specs=[head_blk(0), head_blk(1), head_blk(2), head_blk(3),
                  pl.BlockSpec((depth, hp, 1, LANES), lambda b, h, c: (0, h, 0, 0)),
                  pl.BlockSpec((1, LANES), lambda b, h, c: (0, 0)),
                  pl.BlockSpec((CHUNK, CHUNK), lambda b, h, c: (0, 0))],
        out_specs=[pl.BlockSpec((cb, hp * LANES), lambda b, h, c: (b * nblk + c, h)),
                   pl.BlockSpec((1, hp, LANES, LANES), lambda b, h, c: (b, h, 0, 0))],
        out_shape=[jax.ShapeDtypeStruct((m, heads * LANES), BF16),
                   jax.ShapeDtypeStruct((batch, heads, LANES, LANES), F32)],
        scratch_shapes=[pltpu.VMEM((hp, LANES, LANES), F32)],
        compiler_params=_params(("arbitrary", "arbitrary", "arbitrary")),
    )(proj, proj, proj, proj, lb_raw.reshape(depth, heads, 1, LANES), norm_w.reshape(1, LANES),
      _hgrn_level_table(CHUNK))


def _hgrn_step_kernel(q_ref, f_ref, i_ref, gz_ref, lb_ref, nw_ref, s_ref, o_ref, so_ref, *, layer):
    tb = q_ref.shape[0]
    lb = _hgrn_lower_bound(lb_ref, layer)
    q, f = _hgrn_inputs(q_ref[...], f_ref[...], lb)
    v = i_ref[...]
    outs = []
    group = 4
    for b0 in range(0, tb, group):
        rows = range(b0, min(b0 + group, tb))
        fcols = [_col_from_row(f[b:b + 1]) for b in rows]
        qcols = [_col_from_row(q[b:b + 1]) for b in rows]
        for b, fcol, qcol in zip(rows, fcols, qcols):
            s = fcol * (s_ref[b, 0] - v[b:b + 1]) + v[b:b + 1]
            so_ref[b, 0] = s
            outs.append(jnp.sum(qcol * s, axis=0, keepdims=True))
    o = jnp.concatenate(outs, axis=0)
    o_ref[...] = (_rms(o, nw_ref[...]) * _silu(gz_ref[...])).astype(BF16)


def _hgrn_step(proj, lb_raw, norm_w, state, state_idx, layer, heads, tb):
    bsz = proj.shape[0]
    depth = lb_raw.shape[0]
    head_blk = lambda off: pl.BlockSpec((tb, LANES), lambda i, h: (i, off + h))
    st_in = pl.BlockSpec((None, tb, 1, LANES, LANES), lambda i, h: (state_idx, i, h, 0, 0))
    st_out = pl.BlockSpec((tb, 1, LANES, LANES), lambda i, h: (i, h, 0, 0))
    return pl.pallas_call(
        functools.partial(_hgrn_step_kernel, layer=layer),
        grid=(bsz // tb, heads),
        in_specs=[head_blk(0), head_blk(heads), head_blk(2 * heads), head_blk(3 * heads),
                  pl.BlockSpec((depth, 1, 1, LANES), lambda i, h: (0, h, 0, 0)),
                  pl.BlockSpec((1, LANES), lambda i, h: (0, 0)), st_in],
        out_specs=[pl.BlockSpec((tb, LANES), lambda i, h: (i, h)), st_out],
        out_shape=[jax.ShapeDtypeStruct((bsz, heads * LANES), BF16),
                   jax.ShapeDtypeStruct(state.shape[1:], F32)],
        compiler_params=_params(("arbitrary", "arbitrary")),
    )(proj, proj, proj, proj, lb_raw.reshape(depth, heads, 1, LANES), norm_w.reshape(1, LANES), state)


TM = 512
TN = 1024
TF = 512


def _trunk(x, mem_k, mem_v, states, wts, batch, seq, prompt):
    depth = wts["norm_mix"].shape[0]
    gdn_w = wts["gdn_conv_w"].shape[-1] // 3
    gdn_heads = gdn_w // LANES
    hgrn_heads = wts["hgrn_lb_raw"].shape[1] // LANES
    lru_w = wts["lru_conv_w"].shape[-1]
    new = {"gdn_conv": [], "gdn": [], "lru_conv": [], "lru": [], "hgrn": []}
    for l in range(depth):
        g_mix = wts["norm_mix"][l]
        if l % 2 == 0:
            e = l // 2
            proj = _norm_matmul(x, g_mix, wts["ab_head"], e, 0, 4 * gdn_w, TM, TN)
            ba = _norm_matmul(x, g_mix, wts["ab_head"], e, 4 * gdn_w, LANES, TM, LANES)
            xy = _norm_matmul(x, g_mix, wts["ab_lru"], e, 0, 2 * lru_w, TM, TN)
            gargs = (wts["gdn_conv_w"][e], wts["alog_pad"][e], wts["dtb_pad"][e], wts["gdn_norm_w"][e])
            largs = (wts["lru_conv_w"][e], wts["lru_conv_b"][e], wts["lru_w_a"][e], wts["lru_w_i"][e],
                     wts["lru_b_a"][e], wts["lru_b_i"][e], wts["lru_lam"][e])
            if prompt:
                o_a, s_new = _gdn_prompt(proj, ba, *gargs, batch, seq, gdn_heads, 512)
                o_b, h_new = _lru_prompt(xy, *largs, batch, seq, 256)
                tail = lambda t, w: t.reshape(batch, seq, -1)[:, seq - (CONV_WIDTH - 1):, :w]
                new["gdn_conv"].append(tail(proj, 3 * gdn_w))
                new["lru_conv"].append(tail(xy, lru_w))
            else:
                gc_state, lc_state = states["gdn_conv"][e], states["lru_conv"][e]
                o_a, s_new = _gdn_step(proj, ba, jnp.swapaxes(gc_state, 0, 1), *gargs, states["gdn"], e,
                                       gdn_heads, 32)
                o_b, h_new = _lru_step(xy, jnp.swapaxes(lc_state, 0, 1), states["lru"][e], *largs, reset=False)
                new["gdn_conv"].append(jnp.concatenate([gc_state[:, 1:], proj[:, None, :3 * gdn_w]], axis=1))
                new["lru_conv"].append(jnp.concatenate([lc_state[:, 1:], xy[:, None, :lru_w]], axis=1))
            new["gdn"].append(s_new)
            new["lru"].append(h_new)
            x = _matmul2_res(o_a, o_b, wts["ab_w_out"], e, x, TM, TN)
        else:
            o_idx = l // 2
            proj = _norm_matmul(x, g_mix, wts["c_w_in"], o_idx, 0, wts["c_w_in"].shape[2], TM, TN)
            hargs = (wts["hgrn_lb_raw"], wts["hgrn_norm_w"][o_idx])
            if prompt:
                o_c, s_new = _hgrn_prompt(proj, *hargs, l, batch, seq, hgrn_heads, 512)
            else:
                o_c, s_new = _hgrn_step(proj, *hargs, states["hgrn"], o_idx, l, hgrn_heads, 32)
            new["hgrn"].append(s_new)
            x = _matmul_res(o_c, wts["c_w_out"], o_idx, x, TM, TN)
        if prompt:
            x = _mem_attn_prompt(x, wts["norm_mem"][l], wts["mem_w_q"], mem_k, mem_v, wts["mem_w_o"], l, seq, 256)
        else:
            q = _norm_matmul(x, wts["norm_mem"][l], wts["mem_w_q"], l, 0, wts["mem_w_q"].shape[2], TM, TN)
            o = _mem_attn_sample(q, mem_k, mem_v, l, 8)
            x = _matmul_res(o.reshape(batch, -1).astype(BF16), wts["mem_w_o"], l, x, TM, TN)
        x = _ffn(x, wts["norm_ffn"][l], wts["ffn_w_up"], wts["ffn_w_down"], l, wts["norm_final"],
                 l == depth - 1, TM, TF)
    return x, {n: jnp.stack(v) for n, v in new.items()}


def kernel(x_prompt, x_sample, cache_mem_k, cache_mem_v, state_gdn_conv, state_gdn, state_lru_conv, state_lru, state_hgrn, mem_prompt, norm_mix, norm_mem, norm_mem_kv, norm_ffn, norm_final, ab_w_in, ab_w_out, gdn_conv_w, gdn_a_log, gdn_dt_bias, gdn_norm_w, lru_conv_w, lru_conv_b, lru_w_a, lru_b_a, lru_w_i, lru_b_i, lru_lam, c_w_in, c_w_out, hgrn_lb_raw, hgrn_norm_w, mem_w_q, mem_w_k, mem_w_v, mem_w_o, ffn_w_up, ffn_w_down):
    bp, lp, d = x_prompt.shape
    bs, ls, _ = x_sample.shape
    assert ls == 1, "the sample group advances one token per call"
    depth = norm_mix.shape[0]
    gdn_heads = gdn_a_log.shape[1]
    gdn_w = gdn_heads * LANES
    mem_tokens, mem_heads, mem_hd = cache_mem_k.shape[2:]
    mem_w = mem_heads * mem_hd
    n_ba = 2 * gdn_heads
    assert n_ba <= LANES

    pad_ba = lambda a: jnp.pad(a, ((0, 0), (gdn_heads, LANES - n_ba)))[:, None, :]
    ab_bf = ab_w_in.astype(BF16)
    wts = dict(
        norm_mix=norm_mix, norm_mem=norm_mem, norm_ffn=norm_ffn, norm_final=norm_final,
        ab_head=ab_bf, ab_lru=ab_bf[:, :, 4 * gdn_w + n_ba:],
        ab_w_out=ab_w_out.astype(BF16),
        gdn_conv_w=gdn_conv_w, alog_pad=pad_ba(gdn_a_log), dtb_pad=pad_ba(gdn_dt_bias), gdn_norm_w=gdn_norm_w,
        lru_conv_w=lru_conv_w, lru_conv_b=lru_conv_b, lru_w_a=lru_w_a.astype(BF16), lru_w_i=lru_w_i.astype(BF16),
        lru_b_a=lru_b_a, lru_b_i=lru_b_i, lru_lam=lru_lam,
        c_w_in=c_w_in.astype(BF16), c_w_out=c_w_out.astype(BF16),
        hgrn_lb_raw=hgrn_lb_raw, hgrn_norm_w=hgrn_norm_w,
        mem_w_q=mem_w_q.astype(BF16), mem_w_o=mem_w_o.astype(BF16),
        ffn_w_up=ffn_w_up.astype(BF16), ffn_w_down=ffn_w_down.astype(BF16),
    )

    mem_rows = mem_prompt.reshape(bp * mem_tokens, d)
    w_kv = jnp.concatenate([mem_w_k, mem_w_v], axis=-1).astype(BF16)
    kv = jnp.stack([_norm_matmul(mem_rows, norm_mem_kv[l], w_kv, l, 0, 2 * mem_w, TM, TN) for l in range(depth)])
    p_mem_k = kv[:, :, :mem_w].reshape(depth, bp, mem_tokens, mem_w)
    p_mem_v = kv[:, :, mem_w:].reshape(depth, bp, mem_tokens, mem_w)

    y_p, new_p = _trunk(x_prompt.reshape(bp * lp, d), p_mem_k, p_mem_v, None, wts, bp, lp, True)
    states = dict(gdn_conv=state_gdn_conv, gdn=state_gdn, lru_conv=state_lru_conv, lru=state_lru,
                  hgrn=state_hgrn)
    y_s, new_s = _trunk(x_sample.reshape(bs * ls, d), cache_mem_k, cache_mem_v, states, wts, bs, ls, False)

    order = ("gdn_conv", "gdn", "lru_conv", "lru", "hgrn")
    mem5 = lambda t: t.reshape(depth, bp, mem_tokens, mem_heads, mem_hd)
    return (y_p.reshape(bp, lp, d), y_s.reshape(bs, ls, d), mem5(p_mem_k), mem5(p_mem_v),
            *(new_p[n] for n in order), *(new_s[n] for n in order))
```

```python
import functools
import math

import jax
import jax.numpy as jnp
import numpy as np
from jax import lax
from jax.experimental import pallas as pl
from jax.experimental.pallas import tpu as pltpu

F32 = jnp.float32
BF16 = jnp.bfloat16
EPS = 1e-6
LANES = 128
CONV_WIDTH = 4
LRU_C = 8.0
VMEM_LIMIT_BYTES = 56 * 1024 * 1024
TM = 512
TN = 1024
TF = 512
TN_CAST = 512
TF_CAST = 256
CHUNK = 128
GDN_HEADS_PER_STEP = 2
HGRN_HEADS_PER_STEP = 2


def _params(sem):
    return pltpu.CompilerParams(dimension_semantics=sem, vmem_limit_bytes=VMEM_LIMIT_BYTES)


def _sigmoid(x):
    return jax.nn.sigmoid(x)


def _silu(x):
    return x * _sigmoid(x)


def _softplus(x):
    return jnp.maximum(x, 0.0) + jnp.log1p(jnp.exp(-jnp.abs(x)))


def _neg_expm1(x):
    t = jnp.tanh(0.5 * x)
    return -2.0 * t / (1.0 - t)


def _gelu_tanh(x):
    c = math.sqrt(2.0 / math.pi)
    return 0.5 * x * (1.0 + jnp.tanh(c * (x + 0.044715 * (x * x * x))))


def _rms(x, w):
    return x * lax.rsqrt(jnp.mean(x * x, axis=-1, keepdims=True) + EPS) * w


def _l2n(x):
    return x * lax.rsqrt(jnp.sum(x * x, axis=-1, keepdims=True) + EPS)


def _dot(a, b):
    return jnp.dot(a.astype(BF16), b.astype(BF16), preferred_element_type=F32)


def _dot_nt(a, b):
    return lax.dot_general(a.astype(BF16), b.astype(BF16), (((1,), (1,)), ((), ())),
                           preferred_element_type=F32)


def _dot_tn(a, b):
    return lax.dot_general(a.astype(BF16), b.astype(BF16), (((0,), (0,)), ((), ())),
                           preferred_element_type=F32)


def _split2(a):
    hi = a.astype(BF16)
    lo = (a - hi.astype(F32)).astype(BF16)
    return hi, lo


def _dot_hp(a_parts, b_parts):
    ah, al = a_parts
    bh, bl = b_parts
    return jnp.dot(jnp.concatenate([ah, ah, al], axis=1), jnp.concatenate([bh, bl, bh], axis=0),
                   preferred_element_type=F32)


def _dot_exact_lhs(t, b):
    b1 = b.astype(BF16)
    r1 = b - b1.astype(F32)
    b2 = r1.astype(BF16)
    b3 = (r1 - b2.astype(F32)).astype(BF16)
    return jnp.dot(jnp.concatenate([t, t, t], axis=1), jnp.concatenate([b1, b2, b3], axis=0),
                   preferred_element_type=F32)


def _iota2(n, axis):
    return lax.broadcasted_iota(jnp.int32, (n, n), axis)


def _col_from_row(row):
    n = row.shape[-1]
    return jnp.broadcast_to(row, (n, n)).T


def _norm_matmul_kernel(x_ref, g_ref, w_ref, o_ref, *rest):
    xn_ref = rest[-1]

    @pl.when(pl.program_id(1) == 0)
    def _():
        xn_ref[...] = _rms(x_ref[...], g_ref[...]).astype(BF16)

    w = w_ref[...]
    if len(rest) == 2:
        w = w.astype(BF16)
        rest[0][...] = w
    o_ref[...] = jnp.dot(xn_ref[...], w, preferred_element_type=F32)


def _norm_matmul(x, g, w, layer, n0, n, tm, tn):
    m, k = x.shape
    cast = w.dtype != BF16
    tm = min(tm, m)
    tn = min(tn, n, TN_CAST) if cast else min(tn, n)
    j0 = n0 // tn
    out_specs = [pl.BlockSpec((tm, tn), lambda i, j: (i, j))]
    out_shape = [jax.ShapeDtypeStruct((m, n), F32)]
    if cast:
        assert m == tm and n0 == 0 and n == w.shape[2], "each weight tile must be visited exactly once"
        out_specs.append(pl.BlockSpec((None, k, tn), lambda i, j: (0, 0, j)))
        out_shape.append(jax.ShapeDtypeStruct((1, k, n), BF16))
    res = pl.pallas_call(
        _norm_matmul_kernel,
        grid=(m // tm, n // tn),
        in_specs=[pl.BlockSpec((tm, k), lambda i, j: (i, 0)),
                  pl.BlockSpec((1, k), lambda i, j: (0, 0)),
                  pl.BlockSpec((None, k, tn), lambda i, j: (layer, 0, j0 + j))],
        out_specs=out_specs,
        out_shape=out_shape,
        scratch_shapes=[pltpu.VMEM((tm, k), BF16)],
        compiler_params=_params(("arbitrary", "arbitrary")),
    )(x, g.reshape(1, k), w)
    return (res[0], res[1]) if cast else (res[0], w)


def _matmul2_res_kernel(a1_ref, a2_ref, w1_ref, w2_ref, r_ref, o_ref):
    acc = jnp.dot(a1_ref[...], w1_ref[...], preferred_element_type=F32)
    acc = acc + jnp.dot(a2_ref[...], w2_ref[...], preferred_element_type=F32)
    o_ref[...] = r_ref[...] + acc


def _matmul2_res(a1, a2, w, layer, res, tm, tn):
    m, kh = a1.shape
    n = w.shape[2]
    tm = min(tm, m)
    tn = min(tn, n)
    return pl.pallas_call(
        _matmul2_res_kernel,
        grid=(m // tm, n // tn),
        in_specs=[pl.BlockSpec((tm, kh), lambda i, j: (i, 0)),
                  pl.BlockSpec((tm, kh), lambda i, j: (i, 0)),
                  pl.BlockSpec((None, kh, tn), lambda i, j: (layer, 0, j)),
                  pl.BlockSpec((None, kh, tn), lambda i, j: (layer, 1, j)),
                  pl.BlockSpec((tm, tn), lambda i, j: (i, j))],
        out_specs=pl.BlockSpec((tm, tn), lambda i, j: (i, j)),
        out_shape=jax.ShapeDtypeStruct((m, n), F32),
        compiler_params=_params(("arbitrary", "arbitrary")),
    )(a1, a2, w, w, res)


def _matmul_res_kernel(a_ref, w_ref, r_ref, o_ref, *wb_ref):
    w = w_ref[...]
    if wb_ref:
        w = w.astype(BF16)
        wb_ref[0][...] = w
    o_ref[...] = r_ref[...] + jnp.dot(a_ref[...], w, preferred_element_type=F32)


def _matmul_res(a, w, layer, res, tm, tn):
    m, k = a.shape
    n = w.shape[2]
    cast = w.dtype != BF16
    tm = min(tm, m)
    tn = min(tn, n, TN_CAST) if cast else min(tn, n)
    out_specs = [pl.BlockSpec((tm, tn), lambda i, j: (i, j))]
    out_shape = [jax.ShapeDtypeStruct((m, n), F32)]
    if cast:
        assert m == tm, "each weight tile must be visited exactly once"
        out_specs.append(pl.BlockSpec((None, k, tn), lambda i, j: (0, 0, j)))
        out_shape.append(jax.ShapeDtypeStruct((1, k, n), BF16))
    out = pl.pallas_call(
        _matmul_res_kernel,
        grid=(m // tm, n // tn),
        in_specs=[pl.BlockSpec((tm, k), lambda i, j: (i, 0)),
                  pl.BlockSpec((None, k, tn), lambda i, j: (layer, 0, j)),
                  pl.BlockSpec((tm, tn), lambda i, j: (i, j))],
        out_specs=out_specs,
        out_shape=out_shape,
        compiler_params=_params(("arbitrary", "arbitrary")),
    )(a, w, res)
    return (out[0], out[1]) if cast else (out[0], w)


def _ffn_kernel(x_ref, g_ref, wu_ref, wd_ref, gf_ref, o_ref, *rest, final_norm):
    xn_ref = rest[-1]
    f = pl.program_id(1)

    @pl.when(f == 0)
    def _():
        x = x_ref[...]
        xn_ref[...] = _rms(x, g_ref[...]).astype(BF16)
        o_ref[...] = x

    wu, wd = wu_ref[...], wd_ref[...]
    if len(rest) == 3:
        wu, wd = wu.astype(BF16), wd.astype(BF16)
        rest[0][...] = wu
        rest[1][...] = wd
    h = jnp.dot(xn_ref[...], wu, preferred_element_type=F32)
    h = jnp.square(jnp.maximum(h, 0.0)).astype(BF16)
    o_ref[...] += jnp.dot(h, wd, preferred_element_type=F32)

    if final_norm:
        @pl.when(f == pl.num_programs(1) - 1)
        def _():
            o_ref[...] = _rms(o_ref[...], gf_ref[...])


def _ffn(x, g, w_up, w_down, layer, g_final, final_norm, tm, tf):
    m, d = x.shape
    dff = w_up.shape[2]
    cast = w_up.dtype != BF16
    tm = min(tm, m)
    tf = min(tf, TF_CAST) if cast else tf
    out_specs = [pl.BlockSpec((tm, d), lambda i, f: (i, 0))]
    out_shape = [jax.ShapeDtypeStruct((m, d), F32)]
    if cast:
        assert m == tm, "each weight tile must be visited exactly once"
        out_specs += [pl.BlockSpec((None, d, tf), lambda i, f: (0, 0, f)),
                      pl.BlockSpec((None, tf, d), lambda i, f: (0, f, 0))]
        out_shape += [jax.ShapeDtypeStruct((1, d, dff), BF16), jax.ShapeDtypeStruct((1, dff, d), BF16)]
    out = pl.pallas_call(
        functools.partial(_ffn_kernel, final_norm=final_norm),
        grid=(m // tm, dff // tf),
        in_specs=[pl.BlockSpec((tm, d), lambda i, f: (i, 0), pipeline_mode=pl.Buffered(1)),
                  pl.BlockSpec((1, d), lambda i, f: (0, 0)),
                  pl.BlockSpec((None, d, tf), lambda i, f: (layer, 0, f)),
                  pl.BlockSpec((None, tf, d), lambda i, f: (layer, f, 0)),
                  pl.BlockSpec((1, d), lambda i, f: (0, 0))],
        out_specs=out_specs,
        out_shape=out_shape,
        scratch_shapes=[pltpu.VMEM((tm, d), BF16)],
        compiler_params=_params(("arbitrary", "arbitrary")),
    )(x, g.reshape(1, d), w_up, w_down, g_final.reshape(1, d))
    return (out[0], out[1], out[2]) if cast else (out[0], w_up, w_down)


def _mem_attn_prompt_kernel(x_ref, g_ref, wq_ref, k_ref, v_ref, wo_ref, o_ref, *, heads):
    x = x_ref[...]
    xn = _rms(x, g_ref[...]).astype(BF16)
    q = jnp.dot(xn, wq_ref[...], preferred_element_type=F32)
    k = k_ref[0].astype(BF16)
    v = v_ref[0].astype(BF16)
    scale = LANES ** -0.5
    outs = []
    for h in range(heads):
        sl = slice(h * LANES, (h + 1) * LANES)
        s = _dot_nt(q[:, sl], k[:, sl]) * scale
        e = jnp.exp(s - jnp.max(s, axis=-1, keepdims=True))
        p = e / jnp.sum(e, axis=-1, keepdims=True)
        outs.append(_dot(p, v[:, sl]))
    o = jnp.concatenate(outs, axis=-1).astype(BF16)
    o_ref[...] = x + jnp.dot(o, wo_ref[...], preferred_element_type=F32)


def _mem_attn_prompt(x, g, w_q, mem_k, mem_v, w_o, layer, w_layer, seq, tl):
    m, d = x.shape
    _, _, t, w = mem_k.shape
    nblk = seq // tl
    return pl.pallas_call(
        functools.partial(_mem_attn_prompt_kernel, heads=w // LANES),
        grid=(m // tl,),
        in_specs=[pl.BlockSpec((tl, d), lambda i: (i, 0)),
                  pl.BlockSpec((1, d), lambda i: (0, 0)),
                  pl.BlockSpec((None, d, w), lambda i: (w_layer, 0, 0)),
                  pl.BlockSpec((None, 1, t, w), lambda i: (layer, i // nblk, 0, 0)),
                  pl.BlockSpec((None, 1, t, w), lambda i: (layer, i // nblk, 0, 0)),
                  pl.BlockSpec((None, w, d), lambda i: (w_layer, 0, 0))],
        out_specs=pl.BlockSpec((tl, d), lambda i: (i, 0)),
        out_shape=jax.ShapeDtypeStruct((m, d), F32),
        compiler_params=_params(("arbitrary",)),
    )(x, g.reshape(1, d), w_q, mem_k, mem_v, w_o)


def _mem_attn_sample_kernel(q_ref, k_ref, v_ref, o_ref, *, tb, heads):
    scale = LANES ** -0.5
    for b in range(tb):
        q = jnp.concatenate([q_ref[b:b + 1, h * LANES:(h + 1) * LANES] for h in range(heads)], axis=0)
        s = jnp.sum(k_ref[b] * (q * scale)[None], axis=-1, keepdims=True)
        e = jnp.exp(s - jnp.max(s, axis=0, keepdims=True))
        o_ref[b] = jnp.sum(e * v_ref[b], axis=0) / jnp.sum(e, axis=0)


def _mem_attn_sample(q, mem_k, mem_v, layer, tb):
    b = q.shape[0]
    _, _, t, heads, hd = mem_k.shape
    kv_blk = pl.BlockSpec((None, tb, t, heads, hd), lambda i: (layer, i, 0, 0, 0))
    return pl.pallas_call(
        functools.partial(_mem_attn_sample_kernel, tb=tb, heads=heads),
        grid=(b // tb,),
        in_specs=[pl.BlockSpec((tb, heads * hd), lambda i: (i, 0)), kv_blk, kv_blk],
        out_specs=pl.BlockSpec((tb, heads, hd), lambda i: (i, 0, 0)),
        out_shape=jax.ShapeDtypeStruct((b, heads, hd), F32),
        compiler_params=_params(("arbitrary",)),
    )(q, mem_k, mem_v)


def _conv_block(x_ref, buf, w):
    cb = x_ref.shape[0]
    buf[8:8 + cb, :] = x_ref[...]
    y = buf[5:5 + cb, :] * w[0:1]
    y = y + buf[6:6 + cb, :] * w[1:2]
    y = y + buf[7:7 + cb, :] * w[2:3]
    y = y + buf[8:8 + cb, :] * w[3:4]
    buf[0:8, :] = buf[cb:cb + 8, :]
    return y


def _gdn_gates(ba, alog, dtb, h, heads):
    lane = lax.broadcasted_iota(jnp.int32, ba.shape, 1)
    beta_all = _sigmoid(ba)
    g_all = -jnp.exp(alog) * _softplus(ba + dtb)
    beta = jnp.sum(jnp.where(lane == h, beta_all, 0.0), axis=1, keepdims=True)
    g = jnp.sum(jnp.where(lane == heads + h, g_all, 0.0), axis=1, keepdims=True)
    return beta, g


def _gdn_prompt_kernel(q_ref, k_ref, v_ref, z_ref, ba_ref, cwq_ref, cwk_ref, cwv_ref, alog_ref, dtb_ref,
                       nw_ref, o_ref, s_ref, qbuf, kbuf, vbuf, s_scr, *, heads):
    hp = s_scr.shape[0]
    blk = pl.program_id(2)
    cb = q_ref.shape[0]
    c = CHUNK
    n_chunks = cb // c

    @pl.when(blk == 0)
    def _():
        s_scr[...] = jnp.zeros_like(s_scr)
        for buf in (qbuf, kbuf, vbuf):
            buf[0:8, :] = jnp.zeros((8, hp * LANES), F32)

    for x_ref, buf in ((q_ref, qbuf), (k_ref, kbuf), (v_ref, vbuf)):
        buf[8:8 + cb, :] = x_ref[...]
    conv_w = (cwq_ref[...], cwk_ref[...], cwv_ref[...])
    ba = ba_ref[...]
    gates = [_gdn_gates(ba, alog_ref[...], dtb_ref[...], pl.program_id(1) * hp + j, heads) for j in range(hp)]
    head_lanes = [slice(j * LANES, (j + 1) * LANES) for j in range(hp)]

    row = _iota2(c, 0)
    col = _iota2(c, 1)
    causal = col <= row
    strict = col < row
    tri = causal.astype(BF16)
    eye = (col == row).astype(F32)
    nw = nw_ref[...]

    def conv_rows(buf, w, r0):
        y = buf[r0 + 5:r0 + 5 + c, :] * w[0:1]
        y = y + buf[r0 + 6:r0 + 6 + c, :] * w[1:2]
        y = y + buf[r0 + 7:r0 + 7 + c, :] * w[2:3]
        return y + buf[r0 + 8:r0 + 8 + c, :] * w[3:4]

    conv = [[_silu(conv_rows(buf, w, ci * c)) for buf, w in zip((qbuf, kbuf, vbuf), conv_w)]
            for ci in range(n_chunks)]
    items = [(ci, j) for ci in range(n_chunks) for j in range(hp)]
    rows = lambda ci: slice(ci * c, (ci + 1) * c)
    qs = [_l2n(conv[ci][0][:, head_lanes[j]]) * (LANES ** -0.5) for ci, j in items]
    ks = [_l2n(conv[ci][1][:, head_lanes[j]]) for ci, j in items]
    vs = [conv[ci][2][:, head_lanes[j]] for ci, j in items]
    betas = [gates[j][0][rows(ci)] for ci, j in items]
    gcs = [_dot_exact_lhs(tri, jnp.broadcast_to(gates[j][1][rows(ci)], (c, c))) for ci, j in items]
    decays = [jnp.exp(jnp.where(causal, gc - gc.T, -jnp.inf)) for gc in gcs]
    kbs = [k * b for k, b in zip(ks, betas)]
    ms = [jnp.where(strict, _dot_nt(kb, k) * dec, 0.0) for kb, k, dec in zip(kbs, ks, decays)]
    xs = [_split2(-m) for m in ms]
    ts = [eye - m for m in ms]
    for _ in range(int(math.log2(c)) - 1):
        xs = [_split2(_dot_hp(xp, xp)) for xp in xs]
        ts = [t + _dot_hp(_split2(t), xp) for t, xp in zip(ts, xs)]
    egs = [jnp.exp(gc) for gc in gcs]
    uws = [_dot(t, jnp.concatenate([v * b, kb * eg], axis=1))
           for t, v, b, kb, eg in zip(ts, vs, betas, kbs, egs)]
    attns = [jnp.where(causal, _dot_nt(q, k) * dec, 0.0) for q, k, dec in zip(qs, ks, decays)]
    g_lasts = [gc[c - 1:c, :] for gc in gcs]
    kd_uws = [_dot_tn(k * jnp.exp(gl - gc), uw) for k, gl, gc, uw in zip(ks, g_lasts, gcs, uws)]
    at_uws = [_dot(attn, uw) for attn, uw in zip(attns, uws)]
    lhs = [jnp.concatenate([q * eg - at[:, LANES:], eye * jnp.exp(gl) - kd[:, LANES:]], axis=0)
           for q, eg, at, gl, kd in zip(qs, egs, at_uws, g_lasts, kd_uws)]

    states = [s_scr[j] for j in range(hp)]
    outs = []
    for it, (ci, j) in enumerate(items):
        both = _dot(lhs[it], states[j])
        outs.append(at_uws[it][:, :LANES] + both[:c])
        states[j] = both[c:] + kd_uws[it][:, :LANES]
    for j in range(hp):
        s_scr[j] = states[j]
    for buf in (qbuf, kbuf, vbuf):
        buf[0:8, :] = buf[cb:cb + 8, :]
    for (ci, j), o in zip(items, outs):
        o_ref[rows(ci), head_lanes[j]] = (_rms(o, nw) * _silu(z_ref[rows(ci), head_lanes[j]])).astype(BF16)

    @pl.when(blk == pl.num_programs(2) - 1)
    def _():
        s_ref[0] = s_scr[...]


def _gdn_prompt(proj, ba, conv_w, alog_pad, dtb_pad, norm_w, batch, seq, heads, cb):
    m = proj.shape[0]
    nblk = seq // cb
    hp = GDN_HEADS_PER_STEP
    assert heads % hp == 0
    groups = heads // hp
    rows = lambda b, h, c: b * nblk + c
    head_blk = lambda part: pl.BlockSpec((cb, hp * LANES), lambda b, h, c: (rows(b, h, c), part * groups + h))
    cw_blk = lambda part: pl.BlockSpec((CONV_WIDTH, hp * LANES), lambda b, h, c: (0, part * groups + h))
    row128 = pl.BlockSpec((1, LANES), lambda b, h, c: (0, 0))
    return pl.pallas_call(
        functools.partial(_gdn_prompt_kernel, heads=heads),
        grid=(batch, groups, nblk),
        in_specs=[head_blk(0), head_blk(1), head_blk(2), head_blk(3),
                  pl.BlockSpec((cb, LANES), lambda b, h, c: (rows(b, h, c), 0)),
                  cw_blk(0), cw_blk(1), cw_blk(2), row128, row128, row128],
        out_specs=[pl.BlockSpec((cb, hp * LANES), lambda b, h, c: (rows(b, h, c), h)),
                   pl.BlockSpec((1, hp, LANES, LANES), lambda b, h, c: (b, h, 0, 0))],
        out_shape=[jax.ShapeDtypeStruct((m, heads * LANES), BF16),
                   jax.ShapeDtypeStruct((batch, heads, LANES, LANES), F32)],
        scratch_shapes=[pltpu.VMEM((cb + 8, hp * LANES), F32)] * 3 + [pltpu.VMEM((hp, LANES, LANES), F32)],
        compiler_params=_params(("arbitrary", "arbitrary", "arbitrary")),
    )(proj, proj, proj, proj, ba, conv_w, conv_w, conv_w, alog_pad, dtb_pad, norm_w.reshape(1, LANES))


def _gdn_step_kernel(q_ref, k_ref, v_ref, z_ref, ba_ref, cq_ref, ck_ref, cv_ref, cwq_ref, cwk_ref, cwv_ref,
                     alog_ref, dtb_ref, nw_ref, s_ref, o_ref, so_ref, *, heads):
    h = pl.program_id(1)
    tb = q_ref.shape[0]

    def conv(x_ref, c_ref, w_ref):
        w = w_ref[...]
        y = c_ref[0] * w[0:1]
        y = y + c_ref[1] * w[1:2]
        y = y + c_ref[2] * w[2:3]
        return y + x_ref[...] * w[3:4]

    q = _l2n(_silu(conv(q_ref, cq_ref, cwq_ref))) * (LANES ** -0.5)
    k = _l2n(_silu(conv(k_ref, ck_ref, cwk_ref)))
    v = _silu(conv(v_ref, cv_ref, cwv_ref))
    beta, g = _gdn_gates(ba_ref[...], alog_ref[...], dtb_ref[...], h, heads)
    eg = jnp.exp(g)
    outs = []
    group = 4
    for b0 in range(0, tb, group):
        rows = range(b0, min(b0 + group, tb))
        kcols = [_col_from_row(k[b:b + 1]) for b in rows]
        qcols = [_col_from_row(q[b:b + 1]) for b in rows]
        for b, kcol, qcol in zip(rows, kcols, qcols):
            r = slice(b, b + 1)
            s = s_ref[b, 0] * eg[r]
            v_new = beta[r] * (v[r] - jnp.sum(kcol * s, axis=0, keepdims=True))
            s = s + kcol * v_new
            so_ref[b, 0] = s
            outs.append(jnp.sum(qcol * s, axis=0, keepdims=True))
    o = jnp.concatenate(outs, axis=0)
    o_ref[...] = (_rms(o, nw_ref[...]) * _silu(z_ref[...])).astype(BF16)


def _gdn_step(proj, ba, conv_state_t, conv_w, alog_pad, dtb_pad, norm_w, state, layer, heads, tb):
    bsz = proj.shape[0]
    head_blk = lambda off: pl.BlockSpec((tb, LANES), lambda i, h: (i, off + h))
    cs_blk = lambda off: pl.BlockSpec((CONV_WIDTH - 1, tb, LANES), lambda i, h: (0, i, off + h))
    cw_blk = lambda off: pl.BlockSpec((CONV_WIDTH, LANES), lambda i, h: (0, off + h))
    row128 = pl.BlockSpec((1, LANES), lambda i, h: (0, 0))
    st_in = pl.BlockSpec((None, tb, 1, LANES, LANES), lambda i, h: (layer, i, h, 0, 0))
    st_out = pl.BlockSpec((tb, 1, LANES, LANES), lambda i, h: (i, h, 0, 0))
    return pl.pallas_call(
        functools.partial(_gdn_step_kernel, heads=heads),
        grid=(bsz // tb, heads),
        in_specs=[head_blk(0), head_blk(heads), head_blk(2 * heads), head_blk(3 * heads),
                  pl.BlockSpec((tb, LANES), lambda i, h: (i, 0)),
                  cs_blk(0), cs_blk(heads), cs_blk(2 * heads),
                  cw_blk(0), cw_blk(heads), cw_blk(2 * heads), row128, row128, row128, st_in],
        out_specs=[pl.BlockSpec((tb, LANES), lambda i, h: (i, h)), st_out],
        out_shape=[jax.ShapeDtypeStruct((bsz, heads * LANES), BF16),
                   jax.ShapeDtypeStruct(state.shape[1:], F32)],
        compiler_params=_params(("arbitrary", "arbitrary")),
    )(proj, proj, proj, proj, ba, conv_state_t, conv_state_t, conv_state_t, conv_w, conv_w, conv_w,
      alog_pad, dtb_pad, norm_w.reshape(1, LANES), state)


def _lru_gates(x, wa_ref, wi_ref, ba, bi, lam):
    nb = wa_ref.shape[0]
    ga, gi = [], []
    for s in range(nb):
        xs = x[:, s * LANES:(s + 1) * LANES].astype(BF16)
        ga.append(jnp.dot(xs, wa_ref[s], preferred_element_type=F32))
        gi.append(jnp.dot(xs, wi_ref[s], preferred_element_type=F32))
    gate_a = _sigmoid(jnp.concatenate(ga, axis=-1) + ba)
    gate_i = _sigmoid(jnp.concatenate(gi, axis=-1) + bi)
    log_a = -LRU_C * gate_a * _softplus(-lam)
    return log_a, gate_i


def _lru_prompt_kernel(xl_ref, yl_ref, cw_ref, cb_ref, wa_ref, wi_ref, ba_ref, bi_ref, lam_ref,
                       o_ref, hl_ref, xbuf, abuf, bbuf, h_scr):
    blk = pl.program_id(1)
    cb = xl_ref.shape[0]
    width = xl_ref.shape[1]

    @pl.when(blk == 0)
    def _():
        h_scr[...] = jnp.zeros_like(h_scr)
        xbuf[0:8, :] = jnp.zeros((8, width), F32)

    x = _conv_block(xl_ref, xbuf, cw_ref[...]) + cb_ref[...]
    log_a, gate_i = _lru_gates(x, wa_ref, wi_ref, ba_ref[...], bi_ref[...], lam_ref[...])
    mult = jnp.sqrt(_neg_expm1(2.0 * log_a))
    first = (lax.broadcasted_iota(jnp.int32, (cb, 1), 0) == 0) & (blk == 0)
    mult = jnp.where(first, 1.0, mult)
    abuf[...] = jnp.exp(log_a)
    bbuf[...] = mult * gate_i * x

    def step(t, h):
        h = abuf[pl.ds(t, 1), :] * h + bbuf[pl.ds(t, 1), :]
        bbuf[pl.ds(t, 1), :] = h
        return h

    h_last = lax.fori_loop(0, cb, step, h_scr[...], unroll=8)
    h_scr[...] = h_last
    o_ref[...] = (bbuf[...] * _gelu_tanh(yl_ref[...])).astype(BF16)

    @pl.when(blk == pl.num_programs(1) - 1)
    def _():
        hl_ref[0] = h_last


def _lru_prompt(xy, conv_w, conv_b, w_a, w_i, b_a, b_i, lam, batch, seq, cb):
    m = xy.shape[0]
    width = xy.shape[1] // 2
    nblk = seq // cb
    nb = w_a.shape[0]
    vec = pl.BlockSpec((1, width), lambda b, c: (0, 0))
    wblk = pl.BlockSpec((nb, LANES, LANES), lambda b, c: (0, 0, 0))
    out, h_last = pl.pallas_call(
        _lru_prompt_kernel,
        grid=(batch, nblk),
        in_specs=[pl.BlockSpec((cb, width), lambda b, c: (b * nblk + c, 0)),
                  pl.BlockSpec((cb, width), lambda b, c: (b * nblk + c, 1)),
                  pl.BlockSpec((CONV_WIDTH, width), lambda b, c: (0, 0)),
                  vec, wblk, wblk, vec, vec, vec],
        out_specs=[pl.BlockSpec((cb, width), lambda b, c: (b * nblk + c, 0)),
                   pl.BlockSpec((1, 1, width), lambda b, c: (b, 0, 0))],
        out_shape=[jax.ShapeDtypeStruct((m, width), BF16),
                   jax.ShapeDtypeStruct((batch, 1, width), F32)],
        scratch_shapes=[pltpu.VMEM((cb + 8, width), F32), pltpu.VMEM((cb, width), F32),
                        pltpu.VMEM((cb, width), F32), pltpu.VMEM((1, width), F32)],
        compiler_params=_params(("arbitrary", "arbitrary")),
    )(xy, xy, conv_w, conv_b.reshape(1, width), w_a, w_i, b_a.reshape(1, width), b_i.reshape(1, width),
      lam.reshape(1, width))
    return out, h_last.reshape(batch, width)


def _lru_step_kernel(xl_ref, yl_ref, cs_ref, h0_ref, cw_ref, cb_ref, wa_ref, wi_ref, ba_ref, bi_ref, lam_ref,
                     o_ref, h_ref, *, reset):
    w = cw_ref[...]
    x = cs_ref[0] * w[0:1]
    x = x + cs_ref[1] * w[1:2]
    x = x + cs_ref[2] * w[2:3]
    x = x + xl_ref[...] * w[3:4] + cb_ref[...]
    log_a, gate_i = _lru_gates(x, wa_ref, wi_ref, ba_ref[...], bi_ref[...], lam_ref[...])
    mult = 1.0 if reset else jnp.sqrt(_neg_expm1(2.0 * log_a))
    h = jnp.exp(log_a) * h0_ref[...] + mult * gate_i * x
    h_ref[...] = h
    o_ref[...] = (h * _gelu_tanh(yl_ref[...])).astype(BF16)


def _lru_step(xy, conv_state_t, h0, conv_w, conv_b, w_a, w_i, b_a, b_i, lam, reset):
    bsz = xy.shape[0]
    width = xy.shape[1] // 2
    nb = w_a.shape[0]
    vec = pl.BlockSpec((1, width), lambda i: (0, 0))
    wblk = pl.BlockSpec((nb, LANES, LANES), lambda i: (0, 0, 0))
    full = pl.BlockSpec((bsz, width), lambda i: (0, 0))
    return pl.pallas_call(
        functools.partial(_lru_step_kernel, reset=reset),
        grid=(1,),
        in_specs=[full, pl.BlockSpec((bsz, width), lambda i: (0, 1)),
                  pl.BlockSpec((CONV_WIDTH - 1, bsz, width), lambda i: (0, 0, 0)), full,
                  pl.BlockSpec((CONV_WIDTH, width), lambda i: (0, 0)), vec, wblk, wblk, vec, vec, vec],
        out_specs=[full, full],
        out_shape=[jax.ShapeDtypeStruct((bsz, width), BF16), jax.ShapeDtypeStruct((bsz, width), F32)],
        compiler_params=_params(("arbitrary",)),
    )(xy, xy, conv_state_t, h0, conv_w, conv_b.reshape(1, width), w_a, w_i, b_a.reshape(1, width),
      b_i.reshape(1, width), lam.reshape(1, width))


def _hgrn_lower_bound(lb_ref, layer, j=0):
    depth = lb_ref.shape[0]
    raw = [lb_ref[l, j] for l in range(depth)]
    mx = raw[0]
    for r in raw[1:]:
        mx = jnp.maximum(mx, r)
    ex = [jnp.exp(r - mx) for r in raw]
    tot = ex[0]
    for e in ex[1:]:
        tot = tot + e
    wts = [e / tot for e in ex]
    cum = wts[0]
    for w in wts[1:layer + 1]:
        cum = cum + w
    return cum - wts[0]


def _hgrn_inputs(q_raw, f_raw, lb):
    q = _silu(q_raw) * (LANES ** -0.5)
    f = lb + (1.0 - lb) * _sigmoid(f_raw)
    return q, f


def _level_ref(g, half):
    c = g.shape[0]
    sub = 8
    if 2 * half >= sub:
        g3 = g.reshape(c // (2 * half), 2 * half, LANES)
        return jnp.broadcast_to(g3[:, half - 1:half, :], g3.shape).reshape(c, LANES)
    g3 = g.reshape(c // sub, sub, LANES)
    rin = lax.broadcasted_iota(jnp.int32, g3.shape, 1)
    out = jnp.broadcast_to(g3[:, sub - half - 1:sub - half, :], g3.shape)
    for start in range(sub - 4 * half, -1, -2 * half):
        out = jnp.where(rin < start + 2 * half, jnp.broadcast_to(g3[:, start + half - 1:start + half, :], g3.shape),
                        out)
    return out.reshape(c, LANES)


def _hgrn_level_table(c):
    i = np.arange(c)[:, None]
    j = np.arange(c)[None, :]
    nlev = int(math.log2(c))
    top_bit = sum(((i ^ j) >> b > 0).astype(np.int32) for b in range(1, nlev))
    return jnp.asarray(np.where(j < i, nlev - 1 - top_bit, np.where(j == i, nlev, -1)), jnp.int32)


def _hgrn_prompt_kernel(q_ref, f_ref, i_ref, gz_ref, lb_ref, nw_ref, lvl_ref, o_ref, s_ref, s_scr, *, layer):
    hp = s_scr.shape[0]
    blk = pl.program_id(2)
    cb = q_ref.shape[0]
    c = CHUNK

    @pl.when(blk == 0)
    def _():
        s_scr[...] = jnp.zeros_like(s_scr)

    lbs = [_hgrn_lower_bound(lb_ref, layer, j) for j in range(hp)]
    head_lanes = [slice(j * LANES, (j + 1) * LANES) for j in range(hp)]
    tri = (_iota2(c, 1) <= _iota2(c, 0)).astype(BF16)
    nw = nw_ref[...]
    lvl = lvl_ref[...]
    halves = [c >> (i + 1) for i in range(int(math.log2(c)))]

    items = [(slice(ci * c, (ci + 1) * c), j) for ci in range(cb // c) for j in range(hp)]
    qf = [_hgrn_inputs(q_ref[sl, head_lanes[j]], f_ref[sl, head_lanes[j]], lbs[j]) for sl, j in items]
    qs = [q for q, _ in qf]
    ks = [1.0 - f for _, f in qf]
    vs = [i_ref[sl, head_lanes[j]] for sl, j in items]
    gs = [_dot_exact_lhs(tri, jnp.log(f)) for _, f in qf]

    amats = [_dot_nt(q, k) for q, k in zip(qs, ks)]
    for li, half in enumerate(halves):
        es = [jnp.exp(-jnp.abs(g - _level_ref(g, half))) for g in gs]
        ps = [_dot_nt(q * e, k * e) for q, k, e in zip(qs, ks, es)]
        amats = [jnp.where(lvl == li, p, a) for p, a in zip(ps, amats)]
    intra = [_dot(jnp.where(lvl >= 0, a, 0.0), v) for a, v in zip(amats, vs)]

    g_lasts = [g[c - 1:c, :] for g in gs]
    qdec = [q * jnp.exp(g) for q, g in zip(qs, gs)]
    sdec = [_col_from_row(jnp.exp(gl)) for gl in g_lasts]
    sadd = [_dot_tn(k * jnp.exp(gl - g), v) for k, gl, g, v in zip(ks, g_lasts, gs, vs)]

    states = [s_scr[j] for j in range(hp)]
    outs = []
    for (sl, j), o, qd, dec, add in zip(items, intra, qdec, sdec, sadd):
        outs.append(o + _dot(qd, states[j]))
        states[j] = states[j] * dec + add
    for j in range(hp):
        s_scr[j] = states[j]
    for (sl, j), o in zip(items, outs):
        o_ref[sl, head_lanes[j]] = (_rms(o, nw) * _silu(gz_ref[sl, head_lanes[j]])).astype(BF16)

    @pl.when(blk == pl.num_programs(2) - 1)
    def _():
        s_ref[0] = s_scr[...]


def _hgrn_prompt(proj, lb_raw, norm_w, layer, batch, seq, heads, cb):
    m = proj.shape[0]
    nblk = seq // cb
    depth = lb_raw.shape[0]
    hp = HGRN_HEADS_PER_STEP
    assert heads % hp == 0
    groups = heads // hp
    head_blk = lambda part: pl.BlockSpec((cb, hp * LANES), lambda b, h, c: (b * nblk + c, part * groups + h))
    return pl.pallas_call(
        functools.partial(_hgrn_prompt_kernel, layer=layer),
        grid=(batch, groups, nblk),
        in_specs=[head_blk(0), head_blk(1), head_blk(2), head_blk(3),
                  pl.BlockSpec((depth, hp, 1, LANES), lambda b, h, c: (0, h, 0, 0)),
                  pl.BlockSpec((1, LANES), lambda b, h, c: (0, 0)),
                  pl.BlockSpec((CHUNK, CHUNK), lambda b, h, c: (0, 0))],
        out_specs=[pl.BlockSpec((cb, hp * LANES), lambda b, h, c: (b * nblk + c, h)),
                   pl.BlockSpec((1, hp, LANES, LANES), lambda b, h, c: (b, h, 0, 0))],
        out_shape=[jax.ShapeDtypeStruct((m, heads * LANES), BF16),
                   jax.ShapeDtypeStruct((batch, heads, LANES, LANES), F32)],
        scratch_shapes=[pltpu.VMEM((hp, LANES, LANES), F32)],
        compiler_params=_params(("arbitrary", "arbitrary", "arbitrary")),
    )(proj, proj, proj, proj, lb_raw.reshape(depth, heads, 1, LANES), norm_w.reshape(1, LANES),
      _hgrn_level_table(CHUNK))


def _hgrn_step_kernel(q_ref, f_ref, i_ref, gz_ref, lb_ref, nw_ref, s_ref, o_ref, so_ref, *, layer):
    tb = q_ref.shape[0]
    lb = _hgrn_lower_bound(lb_ref, layer)
    q, f = _hgrn_inputs(q_ref[...], f_ref[...], lb)
    v = i_ref[...]
    outs = []
    group = 4
    for b0 in range(0, tb, group):
        rows = range(b0, min(b0 + group, tb))
        fcols = [_col_from_row(f[b:b + 1]) for b in rows]
        qcols = [_col_from_row(q[b:b + 1]) for b in rows]
        for b, fcol, qcol in zip(rows, fcols, qcols):
            s = fcol * (s_ref[b, 0] - v[b:b + 1]) + v[b:b + 1]
            so_ref[b, 0] = s
            outs.append(jnp.sum(qcol * s, axis=0, keepdims=True))
    o = jnp.concatenate(outs, axis=0)
    o_ref[...] = (_rms(o, nw_ref[...]) * _silu(gz_ref[...])).astype(BF16)


def _hgrn_step(proj, lb_raw, norm_w, state, state_idx, layer, heads, tb):
    bsz = proj.shape[0]
    depth = lb_raw.shape[0]
    head_blk = lambda off: pl.BlockSpec((tb, LANES), lambda i, h: (i, off + h))
    st_in = pl.BlockSpec((None, tb, 1, LANES, LANES), lambda i, h: (state_idx, i, h, 0, 0))
    st_out = pl.BlockSpec((tb, 1, LANES, LANES), lambda i, h: (i, h, 0, 0))
    return pl.pallas_call(
        functools.partial(_hgrn_step_kernel, layer=layer),
        grid=(bsz // tb, heads),
        in_specs=[head_blk(0), head_blk(heads), head_blk(2 * heads), head_blk(3 * heads),
                  pl.BlockSpec((depth, 1, 1, LANES), lambda i, h: (0, h, 0, 0)),
                  pl.BlockSpec((1, LANES), lambda i, h: (0, 0)), st_in],
        out_specs=[pl.BlockSpec((tb, LANES), lambda i, h: (i, h)), st_out],
        out_shape=[jax.ShapeDtypeStruct((bsz, heads * LANES), BF16),
                   jax.ShapeDtypeStruct(state.shape[1:], F32)],
        compiler_params=_params(("arbitrary", "arbitrary")),
    )(proj, proj, proj, proj, lb_raw.reshape(depth, heads, 1, LANES), norm_w.reshape(1, LANES), state)


def _trunk(x, mem_k, mem_v, states, wts, big, batch, seq, prompt):
    depth = wts["norm_mix"].shape[0]
    gdn_w = wts["gdn_conv_w"].shape[-1] // 3
    gdn_heads = gdn_w // LANES
    hgrn_heads = wts["hgrn_lb_raw"].shape[1] // LANES
    lru_w = wts["lru_conv_w"].shape[-1]
    bf = {name: list(handles) for name, handles in big.items()}
    new = {"gdn_conv": [], "gdn": [], "lru_conv": [], "lru": [], "hgrn": []}

    def keep(name, idx, w_used):
        w, li = bf[name][idx]
        bf[name][idx] = (w_used, li if w_used is w else 0)

    for l in range(depth):
        g_mix = wts["norm_mix"][l]
        if l % 2 == 0:
            e = l // 2
            proj, _ = _norm_matmul(x, g_mix, wts["ab_head"], e, 0, 4 * gdn_w, TM, TN)
            ba, _ = _norm_matmul(x, g_mix, wts["ab_head"], e, 4 * gdn_w, LANES, TM, LANES)
            xy, _ = _norm_matmul(x, g_mix, wts["ab_lru"], e, 0, 2 * lru_w, TM, TN)
            gargs = (wts["gdn_conv_w"][e], wts["alog_pad"][e], wts["dtb_pad"][e], wts["gdn_norm_w"][e])
            largs = (wts["lru_conv_w"][e], wts["lru_conv_b"][e], wts["lru_w_a"][e], wts["lru_w_i"][e],
                     wts["lru_b_a"][e], wts["lru_b_i"][e], wts["lru_lam"][e])
            w_out, li = bf["ab_w_out"][e]
            if prompt:
                o_a, s_new = _gdn_prompt(proj, ba, *gargs, batch, seq, gdn_heads, 512)
                o_b, h_new = _lru_prompt(xy, *largs, batch, seq, 256)
                tail = lambda t, w: t.reshape(batch, seq, -1)[:, seq - (CONV_WIDTH - 1):, :w]
                new["gdn_conv"].append(tail(proj, 3 * gdn_w))
                new["lru_conv"].append(tail(xy, lru_w))
                x = _matmul2_res(o_a, o_b, w_out, li, x, TM, TN)
            else:
                gc_state, lc_state = states["gdn_conv"][e], states["lru_conv"][e]
                o_a, s_new = _gdn_step(proj, ba, jnp.swapaxes(gc_state, 0, 1), *gargs, states["gdn"], e,
                                       gdn_heads, 32)
                o_b, h_new = _lru_step(xy, jnp.swapaxes(lc_state, 0, 1), states["lru"][e], *largs, reset=False)
                new["gdn_conv"].append(jnp.concatenate([gc_state[:, 1:], proj[:, None, :3 * gdn_w]], axis=1))
                new["lru_conv"].append(jnp.concatenate([lc_state[:, 1:], xy[:, None, :lru_w]], axis=1))
                x, w_used = _matmul_res(jnp.concatenate([o_a, o_b], axis=1), w_out, li, x, TM, TN)
                keep("ab_w_out", e, w_used)
            new["gdn"].append(s_new)
            new["lru"].append(h_new)
        else:
            o_idx = l // 2
            w_in, li = bf["c_w_in"][o_idx]
            proj, w_used = _norm_matmul(x, g_mix, w_in, li, 0, w_in.shape[2], TM, TN)
            keep("c_w_in", o_idx, w_used)
            hargs = (wts["hgrn_lb_raw"], wts["hgrn_norm_w"][o_idx])
            if prompt:
                o_c, s_new = _hgrn_prompt(proj, *hargs, l, batch, seq, hgrn_heads, 512)
            else:
                o_c, s_new = _hgrn_step(proj, *hargs, states["hgrn"], o_idx, l, hgrn_heads, 32)
            new["hgrn"].append(s_new)
            w_out, li = bf["c_w_out"][o_idx]
            x, w_used = _matmul_res(o_c, w_out, li, x, TM, TN)
            keep("c_w_out", o_idx, w_used)
        (w_q, lq), (w_o, lo) = bf["mem_w_q"][l], bf["mem_w_o"][l]
        if prompt:
            assert lq == lo
            x = _mem_attn_prompt(x, wts["norm_mem"][l], w_q, mem_k, mem_v, w_o, l, lq, seq, 256)
        else:
            q, w_used = _norm_matmul(x, wts["norm_mem"][l], w_q, lq, 0, w_q.shape[2], TM, TN)
            keep("mem_w_q", l, w_used)
            o = _mem_attn_sample(q, mem_k, mem_v, l, 8)
            x, w_used = _matmul_res(o.reshape(batch, -1).astype(BF16), w_o, lo, x, TM, TN)
            keep("mem_w_o", l, w_used)
        (w_up, lu), (w_dn, ld) = bf["ffn_w_up"][l], bf["ffn_w_down"][l]
        assert lu == ld
        x, up_used, dn_used = _ffn(x, wts["norm_ffn"][l], w_up, w_dn, lu, wts["norm_final"], l == depth - 1,
                                   TM, TF)
        keep("ffn_w_up", l, up_used)
        keep("ffn_w_down", l, dn_used)
    return x, {n: jnp.stack(v) for n, v in new.items()}, bf


def kernel(x_prompt, x_sample, cache_mem_k, cache_mem_v, state_gdn_conv, state_gdn, state_lru_conv, state_lru, state_hgrn, mem_prompt, norm_mix, norm_mem, norm_mem_kv, norm_ffn, norm_final, ab_w_in, ab_w_out, gdn_conv_w, gdn_a_log, gdn_dt_bias, gdn_norm_w, lru_conv_w, lru_conv_b, lru_w_a, lru_b_a, lru_w_i, lru_b_i, lru_lam, c_w_in, c_w_out, hgrn_lb_raw, hgrn_norm_w, mem_w_q, mem_w_k, mem_w_v, mem_w_o, ffn_w_up, ffn_w_down):
    bp, lp, d = x_prompt.shape
    bs, ls, _ = x_sample.shape
    assert ls == 1, "the sample group advances one token per call"
    depth = norm_mix.shape[0]
    gdn_heads = gdn_a_log.shape[1]
    gdn_w = gdn_heads * LANES
    mem_tokens, mem_heads, mem_hd = cache_mem_k.shape[2:]
    mem_w = mem_heads * mem_hd
    n_ba = 2 * gdn_heads
    assert n_ba <= LANES

    pad_ba = lambda a: jnp.pad(a, ((0, 0), (gdn_heads, LANES - n_ba)))[:, None, :]
    ab_bf = ab_w_in.astype(BF16)
    wts = dict(
        norm_mix=norm_mix, norm_mem=norm_mem, norm_ffn=norm_ffn, norm_final=norm_final,
        ab_head=ab_bf, ab_lru=ab_bf[:, :, 4 * gdn_w + n_ba:],
        gdn_conv_w=gdn_conv_w, alog_pad=pad_ba(gdn_a_log), dtb_pad=pad_ba(gdn_dt_bias), gdn_norm_w=gdn_norm_w,
        lru_conv_w=lru_conv_w, lru_conv_b=lru_conv_b, lru_w_a=lru_w_a.astype(BF16), lru_w_i=lru_w_i.astype(BF16),
        lru_b_a=lru_b_a, lru_b_i=lru_b_i, lru_lam=lru_lam,
        hgrn_lb_raw=hgrn_lb_raw, hgrn_norm_w=hgrn_norm_w,
    )
    big = dict(ab_w_out=ab_w_out, c_w_in=c_w_in, c_w_out=c_w_out, mem_w_q=mem_w_q, mem_w_o=mem_w_o,
               ffn_w_up=ffn_w_up, ffn_w_down=ffn_w_down)
    big = {name: [(w, l) for l in range(w.shape[0])] for name, w in big.items()}

    states = dict(gdn_conv=state_gdn_conv, gdn=state_gdn, lru_conv=state_lru_conv, lru=state_lru,
                  hgrn=state_hgrn)
    y_s, new_s, big_bf = _trunk(x_sample.reshape(bs * ls, d), cache_mem_k, cache_mem_v, states, wts, big,
                                bs, ls, False)

    mem_rows = mem_prompt.reshape(bp * mem_tokens, d)
    w_kv = jnp.concatenate([mem_w_k, mem_w_v], axis=-1).astype(BF16)
    kv = jnp.stack([_norm_matmul(mem_rows, norm_mem_kv[l], w_kv, l, 0, 2 * mem_w, TM, TN)[0] for l in range(depth)])
    p_mem_k = kv[:, :, :mem_w].reshape(depth, bp, mem_tokens, mem_w)
    p_mem_v = kv[:, :, mem_w:].reshape(depth, bp, mem_tokens, mem_w)

    y_p, new_p, _ = _trunk(x_prompt.reshape(bp * lp, d), p_mem_k, p_mem_v, None, wts, big_bf, bp, lp, True)

    order = ("gdn_conv", "gdn", "lru_conv", "lru", "hgrn")
    mem5 = lambda t: t.reshape(depth, bp, mem_tokens, mem_heads, mem_hd)
    return (y_p.reshape(bp, lp, d), y_s.reshape(bs, ls, d), mem5(p_mem_k), mem5(p_mem_v),
            *(new_p[n] for n in order), *(new_s[n] for n in order))
```

```python
import functools
import math

import jax
import jax.numpy as jnp
import numpy as np
from jax import lax
from jax.experimental import pallas as pl
from jax.experimental.pallas import tpu as pltpu

F32 = jnp.float32
BF16 = jnp.bfloat16
EPS = 1e-6
LANES = 128
CONV_WIDTH = 4
LRU_C = 8.0
VMEM_LIMIT_BYTES = 56 * 1024 * 1024
TM = 512
TN = 1024
TF = 512
TN_CAST = 512
TF_CAST = 256
ROW_GROUPS = 4
CHUNK = 128
GDN_HEADS_PER_STEP = 2
HGRN_HEADS_PER_STEP = 2


def _params(sem):
    return pltpu.CompilerParams(dimension_semantics=sem, vmem_limit_bytes=VMEM_LIMIT_BYTES)


def _sigmoid(x):
    return jax.nn.sigmoid(x)


def _silu(x):
    return x * _sigmoid(x)


def _softplus(x):
    return jnp.maximum(x, 0.0) + jnp.log1p(jnp.exp(-jnp.abs(x)))


def _neg_expm1(x):
    t = jnp.tanh(0.5 * x)
    return -2.0 * t / (1.0 - t)


def _gelu_tanh(x):
    c = math.sqrt(2.0 / math.pi)
    return 0.5 * x * (1.0 + jnp.tanh(c * (x + 0.044715 * (x * x * x))))


def _rms(x, w):
    return x * lax.rsqrt(jnp.mean(x * x, axis=-1, keepdims=True) + EPS) * w


def _l2n(x):
    return x * lax.rsqrt(jnp.sum(x * x, axis=-1, keepdims=True) + EPS)


def _dot(a, b):
    return jnp.dot(a.astype(BF16), b.astype(BF16), preferred_element_type=F32)


def _dot_nt(a, b):
    return lax.dot_general(a.astype(BF16), b.astype(BF16), (((1,), (1,)), ((), ())),
                           preferred_element_type=F32)


def _dot_tn(a, b):
    return lax.dot_general(a.astype(BF16), b.astype(BF16), (((0,), (0,)), ((), ())),
                           preferred_element_type=F32)


def _split2(a):
    hi = a.astype(BF16)
    lo = (a - hi.astype(F32)).astype(BF16)
    return hi, lo


def _dot_hp(a_parts, b_parts):
    ah, al = a_parts
    bh, bl = b_parts
    return jnp.dot(jnp.concatenate([ah, ah, al], axis=1), jnp.concatenate([bh, bl, bh], axis=0),
                   preferred_element_type=F32)


def _dot_exact_lhs(t, b):
    b1 = b.astype(BF16)
    r1 = b - b1.astype(F32)
    b2 = r1.astype(BF16)
    b3 = (r1 - b2.astype(F32)).astype(BF16)
    return jnp.dot(jnp.concatenate([t, t, t], axis=1), jnp.concatenate([b1, b2, b3], axis=0),
                   preferred_element_type=F32)


def _iota2(n, axis):
    return lax.broadcasted_iota(jnp.int32, (n, n), axis)


def _col_from_row(row):
    n = row.shape[-1]
    return jnp.broadcast_to(row, (n, n)).T


def _row_groups(rows):
    n = ROW_GROUPS if rows % (8 * ROW_GROUPS) == 0 else 1
    return [slice(r * rows // n, (r + 1) * rows // n) for r in range(n)]


def _norm_matmul_kernel(x_ref, g_ref, w_ref, o_ref, *rest):
    xn_ref = rest[-1]
    if len(rest) == 2:
        rest[0][...] = w_ref[...].astype(BF16)
        w_ref = rest[0]
    first = pl.program_id(1) == 0

    @pl.when(first)
    def _():
        for rows in _row_groups(x_ref.shape[0]):
            xn = _rms(x_ref[rows, :], g_ref[...]).astype(BF16)
            xn_ref[rows, :] = xn
            o_ref[rows, :] = jnp.dot(xn, w_ref[...], preferred_element_type=F32)

    @pl.when(jnp.logical_not(first))
    def _():
        o_ref[...] = jnp.dot(xn_ref[...], w_ref[...], preferred_element_type=F32)


def _norm_matmul(x, g, w, layer, n0, n, tm, tn):
    m, k = x.shape
    cast = w.dtype != BF16
    tm = min(tm, m)
    tn = min(tn, n, TN_CAST) if cast else min(tn, n)
    j0 = n0 // tn
    out_specs = [pl.BlockSpec((tm, tn), lambda i, j: (i, j))]
    out_shape = [jax.ShapeDtypeStruct((m, n), F32)]
    if cast:
        assert m == tm and n0 == 0 and n == w.shape[2], "each weight tile must be visited exactly once"
        out_specs.append(pl.BlockSpec((None, k, tn), lambda i, j: (0, 0, j)))
        out_shape.append(jax.ShapeDtypeStruct((1, k, n), BF16))
    res = pl.pallas_call(
        _norm_matmul_kernel,
        grid=(m // tm, n // tn),
        in_specs=[pl.BlockSpec((tm, k), lambda i, j: (i, 0)),
                  pl.BlockSpec((1, k), lambda i, j: (0, 0)),
                  pl.BlockSpec((None, k, tn), lambda i, j: (layer, 0, j0 + j))],
        out_specs=out_specs,
        out_shape=out_shape,
        scratch_shapes=[pltpu.VMEM((tm, k), BF16)],
        compiler_params=_params(("arbitrary", "arbitrary")),
    )(x, g.reshape(1, k), w)
    return (res[0], res[1]) if cast else (res[0], w)


def _matmul2_res_kernel(a1_ref, a2_ref, w1_ref, w2_ref, r_ref, o_ref):
    acc = jnp.dot(a1_ref[...], w1_ref[...], preferred_element_type=F32)
    acc = acc + jnp.dot(a2_ref[...], w2_ref[...], preferred_element_type=F32)
    o_ref[...] = r_ref[...] + acc


def _matmul2_res(a1, a2, w, layer, res, tm, tn):
    m, kh = a1.shape
    n = w.shape[2]
    tm = min(tm, m)
    tn = min(tn, n)
    return pl.pallas_call(
        _matmul2_res_kernel,
        grid=(m // tm, n // tn),
        in_specs=[pl.BlockSpec((tm, kh), lambda i, j: (i, 0)),
                  pl.BlockSpec((tm, kh), lambda i, j: (i, 0)),
                  pl.BlockSpec((None, kh, tn), lambda i, j: (layer, 0, j)),
                  pl.BlockSpec((None, kh, tn), lambda i, j: (layer, 1, j)),
                  pl.BlockSpec((tm, tn), lambda i, j: (i, j))],
        out_specs=pl.BlockSpec((tm, tn), lambda i, j: (i, j)),
        out_shape=jax.ShapeDtypeStruct((m, n), F32),
        compiler_params=_params(("arbitrary", "arbitrary")),
    )(a1, a2, w, w, res)


def _matmul_res_kernel(a_ref, w_ref, r_ref, o_ref, *wb_ref):
    w = w_ref[...]
    if wb_ref:
        w = w.astype(BF16)
        wb_ref[0][...] = w
    o_ref[...] = r_ref[...] + jnp.dot(a_ref[...], w, preferred_element_type=F32)


def _matmul_res(a, w, layer, res, tm, tn):
    m, k = a.shape
    n = w.shape[2]
    cast = w.dtype != BF16
    tm = min(tm, m)
    tn = min(tn, n, TN_CAST) if cast else min(tn, n)
    out_specs = [pl.BlockSpec((tm, tn), lambda i, j: (i, j))]
    out_shape = [jax.ShapeDtypeStruct((m, n), F32)]
    if cast:
        assert m == tm, "each weight tile must be visited exactly once"
        out_specs.append(pl.BlockSpec((None, k, tn), lambda i, j: (0, 0, j)))
        out_shape.append(jax.ShapeDtypeStruct((1, k, n), BF16))
    out = pl.pallas_call(
        _matmul_res_kernel,
        grid=(m // tm, n // tn),
        in_specs=[pl.BlockSpec((tm, k), lambda i, j: (i, 0)),
                  pl.BlockSpec((None, k, tn), lambda i, j: (layer, 0, j)),
                  pl.BlockSpec((tm, tn), lambda i, j: (i, j))],
        out_specs=out_specs,
        out_shape=out_shape,
        compiler_params=_params(("arbitrary", "arbitrary")),
    )(a, w, res)
    return (out[0], out[1]) if cast else (out[0], w)


def _ffn_kernel(x_ref, g_ref, wu_ref, wd_ref, gf_ref, o_ref, *rest, final_norm):
    xn_ref = rest[-1]
    f = pl.program_id(1)
    if len(rest) == 3:
        rest[0][...] = wu_ref[...].astype(BF16)
        rest[1][...] = wd_ref[...].astype(BF16)
        wu_ref, wd_ref = rest[0], rest[1]

    def mlp(xn):
        h = jnp.dot(xn, wu_ref[...], preferred_element_type=F32)
        h = jnp.square(jnp.maximum(h, 0.0)).astype(BF16)
        return jnp.dot(h, wd_ref[...], preferred_element_type=F32)

    @pl.when(f == 0)
    def _():
        for rows in _row_groups(x_ref.shape[0]):
            x = x_ref[rows, :]
            xn = _rms(x, g_ref[...]).astype(BF16)
            xn_ref[rows, :] = xn
            o_ref[rows, :] = x + mlp(xn)

    @pl.when(f != 0)
    def _():
        o_ref[...] += mlp(xn_ref[...])

    if final_norm:
        @pl.when(f == pl.num_programs(1) - 1)
        def _():
            o_ref[...] = _rms(o_ref[...], gf_ref[...])


def _ffn(x, g, w_up, w_down, layer, g_final, final_norm, tm, tf):
    m, d = x.shape
    dff = w_up.shape[2]
    cast = w_up.dtype != BF16
    tm = min(tm, m)
    tf = min(tf, TF_CAST) if cast else tf
    out_specs = [pl.BlockSpec((tm, d), lambda i, f: (i, 0))]
    out_shape = [jax.ShapeDtypeStruct((m, d), F32)]
    if cast:
        assert m == tm, "each weight tile must be visited exactly once"
        out_specs += [pl.BlockSpec((None, d, tf), lambda i, f: (0, 0, f)),
                      pl.BlockSpec((None, tf, d), lambda i, f: (0, f, 0))]
        out_shape += [jax.ShapeDtypeStruct((1, d, dff), BF16), jax.ShapeDtypeStruct((1, dff, d), BF16)]
    out = pl.pallas_call(
        functools.partial(_ffn_kernel, final_norm=final_norm),
        grid=(m // tm, dff // tf),
        in_specs=[pl.BlockSpec((tm, d), lambda i, f: (i, 0), pipeline_mode=pl.Buffered(1)),
                  pl.BlockSpec((1, d), lambda i, f: (0, 0)),
                  pl.BlockSpec((None, d, tf), lambda i, f: (layer, 0, f)),
                  pl.BlockSpec((None, tf, d), lambda i, f: (layer, f, 0)),
                  pl.BlockSpec((1, d), lambda i, f: (0, 0))],
        out_specs=out_specs,
        out_shape=out_shape,
        scratch_shapes=[pltpu.VMEM((tm, d), BF16)],
        compiler_params=_params(("arbitrary", "arbitrary")),
    )(x, g.reshape(1, d), w_up, w_down, g_final.reshape(1, d))
    return (out[0], out[1], out[2]) if cast else (out[0], w_up, w_down)


def _mem_attn_prompt_kernel(x_ref, g_ref, wq_ref, k_ref, v_ref, wo_ref, o_ref, *, heads):
    x = x_ref[...]
    xn = _rms(x, g_ref[...]).astype(BF16)
    q = jnp.dot(xn, wq_ref[...], preferred_element_type=F32)
    k = k_ref[0].astype(BF16)
    v = v_ref[0].astype(BF16)
    scale = LANES ** -0.5
    outs = []
    for h in range(heads):
        sl = slice(h * LANES, (h + 1) * LANES)
        s = _dot_nt(q[:, sl], k[:, sl]) * scale
        e = jnp.exp(s - jnp.max(s, axis=-1, keepdims=True))
        p = e / jnp.sum(e, axis=-1, keepdims=True)
        outs.append(_dot(p, v[:, sl]))
    o = jnp.concatenate(outs, axis=-1).astype(BF16)
    o_ref[...] = x + jnp.dot(o, wo_ref[...], preferred_element_type=F32)


def _mem_attn_prompt(x, g, w_q, mem_k, mem_v, w_o, layer, w_layer, seq, tl):
    m, d = x.shape
    _, _, t, w = mem_k.shape
    nblk = seq // tl
    return pl.pallas_call(
        functools.partial(_mem_attn_prompt_kernel, heads=w // LANES),
        grid=(m // tl,),
        in_specs=[pl.BlockSpec((tl, d), lambda i: (i, 0)),
                  pl.BlockSpec((1, d), lambda i: (0, 0)),
                  pl.BlockSpec((None, d, w), lambda i: (w_layer, 0, 0)),
                  pl.BlockSpec((None, 1, t, w), lambda i: (layer, i // nblk, 0, 0)),
                  pl.BlockSpec((None, 1, t, w), lambda i: (layer, i // nblk, 0, 0)),
                  pl.BlockSpec((None, w, d), lambda i: (w_layer, 0, 0))],
        out_specs=pl.BlockSpec((tl, d), lambda i: (i, 0)),
        out_shape=jax.ShapeDtypeStruct((m, d), F32),
        compiler_params=_params(("arbitrary",)),
    )(x, g.reshape(1, d), w_q, mem_k, mem_v, w_o)


def _mem_attn_sample_kernel(q_ref, k_ref, v_ref, o_ref, *, tb, heads):
    scale = LANES ** -0.5
    for b in range(tb):
        q = jnp.concatenate([q_ref[b:b + 1, h * LANES:(h + 1) * LANES] for h in range(heads)], axis=0)
        s = jnp.sum(k_ref[b] * (q * scale)[None], axis=-1, keepdims=True)
        e = jnp.exp(s - jnp.max(s, axis=0, keepdims=True))
        o_ref[b] = jnp.sum(e * v_ref[b], axis=0) / jnp.sum(e, axis=0)


def _mem_attn_sample(q, mem_k, mem_v, layer, tb):
    b = q.shape[0]
    _, _, t, heads, hd = mem_k.shape
    kv_blk = pl.BlockSpec((None, tb, t, heads, hd), lambda i: (layer, i, 0, 0, 0))
    return pl.pallas_call(
        functools.partial(_mem_attn_sample_kernel, tb=tb, heads=heads),
        grid=(b // tb,),
        in_specs=[pl.BlockSpec((tb, heads * hd), lambda i: (i, 0)), kv_blk, kv_blk],
        out_specs=pl.BlockSpec((tb, heads, hd), lambda i: (i, 0, 0)),
        out_shape=jax.ShapeDtypeStruct((b, heads, hd), F32),
        compiler_params=_params(("arbitrary",)),
    )(q, mem_k, mem_v)


def _conv_block(x_ref, buf, w):
    cb = x_ref.shape[0]
    buf[8:8 + cb, :] = x_ref[...]
    y = buf[5:5 + cb, :] * w[0:1]
    y = y + buf[6:6 + cb, :] * w[1:2]
    y = y + buf[7:7 + cb, :] * w[2:3]
    y = y + buf[8:8 + cb, :] * w[3:4]
    buf[0:8, :] = buf[cb:cb + 8, :]
    return y


def _gdn_gates(ba, alog, dtb, h, heads):
    lane = lax.broadcasted_iota(jnp.int32, ba.shape, 1)
    beta_all = _sigmoid(ba)
    g_all = -jnp.exp(alog) * _softplus(ba + dtb)
    beta = jnp.sum(jnp.where(lane == h, beta_all, 0.0), axis=1, keepdims=True)
    g = jnp.sum(jnp.where(lane == heads + h, g_all, 0.0), axis=1, keepdims=True)
    return beta, g


def _gdn_prompt_kernel(q_ref, k_ref, v_ref, z_ref, ba_ref, cwq_ref, cwk_ref, cwv_ref, alog_ref, dtb_ref,
                       nw_ref, o_ref, s_ref, qbuf, kbuf, vbuf, s_scr, *, heads):
    hp = s_scr.shape[0]
    blk = pl.program_id(2)
    cb = q_ref.shape[0]
    c = CHUNK
    n_chunks = cb // c

    @pl.when(blk == 0)
    def _():
        s_scr[...] = jnp.zeros_like(s_scr)
        for buf in (qbuf, kbuf, vbuf):
            buf[0:8, :] = jnp.zeros((8, hp * LANES), F32)

    for x_ref, buf in ((q_ref, qbuf), (k_ref, kbuf), (v_ref, vbuf)):
        buf[8:8 + cb, :] = x_ref[...]
    conv_w = (cwq_ref[...], cwk_ref[...], cwv_ref[...])
    ba = ba_ref[...]
    gates = [_gdn_gates(ba, alog_ref[...], dtb_ref[...], pl.program_id(1) * hp + j, heads) for j in range(hp)]
    head_lanes = [slice(j * LANES, (j + 1) * LANES) for j in range(hp)]

    row = _iota2(c, 0)
    col = _iota2(c, 1)
    causal = col <= row
    strict = col < row
    tri = causal.astype(BF16)
    eye = (col == row).astype(F32)
    nw = nw_ref[...]

    def conv_rows(buf, w, r0):
        y = buf[r0 + 5:r0 + 5 + c, :] * w[0:1]
        y = y + buf[r0 + 6:r0 + 6 + c, :] * w[1:2]
        y = y + buf[r0 + 7:r0 + 7 + c, :] * w[2:3]
        return y + buf[r0 + 8:r0 + 8 + c, :] * w[3:4]

    conv = [[_silu(conv_rows(buf, w, ci * c)) for buf, w in zip((qbuf, kbuf, vbuf), conv_w)]
            for ci in range(n_chunks)]
    items = [(ci, j) for ci in range(n_chunks) for j in range(hp)]
    rows = lambda ci: slice(ci * c, (ci + 1) * c)
    qs = [_l2n(conv[ci][0][:, head_lanes[j]]) * (LANES ** -0.5) for ci, j in items]
    ks = [_l2n(conv[ci][1][:, head_lanes[j]]) for ci, j in items]
    vs = [conv[ci][2][:, head_lanes[j]] for ci, j in items]
    betas = [gates[j][0][rows(ci)] for ci, j in items]
    gcs = [_dot_exact_lhs(tri, jnp.broadcast_to(gates[j][1][rows(ci)], (c, c))) for ci, j in items]
    decays = [jnp.exp(jnp.where(causal, gc - gc.T, -jnp.inf)) for gc in gcs]
    kbs = [k * b for k, b in zip(ks, betas)]
    ms = [jnp.where(strict, _dot_nt(kb, k) * dec, 0.0) for kb, k, dec in zip(kbs, ks, decays)]
    xs = [_split2(-m) for m in ms]
    ts = [eye - m for m in ms]
    for _ in range(int(math.log2(c)) - 1):
        xs = [_split2(_dot_hp(xp, xp)) for xp in xs]
        ts = [t + _dot_hp(_split2(t), xp) for t, xp in zip(ts, xs)]
    egs = [jnp.exp(gc) for gc in gcs]
    uws = [_dot(t, jnp.concatenate([v * b, kb * eg], axis=1))
           for t, v, b, kb, eg in zip(ts, vs, betas, kbs, egs)]
    attns = [jnp.where(causal, _dot_nt(q, k) * dec, 0.0) for q, k, dec in zip(qs, ks, decays)]
    g_lasts = [gc[c - 1:c, :] for gc in gcs]
    kd_uws = [_dot_tn(k * jnp.exp(gl - gc), uw) for k, gl, gc, uw in zip(ks, g_lasts, gcs, uws)]
    at_uws = [_dot(attn, uw) for attn, uw in zip(attns, uws)]
    lhs = [jnp.concatenate([q * eg - at[:, LANES:], eye * jnp.exp(gl) - kd[:, LANES:]], axis=0)
           for q, eg, at, gl, kd in zip(qs, egs, at_uws, g_lasts, kd_uws)]

    states = [s_scr[j] for j in range(hp)]
    outs = []
    for it, (ci, j) in enumerate(items):
        both = _dot(lhs[it], states[j])
        outs.append(at_uws[it][:, :LANES] + both[:c])
        states[j] = both[c:] + kd_uws[it][:, :LANES]
    for j in range(hp):
        s_scr[j] = states[j]
    for buf in (qbuf, kbuf, vbuf):
        buf[0:8, :] = buf[cb:cb + 8, :]
    for (ci, j), o in zip(items, outs):
        o_ref[rows(ci), head_lanes[j]] = (_rms(o, nw) * _silu(z_ref[rows(ci), head_lanes[j]])).astype(BF16)

    @pl.when(blk == pl.num_programs(2) - 1)
    def _():
        s_ref[0] = s_scr[...]


def _gdn_prompt(proj, ba, conv_w, alog_pad, dtb_pad, norm_w, batch, seq, heads, cb):
    m = proj.shape[0]
    nblk = seq // cb
    hp = GDN_HEADS_PER_STEP
    assert heads % hp == 0
    groups = heads // hp
    rows = lambda b, h, c: b * nblk + c
    head_blk = lambda part: pl.BlockSpec((cb, hp * LANES), lambda b, h, c: (rows(b, h, c), part * groups + h))
    cw_blk = lambda part: pl.BlockSpec((CONV_WIDTH, hp * LANES), lambda b, h, c: (0, part * groups + h))
    row128 = pl.BlockSpec((1, LANES), lambda b, h, c: (0, 0))
    return pl.pallas_call(
        functools.partial(_gdn_prompt_kernel, heads=heads),
        grid=(batch, groups, nblk),
        in_specs=[head_blk(0), head_blk(1), head_blk(2), head_blk(3),
                  pl.BlockSpec((cb, LANES), lambda b, h, c: (rows(b, h, c), 0)),
                  cw_blk(0), cw_blk(1), cw_blk(2), row128, row128, row128],
        out_specs=[pl.BlockSpec((cb, hp * LANES), lambda b, h, c: (rows(b, h, c), h)),
                   pl.BlockSpec((1, hp, LANES, LANES), lambda b, h, c: (b, h, 0, 0))],
        out_shape=[jax.ShapeDtypeStruct((m, heads * LANES), BF16),
                   jax.ShapeDtypeStruct((batch, heads, LANES, LANES), F32)],
        scratch_shapes=[pltpu.VMEM((cb + 8, hp * LANES), F32)] * 3 + [pltpu.VMEM((hp, LANES, LANES), F32)],
        compiler_params=_params(("arbitrary", "arbitrary", "arbitrary")),
    )(proj, proj, proj, proj, ba, conv_w, conv_w, conv_w, alog_pad, dtb_pad, norm_w.reshape(1, LANES))


def _gdn_step_kernel(q_ref, k_ref, v_ref, z_ref, ba_ref, cq_ref, ck_ref, cv_ref, cwq_ref, cwk_ref, cwv_ref,
                     alog_ref, dtb_ref, nw_ref, s_ref, o_ref, so_ref, *, heads):
    h = pl.program_id(1)
    tb = q_ref.shape[0]

    def conv(x_ref, c_ref, w_ref):
        w = w_ref[...]
        y = c_ref[0] * w[0:1]
        y = y + c_ref[1] * w[1:2]
        y = y + c_ref[2] * w[2:3]
        return y + x_ref[...] * w[3:4]

    q = _l2n(_silu(conv(q_ref, cq_ref, cwq_ref))) * (LANES ** -0.5)
    k = _l2n(_silu(conv(k_ref, ck_ref, cwk_ref)))
    v = _silu(conv(v_ref, cv_ref, cwv_ref))
    beta, g = _gdn_gates(ba_ref[...], alog_ref[...], dtb_ref[...], h, heads)
    eg = jnp.exp(g)
    outs = []
    group = 4
    for b0 in range(0, tb, group):
        rows = range(b0, min(b0 + group, tb))
        kcols = [_col_from_row(k[b:b + 1]) for b in rows]
        qcols = [_col_from_row(q[b:b + 1]) for b in rows]
        for b, kcol, qcol in zip(rows, kcols, qcols):
            r = slice(b, b + 1)
            s = s_ref[b, 0] * eg[r]
            v_new = beta[r] * (v[r] - jnp.sum(kcol * s, axis=0, keepdims=True))
            s = s + kcol * v_new
            so_ref[b, 0] = s
            outs.append(jnp.sum(qcol * s, axis=0, keepdims=True))
    o = jnp.concatenate(outs, axis=0)
    o_ref[...] = (_rms(o, nw_ref[...]) * _silu(z_ref[...])).astype(BF16)


def _gdn_step(proj, ba, conv_state_t, conv_w, alog_pad, dtb_pad, norm_w, state, layer, heads, tb):
    bsz = proj.shape[0]
    head_blk = lambda off: pl.BlockSpec((tb, LANES), lambda i, h: (i, off + h))
    cs_blk = lambda off: pl.BlockSpec((CONV_WIDTH - 1, tb, LANES), lambda i, h: (0, i, off + h))
    cw_blk = lambda off: pl.BlockSpec((CONV_WIDTH, LANES), lambda i, h: (0, off + h))
    row128 = pl.BlockSpec((1, LANES), lambda i, h: (0, 0))
    st_in = pl.BlockSpec((None, tb, 1, LANES, LANES), lambda i, h: (layer, i, h, 0, 0))
    st_out = pl.BlockSpec((tb, 1, LANES, LANES), lambda i, h: (i, h, 0, 0))
    return pl.pallas_call(
        functools.partial(_gdn_step_kernel, heads=heads),
        grid=(bsz // tb, heads),
        in_specs=[head_blk(0), head_blk(heads), head_blk(2 * heads), head_blk(3 * heads),
                  pl.BlockSpec((tb, LANES), lambda i, h: (i, 0)),
                  cs_blk(0), cs_blk(heads), cs_blk(2 * heads),
                  cw_blk(0), cw_blk(heads), cw_blk(2 * heads), row128, row128, row128, st_in],
        out_specs=[pl.BlockSpec((tb, LANES), lambda i, h: (i, h)), st_out],
        out_shape=[jax.ShapeDtypeStruct((bsz, heads * LANES), BF16),
                   jax.ShapeDtypeStruct(state.shape[1:], F32)],
        compiler_params=_params(("arbitrary", "arbitrary")),
    )(proj, proj, proj, proj, ba, conv_state_t, conv_state_t, conv_state_t, conv_w, conv_w, conv_w,
      alog_pad, dtb_pad, norm_w.reshape(1, LANES), state)


def _lru_gates(x, wa_ref, wi_ref, ba, bi, lam):
    nb = wa_ref.shape[0]
    ga, gi = [], []
    for s in range(nb):
        xs = x[:, s * LANES:(s + 1) * LANES].astype(BF16)
        ga.append(jnp.dot(xs, wa_ref[s], preferred_element_type=F32))
        gi.append(jnp.dot(xs, wi_ref[s], preferred_element_type=F32))
    gate_a = _sigmoid(jnp.concatenate(ga, axis=-1) + ba)
    gate_i = _sigmoid(jnp.concatenate(gi, axis=-1) + bi)
    log_a = -LRU_C * gate_a * _softplus(-lam)
    return log_a, gate_i


def _lru_prompt_kernel(xl_ref, yl_ref, cw_ref, cb_ref, wa_ref, wi_ref, ba_ref, bi_ref, lam_ref,
                       o_ref, hl_ref, xbuf, abuf, bbuf, h_scr):
    blk = pl.program_id(1)
    cb = xl_ref.shape[0]
    width = xl_ref.shape[1]

    @pl.when(blk == 0)
    def _():
        h_scr[...] = jnp.zeros_like(h_scr)
        xbuf[0:8, :] = jnp.zeros((8, width), F32)

    x = _conv_block(xl_ref, xbuf, cw_ref[...]) + cb_ref[...]
    log_a, gate_i = _lru_gates(x, wa_ref, wi_ref, ba_ref[...], bi_ref[...], lam_ref[...])
    mult = jnp.sqrt(_neg_expm1(2.0 * log_a))
    first = (lax.broadcasted_iota(jnp.int32, (cb, 1), 0) == 0) & (blk == 0)
    mult = jnp.where(first, 1.0, mult)
    abuf[...] = jnp.exp(log_a)
    bbuf[...] = mult * gate_i * x

    def step(t, h):
        h = abuf[pl.ds(t, 1), :] * h + bbuf[pl.ds(t, 1), :]
        bbuf[pl.ds(t, 1), :] = h
        return h

    h_last = lax.fori_loop(0, cb, step, h_scr[...], unroll=8)
    h_scr[...] = h_last
    o_ref[...] = (bbuf[...] * _gelu_tanh(yl_ref[...])).astype(BF16)

    @pl.when(blk == pl.num_programs(1) - 1)
    def _():
        hl_ref[0] = h_last


def _lru_prompt(xy, conv_w, conv_b, w_a, w_i, b_a, b_i, lam, batch, seq, cb):
    m = xy.shape[0]
    width = xy.shape[1] // 2
    nblk = seq // cb
    nb = w_a.shape[0]
    vec = pl.BlockSpec((1, width), lambda b, c: (0, 0))
    wblk = pl.BlockSpec((nb, LANES, LANES), lambda b, c: (0, 0, 0))
    out, h_last = pl.pallas_call(
        _lru_prompt_kernel,
        grid=(batch, nblk),
        in_specs=[pl.BlockSpec((cb, width), lambda b, c: (b * nblk + c, 0)),
                  pl.BlockSpec((cb, width), lambda b, c: (b * nblk + c, 1)),
                  pl.BlockSpec((CONV_WIDTH, width), lambda b, c: (0, 0)),
                  vec, wblk, wblk, vec, vec, vec],
        out_specs=[pl.BlockSpec((cb, width), lambda b, c: (b * nblk + c, 0)),
                   pl.BlockSpec((1, 1, width), lambda b, c: (b, 0, 0))],
        out_shape=[jax.ShapeDtypeStruct((m, width), BF16),
                   jax.ShapeDtypeStruct((batch, 1, width), F32)],
        scratch_shapes=[pltpu.VMEM((cb + 8, width), F32), pltpu.VMEM((cb, width), F32),
                        pltpu.VMEM((cb, width), F32), pltpu.VMEM((1, width), F32)],
        compiler_params=_params(("arbitrary", "arbitrary")),
    )(xy, xy, conv_w, conv_b.reshape(1, width), w_a, w_i, b_a.reshape(1, width), b_i.reshape(1, width),
      lam.reshape(1, width))
    return out, h_last.reshape(batch, width)


def _lru_step_kernel(xl_ref, yl_ref, cs_ref, h0_ref, cw_ref, cb_ref, wa_ref, wi_ref, ba_ref, bi_ref, lam_ref,
                     o_ref, h_ref, *, reset):
    w = cw_ref[...]
    x = cs_ref[0] * w[0:1]
    x = x + cs_ref[1] * w[1:2]
    x = x + cs_ref[2] * w[2:3]
    x = x + xl_ref[...] * w[3:4] + cb_ref[...]
    log_a, gate_i = _lru_gates(x, wa_ref, wi_ref, ba_ref[...], bi_ref[...], lam_ref[...])
    mult = 1.0 if reset else jnp.sqrt(_neg_expm1(2.0 * log_a))
    h = jnp.exp(log_a) * h0_ref[...] + mult * gate_i * x
    h_ref[...] = h
    o_ref[...] = (h * _gelu_tanh(yl_ref[...])).astype(BF16)


def _lru_step(xy, conv_state_t, h0, conv_w, conv_b, w_a, w_i, b_a, b_i, lam, reset):
    bsz = xy.shape[0]
    width = xy.shape[1] // 2
    nb = w_a.shape[0]
    vec = pl.BlockSpec((1, width), lambda i: (0, 0))
    wblk = pl.BlockSpec((nb, LANES, LANES), lambda i: (0, 0, 0))
    full = pl.BlockSpec((bsz, width), lambda i: (0, 0))
    return pl.pallas_call(
        functools.partial(_lru_step_kernel, reset=reset),
        grid=(1,),
        in_specs=[full, pl.BlockSpec((bsz, width), lambda i: (0, 1)),
                  pl.BlockSpec((CONV_WIDTH - 1, bsz, width), lambda i: (0, 0, 0)), full,
                  pl.BlockSpec((CONV_WIDTH, width), lambda i: (0, 0)), vec, wblk, wblk, vec, vec, vec],
        out_specs=[full, full],
        out_shape=[jax.ShapeDtypeStruct((bsz, width), BF16), jax.ShapeDtypeStruct((bsz, width), F32)],
        compiler_params=_params(("arbitrary",)),
    )(xy, xy, conv_state_t, h0, conv_w, conv_b.reshape(1, width), w_a, w_i, b_a.reshape(1, width),
      b_i.reshape(1, width), lam.reshape(1, width))


def _hgrn_lower_bound(lb_ref, layer, j=0):
    depth = lb_ref.shape[0]
    raw = [lb_ref[l, j] for l in range(depth)]
    mx = raw[0]
    for r in raw[1:]:
        mx = jnp.maximum(mx, r)
    ex = [jnp.exp(r - mx) for r in raw]
    tot = ex[0]
    for e in ex[1:]:
        tot = tot + e
    wts = [e / tot for e in ex]
    cum = wts[0]
    for w in wts[1:layer + 1]:
        cum = cum + w
    return cum - wts[0]


def _hgrn_inputs(q_raw, f_raw, lb):
    q = _silu(q_raw) * (LANES ** -0.5)
    f = lb + (1.0 - lb) * _sigmoid(f_raw)
    return q, f


def _level_ref(g, half):
    c = g.shape[0]
    sub = 8
    if 2 * half >= sub:
        g3 = g.reshape(c // (2 * half), 2 * half, LANES)
        return jnp.broadcast_to(g3[:, half - 1:half, :], g3.shape).reshape(c, LANES)
    g3 = g.reshape(c // sub, sub, LANES)
    rin = lax.broadcasted_iota(jnp.int32, g3.shape, 1)
    out = jnp.broadcast_to(g3[:, sub - half - 1:sub - half, :], g3.shape)
    for start in range(sub - 4 * half, -1, -2 * half):
        out = jnp.where(rin < start + 2 * half, jnp.broadcast_to(g3[:, start + half - 1:start + half, :], g3.shape),
                        out)
    return out.reshape(c, LANES)


def _hgrn_level_table(c):
    i = np.arange(c)[:, None]
    j = np.arange(c)[None, :]
    nlev = int(math.log2(c))
    top_bit = sum(((i ^ j) >> b > 0).astype(np.int32) for b in range(1, nlev))
    return jnp.asarray(np.where(j < i, nlev - 1 - top_bit, np.where(j == i, nlev, -1)), jnp.int32)


def _hgrn_prompt_kernel(q_ref, f_ref, i_ref, gz_ref, lb_ref, nw_ref, lvl_ref, o_ref, s_ref, s_scr, *, layer):
    hp = s_scr.shape[0]
    blk = pl.program_id(2)
    cb = q_ref.shape[0]
    c = CHUNK

    @pl.when(blk == 0)
    def _():
        s_scr[...] = jnp.zeros_like(s_scr)

    lbs = [_hgrn_lower_bound(lb_ref, layer, j) for j in range(hp)]
    head_lanes = [slice(j * LANES, (j + 1) * LANES) for j in range(hp)]
    tri = (_iota2(c, 1) <= _iota2(c, 0)).astype(BF16)
    nw = nw_ref[...]
    lvl = lvl_ref[...]
    halves = [c >> (i + 1) for i in range(int(math.log2(c)))]

    items = [(slice(ci * c, (ci + 1) * c), j) for ci in range(cb // c) for j in range(hp)]
    qf = [_hgrn_inputs(q_ref[sl, head_lanes[j]], f_ref[sl, head_lanes[j]], lbs[j]) for sl, j in items]
    qs = [q for q, _ in qf]
    ks = [1.0 - f for _, f in qf]
    vs = [i_ref[sl, head_lanes[j]] for sl, j in items]
    gs = [_dot_exact_lhs(tri, jnp.log(f)) for _, f in qf]

    amats = [_dot_nt(q, k) for q, k in zip(qs, ks)]
    for li, half in enumerate(halves):
        es = [jnp.exp(-jnp.abs(g - _level_ref(g, half))) for g in gs]
        ps = [_dot_nt(q * e, k * e) for q, k, e in zip(qs, ks, es)]
        amats = [jnp.where(lvl == li, p, a) for p, a in zip(ps, amats)]
    intra = [_dot(jnp.where(lvl >= 0, a, 0.0), v) for a, v in zip(amats, vs)]

    g_lasts = [g[c - 1:c, :] for g in gs]
    qdec = [q * jnp.exp(g) for q, g in zip(qs, gs)]
    sdec = [_col_from_row(jnp.exp(gl)) for gl in g_lasts]
    sadd = [_dot_tn(k * jnp.exp(gl - g), v) for k, gl, g, v in zip(ks, g_lasts, gs, vs)]

    states = [s_scr[j] for j in range(hp)]
    outs = []
    for (sl, j), o, qd, dec, add in zip(items, intra, qdec, sdec, sadd):
        outs.append(o + _dot(qd, states[j]))
        states[j] = states[j] * dec + add
    for j in range(hp):
        s_scr[j] = states[j]
    for (sl, j), o in zip(items, outs):
        o_ref[sl, head_lanes[j]] = (_rms(o, nw) * _silu(gz_ref[sl, head_lanes[j]])).astype(BF16)

    @pl.when(blk == pl.num_programs(2) - 1)
    def _():
        s_ref[0] = s_scr[...]


def _hgrn_prompt(proj, lb_raw, norm_w, layer, batch, seq, heads, cb):
    m = proj.shape[0]
    nblk = seq // cb
    depth = lb_raw.shape[0]
    hp = HGRN_HEADS_PER_STEP
    assert heads % hp == 0
    groups = heads // hp
    head_blk = lambda part: pl.BlockSpec((cb, hp * LANES), lambda b, h, c: (b * nblk + c, part * groups + h))
    return pl.pallas_call(
        functools.partial(_hgrn_prompt_kernel, layer=layer),
        grid=(batch, groups, nblk),
        in_specs=[head_blk(0), head_blk(1), head_blk(2), head_blk(3),
                  pl.BlockSpec((depth, hp, 1, LANES), lambda b, h, c: (0, h, 0, 0)),
                  pl.BlockSpec((1, LANES), lambda b, h, c: (0, 0)),
                  pl.BlockSpec((CHUNK, CHUNK), lambda b, h, c: (0, 0))],
        out_specs=[pl.BlockSpec((cb, hp * LANES), lambda b, h, c: (b * nblk + c, h)),
                   pl.BlockSpec((1, hp, LANES, LANES), lambda b, h, c: (b, h, 0, 0))],
        out_shape=[jax.ShapeDtypeStruct((m, heads * LANES), BF16),
                   jax.ShapeDtypeStruct((batch, heads, LANES, LANES), F32)],
        scratch_shapes=[pltpu.VMEM((hp, LANES, LANES), F32)],
        compiler_params=_params(("arbitrary", "arbitrary", "arbitrary")),
    )(proj, proj, proj, proj, lb_raw.reshape(depth, heads, 1, LANES), norm_w.reshape(1, LANES),
      _hgrn_level_table(CHUNK))


def _hgrn_step_kernel(q_ref, f_ref, i_ref, gz_ref, lb_ref, nw_ref, s_ref, o_ref, so_ref, *, layer):
    tb = q_ref.shape[0]
    lb = _hgrn_lower_bound(lb_ref, layer)
    q, f = _hgrn_inputs(q_ref[...], f_ref[...], lb)
    v = i_ref[...]
    outs = []
    group = 4
    for b0 in range(0, tb, group):
        rows = range(b0, min(b0 + group, tb))
        fcols = [_col_from_row(f[b:b + 1]) for b in rows]
        qcols = [_col_from_row(q[b:b + 1]) for b in rows]
        for b, fcol, qcol in zip(rows, fcols, qcols):
            s = fcol * (s_ref[b, 0] - v[b:b + 1]) + v[b:b + 1]
            so_ref[b, 0] = s
            outs.append(jnp.sum(qcol * s, axis=0, keepdims=True))
    o = jnp.concatenate(outs, axis=0)
    o_ref[...] = (_rms(o, nw_ref[...]) * _silu(gz_ref[...])).astype(BF16)


def _hgrn_step(proj, lb_raw, norm_w, state, state_idx, layer, heads, tb):
    bsz = proj.shape[0]
    depth = lb_raw.shape[0]
    head_blk = lambda off: pl.BlockSpec((tb, LANES), lambda i, h: (i, off + h))
    st_in = pl.BlockSpec((None, tb, 1, LANES, LANES), lambda i, h: (state_idx, i, h, 0, 0))
    st_out = pl.BlockSpec((tb, 1, LANES, LANES), lambda i, h: (i, h, 0, 0))
    return pl.pallas_call(
        functools.partial(_hgrn_step_kernel, layer=layer),
        grid=(bsz // tb, heads),
        in_specs=[head_blk(0), head_blk(heads), head_blk(2 * heads), head_blk(3 * heads),
                  pl.BlockSpec((depth, 1, 1, LANES), lambda i, h: (0, h, 0, 0)),
                  pl.BlockSpec((1, LANES), lambda i, h: (0, 0)), st_in],
        out_specs=[pl.BlockSpec((tb, LANES), lambda i, h: (i, h)), st_out],
        out_shape=[jax.ShapeDtypeStruct((bsz, heads * LANES), BF16),
                   jax.ShapeDtypeStruct(state.shape[1:], F32)],
        compiler_params=_params(("arbitrary", "arbitrary")),
    )(proj, proj, proj, proj, lb_raw.reshape(depth, heads, 1, LANES), norm_w.reshape(1, LANES), state)


def _trunk(x, mem_k, mem_v, states, wts, big, batch, seq, prompt):
    depth = wts["norm_mix"].shape[0]
    gdn_w = wts["gdn_conv_w"].shape[-1] // 3
    gdn_heads = gdn_w // LANES
    hgrn_heads = wts["hgrn_lb_raw"].shape[1] // LANES
    lru_w = wts["lru_conv_w"].shape[-1]
    bf = {name: list(handles) for name, handles in big.items()}
    new = {"gdn_conv": [], "gdn": [], "lru_conv": [], "lru": [], "hgrn": []}

    def keep(name, idx, w_used):
        w, li = bf[name][idx]
        bf[name][idx] = (w_used, li if w_used is w else 0)

    for l in range(depth):
        g_mix = wts["norm_mix"][l]
        if l % 2 == 0:
            e = l // 2
            proj, _ = _norm_matmul(x, g_mix, wts["ab_head"], e, 0, 4 * gdn_w, TM, TN)
            ba, _ = _norm_matmul(x, g_mix, wts["ab_head"], e, 4 * gdn_w, LANES, TM, LANES)
            xy, _ = _norm_matmul(x, g_mix, wts["ab_lru"], e, 0, 2 * lru_w, TM, TN)
            gargs = (wts["gdn_conv_w"][e], wts["alog_pad"][e], wts["dtb_pad"][e], wts["gdn_norm_w"][e])
            largs = (wts["lru_conv_w"][e], wts["lru_conv_b"][e], wts["lru_w_a"][e], wts["lru_w_i"][e],
                     wts["lru_b_a"][e], wts["lru_b_i"][e], wts["lru_lam"][e])
            w_out, li = bf["ab_w_out"][e]
            if prompt:
                o_a, s_new = _gdn_prompt(proj, ba, *gargs, batch, seq, gdn_heads, 512)
                o_b, h_new = _lru_prompt(xy, *largs, batch, seq, 256)
                tail = lambda t, w: t.reshape(batch, seq, -1)[:, seq - (CONV_WIDTH - 1):, :w]
                new["gdn_conv"].append(tail(proj, 3 * gdn_w))
                new["lru_conv"].append(tail(xy, lru_w))
                x = _matmul2_res(o_a, o_b, w_out, li, x, TM, TN)
            else:
                gc_state, lc_state = states["gdn_conv"][e], states["lru_conv"][e]
                o_a, s_new = _gdn_step(proj, ba, jnp.swapaxes(gc_state, 0, 1), *gargs, states["gdn"], e,
                                       gdn_heads, 32)
                o_b, h_new = _lru_step(xy, jnp.swapaxes(lc_state, 0, 1), states["lru"][e], *largs, reset=False)
                new["gdn_conv"].append(jnp.concatenate([gc_state[:, 1:], proj[:, None, :3 * gdn_w]], axis=1))
                new["lru_conv"].append(jnp.concatenate([lc_state[:, 1:], xy[:, None, :lru_w]], axis=1))
                x, w_used = _matmul_res(jnp.concatenate([o_a, o_b], axis=1), w_out, li, x, TM, TN)
                keep("ab_w_out", e, w_used)
            new["gdn"].append(s_new)
            new["lru"].append(h_new)
        else:
            o_idx = l // 2
            w_in, li = bf["c_w_in"][o_idx]
            proj, w_used = _norm_matmul(x, g_mix, w_in, li, 0, w_in.shape[2], TM, TN)
            keep("c_w_in", o_idx, w_used)
            hargs = (wts["hgrn_lb_raw"], wts["hgrn_norm_w"][o_idx])
            if prompt:
                o_c, s_new = _hgrn_prompt(proj, *hargs, l, batch, seq, hgrn_heads, 512)
            else:
                o_c, s_new = _hgrn_step(proj, *hargs, states["hgrn"], o_idx, l, hgrn_heads, 32)
            new["hgrn"].append(s_new)
            w_out, li = bf["c_w_out"][o_idx]
            x, w_used = _matmul_res(o_c, w_out, li, x, TM, TN)
            keep("c_w_out", o_idx, w_used)
        (w_q, lq), (w_o, lo) = bf["mem_w_q"][l], bf["mem_w_o"][l]
        if prompt:
            assert lq == lo
            x = _mem_attn_prompt(x, wts["norm_mem"][l], w_q, mem_k, mem_v, w_o, l, lq, seq, 256)
        else:
            q, w_used = _norm_matmul(x, wts["norm_mem"][l], w_q, lq, 0, w_q.shape[2], TM, TN)
            keep("mem_w_q", l, w_used)
            o = _mem_attn_sample(q, mem_k, mem_v, l, 8)
            x, w_used = _matmul_res(o.reshape(batch, -1).astype(BF16), w_o, lo, x, TM, TN)
            keep("mem_w_o", l, w_used)
        (w_up, lu), (w_dn, ld) = bf["ffn_w_up"][l], bf["ffn_w_down"][l]
        assert lu == ld
        x, up_used, dn_used = _ffn(x, wts["norm_ffn"][l], w_up, w_dn, lu, wts["norm_final"], l == depth - 1,
                                   TM, TF)
        keep("ffn_w_up", l, up_used)
        keep("ffn_w_down", l, dn_used)
    return x, {n: jnp.stack(v) for n, v in new.items()}, bf


def kernel(x_prompt, x_sample, cache_mem_k, cache_mem_v, state_gdn_conv, state_gdn, state_lru_conv, state_lru, state_hgrn, mem_prompt, norm_mix, norm_mem, norm_mem_kv, norm_ffn, norm_final, ab_w_in, ab_w_out, gdn_conv_w, gdn_a_log, gdn_dt_bias, gdn_norm_w, lru_conv_w, lru_conv_b, lru_w_a, lru_b_a, lru_w_i, lru_b_i, lru_lam, c_w_in, c_w_out, hgrn_lb_raw, hgrn_norm_w, mem_w_q, mem_w_k, mem_w_v, mem_w_o, ffn_w_up, ffn_w_down):
    bp, lp, d = x_prompt.shape
    bs, ls, _ = x_sample.shape
    assert ls == 1, "the sample group advances one token per call"
    depth = norm_mix.shape[0]
    gdn_heads = gdn_a_log.shape[1]
    gdn_w = gdn_heads * LANES
    mem_tokens, mem_heads, mem_hd = cache_mem_k.shape[2:]
    mem_w = mem_heads * mem_hd
    n_ba = 2 * gdn_heads
    assert n_ba <= LANES

    pad_ba = lambda a: jnp.pad(a, ((0, 0), (gdn_heads, LANES - n_ba)))[:, None, :]
    ab_bf = ab_w_in.astype(BF16)
    wts = dict(
        norm_mix=norm_mix, norm_mem=norm_mem, norm_ffn=norm_ffn, norm_final=norm_final,
        ab_head=ab_bf, ab_lru=ab_bf[:, :, 4 * gdn_w + n_ba:],
        gdn_conv_w=gdn_conv_w, alog_pad=pad_ba(gdn_a_log), dtb_pad=pad_ba(gdn_dt_bias), gdn_norm_w=gdn_norm_w,
        lru_conv_w=lru_conv_w, lru_conv_b=lru_conv_b, lru_w_a=lru_w_a.astype(BF16), lru_w_i=lru_w_i.astype(BF16),
        lru_b_a=lru_b_a, lru_b_i=lru_b_i, lru_lam=lru_lam,
        hgrn_lb_raw=hgrn_lb_raw, hgrn_norm_w=hgrn_norm_w,
    )
    big = dict(ab_w_out=ab_w_out, c_w_in=c_w_in, c_w_out=c_w_out, mem_w_q=mem_w_q, mem_w_o=mem_w_o,
               ffn_w_up=ffn_w_up, ffn_w_down=ffn_w_down)
    big = {name: [(w, l) for l in range(w.shape[0])] for name, w in big.items()}

    states = dict(gdn_conv=state_gdn_conv, gdn=state_gdn, lru_conv=state_lru_conv, lru=state_lru,
                  hgrn=state_hgrn)
    y_s, new_s, big_bf = _trunk(x_sample.reshape(bs * ls, d), cache_mem_k, cache_mem_v, states, wts, big,
                                bs, ls, False)

    mem_rows = mem_prompt.reshape(bp * mem_tokens, d)
    w_kv = jnp.concatenate([mem_w_k, mem_w_v], axis=-1).astype(BF16)
    kv = jnp.stack([_norm_matmul(mem_rows, norm_mem_kv[l], w_kv, l, 0, 2 * mem_w, TM, TN)[0] for l in range(depth)])
    p_mem_k = kv[:, :, :mem_w].reshape(depth, bp, mem_tokens, mem_w)
    p_mem_v = kv[:, :, mem_w:].reshape(depth, bp, mem_tokens, mem_w)

    y_p, new_p, _ = _trunk(x_prompt.reshape(bp * lp, d), p_mem_k, p_mem_v, None, wts, big_bf, bp, lp, True)

    order = ("gdn_conv", "gdn", "lru_conv", "lru", "hgrn")
    mem5 = lambda t: t.reshape(depth, bp, mem_tokens, mem_heads, mem_hd)
    return (y_p.reshape(bp, lp, d), y_s.reshape(bs, ls, d), mem5(p_mem_k), mem5(p_mem_v),
            *(new_p[n] for n in order), *(new_s[n] for n in order))
```

```python
import functools
import math

import jax
import jax.numpy as jnp
import numpy as np
from jax import lax
from jax.experimental import pallas as pl
from jax.experimental.pallas import tpu as pltpu

F32 = jnp.float32
BF16 = jnp.bfloat16
EPS = 1e-6
LANES = 128
CONV_WIDTH = 4
LRU_C = 8.0
VMEM_LIMIT_BYTES = 56 * 1024 * 1024
TM = 512
TN = 1024
TF = 512
TN_CAST = 512
TF_CAST = 256
ROW_GROUPS = 4
CHUNK = 128
GDN_HEADS_PER_STEP = 4
HGRN_HEADS_PER_STEP = 4


def _params(sem):
    return pltpu.CompilerParams(dimension_semantics=sem, vmem_limit_bytes=VMEM_LIMIT_BYTES)


def _sigmoid(x):
    return jax.nn.sigmoid(x)


def _silu(x):
    return x * _sigmoid(x)


def _softplus(x):
    return jnp.maximum(x, 0.0) + jnp.log1p(jnp.exp(-jnp.abs(x)))


def _neg_expm1(x):
    t = jnp.tanh(0.5 * x)
    return -2.0 * t / (1.0 - t)


def _gelu_tanh(x):
    c = math.sqrt(2.0 / math.pi)
    return 0.5 * x * (1.0 + jnp.tanh(c * (x + 0.044715 * (x * x * x))))


def _rms(x, w):
    return x * lax.rsqrt(jnp.mean(x * x, axis=-1, keepdims=True) + EPS) * w


def _l2n(x):
    return x * lax.rsqrt(jnp.sum(x * x, axis=-1, keepdims=True) + EPS)


def _dot(a, b):
    return jnp.dot(a.astype(BF16), b.astype(BF16), preferred_element_type=F32)


def _dot_nt(a, b):
    return lax.dot_general(a.astype(BF16), b.astype(BF16), (((1,), (1,)), ((), ())),
                           preferred_element_type=F32)


def _dot_tn(a, b):
    return lax.dot_general(a.astype(BF16), b.astype(BF16), (((0,), (0,)), ((), ())),
                           preferred_element_type=F32)


def _split2(a):
    hi = a.astype(BF16)
    lo = (a - hi.astype(F32)).astype(BF16)
    return hi, lo


def _dot_hp(a_parts, b_parts):
    ah, al = a_parts
    bh, bl = b_parts
    return jnp.dot(jnp.concatenate([ah, ah, al], axis=1), jnp.concatenate([bh, bl, bh], axis=0),
                   preferred_element_type=F32)


def _dot_exact_lhs(t, b):
    b1 = b.astype(BF16)
    r1 = b - b1.astype(F32)
    b2 = r1.astype(BF16)
    b3 = (r1 - b2.astype(F32)).astype(BF16)
    return jnp.dot(jnp.concatenate([t, t, t], axis=1), jnp.concatenate([b1, b2, b3], axis=0),
                   preferred_element_type=F32)


def _iota2(n, axis):
    return lax.broadcasted_iota(jnp.int32, (n, n), axis)


def _col_from_row(row):
    n = row.shape[-1]
    return jnp.broadcast_to(row, (n, n)).T


def _row_groups(rows):
    n = ROW_GROUPS if rows % (8 * ROW_GROUPS) == 0 else 1
    return [slice(r * rows // n, (r + 1) * rows // n) for r in range(n)]


def _norm_matmul_kernel(x_ref, g_ref, w_ref, o_ref, *rest):
    xn_ref = rest[-1]
    if len(rest) == 2:
        rest[0][...] = w_ref[...].astype(BF16)
        w_ref = rest[0]
    first = pl.program_id(1) == 0

    @pl.when(first)
    def _():
        for rows in _row_groups(x_ref.shape[0]):
            xn = _rms(x_ref[rows, :], g_ref[...]).astype(BF16)
            xn_ref[rows, :] = xn
            o_ref[rows, :] = jnp.dot(xn, w_ref[...], preferred_element_type=F32)

    @pl.when(jnp.logical_not(first))
    def _():
        o_ref[...] = jnp.dot(xn_ref[...], w_ref[...], preferred_element_type=F32)


def _norm_matmul(x, g, w, layer, n0, n, tm, tn):
    m, k = x.shape
    cast = w.dtype != BF16
    tm = min(tm, m)
    tn = min(tn, n, TN_CAST) if cast else min(tn, n)
    j0 = n0 // tn
    out_specs = [pl.BlockSpec((tm, tn), lambda i, j: (i, j))]
    out_shape = [jax.ShapeDtypeStruct((m, n), F32)]
    if cast:
        assert m == tm and n0 == 0 and n == w.shape[2], "each weight tile must be visited exactly once"
        out_specs.append(pl.BlockSpec((None, k, tn), lambda i, j: (0, 0, j)))
        out_shape.append(jax.ShapeDtypeStruct((1, k, n), BF16))
    res = pl.pallas_call(
        _norm_matmul_kernel,
        grid=(m // tm, n // tn),
        in_specs=[pl.BlockSpec((tm, k), lambda i, j: (i, 0)),
                  pl.BlockSpec((1, k), lambda i, j: (0, 0)),
                  pl.BlockSpec((None, k, tn), lambda i, j: (layer, 0, j0 + j))],
        out_specs=out_specs,
        out_shape=out_shape,
        scratch_shapes=[pltpu.VMEM((tm, k), BF16)],
        compiler_params=_params(("arbitrary", "arbitrary")),
    )(x, g.reshape(1, k), w)
    return (res[0], res[1]) if cast else (res[0], w)


def _matmul2_res_kernel(a1_ref, a2_ref, w1_ref, w2_ref, r_ref, o_ref):
    acc = jnp.dot(a1_ref[...], w1_ref[...], preferred_element_type=F32)
    acc = acc + jnp.dot(a2_ref[...], w2_ref[...], preferred_element_type=F32)
    o_ref[...] = r_ref[...] + acc


def _matmul2_res(a1, a2, w, layer, res, tm, tn):
    m, kh = a1.shape
    n = w.shape[2]
    tm = min(tm, m)
    tn = min(tn, n)
    return pl.pallas_call(
        _matmul2_res_kernel,
        grid=(m // tm, n // tn),
        in_specs=[pl.BlockSpec((tm, kh), lambda i, j: (i, 0)),
                  pl.BlockSpec((tm, kh), lambda i, j: (i, 0)),
                  pl.BlockSpec((None, kh, tn), lambda i, j: (layer, 0, j)),
                  pl.BlockSpec((None, kh, tn), lambda i, j: (layer, 1, j)),
                  pl.BlockSpec((tm, tn), lambda i, j: (i, j))],
        out_specs=pl.BlockSpec((tm, tn), lambda i, j: (i, j)),
        out_shape=jax.ShapeDtypeStruct((m, n), F32),
        compiler_params=_params(("arbitrary", "arbitrary")),
    )(a1, a2, w, w, res)


def _matmul_res_kernel(a_ref, w_ref, r_ref, o_ref, *wb_ref):
    w = w_ref[...]
    if wb_ref:
        w = w.astype(BF16)
        wb_ref[0][...] = w
    o_ref[...] = r_ref[...] + jnp.dot(a_ref[...], w, preferred_element_type=F32)


def _matmul_res(a, w, layer, res, tm, tn):
    m, k = a.shape
    n = w.shape[2]
    cast = w.dtype != BF16
    tm = min(tm, m)
    tn = min(tn, n, TN_CAST) if cast else min(tn, n)
    out_specs = [pl.BlockSpec((tm, tn), lambda i, j: (i, j))]
    out_shape = [jax.ShapeDtypeStruct((m, n), F32)]
    if cast:
        assert m == tm, "each weight tile must be visited exactly once"
        out_specs.append(pl.BlockSpec((None, k, tn), lambda i, j: (0, 0, j)))
        out_shape.append(jax.ShapeDtypeStruct((1, k, n), BF16))
    out = pl.pallas_call(
        _matmul_res_kernel,
        grid=(m // tm, n // tn),
        in_specs=[pl.BlockSpec((tm, k), lambda i, j: (i, 0)),
                  pl.BlockSpec((None, k, tn), lambda i, j: (layer, 0, j)),
                  pl.BlockSpec((tm, tn), lambda i, j: (i, j))],
        out_specs=out_specs,
        out_shape=out_shape,
        compiler_params=_params(("arbitrary", "arbitrary")),
    )(a, w, res)
    return (out[0], out[1]) if cast else (out[0], w)


def _ffn_kernel(x_ref, g_ref, wu_ref, wd_ref, gf_ref, o_ref, *rest, final_norm):
    xn_ref = rest[-1]
    f = pl.program_id(1)
    if len(rest) == 3:
        rest[0][...] = wu_ref[...].astype(BF16)
        rest[1][...] = wd_ref[...].astype(BF16)
        wu_ref, wd_ref = rest[0], rest[1]

    def mlp(xn):
        h = jnp.dot(xn, wu_ref[...], preferred_element_type=F32)
        h = jnp.square(jnp.maximum(h, 0.0)).astype(BF16)
        return jnp.dot(h, wd_ref[...], preferred_element_type=F32)

    @pl.when(f == 0)
    def _():
        for rows in _row_groups(x_ref.shape[0]):
            x = x_ref[rows, :]
            xn = _rms(x, g_ref[...]).astype(BF16)
            xn_ref[rows, :] = xn
            o_ref[rows, :] = x + mlp(xn)

    @pl.when(f != 0)
    def _():
        o_ref[...] += mlp(xn_ref[...])

    if final_norm:
        @pl.when(f == pl.num_programs(1) - 1)
        def _():
            o_ref[...] = _rms(o_ref[...], gf_ref[...])


def _ffn(x, g, w_up, w_down, layer, g_final, final_norm, tm, tf):
    m, d = x.shape
    dff = w_up.shape[2]
    cast = w_up.dtype != BF16
    tm = min(tm, m)
    tf = min(tf, TF_CAST) if cast else tf
    out_specs = [pl.BlockSpec((tm, d), lambda i, f: (i, 0))]
    out_shape = [jax.ShapeDtypeStruct((m, d), F32)]
    if cast:
        assert m == tm, "each weight tile must be visited exactly once"
        out_specs += [pl.BlockSpec((None, d, tf), lambda i, f: (0, 0, f)),
                      pl.BlockSpec((None, tf, d), lambda i, f: (0, f, 0))]
        out_shape += [jax.ShapeDtypeStruct((1, d, dff), BF16), jax.ShapeDtypeStruct((1, dff, d), BF16)]
    out = pl.pallas_call(
        functools.partial(_ffn_kernel, final_norm=final_norm),
        grid=(m // tm, dff // tf),
        in_specs=[pl.BlockSpec((tm, d), lambda i, f: (i, 0), pipeline_mode=pl.Buffered(1)),
                  pl.BlockSpec((1, d), lambda i, f: (0, 0)),
                  pl.BlockSpec((None, d, tf), lambda i, f: (layer, 0, f)),
                  pl.BlockSpec((None, tf, d), lambda i, f: (layer, f, 0)),
                  pl.BlockSpec((1, d), lambda i, f: (0, 0))],
        out_specs=out_specs,
        out_shape=out_shape,
        scratch_shapes=[pltpu.VMEM((tm, d), BF16)],
        compiler_params=_params(("arbitrary", "arbitrary")),
    )(x, g.reshape(1, d), w_up, w_down, g_final.reshape(1, d))
    return (out[0], out[1], out[2]) if cast else (out[0], w_up, w_down)


def _mem_attn_prompt_kernel(x_ref, g_ref, wq_ref, k_ref, v_ref, wo_ref, o_ref, *, heads):
    x = x_ref[...]
    xn = _rms(x, g_ref[...]).astype(BF16)
    q = jnp.dot(xn, wq_ref[...], preferred_element_type=F32)
    k = k_ref[0].astype(BF16)
    v = v_ref[0].astype(BF16)
    scale = LANES ** -0.5
    outs = []
    for h in range(heads):
        sl = slice(h * LANES, (h + 1) * LANES)
        s = _dot_nt(q[:, sl], k[:, sl]) * scale
        e = jnp.exp(s - jnp.max(s, axis=-1, keepdims=True))
        p = e / jnp.sum(e, axis=-1, keepdims=True)
        outs.append(_dot(p, v[:, sl]))
    o = jnp.concatenate(outs, axis=-1).astype(BF16)
    o_ref[...] = x + jnp.dot(o, wo_ref[...], preferred_element_type=F32)


def _mem_attn_prompt(x, g, w_q, mem_k, mem_v, w_o, layer, w_layer, seq, tl):
    m, d = x.shape
    _, _, t, w = mem_k.shape
    nblk = seq // tl
    return pl.pallas_call(
        functools.partial(_mem_attn_prompt_kernel, heads=w // LANES),
        grid=(m // tl,),
        in_specs=[pl.BlockSpec((tl, d), lambda i: (i, 0)),
                  pl.BlockSpec((1, d), lambda i: (0, 0)),
                  pl.BlockSpec((None, d, w), lambda i: (w_layer, 0, 0)),
                  pl.BlockSpec((None, 1, t, w), lambda i: (layer, i // nblk, 0, 0)),
                  pl.BlockSpec((None, 1, t, w), lambda i: (layer, i // nblk, 0, 0)),
                  pl.BlockSpec((None, w, d), lambda i: (w_layer, 0, 0))],
        out_specs=pl.BlockSpec((tl, d), lambda i: (i, 0)),
        out_shape=jax.ShapeDtypeStruct((m, d), F32),
        compiler_params=_params(("arbitrary",)),
    )(x, g.reshape(1, d), w_q, mem_k, mem_v, w_o)


def _mem_attn_sample_kernel(q_ref, k_ref, v_ref, o_ref, *, tb, heads):
    scale = LANES ** -0.5
    for b in range(tb):
        q = jnp.concatenate([q_ref[b:b + 1, h * LANES:(h + 1) * LANES] for h in range(heads)], axis=0)
        s = jnp.sum(k_ref[b] * (q * scale)[None], axis=-1, keepdims=True)
        e = jnp.exp(s - jnp.max(s, axis=0, keepdims=True))
        o_ref[b] = jnp.sum(e * v_ref[b], axis=0) / jnp.sum(e, axis=0)


def _mem_attn_sample(q, mem_k, mem_v, layer, tb):
    b = q.shape[0]
    _, _, t, heads, hd = mem_k.shape
    kv_blk = pl.BlockSpec((None, tb, t, heads, hd), lambda i: (layer, i, 0, 0, 0))
    return pl.pallas_call(
        functools.partial(_mem_attn_sample_kernel, tb=tb, heads=heads),
        grid=(b // tb,),
        in_specs=[pl.BlockSpec((tb, heads * hd), lambda i: (i, 0)), kv_blk, kv_blk],
        out_specs=pl.BlockSpec((tb, heads, hd), lambda i: (i, 0, 0)),
        out_shape=jax.ShapeDtypeStruct((b, heads, hd), F32),
        compiler_params=_params(("arbitrary",)),
    )(q, mem_k, mem_v)


def _conv_block(x_ref, buf, w):
    cb = x_ref.shape[0]
    buf[8:8 + cb, :] = x_ref[...]
    y = buf[5:5 + cb, :] * w[0:1]
    y = y + buf[6:6 + cb, :] * w[1:2]
    y = y + buf[7:7 + cb, :] * w[2:3]
    y = y + buf[8:8 + cb, :] * w[3:4]
    buf[0:8, :] = buf[cb:cb + 8, :]
    return y


def _gdn_gates(ba, alog, dtb, h, heads):
    lane = lax.broadcasted_iota(jnp.int32, ba.shape, 1)
    beta_all = _sigmoid(ba)
    g_all = -jnp.exp(alog) * _softplus(ba + dtb)
    beta = jnp.sum(jnp.where(lane == h, beta_all, 0.0), axis=1, keepdims=True)
    g = jnp.sum(jnp.where(lane == heads + h, g_all, 0.0), axis=1, keepdims=True)
    return beta, g


def _gdn_prompt_kernel(q_ref, k_ref, v_ref, z_ref, ba_ref, cwq_ref, cwk_ref, cwv_ref, alog_ref, dtb_ref,
                       nw_ref, o_ref, s_ref, qbuf, kbuf, vbuf, s_scr, *, heads):
    hp = s_scr.shape[0]
    blk = pl.program_id(2)
    cb = q_ref.shape[0]
    c = CHUNK
    n_chunks = cb // c

    @pl.when(blk == 0)
    def _():
        s_scr[...] = jnp.zeros_like(s_scr)
        for buf in (qbuf, kbuf, vbuf):
            buf[0:8, :] = jnp.zeros((8, hp * LANES), F32)

    for x_ref, buf in ((q_ref, qbuf), (k_ref, kbuf), (v_ref, vbuf)):
        buf[8:8 + cb, :] = x_ref[...]
    conv_w = (cwq_ref[...], cwk_ref[...], cwv_ref[...])
    ba = ba_ref[...]
    gates = [_gdn_gates(ba, alog_ref[...], dtb_ref[...], pl.program_id(1) * hp + j, heads) for j in range(hp)]
    head_lanes = [slice(j * LANES, (j + 1) * LANES) for j in range(hp)]

    row = _iota2(c, 0)
    col = _iota2(c, 1)
    causal = col <= row
    strict = col < row
    tri = causal.astype(BF16)
    eye = (col == row).astype(F32)
    nw = nw_ref[...]

    def conv_rows(buf, w, r0):
        y = buf[r0 + 5:r0 + 5 + c, :] * w[0:1]
        y = y + buf[r0 + 6:r0 + 6 + c, :] * w[1:2]
        y = y + buf[r0 + 7:r0 + 7 + c, :] * w[2:3]
        return y + buf[r0 + 8:r0 + 8 + c, :] * w[3:4]

    conv = [[_silu(conv_rows(buf, w, ci * c)) for buf, w in zip((qbuf, kbuf, vbuf), conv_w)]
            for ci in range(n_chunks)]
    items = [(ci, j) for ci in range(n_chunks) for j in range(hp)]
    rows = lambda ci: slice(ci * c, (ci + 1) * c)
    qs = [_l2n(conv[ci][0][:, head_lanes[j]]) * (LANES ** -0.5) for ci, j in items]
    ks = [_l2n(conv[ci][1][:, head_lanes[j]]) for ci, j in items]
    vs = [conv[ci][2][:, head_lanes[j]] for ci, j in items]
    betas = [gates[j][0][rows(ci)] for ci, j in items]
    gcs = [_dot_exact_lhs(tri, jnp.broadcast_to(gates[j][1][rows(ci)], (c, c))) for ci, j in items]
    decays = [jnp.exp(jnp.where(causal, gc - gc.T, -jnp.inf)) for gc in gcs]
    kbs = [k * b for k, b in zip(ks, betas)]
    ms = [jnp.where(strict, _dot_nt(kb, k) * dec, 0.0) for kb, k, dec in zip(kbs, ks, decays)]
    xs = [_split2(-m) for m in ms]
    ts = [eye - m for m in ms]
    for _ in range(int(math.log2(c)) - 1):
        xs = [_split2(_dot_hp(xp, xp)) for xp in xs]
        ts = [t + _dot_hp(_split2(t), xp) for t, xp in zip(ts, xs)]
    egs = [jnp.exp(gc) for gc in gcs]
    uws = [_dot(t, jnp.concatenate([v * b, kb * eg], axis=1))
           for t, v, b, kb, eg in zip(ts, vs, betas, kbs, egs)]
    attns = [jnp.where(causal, _dot_nt(q, k) * dec, 0.0) for q, k, dec in zip(qs, ks, decays)]
    g_lasts = [gc[c - 1:c, :] for gc in gcs]
    kd_uws = [_dot_tn(k * jnp.exp(gl - gc), uw) for k, gl, gc, uw in zip(ks, g_lasts, gcs, uws)]
    at_uws = [_dot(attn, uw) for attn, uw in zip(attns, uws)]
    lhs = [jnp.concatenate([q * eg - at[:, LANES:], eye * jnp.exp(gl) - kd[:, LANES:]], axis=0)
           for q, eg, at, gl, kd in zip(qs, egs, at_uws, g_lasts, kd_uws)]

    states = [s_scr[j] for j in range(hp)]
    outs = []
    for it, (ci, j) in enumerate(items):
        both = _dot(lhs[it], states[j])
        outs.append(at_uws[it][:, :LANES] + both[:c])
        states[j] = both[c:] + kd_uws[it][:, :LANES]
    for j in range(hp):
        s_scr[j] = states[j]
    for buf in (qbuf, kbuf, vbuf):
        buf[0:8, :] = buf[cb:cb + 8, :]
    for (ci, j), o in zip(items, outs):
        o_ref[rows(ci), head_lanes[j]] = (_rms(o, nw) * _silu(z_ref[rows(ci), head_lanes[j]])).astype(BF16)

    @pl.when(blk == pl.num_programs(2) - 1)
    def _():
        s_ref[0] = s_scr[...]


def _gdn_prompt(proj, ba, conv_w, alog_pad, dtb_pad, norm_w, batch, seq, heads, cb):
    m = proj.shape[0]
    nblk = seq // cb
    hp = GDN_HEADS_PER_STEP
    assert heads % hp == 0
    groups = heads // hp
    rows = lambda b, h, c: b * nblk + c
    head_blk = lambda part: pl.BlockSpec((cb, hp * LANES), lambda b, h, c: (rows(b, h, c), part * groups + h))
    cw_blk = lambda part: pl.BlockSpec((CONV_WIDTH, hp * LANES), lambda b, h, c: (0, part * groups + h))
    row128 = pl.BlockSpec((1, LANES), lambda b, h, c: (0, 0))
    return pl.pallas_call(
        functools.partial(_gdn_prompt_kernel, heads=heads),
        grid=(batch, groups, nblk),
        in_specs=[head_blk(0), head_blk(1), head_blk(2), head_blk(3),
                  pl.BlockSpec((cb, LANES), lambda b, h, c: (rows(b, h, c), 0)),
                  cw_blk(0), cw_blk(1), cw_blk(2), row128, row128, row128],
        out_specs=[pl.BlockSpec((cb, hp * LANES), lambda b, h, c: (rows(b, h, c), h)),
                   pl.BlockSpec((1, hp, LANES, LANES), lambda b, h, c: (b, h, 0, 0))],
        out_shape=[jax.ShapeDtypeStruct((m, heads * LANES), BF16),
                   jax.ShapeDtypeStruct((batch, heads, LANES, LANES), F32)],
        scratch_shapes=[pltpu.VMEM((cb + 8, hp * LANES), F32)] * 3 + [pltpu.VMEM((hp, LANES, LANES), F32)],
        compiler_params=_params(("arbitrary", "arbitrary", "arbitrary")),
    )(proj, proj, proj, proj, ba, conv_w, conv_w, conv_w, alog_pad, dtb_pad, norm_w.reshape(1, LANES))


def _gdn_step_kernel(q_ref, k_ref, v_ref, z_ref, ba_ref, cq_ref, ck_ref, cv_ref, cwq_ref, cwk_ref, cwv_ref,
                     alog_ref, dtb_ref, nw_ref, s_ref, o_ref, so_ref, *, heads):
    h = pl.program_id(1)
    tb = q_ref.shape[0]

    def conv(x_ref, c_ref, w_ref):
        w = w_ref[...]
        y = c_ref[0] * w[0:1]
        y = y + c_ref[1] * w[1:2]
        y = y + c_ref[2] * w[2:3]
        return y + x_ref[...] * w[3:4]

    q = _l2n(_silu(conv(q_ref, cq_ref, cwq_ref))) * (LANES ** -0.5)
    k = _l2n(_silu(conv(k_ref, ck_ref, cwk_ref)))
    v = _silu(conv(v_ref, cv_ref, cwv_ref))
    beta, g = _gdn_gates(ba_ref[...], alog_ref[...], dtb_ref[...], h, heads)
    eg = jnp.exp(g)
    outs = []
    group = 4
    for b0 in range(0, tb, group):
        rows = range(b0, min(b0 + group, tb))
        kcols = [_col_from_row(k[b:b + 1]) for b in rows]
        qcols = [_col_from_row(q[b:b + 1]) for b in rows]
        for b, kcol, qcol in zip(rows, kcols, qcols):
            r = slice(b, b + 1)
            s = s_ref[b, 0] * eg[r]
            v_new = beta[r] * (v[r] - jnp.sum(kcol * s, axis=0, keepdims=True))
            s = s + kcol * v_new
            so_ref[b, 0] = s
            outs.append(jnp.sum(qcol * s, axis=0, keepdims=True))
    o = jnp.concatenate(outs, axis=0)
    o_ref[...] = (_rms(o, nw_ref[...]) * _silu(z_ref[...])).astype(BF16)


def _gdn_step(proj, ba, conv_state_t, conv_w, alog_pad, dtb_pad, norm_w, state, layer, heads, tb):
    bsz = proj.shape[0]
    head_blk = lambda off: pl.BlockSpec((tb, LANES), lambda i, h: (i, off + h))
    cs_blk = lambda off: pl.BlockSpec((CONV_WIDTH - 1, tb, LANES), lambda i, h: (0, i, off + h))
    cw_blk = lambda off: pl.BlockSpec((CONV_WIDTH, LANES), lambda i, h: (0, off + h))
    row128 = pl.BlockSpec((1, LANES), lambda i, h: (0, 0))
    st_in = pl.BlockSpec((None, tb, 1, LANES, LANES), lambda i, h: (layer, i, h, 0, 0))
    st_out = pl.BlockSpec((tb, 1, LANES, LANES), lambda i, h: (i, h, 0, 0))
    return pl.pallas_call(
        functools.partial(_gdn_step_kernel, heads=heads),
        grid=(bsz // tb, heads),
        in_specs=[head_blk(0), head_blk(heads), head_blk(2 * heads), head_blk(3 * heads),
                  pl.BlockSpec((tb, LANES), lambda i, h: (i, 0)),
                  cs_blk(0), cs_blk(heads), cs_blk(2 * heads),
                  cw_blk(0), cw_blk(heads), cw_blk(2 * heads), row128, row128, row128, st_in],
        out_specs=[pl.BlockSpec((tb, LANES), lambda i, h: (i, h)), st_out],
        out_shape=[jax.ShapeDtypeStruct((bsz, heads * LANES), BF16),
                   jax.ShapeDtypeStruct(state.shape[1:], F32)],
        compiler_params=_params(("arbitrary", "arbitrary")),
    )(proj, proj, proj, proj, ba, conv_state_t, conv_state_t, conv_state_t, conv_w, conv_w, conv_w,
      alog_pad, dtb_pad, norm_w.reshape(1, LANES), state)


def _lru_gates(x, wa_ref, wi_ref, ba, bi, lam):
    nb = wa_ref.shape[0]
    ga, gi = [], []
    for s in range(nb):
        xs = x[:, s * LANES:(s + 1) * LANES].astype(BF16)
        ga.append(jnp.dot(xs, wa_ref[s], preferred_element_type=F32))
        gi.append(jnp.dot(xs, wi_ref[s], preferred_element_type=F32))
    gate_a = _sigmoid(jnp.concatenate(ga, axis=-1) + ba)
    gate_i = _sigmoid(jnp.concatenate(gi, axis=-1) + bi)
    log_a = -LRU_C * gate_a * _softplus(-lam)
    return log_a, gate_i


def _lru_prompt_kernel(xl_ref, yl_ref, cw_ref, cb_ref, wa_ref, wi_ref, ba_ref, bi_ref, lam_ref,
                       o_ref, hl_ref, xbuf, abuf, bbuf, h_scr):
    blk = pl.program_id(1)
    cb = xl_ref.shape[0]
    width = xl_ref.shape[1]

    @pl.when(blk == 0)
    def _():
        h_scr[...] = jnp.zeros_like(h_scr)
        xbuf[0:8, :] = jnp.zeros((8, width), F32)

    x = _conv_block(xl_ref, xbuf, cw_ref[...]) + cb_ref[...]
    log_a, gate_i = _lru_gates(x, wa_ref, wi_ref, ba_ref[...], bi_ref[...], lam_ref[...])
    mult = jnp.sqrt(_neg_expm1(2.0 * log_a))
    first = (lax.broadcasted_iota(jnp.int32, (cb, 1), 0) == 0) & (blk == 0)
    mult = jnp.where(first, 1.0, mult)
    abuf[...] = jnp.exp(log_a)
    bbuf[...] = mult * gate_i * x

    def step(t, h):
        h = abuf[pl.ds(t, 1), :] * h + bbuf[pl.ds(t, 1), :]
        bbuf[pl.ds(t, 1), :] = h
        return h

    h_last = lax.fori_loop(0, cb, step, h_scr[...], unroll=8)
    h_scr[...] = h_last
    o_ref[...] = (bbuf[...] * _gelu_tanh(yl_ref[...])).astype(BF16)

    @pl.when(blk == pl.num_programs(1) - 1)
    def _():
        hl_ref[0] = h_last


def _lru_prompt(xy, conv_w, conv_b, w_a, w_i, b_a, b_i, lam, batch, seq, cb):
    m = xy.shape[0]
    width = xy.shape[1] // 2
    nblk = seq // cb
    nb = w_a.shape[0]
    vec = pl.BlockSpec((1, width), lambda b, c: (0, 0))
    wblk = pl.BlockSpec((nb, LANES, LANES), lambda b, c: (0, 0, 0))
    out, h_last = pl.pallas_call(
        _lru_prompt_kernel,
        grid=(batch, nblk),
        in_specs=[pl.BlockSpec((cb, width), lambda b, c: (b * nblk + c, 0)),
                  pl.BlockSpec((cb, width), lambda b, c: (b * nblk + c, 1)),
                  pl.BlockSpec((CONV_WIDTH, width), lambda b, c: (0, 0)),
                  vec, wblk, wblk, vec, vec, vec],
        out_specs=[pl.BlockSpec((cb, width), lambda b, c: (b * nblk + c, 0)),
                   pl.BlockSpec((1, 1, width), lambda b, c: (b, 0, 0))],
        out_shape=[jax.ShapeDtypeStruct((m, width), BF16),
                   jax.ShapeDtypeStruct((batch, 1, width), F32)],
        scratch_shapes=[pltpu.VMEM((cb + 8, width), F32), pltpu.VMEM((cb, width), F32),
                        pltpu.VMEM((cb, width), F32), pltpu.VMEM((1, width), F32)],
        compiler_params=_params(("arbitrary", "arbitrary")),
    )(xy, xy, conv_w, conv_b.reshape(1, width), w_a, w_i, b_a.reshape(1, width), b_i.reshape(1, width),
      lam.reshape(1, width))
    return out, h_last.reshape(batch, width)


def _lru_step_kernel(xl_ref, yl_ref, cs_ref, h0_ref, cw_ref, cb_ref, wa_ref, wi_ref, ba_ref, bi_ref, lam_ref,
                     o_ref, h_ref, *, reset):
    w = cw_ref[...]
    x = cs_ref[0] * w[0:1]
    x = x + cs_ref[1] * w[1:2]
    x = x + cs_ref[2] * w[2:3]
    x = x + xl_ref[...] * w[3:4] + cb_ref[...]
    log_a, gate_i = _lru_gates(x, wa_ref, wi_ref, ba_ref[...], bi_ref[...], lam_ref[...])
    mult = 1.0 if reset else jnp.sqrt(_neg_expm1(2.0 * log_a))
    h = jnp.exp(log_a) * h0_ref[...] + mult * gate_i * x
    h_ref[...] = h
    o_ref[...] = (h * _gelu_tanh(yl_ref[...])).astype(BF16)


def _lru_step(xy, conv_state_t, h0, conv_w, conv_b, w_a, w_i, b_a, b_i, lam, reset):
    bsz = xy.shape[0]
    width = xy.shape[1] // 2
    nb = w_a.shape[0]
    vec = pl.BlockSpec((1, width), lambda i: (0, 0))
    wblk = pl.BlockSpec((nb, LANES, LANES), lambda i: (0, 0, 0))
    full = pl.BlockSpec((bsz, width), lambda i: (0, 0))
    return pl.pallas_call(
        functools.partial(_lru_step_kernel, reset=reset),
        grid=(1,),
        in_specs=[full, pl.BlockSpec((bsz, width), lambda i: (0, 1)),
                  pl.BlockSpec((CONV_WIDTH - 1, bsz, width), lambda i: (0, 0, 0)), full,
                  pl.BlockSpec((CONV_WIDTH, width), lambda i: (0, 0)), vec, wblk, wblk, vec, vec, vec],
        out_specs=[full, full],
        out_shape=[jax.ShapeDtypeStruct((bsz, width), BF16), jax.ShapeDtypeStruct((bsz, width), F32)],
        compiler_params=_params(("arbitrary",)),
    )(xy, xy, conv_state_t, h0, conv_w, conv_b.reshape(1, width), w_a, w_i, b_a.reshape(1, width),
      b_i.reshape(1, width), lam.reshape(1, width))


def _hgrn_lower_bound(lb_ref, layer, j=0):
    depth = lb_ref.shape[0]
    raw = [lb_ref[l, j] for l in range(depth)]
    mx = raw[0]
    for r in raw[1:]:
        mx = jnp.maximum(mx, r)
    ex = [jnp.exp(r - mx) for r in raw]
    tot = ex[0]
    for e in ex[1:]:
        tot = tot + e
    wts = [e / tot for e in ex]
    cum = wts[0]
    for w in wts[1:layer + 1]:
        cum = cum + w
    return cum - wts[0]


def _hgrn_inputs(q_raw, f_raw, lb):
    q = _silu(q_raw) * (LANES ** -0.5)
    f = lb + (1.0 - lb) * _sigmoid(f_raw)
    return q, f


def _level_ref(g, half):
    c = g.shape[0]
    sub = 8
    if 2 * half >= sub:
        g3 = g.reshape(c // (2 * half), 2 * half, LANES)
        return jnp.broadcast_to(g3[:, half - 1:half, :], g3.shape).reshape(c, LANES)
    g3 = g.reshape(c // sub, sub, LANES)
    rin = lax.broadcasted_iota(jnp.int32, g3.shape, 1)
    out = jnp.broadcast_to(g3[:, sub - half - 1:sub - half, :], g3.shape)
    for start in range(sub - 4 * half, -1, -2 * half):
        out = jnp.where(rin < start + 2 * half, jnp.broadcast_to(g3[:, start + half - 1:start + half, :], g3.shape),
                        out)
    return out.reshape(c, LANES)


def _hgrn_level_table(c):
    i = np.arange(c)[:, None]
    j = np.arange(c)[None, :]
    nlev = int(math.log2(c))
    top_bit = sum(((i ^ j) >> b > 0).astype(np.int32) for b in range(1, nlev))
    return jnp.asarray(np.where(j < i, nlev - 1 - top_bit, np.where(j == i, nlev, -1)), jnp.int32)


def _hgrn_prompt_kernel(q_ref, f_ref, i_ref, gz_ref, lb_ref, nw_ref, lvl_ref, o_ref, s_ref, s_scr, *, layer):
    hp = s_scr.shape[0]
    blk = pl.program_id(2)
    cb = q_ref.shape[0]
    c = CHUNK

    @pl.when(blk == 0)
    def _():
        s_scr[...] = jnp.zeros_like(s_scr)

    lbs = [_hgrn_lower_bound(lb_ref, layer, j) for j in range(hp)]
    head_lanes = [slice(j * LANES, (j + 1) * LANES) for j in range(hp)]
    tri = (_iota2(c, 1) <= _iota2(c, 0)).astype(BF16)
    nw = nw_ref[...]
    lvl = lvl_ref[...]
    halves = [c >> (i + 1) for i in range(int(math.log2(c)))]

    items = [(slice(ci * c, (ci + 1) * c), j) for ci in range(cb // c) for j in range(hp)]
    qf = [_hgrn_inputs(q_ref[sl, head_lanes[j]], f_ref[sl, head_lanes[j]], lbs[j]) for sl, j in items]
    qs = [q for q, _ in qf]
    ks = [1.0 - f for _, f in qf]
    vs = [i_ref[sl, head_lanes[j]] for sl, j in items]
    gs = [_dot_exact_lhs(tri, jnp.log(f)) for _, f in qf]

    amats = [_dot_nt(q, k) for q, k in zip(qs, ks)]
    for li, half in enumerate(halves):
        es = [jnp.exp(-jnp.abs(g - _level_ref(g, half))) for g in gs]
        ps = [_dot_nt(q * e, k * e) for q, k, e in zip(qs, ks, es)]
        amats = [jnp.where(lvl == li, p, a) for p, a in zip(ps, amats)]
    intra = [_dot(jnp.where(lvl >= 0, a, 0.0), v) for a, v in zip(amats, vs)]

    g_lasts = [g[c - 1:c, :] for g in gs]
    qdec = [q * jnp.exp(g) for q, g in zip(qs, gs)]
    sdec = [_col_from_row(jnp.exp(gl)) for gl in g_lasts]
    sadd = [_dot_tn(k * jnp.exp(gl - g), v) for k, gl, g, v in zip(ks, g_lasts, gs, vs)]

    states = [s_scr[j] for j in range(hp)]
    outs = []
    for (sl, j), o, qd, dec, add in zip(items, intra, qdec, sdec, sadd):
        outs.append(o + _dot(qd, states[j]))
        states[j] = states[j] * dec + add
    for j in range(hp):
        s_scr[j] = states[j]
    for (sl, j), o in zip(items, outs):
        o_ref[sl, head_lanes[j]] = (_rms(o, nw) * _silu(gz_ref[sl, head_lanes[j]])).astype(BF16)

    @pl.when(blk == pl.num_programs(2) - 1)
    def _():
        s_ref[0] = s_scr[...]


def _hgrn_prompt(proj, lb_raw, norm_w, layer, batch, seq, heads, cb):
    m = proj.shape[0]
    nblk = seq // cb
    depth = lb_raw.shape[0]
    hp = HGRN_HEADS_PER_STEP
    assert heads % hp == 0
    groups = heads // hp
    head_blk = lambda part: pl.BlockSpec((cb, hp * LANES), lambda b, h, c: (b * nblk + c, part * groups + h))
    return pl.pallas_call(
        functools.partial(_hgrn_prompt_kernel, layer=layer),
        grid=(batch, groups, nblk),
        in_specs=[head_blk(0), head_blk(1), head_blk(2), head_blk(3),
                  pl.BlockSpec((depth, hp, 1, LANES), lambda b, h, c: (0, h, 0, 0)),
                  pl.BlockSpec((1, LANES), lambda b, h, c: (0, 0)),
                  pl.BlockSpec((CHUNK, CHUNK), lambda b, h, c: (0, 0))],
        out_specs=[pl.BlockSpec((cb, hp * LANES), lambda b, h, c: (b * nblk + c, h)),
                   pl.BlockSpec((1, hp, LANES, LANES), lambda b, h, c: (b, h, 0, 0))],
        out_shape=[jax.ShapeDtypeStruct((m, heads * LANES), BF16),
                   jax.ShapeDtypeStruct((batch, heads, LANES, LANES), F32)],
        scratch_shapes=[pltpu.VMEM((hp, LANES, LANES), F32)],
        compiler_params=_params(("arbitrary", "arbitrary", "arbitrary")),
    )(proj, proj, proj, proj, lb_raw.reshape(depth, heads, 1, LANES), norm_w.reshape(1, LANES),
      _hgrn_level_table(CHUNK))


def _hgrn_step_kernel(q_ref, f_ref, i_ref, gz_ref, lb_ref, nw_ref, s_ref, o_ref, so_ref, *, layer):
    tb = q_ref.shape[0]
    lb = _hgrn_lower_bound(lb_ref, layer)
    q, f = _hgrn_inputs(q_ref[...], f_ref[...], lb)
    v = i_ref[...]
    outs = []
    group = 4
    for b0 in range(0, tb, group):
        rows = range(b0, min(b0 + group, tb))
        fcols = [_col_from_row(f[b:b + 1]) for b in rows]
        qcols = [_col_from_row(q[b:b + 1]) for b in rows]
        for b, fcol, qcol in zip(rows, fcols, qcols):
            s = fcol * (s_ref[b, 0] - v[b:b + 1]) + v[b:b + 1]
            so_ref[b, 0] = s
            outs.append(jnp.sum(qcol * s, axis=0, keepdims=True))
    o = jnp.concatenate(outs, axis=0)
    o_ref[...] = (_rms(o, nw_ref[...]) * _silu(gz_ref[...])).astype(BF16)


def _hgrn_step(proj, lb_raw, norm_w, state, state_idx, layer, heads, tb):
    bsz = proj.shape[0]
    depth = lb_raw.shape[0]
    head_blk = lambda off: pl.BlockSpec((tb, LANES), lambda i, h: (i, off + h))
    st_in = pl.BlockSpec((None, tb, 1, LANES, LANES), lambda i, h: (state_idx, i, h, 0, 0))
    st_out = pl.BlockSpec((tb, 1, LANES, LANES), lambda i, h: (i, h, 0, 0))
    return pl.pallas_call(
        functools.partial(_hgrn_step_kernel, layer=layer),
        grid=(bsz // tb, heads),
        in_specs=[head_blk(0), head_blk(heads), head_blk(2 * heads), head_blk(3 * heads),
                  pl.BlockSpec((depth, 1, 1, LANES), lambda i, h: (0, h, 0, 0)),
                  pl.BlockSpec((1, LANES), lambda i, h: (0, 0)), st_in],
        out_specs=[pl.BlockSpec((tb, LANES), lambda i, h: (i, h)), st_out],
        out_shape=[jax.ShapeDtypeStruct((bsz, heads * LANES), BF16),
                   jax.ShapeDtypeStruct(state.shape[1:], F32)],
        compiler_params=_params(("arbitrary", "arbitrary")),
    )(proj, proj, proj, proj, lb_raw.reshape(depth, heads, 1, LANES), norm_w.reshape(1, LANES), state)


def _trunk(x, mem_k, mem_v, states, wts, big, batch, seq, prompt):
    depth = wts["norm_mix"].shape[0]
    gdn_w = wts["gdn_conv_w"].shape[-1] // 3
    gdn_heads = gdn_w // LANES
    hgrn_heads = wts["hgrn_lb_raw"].shape[1] // LANES
    lru_w = wts["lru_conv_w"].shape[-1]
    bf = {name: list(handles) for name, handles in big.items()}
    new = {"gdn_conv": [], "gdn": [], "lru_conv": [], "lru": [], "hgrn": []}

    def keep(name, idx, w_used):
        w, li = bf[name][idx]
        bf[name][idx] = (w_used, li if w_used is w else 0)

    for l in range(depth):
        g_mix = wts["norm_mix"][l]
        if l % 2 == 0:
            e = l // 2
            proj, _ = _norm_matmul(x, g_mix, wts["ab_head"], e, 0, 4 * gdn_w, TM, TN)
            ba, _ = _norm_matmul(x, g_mix, wts["ab_head"], e, 4 * gdn_w, LANES, TM, LANES)
            xy, _ = _norm_matmul(x, g_mix, wts["ab_lru"], e, 0, 2 * lru_w, TM, TN)
            gargs = (wts["gdn_conv_w"][e], wts["alog_pad"][e], wts["dtb_pad"][e], wts["gdn_norm_w"][e])
            largs = (wts["lru_conv_w"][e], wts["lru_conv_b"][e], wts["lru_w_a"][e], wts["lru_w_i"][e],
                     wts["lru_b_a"][e], wts["lru_b_i"][e], wts["lru_lam"][e])
            w_out, li = bf["ab_w_out"][e]
            if prompt:
                o_a, s_new = _gdn_prompt(proj, ba, *gargs, batch, seq, gdn_heads, 512)
                o_b, h_new = _lru_prompt(xy, *largs, batch, seq, 256)
                tail = lambda t, w: t.reshape(batch, seq, -1)[:, seq - (CONV_WIDTH - 1):, :w]
                new["gdn_conv"].append(tail(proj, 3 * gdn_w))
                new["lru_conv"].append(tail(xy, lru_w))
                x = _matmul2_res(o_a, o_b, w_out, li, x, TM, TN)
            else:
                gc_state, lc_state = states["gdn_conv"][e], states["lru_conv"][e]
                o_a, s_new = _gdn_step(proj, ba, jnp.swapaxes(gc_state, 0, 1), *gargs, states["gdn"], e,
                                       gdn_heads, 32)
                o_b, h_new = _lru_step(xy, jnp.swapaxes(lc_state, 0, 1), states["lru"][e], *largs, reset=False)
                new["gdn_conv"].append(jnp.concatenate([gc_state[:, 1:], proj[:, None, :3 * gdn_w]], axis=1))
                new["lru_conv"].append(jnp.concatenate([lc_state[:, 1:], xy[:, None, :lru_w]], axis=1))
                x, w_used = _matmul_res(jnp.concatenate([o_a, o_b], axis=1), w_out, li, x, TM, TN)
                keep("ab_w_out", e, w_used)
            new["gdn"].append(s_new)
            new["lru"].append(h_new)
        else:
            o_idx = l // 2
            w_in, li = bf["c_w_in"][o_idx]
            proj, w_used = _norm_matmul(x, g_mix, w_in, li, 0, w_in.shape[2], TM, TN)
            keep("c_w_in", o_idx, w_used)
            hargs = (wts["hgrn_lb_raw"], wts["hgrn_norm_w"][o_idx])
            if prompt:
                o_c, s_new = _hgrn_prompt(proj, *hargs, l, batch, seq, hgrn_heads, 512)
            else:
                o_c, s_new = _hgrn_step(proj, *hargs, states["hgrn"], o_idx, l, hgrn_heads, 32)
            new["hgrn"].append(s_new)
            w_out, li = bf["c_w_out"][o_idx]
            x, w_used = _matmul_res(o_c, w_out, li, x, TM, TN)
            keep("c_w_out", o_idx, w_used)
        (w_q, lq), (w_o, lo) = bf["mem_w_q"][l], bf["mem_w_o"][l]
        if prompt:
            assert lq == lo
            x = _mem_attn_prompt(x, wts["norm_mem"][l], w_q, mem_k, mem_v, w_o, l, lq, seq, 256)
        else:
            q, w_used = _norm_matmul(x, wts["norm_mem"][l], w_q, lq, 0, w_q.shape[2], TM, TN)
            keep("mem_w_q", l, w_used)
            o = _mem_attn_sample(q, mem_k, mem_v, l, 8)
            x, w_used = _matmul_res(o.reshape(batch, -1).astype(BF16), w_o, lo, x, TM, TN)
            keep("mem_w_o", l, w_used)
        (w_up, lu), (w_dn, ld) = bf["ffn_w_up"][l], bf["ffn_w_down"][l]
        assert lu == ld
        x, up_used, dn_used = _ffn(x, wts["norm_ffn"][l], w_up, w_dn, lu, wts["norm_final"], l == depth - 1,
                                   TM, TF)
        keep("ffn_w_up", l, up_used)
        keep("ffn_w_down", l, dn_used)
    return x, {n: jnp.stack(v) for n, v in new.items()}, bf


def kernel(x_prompt, x_sample, cache_mem_k, cache_mem_v, state_gdn_conv, state_gdn, state_lru_conv, state_lru, state_hgrn, mem_prompt, norm_mix, norm_mem, norm_mem_kv, norm_ffn, norm_final, ab_w_in, ab_w_out, gdn_conv_w, gdn_a_log, gdn_dt_bias, gdn_norm_w, lru_conv_w, lru_conv_b, lru_w_a, lru_b_a, lru_w_i, lru_b_i, lru_lam, c_w_in, c_w_out, hgrn_lb_raw, hgrn_norm_w, mem_w_q, mem_w_k, mem_w_v, mem_w_o, ffn_w_up, ffn_w_down):
    bp, lp, d = x_prompt.shape
    bs, ls, _ = x_sample.shape
    assert ls == 1, "the sample group advances one token per call"
    depth = norm_mix.shape[0]
    gdn_heads = gdn_a_log.shape[1]
    gdn_w = gdn_heads * LANES
    mem_tokens, mem_heads, mem_hd = cache_mem_k.shape[2:]
    mem_w = mem_heads * mem_hd
    n_ba = 2 * gdn_heads
    assert n_ba <= LANES

    pad_ba = lambda a: jnp.pad(a, ((0, 0), (gdn_heads, LANES - n_ba)))[:, None, :]
    ab_bf = ab_w_in.astype(BF16)
    wts = dict(
        norm_mix=norm_mix, norm_mem=norm_mem, norm_ffn=norm_ffn, norm_final=norm_final,
        ab_head=ab_bf, ab_lru=ab_bf[:, :, 4 * gdn_w + n_ba:],
        gdn_conv_w=gdn_conv_w, alog_pad=pad_ba(gdn_a_log), dtb_pad=pad_ba(gdn_dt_bias), gdn_norm_w=gdn_norm_w,
        lru_conv_w=lru_conv_w, lru_conv_b=lru_conv_b, lru_w_a=lru_w_a.astype(BF16), lru_w_i=lru_w_i.astype(BF16),
        lru_b_a=lru_b_a, lru_b_i=lru_b_i, lru_lam=lru_lam,
        hgrn_lb_raw=hgrn_lb_raw, hgrn_norm_w=hgrn_norm_w,
    )
    big = dict(ab_w_out=ab_w_out, c_w_in=c_w_in, c_w_out=c_w_out, mem_w_q=mem_w_q, mem_w_o=mem_w_o,
               ffn_w_up=ffn_w_up, ffn_w_down=ffn_w_down)
    big = {name: [(w, l) for l in range(w.shape[0])] for name, w in big.items()}

    states = dict(gdn_conv=state_gdn_conv, gdn=state_gdn, lru_conv=state_lru_conv, lru=state_lru,
                  hgrn=state_hgrn)
    y_s, new_s, big_bf = _trunk(x_sample.reshape(bs * ls, d), cache_mem_k, cache_mem_v, states, wts, big,
                                bs, ls, False)

    mem_rows = mem_prompt.reshape(bp * mem_tokens, d)
    w_kv = jnp.concatenate([mem_w_k, mem_w_v], axis=-1).astype(BF16)
    kv = jnp.stack([_norm_matmul(mem_rows, norm_mem_kv[l], w_kv, l, 0, 2 * mem_w, TM, TN)[0] for l in range(depth)])
    p_mem_k = kv[:, :, :mem_w].reshape(depth, bp, mem_tokens, mem_w)
    p_mem_v = kv[:, :, mem_w:].reshape(depth, bp, mem_tokens, mem_w)

    y_p, new_p, _ = _trunk(x_prompt.reshape(bp * lp, d), p_mem_k, p_mem_v, None, wts, big_bf, bp, lp, True)

    order = ("gdn_conv", "gdn", "lru_conv", "lru", "hgrn")
    mem5 = lambda t: t.reshape(depth, bp, mem_tokens, mem_heads, mem_hd)
    return (y_p.reshape(bp, lp, d), y_s.reshape(bs, ls, d), mem5(p_mem_k), mem5(p_mem_v),
            *(new_p[n] for n in order), *(new_s[n] for n in order))
```

```python
import functools
import math

import jax
import jax.numpy as jnp
import numpy as np
from jax import lax
from jax.experimental import pallas as pl
from jax.experimental.pallas import tpu as pltpu

F32 = jnp.float32
BF16 = jnp.bfloat16
EPS = 1e-6
LANES = 128
CONV_WIDTH = 4
LRU_C = 8.0
VMEM_LIMIT_BYTES = 56 * 1024 * 1024
TM = 512
TN = 1024
TF = 512
TN_CAST = 512
TF_CAST = 256
ROW_GROUPS = 4
CHUNK = 128
GDN_HEADS_PER_STEP = 4
HGRN_HEADS_PER_STEP = 4


def _params(sem):
    return pltpu.CompilerParams(dimension_semantics=sem, vmem_limit_bytes=VMEM_LIMIT_BYTES)


def _sigmoid(x):
    return jax.nn.sigmoid(x)


def _silu(x):
    return x * _sigmoid(x)


def _softplus(x):
    return jnp.maximum(x, 0.0) + jnp.log1p(jnp.exp(-jnp.abs(x)))


def _neg_expm1(x):
    t = jnp.tanh(0.5 * x)
    return -2.0 * t / (1.0 - t)


def _gelu_tanh(x):
    c = math.sqrt(2.0 / math.pi)
    return 0.5 * x * (1.0 + jnp.tanh(c * (x + 0.044715 * (x * x * x))))


def _rms(x, w):
    return x * lax.rsqrt(jnp.mean(x * x, axis=-1, keepdims=True) + EPS) * w


def _l2n(x):
    return x * lax.rsqrt(jnp.sum(x * x, axis=-1, keepdims=True) + EPS)


def _dot(a, b):
    return jnp.dot(a.astype(BF16), b.astype(BF16), preferred_element_type=F32)


def _dot_nt(a, b):
    return lax.dot_general(a.astype(BF16), b.astype(BF16), (((1,), (1,)), ((), ())),
                           preferred_element_type=F32)


def _dot_tn(a, b):
    return lax.dot_general(a.astype(BF16), b.astype(BF16), (((0,), (0,)), ((), ())),
                           preferred_element_type=F32)


def _split2(a):
    hi = a.astype(BF16)
    lo = (a - hi.astype(F32)).astype(BF16)
    return hi, lo


def _dot_hp(a_parts, b_parts):
    ah, al = a_parts
    bh, bl = b_parts
    return jnp.dot(jnp.concatenate([ah, ah, al], axis=1), jnp.concatenate([bh, bl, bh], axis=0),
                   preferred_element_type=F32)


def _dot_exact_lhs(t, b):
    b1 = b.astype(BF16)
    r1 = b - b1.astype(F32)
    b2 = r1.astype(BF16)
    b3 = (r1 - b2.astype(F32)).astype(BF16)
    return jnp.dot(jnp.concatenate([t, t, t], axis=1), jnp.concatenate([b1, b2, b3], axis=0),
                   preferred_element_type=F32)


def _iota2(n, axis):
    return lax.broadcasted_iota(jnp.int32, (n, n), axis)


def _col_from_row(row):
    n = row.shape[-1]
    return jnp.broadcast_to(row, (n, n)).T


def _row_groups(rows):
    n = ROW_GROUPS if rows % (8 * ROW_GROUPS) == 0 else 1
    return [slice(r * rows // n, (r + 1) * rows // n) for r in range(n)]


def _norm_matmul_kernel(x_ref, g_ref, w_ref, o_ref, *rest):
    xn_ref = rest[-1]
    if len(rest) == 2:
        rest[0][...] = w_ref[...].astype(BF16)
        w_ref = rest[0]
    first = pl.program_id(1) == 0

    @pl.when(first)
    def _():
        for rows in _row_groups(x_ref.shape[0]):
            xn = _rms(x_ref[rows, :], g_ref[...]).astype(BF16)
            xn_ref[rows, :] = xn
            o_ref[rows, :] = jnp.dot(xn, w_ref[...], preferred_element_type=F32)

    @pl.when(jnp.logical_not(first))
    def _():
        o_ref[...] = jnp.dot(xn_ref[...], w_ref[...], preferred_element_type=F32)


def _norm_matmul(x, g, w, layer, n0, n, tm, tn):
    m, k = x.shape
    cast = w.dtype != BF16
    tm = min(tm, m)
    tn = min(tn, n, TN_CAST) if cast else min(tn, n)
    j0 = n0 // tn
    out_specs = [pl.BlockSpec((tm, tn), lambda i, j: (i, j))]
    out_shape = [jax.ShapeDtypeStruct((m, n), F32)]
    if cast:
        assert m == tm and n0 == 0 and n == w.shape[2], "each weight tile must be visited exactly once"
        out_specs.append(pl.BlockSpec((None, k, tn), lambda i, j: (0, 0, j)))
        out_shape.append(jax.ShapeDtypeStruct((1, k, n), BF16))
    res = pl.pallas_call(
        _norm_matmul_kernel,
        grid=(m // tm, n // tn),
        in_specs=[pl.BlockSpec((tm, k), lambda i, j: (i, 0)),
                  pl.BlockSpec((1, k), lambda i, j: (0, 0)),
                  pl.BlockSpec((None, k, tn), lambda i, j: (layer, 0, j0 + j))],
        out_specs=out_specs,
        out_shape=out_shape,
        scratch_shapes=[pltpu.VMEM((tm, k), BF16)],
        compiler_params=_params(("arbitrary", "arbitrary")),
    )(x, g.reshape(1, k), w)
    return (res[0], res[1]) if cast else (res[0], w)


def _matmul2_res_kernel(a1_ref, a2_ref, w1_ref, w2_ref, r_ref, o_ref):
    acc = jnp.dot(a1_ref[...], w1_ref[...], preferred_element_type=F32)
    acc = acc + jnp.dot(a2_ref[...], w2_ref[...], preferred_element_type=F32)
    o_ref[...] = r_ref[...] + acc


def _matmul2_res(a1, a2, w, layer, res, tm, tn):
    m, kh = a1.shape
    n = w.shape[2]
    tm = min(tm, m)
    tn = min(tn, n)
    return pl.pallas_call(
        _matmul2_res_kernel,
        grid=(m // tm, n // tn),
        in_specs=[pl.BlockSpec((tm, kh), lambda i, j: (i, 0)),
                  pl.BlockSpec((tm, kh), lambda i, j: (i, 0)),
                  pl.BlockSpec((None, kh, tn), lambda i, j: (layer, 0, j)),
                  pl.BlockSpec((None, kh, tn), lambda i, j: (layer, 1, j)),
                  pl.BlockSpec((tm, tn), lambda i, j: (i, j))],
        out_specs=pl.BlockSpec((tm, tn), lambda i, j: (i, j)),
        out_shape=jax.ShapeDtypeStruct((m, n), F32),
        compiler_params=_params(("arbitrary", "arbitrary")),
    )(a1, a2, w, w, res)


def _matmul_res_kernel(a_ref, w_ref, r_ref, o_ref, *wb_ref):
    w = w_ref[...]
    if wb_ref:
        w = w.astype(BF16)
        wb_ref[0][...] = w
    o_ref[...] = r_ref[...] + jnp.dot(a_ref[...], w, preferred_element_type=F32)


def _matmul_res(a, w, layer, res, tm, tn):
    m, k = a.shape
    n = w.shape[2]
    cast = w.dtype != BF16
    tm = min(tm, m)
    tn = min(tn, n, TN_CAST) if cast else min(tn, n)
    out_specs = [pl.BlockSpec((tm, tn), lambda i, j: (i, j))]
    out_shape = [jax.ShapeDtypeStruct((m, n), F32)]
    if cast:
        assert m == tm, "each weight tile must be visited exactly once"
        out_specs.append(pl.BlockSpec((None, k, tn), lambda i, j: (0, 0, j)))
        out_shape.append(jax.ShapeDtypeStruct((1, k, n), BF16))
    out = pl.pallas_call(
        _matmul_res_kernel,
        grid=(m // tm, n // tn),
        in_specs=[pl.BlockSpec((tm, k), lambda i, j: (i, 0)),
                  pl.BlockSpec((None, k, tn), lambda i, j: (layer, 0, j)),
                  pl.BlockSpec((tm, tn), lambda i, j: (i, j))],
        out_specs=out_specs,
        out_shape=out_shape,
        compiler_params=_params(("arbitrary", "arbitrary")),
    )(a, w, res)
    return (out[0], out[1]) if cast else (out[0], w)


def _ffn_kernel(x_ref, g_ref, wu_ref, wd_ref, gf_ref, o_ref, *rest, final_norm):
    xn_ref = rest[-1]
    f = pl.program_id(1)
    if len(rest) == 3:
        rest[0][...] = wu_ref[...].astype(BF16)
        rest[1][...] = wd_ref[...].astype(BF16)
        wu_ref, wd_ref = rest[0], rest[1]

    def mlp(xn):
        h = jnp.dot(xn, wu_ref[...], preferred_element_type=F32)
        h = jnp.square(jnp.maximum(h, 0.0)).astype(BF16)
        return jnp.dot(h, wd_ref[...], preferred_element_type=F32)

    @pl.when(f == 0)
    def _():
        for rows in _row_groups(x_ref.shape[0]):
            x = x_ref[rows, :]
            xn = _rms(x, g_ref[...]).astype(BF16)
            xn_ref[rows, :] = xn
            o_ref[rows, :] = x + mlp(xn)

    @pl.when(f != 0)
    def _():
        o_ref[...] += mlp(xn_ref[...])

    if final_norm:
        @pl.when(f == pl.num_programs(1) - 1)
        def _():
            o_ref[...] = _rms(o_ref[...], gf_ref[...])


def _ffn(x, g, w_up, w_down, layer, g_final, final_norm, tm, tf):
    m, d = x.shape
    dff = w_up.shape[2]
    cast = w_up.dtype != BF16
    tm = min(tm, m)
    tf = min(tf, TF_CAST) if cast else tf
    out_specs = [pl.BlockSpec((tm, d), lambda i, f: (i, 0))]
    out_shape = [jax.ShapeDtypeStruct((m, d), F32)]
    if cast:
        assert m == tm, "each weight tile must be visited exactly once"
        out_specs += [pl.BlockSpec((None, d, tf), lambda i, f: (0, 0, f)),
                      pl.BlockSpec((None, tf, d), lambda i, f: (0, f, 0))]
        out_shape += [jax.ShapeDtypeStruct((1, d, dff), BF16), jax.ShapeDtypeStruct((1, dff, d), BF16)]
    out = pl.pallas_call(
        functools.partial(_ffn_kernel, final_norm=final_norm),
        grid=(m // tm, dff // tf),
        in_specs=[pl.BlockSpec((tm, d), lambda i, f: (i, 0), pipeline_mode=pl.Buffered(1)),
                  pl.BlockSpec((1, d), lambda i, f: (0, 0)),
                  pl.BlockSpec((None, d, tf), lambda i, f: (layer, 0, f)),
                  pl.BlockSpec((None, tf, d), lambda i, f: (layer, f, 0)),
                  pl.BlockSpec((1, d), lambda i, f: (0, 0))],
        out_specs=out_specs,
        out_shape=out_shape,
        scratch_shapes=[pltpu.VMEM((tm, d), BF16)],
        compiler_params=_params(("arbitrary", "arbitrary")),
    )(x, g.reshape(1, d), w_up, w_down, g_final.reshape(1, d))
    return (out[0], out[1], out[2]) if cast else (out[0], w_up, w_down)


def _mem_attn_prompt_kernel(x_ref, g_ref, wq_ref, k_ref, v_ref, wo_ref, o_ref, *, heads):
    x = x_ref[...]
    xn = _rms(x, g_ref[...]).astype(BF16)
    q = jnp.dot(xn, wq_ref[...], preferred_element_type=F32)
    k = k_ref[0].astype(BF16)
    v = v_ref[0].astype(BF16)
    scale = LANES ** -0.5
    outs = []
    for h in range(heads):
        sl = slice(h * LANES, (h + 1) * LANES)
        s = _dot_nt(q[:, sl], k[:, sl]) * scale
        e = jnp.exp(s - jnp.max(s, axis=-1, keepdims=True))
        p = e / jnp.sum(e, axis=-1, keepdims=True)
        outs.append(_dot(p, v[:, sl]))
    o = jnp.concatenate(outs, axis=-1).astype(BF16)
    o_ref[...] = x + jnp.dot(o, wo_ref[...], preferred_element_type=F32)


def _mem_attn_prompt(x, g, w_q, mem_k, mem_v, w_o, layer, w_layer, seq, tl):
    m, d = x.shape
    _, _, t, w = mem_k.shape
    nblk = seq // tl
    return pl.pallas_call(
        functools.partial(_mem_attn_prompt_kernel, heads=w // LANES),
        grid=(m // tl,),
        in_specs=[pl.BlockSpec((tl, d), lambda i: (i, 0)),
                  pl.BlockSpec((1, d), lambda i: (0, 0)),
                  pl.BlockSpec((None, d, w), lambda i: (w_layer, 0, 0)),
                  pl.BlockSpec((None, 1, t, w), lambda i: (layer, i // nblk, 0, 0)),
                  pl.BlockSpec((None, 1, t, w), lambda i: (layer, i // nblk, 0, 0)),
                  pl.BlockSpec((None, w, d), lambda i: (w_layer, 0, 0))],
        out_specs=pl.BlockSpec((tl, d), lambda i: (i, 0)),
        out_shape=jax.ShapeDtypeStruct((m, d), F32),
        compiler_params=_params(("arbitrary",)),
    )(x, g.reshape(1, d), w_q, mem_k, mem_v, w_o)


def _mem_attn_sample_kernel(q_ref, k_ref, v_ref, o_ref, *, tb, heads):
    scale = LANES ** -0.5
    for b in range(tb):
        q = jnp.concatenate([q_ref[b:b + 1, h * LANES:(h + 1) * LANES] for h in range(heads)], axis=0)
        s = jnp.sum(k_ref[b] * (q * scale)[None], axis=-1, keepdims=True)
        e = jnp.exp(s - jnp.max(s, axis=0, keepdims=True))
        o_ref[b] = jnp.sum(e * v_ref[b], axis=0) / jnp.sum(e, axis=0)


def _mem_attn_sample(q, mem_k, mem_v, layer, tb):
    b = q.shape[0]
    _, _, t, heads, hd = mem_k.shape
    kv_blk = pl.BlockSpec((None, tb, t, heads, hd), lambda i: (layer, i, 0, 0, 0))
    return pl.pallas_call(
        functools.partial(_mem_attn_sample_kernel, tb=tb, heads=heads),
        grid=(b // tb,),
        in_specs=[pl.BlockSpec((tb, heads * hd), lambda i: (i, 0)), kv_blk, kv_blk],
        out_specs=pl.BlockSpec((tb, heads, hd), lambda i: (i, 0, 0)),
        out_shape=jax.ShapeDtypeStruct((b, heads, hd), F32),
        compiler_params=_params(("arbitrary",)),
    )(q, mem_k, mem_v)


def _conv_block(x_ref, buf, w):
    cb = x_ref.shape[0]
    buf[8:8 + cb, :] = x_ref[...]
    y = buf[5:5 + cb, :] * w[0:1]
    y = y + buf[6:6 + cb, :] * w[1:2]
    y = y + buf[7:7 + cb, :] * w[2:3]
    y = y + buf[8:8 + cb, :] * w[3:4]
    buf[0:8, :] = buf[cb:cb + 8, :]
    return y


def _gdn_gates(ba, alog, dtb, h, heads):
    lane = lax.broadcasted_iota(jnp.int32, ba.shape, 1)
    beta_all = _sigmoid(ba)
    g_all = -jnp.exp(alog) * _softplus(ba + dtb)
    beta = jnp.sum(jnp.where(lane == h, beta_all, 0.0), axis=1, keepdims=True)
    g = jnp.sum(jnp.where(lane == heads + h, g_all, 0.0), axis=1, keepdims=True)
    return beta, g


def _gdn_prompt_kernel(q_ref, k_ref, v_ref, z_ref, ba_ref, cwq_ref, cwk_ref, cwv_ref, alog_ref, dtb_ref,
                       nw_ref, o_ref, s_ref, qbuf, kbuf, vbuf, s_scr, *, heads):
    hp = s_scr.shape[0]
    blk = pl.program_id(2)
    cb = q_ref.shape[0]
    c = CHUNK
    n_chunks = cb // c

    @pl.when(blk == 0)
    def _():
        s_scr[...] = jnp.zeros_like(s_scr)
        for buf in (qbuf, kbuf, vbuf):
            buf[0:8, :] = jnp.zeros((8, hp * LANES), F32)

    for x_ref, buf in ((q_ref, qbuf), (k_ref, kbuf), (v_ref, vbuf)):
        buf[8:8 + cb, :] = x_ref[...]
    conv_w = (cwq_ref[...], cwk_ref[...], cwv_ref[...])
    ba = ba_ref[...]
    gates = [_gdn_gates(ba, alog_ref[...], dtb_ref[...], pl.program_id(1) * hp + j, heads) for j in range(hp)]
    head_lanes = [slice(j * LANES, (j + 1) * LANES) for j in range(hp)]

    row = _iota2(c, 0)
    col = _iota2(c, 1)
    causal = col <= row
    strict = col < row
    tri = causal.astype(BF16)
    eye = (col == row).astype(F32)
    nw = nw_ref[...]

    def conv_rows(buf, w, r0):
        y = buf[r0 + 5:r0 + 5 + c, :] * w[0:1]
        y = y + buf[r0 + 6:r0 + 6 + c, :] * w[1:2]
        y = y + buf[r0 + 7:r0 + 7 + c, :] * w[2:3]
        return y + buf[r0 + 8:r0 + 8 + c, :] * w[3:4]

    conv = [[_silu(conv_rows(buf, w, ci * c)) for buf, w in zip((qbuf, kbuf, vbuf), conv_w)]
            for ci in range(n_chunks)]
    items = [(ci, j) for ci in range(n_chunks) for j in range(hp)]
    rows = lambda ci: slice(ci * c, (ci + 1) * c)
    qs = [_l2n(conv[ci][0][:, head_lanes[j]]) * (LANES ** -0.5) for ci, j in items]
    ks = [_l2n(conv[ci][1][:, head_lanes[j]]) for ci, j in items]
    vs = [conv[ci][2][:, head_lanes[j]] for ci, j in items]
    betas = [gates[j][0][rows(ci)] for ci, j in items]
    gcs = [_dot_exact_lhs(tri, jnp.broadcast_to(gates[j][1][rows(ci)], (c, c))) for ci, j in items]
    decays = [jnp.exp(jnp.where(causal, gc - gc.T, -jnp.inf)) for gc in gcs]
    kbs = [k * b for k, b in zip(ks, betas)]
    ms = [jnp.where(strict, _dot_nt(kb, k) * dec, 0.0) for kb, k, dec in zip(kbs, ks, decays)]
    xs = [_split2(-m) for m in ms]
    ts = [eye - m for m in ms]
    for _ in range(int(math.log2(c)) - 1):
        xs = [_split2(_dot_hp(xp, xp)) for xp in xs]
        ts = [t + _dot_hp(_split2(t), xp) for t, xp in zip(ts, xs)]
    egs = [jnp.exp(gc) for gc in gcs]
    uws = [_dot(t, jnp.concatenate([v * b, kb * eg], axis=1))
           for t, v, b, kb, eg in zip(ts, vs, betas, kbs, egs)]
    attns = [jnp.where(causal, _dot_nt(q, k) * dec, 0.0) for q, k, dec in zip(qs, ks, decays)]
    g_lasts = [gc[c - 1:c, :] for gc in gcs]
    kd_uws = [_dot_tn(k * jnp.exp(gl - gc), uw) for k, gl, gc, uw in zip(ks, g_lasts, gcs, uws)]
    at_uws = [_dot(attn, uw) for attn, uw in zip(attns, uws)]
    lhs = [jnp.concatenate([q * eg - at[:, LANES:], eye * jnp.exp(gl) - kd[:, LANES:]], axis=0)
           for q, eg, at, gl, kd in zip(qs, egs, at_uws, g_lasts, kd_uws)]

    states = [s_scr[j] for j in range(hp)]
    outs = []
    for it, (ci, j) in enumerate(items):
        both = _dot(lhs[it], states[j])
        outs.append(at_uws[it][:, :LANES] + both[:c])
        states[j] = both[c:] + kd_uws[it][:, :LANES]
    for j in range(hp):
        s_scr[j] = states[j]
    for buf in (qbuf, kbuf, vbuf):
        buf[0:8, :] = buf[cb:cb + 8, :]
    for (ci, j), o in zip(items, outs):
        o_ref[rows(ci), head_lanes[j]] = (_rms(o, nw) * _silu(z_ref[rows(ci), head_lanes[j]])).astype(BF16)

    @pl.when(blk == pl.num_programs(2) - 1)
    def _():
        s_ref[0] = s_scr[...]


def _gdn_prompt(proj, ba, conv_w, alog_pad, dtb_pad, norm_w, batch, seq, heads, cb):
    m = proj.shape[0]
    nblk = seq // cb
    hp = GDN_HEADS_PER_STEP
    assert heads % hp == 0
    groups = heads // hp
    rows = lambda b, h, c: b * nblk + c
    head_blk = lambda part: pl.BlockSpec((cb, hp * LANES), lambda b, h, c: (rows(b, h, c), part * groups + h))
    cw_blk = lambda part: pl.BlockSpec((CONV_WIDTH, hp * LANES), lambda b, h, c: (0, part * groups + h))
    row128 = pl.BlockSpec((1, LANES), lambda b, h, c: (0, 0))
    return pl.pallas_call(
        functools.partial(_gdn_prompt_kernel, heads=heads),
        grid=(batch, groups, nblk),
        in_specs=[head_blk(0), head_blk(1), head_blk(2), head_blk(3),
                  pl.BlockSpec((cb, LANES), lambda b, h, c: (rows(b, h, c), 0)),
                  cw_blk(0), cw_blk(1), cw_blk(2), row128, row128, row128],
        out_specs=[pl.BlockSpec((cb, hp * LANES), lambda b, h, c: (rows(b, h, c), h)),
                   pl.BlockSpec((1, hp, LANES, LANES), lambda b, h, c: (b, h, 0, 0))],
        out_shape=[jax.ShapeDtypeStruct((m, heads * LANES), BF16),
                   jax.ShapeDtypeStruct((batch, heads, LANES, LANES), F32)],
        scratch_shapes=[pltpu.VMEM((cb + 8, hp * LANES), F32)] * 3 + [pltpu.VMEM((hp, LANES, LANES), F32)],
        compiler_params=_params(("arbitrary", "arbitrary", "arbitrary")),
    )(proj, proj, proj, proj, ba, conv_w, conv_w, conv_w, alog_pad, dtb_pad, norm_w.reshape(1, LANES))


def _gdn_step_kernel(q_ref, k_ref, v_ref, z_ref, ba_ref, cq_ref, ck_ref, cv_ref, cwq_ref, cwk_ref, cwv_ref,
                     alog_ref, dtb_ref, nw_ref, s_ref, o_ref, so_ref, *, heads):
    h = pl.program_id(1)
    tb = q_ref.shape[0]

    def conv(x_ref, c_ref, w_ref):
        w = w_ref[...]
        y = c_ref[0] * w[0:1]
        y = y + c_ref[1] * w[1:2]
        y = y + c_ref[2] * w[2:3]
        return y + x_ref[...] * w[3:4]

    q = _l2n(_silu(conv(q_ref, cq_ref, cwq_ref))) * (LANES ** -0.5)
    k = _l2n(_silu(conv(k_ref, ck_ref, cwk_ref)))
    v = _silu(conv(v_ref, cv_ref, cwv_ref))
    beta, g = _gdn_gates(ba_ref[...], alog_ref[...], dtb_ref[...], h, heads)
    eg = jnp.exp(g)
    outs = []
    group = 4
    for b0 in range(0, tb, group):
        rows = range(b0, min(b0 + group, tb))
        kcols = [_col_from_row(k[b:b + 1]) for b in rows]
        for b, kcol in zip(rows, kcols):
            r = slice(b, b + 1)
            s = s_ref[b, 0] * eg[r]
            v_new = beta[r] * (v[r] - jnp.sum(kcol * s, axis=0, keepdims=True))
            s = s + kcol * v_new
            so_ref[b, 0] = s
            outs.append(_dot(q[r], s))
    o = jnp.concatenate(outs, axis=0)
    o_ref[...] = (_rms(o, nw_ref[...]) * _silu(z_ref[...])).astype(BF16)


def _gdn_step(proj, ba, conv_state_t, conv_w, alog_pad, dtb_pad, norm_w, state, layer, heads, tb):
    bsz = proj.shape[0]
    head_blk = lambda off: pl.BlockSpec((tb, LANES), lambda i, h: (i, off + h))
    cs_blk = lambda off: pl.BlockSpec((CONV_WIDTH - 1, tb, LANES), lambda i, h: (0, i, off + h))
    cw_blk = lambda off: pl.BlockSpec((CONV_WIDTH, LANES), lambda i, h: (0, off + h))
    row128 = pl.BlockSpec((1, LANES), lambda i, h: (0, 0))
    st_in = pl.BlockSpec((None, tb, 1, LANES, LANES), lambda i, h: (layer, i, h, 0, 0))
    st_out = pl.BlockSpec((tb, 1, LANES, LANES), lambda i, h: (i, h, 0, 0))
    return pl.pallas_call(
        functools.partial(_gdn_step_kernel, heads=heads),
        grid=(bsz // tb, heads),
        in_specs=[head_blk(0), head_blk(heads), head_blk(2 * heads), head_blk(3 * heads),
                  pl.BlockSpec((tb, LANES), lambda i, h: (i, 0)),
                  cs_blk(0), cs_blk(heads), cs_blk(2 * heads),
                  cw_blk(0), cw_blk(heads), cw_blk(2 * heads), row128, row128, row128, st_in],
        out_specs=[pl.BlockSpec((tb, LANES), lambda i, h: (i, h)), st_out],
        out_shape=[jax.ShapeDtypeStruct((bsz, heads * LANES), BF16),
                   jax.ShapeDtypeStruct(state.shape[1:], F32)],
        compiler_params=_params(("arbitrary", "arbitrary")),
    )(proj, proj, proj, proj, ba, conv_state_t, conv_state_t, conv_state_t, conv_w, conv_w, conv_w,
      alog_pad, dtb_pad, norm_w.reshape(1, LANES), state)


def _lru_gates(x, wa_ref, wi_ref, ba, bi, lam):
    nb = wa_ref.shape[0]
    ga, gi = [], []
    for s in range(nb):
        xs = x[:, s * LANES:(s + 1) * LANES].astype(BF16)
        ga.append(jnp.dot(xs, wa_ref[s], preferred_element_type=F32))
        gi.append(jnp.dot(xs, wi_ref[s], preferred_element_type=F32))
    gate_a = _sigmoid(jnp.concatenate(ga, axis=-1) + ba)
    gate_i = _sigmoid(jnp.concatenate(gi, axis=-1) + bi)
    log_a = -LRU_C * gate_a * _softplus(-lam)
    return log_a, gate_i


def _lru_prompt_kernel(xl_ref, yl_ref, cw_ref, cb_ref, wa_ref, wi_ref, ba_ref, bi_ref, lam_ref,
                       o_ref, hl_ref, xbuf, abuf, bbuf, h_scr):
    blk = pl.program_id(1)
    cb = xl_ref.shape[0]
    width = xl_ref.shape[1]

    @pl.when(blk == 0)
    def _():
        h_scr[...] = jnp.zeros_like(h_scr)
        xbuf[0:8, :] = jnp.zeros((8, width), F32)

    x = _conv_block(xl_ref, xbuf, cw_ref[...]) + cb_ref[...]
    log_a, gate_i = _lru_gates(x, wa_ref, wi_ref, ba_ref[...], bi_ref[...], lam_ref[...])
    mult = jnp.sqrt(_neg_expm1(2.0 * log_a))
    first = (lax.broadcasted_iota(jnp.int32, (cb, 1), 0) == 0) & (blk == 0)
    mult = jnp.where(first, 1.0, mult)
    abuf[...] = jnp.exp(log_a)
    bbuf[...] = mult * gate_i * x

    def step(t, h):
        h = abuf[pl.ds(t, 1), :] * h + bbuf[pl.ds(t, 1), :]
        bbuf[pl.ds(t, 1), :] = h
        return h

    h_last = lax.fori_loop(0, cb, step, h_scr[...], unroll=8)
    h_scr[...] = h_last
    o_ref[...] = (bbuf[...] * _gelu_tanh(yl_ref[...])).astype(BF16)

    @pl.when(blk == pl.num_programs(1) - 1)
    def _():
        hl_ref[0] = h_last


def _lru_prompt(xy, conv_w, conv_b, w_a, w_i, b_a, b_i, lam, batch, seq, cb):
    m = xy.shape[0]
    width = xy.shape[1] // 2
    nblk = seq // cb
    nb = w_a.shape[0]
    vec = pl.BlockSpec((1, width), lambda b, c: (0, 0))
    wblk = pl.BlockSpec((nb, LANES, LANES), lambda b, c: (0, 0, 0))
    out, h_last = pl.pallas_call(
        _lru_prompt_kernel,
        grid=(batch, nblk),
        in_specs=[pl.BlockSpec((cb, width), lambda b, c: (b * nblk + c, 0)),
                  pl.BlockSpec((cb, width), lambda b, c: (b * nblk + c, 1)),
                  pl.BlockSpec((CONV_WIDTH, width), lambda b, c: (0, 0)),
                  vec, wblk, wblk, vec, vec, vec],
        out_specs=[pl.BlockSpec((cb, width), lambda b, c: (b * nblk + c, 0)),
                   pl.BlockSpec((1, 1, width), lambda b, c: (b, 0, 0))],
        out_shape=[jax.ShapeDtypeStruct((m, width), BF16),
                   jax.ShapeDtypeStruct((batch, 1, width), F32)],
        scratch_shapes=[pltpu.VMEM((cb + 8, width), F32), pltpu.VMEM((cb, width), F32),
                        pltpu.VMEM((cb, width), F32), pltpu.VMEM((1, width), F32)],
        compiler_params=_params(("arbitrary", "arbitrary")),
    )(xy, xy, conv_w, conv_b.reshape(1, width), w_a, w_i, b_a.reshape(1, width), b_i.reshape(1, width),
      lam.reshape(1, width))
    return out, h_last.reshape(batch, width)


def _lru_step_kernel(xl_ref, yl_ref, cs_ref, h0_ref, cw_ref, cb_ref, wa_ref, wi_ref, ba_ref, bi_ref, lam_ref,
                     o_ref, h_ref, *, reset):
    w = cw_ref[...]
    x = cs_ref[0] * w[0:1]
    x = x + cs_ref[1] * w[1:2]
    x = x + cs_ref[2] * w[2:3]
    x = x + xl_ref[...] * w[3:4] + cb_ref[...]
    log_a, gate_i = _lru_gates(x, wa_ref, wi_ref, ba_ref[...], bi_ref[...], lam_ref[...])
    mult = 1.0 if reset else jnp.sqrt(_neg_expm1(2.0 * log_a))
    h = jnp.exp(log_a) * h0_ref[...] + mult * gate_i * x
    h_ref[...] = h
    o_ref[...] = (h * _gelu_tanh(yl_ref[...])).astype(BF16)


def _lru_step(xy, conv_state_t, h0, conv_w, conv_b, w_a, w_i, b_a, b_i, lam, reset):
    bsz = xy.shape[0]
    width = xy.shape[1] // 2
    nb = w_a.shape[0]
    vec = pl.BlockSpec((1, width), lambda i: (0, 0))
    wblk = pl.BlockSpec((nb, LANES, LANES), lambda i: (0, 0, 0))
    full = pl.BlockSpec((bsz, width), lambda i: (0, 0))
    return pl.pallas_call(
        functools.partial(_lru_step_kernel, reset=reset),
        grid=(1,),
        in_specs=[full, pl.BlockSpec((bsz, width), lambda i: (0, 1)),
                  pl.BlockSpec((CONV_WIDTH - 1, bsz, width), lambda i: (0, 0, 0)), full,
                  pl.BlockSpec((CONV_WIDTH, width), lambda i: (0, 0)), vec, wblk, wblk, vec, vec, vec],
        out_specs=[full, full],
        out_shape=[jax.ShapeDtypeStruct((bsz, width), BF16), jax.ShapeDtypeStruct((bsz, width), F32)],
        compiler_params=_params(("arbitrary",)),
    )(xy, xy, conv_state_t, h0, conv_w, conv_b.reshape(1, width), w_a, w_i, b_a.reshape(1, width),
      b_i.reshape(1, width), lam.reshape(1, width))


def _hgrn_lower_bound(lb_ref, layer, j=0):
    depth = lb_ref.shape[0]
    raw = [lb_ref[l, j] for l in range(depth)]
    mx = raw[0]
    for r in raw[1:]:
        mx = jnp.maximum(mx, r)
    ex = [jnp.exp(r - mx) for r in raw]
    tot = ex[0]
    for e in ex[1:]:
        tot = tot + e
    wts = [e / tot for e in ex]
    cum = wts[0]
    for w in wts[1:layer + 1]:
        cum = cum + w
    return cum - wts[0]


def _hgrn_inputs(q_raw, f_raw, lb):
    q = _silu(q_raw) * (LANES ** -0.5)
    f = lb + (1.0 - lb) * _sigmoid(f_raw)
    return q, f


def _level_ref(g, half):
    c = g.shape[0]
    sub = 8
    if 2 * half >= sub:
        g3 = g.reshape(c // (2 * half), 2 * half, LANES)
        return jnp.broadcast_to(g3[:, half - 1:half, :], g3.shape).reshape(c, LANES)
    g3 = g.reshape(c // sub, sub, LANES)
    rin = lax.broadcasted_iota(jnp.int32, g3.shape, 1)
    out = jnp.broadcast_to(g3[:, sub - half - 1:sub - half, :], g3.shape)
    for start in range(sub - 4 * half, -1, -2 * half):
        out = jnp.where(rin < start + 2 * half, jnp.broadcast_to(g3[:, start + half - 1:start + half, :], g3.shape),
                        out)
    return out.reshape(c, LANES)


def _hgrn_level_table(c):
    i = np.arange(c)[:, None]
    j = np.arange(c)[None, :]
    nlev = int(math.log2(c))
    top_bit = sum(((i ^ j) >> b > 0).astype(np.int32) for b in range(1, nlev))
    return jnp.asarray(np.where(j < i, nlev - 1 - top_bit, np.where(j == i, nlev, -1)), jnp.int32)


def _hgrn_prompt_kernel(q_ref, f_ref, i_ref, gz_ref, lb_ref, nw_ref, lvl_ref, o_ref, s_ref, s_scr, *, layer):
    hp = s_scr.shape[0]
    blk = pl.program_id(2)
    cb = q_ref.shape[0]
    c = CHUNK

    @pl.when(blk == 0)
    def _():
        s_scr[...] = jnp.zeros_like(s_scr)

    lbs = [_hgrn_lower_bound(lb_ref, layer, j) for j in range(hp)]
    head_lanes = [slice(j * LANES, (j + 1) * LANES) for j in range(hp)]
    tri = (_iota2(c, 1) <= _iota2(c, 0)).astype(BF16)
    nw = nw_ref[...]
    lvl = lvl_ref[...]
    halves = [c >> (i + 1) for i in range(int(math.log2(c)))]

    items = [(slice(ci * c, (ci + 1) * c), j) for ci in range(cb // c) for j in range(hp)]
    qf = [_hgrn_inputs(q_ref[sl, head_lanes[j]], f_ref[sl, head_lanes[j]], lbs[j]) for sl, j in items]
    qs = [q for q, _ in qf]
    ks = [1.0 - f for _, f in qf]
    vs = [i_ref[sl, head_lanes[j]] for sl, j in items]
    gs = [_dot_exact_lhs(tri, jnp.log(f)) for _, f in qf]

    amats = [_dot_nt(q, k) for q, k in zip(qs, ks)]
    for li, half in enumerate(halves):
        es = [jnp.exp(-jnp.abs(g - _level_ref(g, half))) for g in gs]
        ps = [_dot_nt(q * e, k * e) for q, k, e in zip(qs, ks, es)]
        amats = [jnp.where(lvl == li, p, a) for p, a in zip(ps, amats)]
    intra = [_dot(jnp.where(lvl >= 0, a, 0.0), v) for a, v in zip(amats, vs)]

    g_lasts = [g[c - 1:c, :] for g in gs]
    qdec = [q * jnp.exp(g) for q, g in zip(qs, gs)]
    sdec = [_col_from_row(jnp.exp(gl)) for gl in g_lasts]
    sadd = [_dot_tn(k * jnp.exp(gl - g), v) for k, gl, g, v in zip(ks, g_lasts, gs, vs)]

    states = [s_scr[j] for j in range(hp)]
    outs = []
    for (sl, j), o, qd, dec, add in zip(items, intra, qdec, sdec, sadd):
        outs.append(o + _dot(qd, states[j]))
        states[j] = states[j] * dec + add
    for j in range(hp):
        s_scr[j] = states[j]
    for (sl, j), o in zip(items, outs):
        o_ref[sl, head_lanes[j]] = (_rms(o, nw) * _silu(gz_ref[sl, head_lanes[j]])).astype(BF16)

    @pl.when(blk == pl.num_programs(2) - 1)
    def _():
        s_ref[0] = s_scr[...]


def _hgrn_prompt(proj, lb_raw, norm_w, layer, batch, seq, heads, cb):
    m = proj.shape[0]
    nblk = seq // cb
    depth = lb_raw.shape[0]
    hp = HGRN_HEADS_PER_STEP
    assert heads % hp == 0
    groups = heads // hp
    head_blk = lambda part: pl.BlockSpec((cb, hp * LANES), lambda b, h, c: (b * nblk + c, part * groups + h))
    return pl.pallas_call(
        functools.partial(_hgrn_prompt_kernel, layer=layer),
        grid=(batch, groups, nblk),
        in_specs=[head_blk(0), head_blk(1), head_blk(2), head_blk(3),
                  pl.BlockSpec((depth, hp, 1, LANES), lambda b, h, c: (0, h, 0, 0)),
                  pl.BlockSpec((1, LANES), lambda b, h, c: (0, 0)),
                  pl.BlockSpec((CHUNK, CHUNK), lambda b, h, c: (0, 0))],
        out_specs=[pl.BlockSpec((cb, hp * LANES), lambda b, h, c: (b * nblk + c, h)),
                   pl.BlockSpec((1, hp, LANES, LANES), lambda b, h, c: (b, h, 0, 0))],
        out_shape=[jax.ShapeDtypeStruct((m, heads * LANES), BF16),
                   jax.ShapeDtypeStruct((batch, heads, LANES, LANES), F32)],
        scratch_shapes=[pltpu.VMEM((hp, LANES, LANES), F32)],
        compiler_params=_params(("arbitrary", "arbitrary", "arbitrary")),
    )(proj, proj, proj, proj, lb_raw.reshape(depth, heads, 1, LANES), norm_w.reshape(1, LANES),
      _hgrn_level_table(CHUNK))


def _hgrn_step_kernel(q_ref, f_ref, i_ref, gz_ref, lb_ref, nw_ref, s_ref, o_ref, so_ref, *, layer):
    tb = q_ref.shape[0]
    lb = _hgrn_lower_bound(lb_ref, layer)
    q, f = _hgrn_inputs(q_ref[...], f_ref[...], lb)
    v = i_ref[...]
    outs = []
    group = 4
    for b0 in range(0, tb, group):
        rows = range(b0, min(b0 + group, tb))
        fcols = [_col_from_row(f[b:b + 1]) for b in rows]
        for b, fcol in zip(rows, fcols):
            s = fcol * (s_ref[b, 0] - v[b:b + 1]) + v[b:b + 1]
            so_ref[b, 0] = s
            outs.append(_dot(q[b:b + 1], s))
    o = jnp.concatenate(outs, axis=0)
    o_ref[...] = (_rms(o, nw_ref[...]) * _silu(gz_ref[...])).astype(BF16)


def _hgrn_step(proj, lb_raw, norm_w, state, state_idx, layer, heads, tb):
    bsz = proj.shape[0]
    depth = lb_raw.shape[0]
    head_blk = lambda off: pl.BlockSpec((tb, LANES), lambda i, h: (i, off + h))
    st_in = pl.BlockSpec((None, tb, 1, LANES, LANES), lambda i, h: (state_idx, i, h, 0, 0))
    st_out = pl.BlockSpec((tb, 1, LANES, LANES), lambda i, h: (i, h, 0, 0))
    return pl.pallas_call(
        functools.partial(_hgrn_step_kernel, layer=layer),
        grid=(bsz // tb, heads),
        in_specs=[head_blk(0), head_blk(heads), head_blk(2 * heads), head_blk(3 * heads),
                  pl.BlockSpec((depth, 1, 1, LANES), lambda i, h: (0, h, 0, 0)),
                  pl.BlockSpec((1, LANES), lambda i, h: (0, 0)), st_in],
        out_specs=[pl.BlockSpec((tb, LANES), lambda i, h: (i, h)), st_out],
        out_shape=[jax.ShapeDtypeStruct((bsz, heads * LANES), BF16),
                   jax.ShapeDtypeStruct(state.shape[1:], F32)],
        compiler_params=_params(("arbitrary", "arbitrary")),
    )(proj, proj, proj, proj, lb_raw.reshape(depth, heads, 1, LANES), norm_w.reshape(1, LANES), state)


def _trunk(x, mem_k, mem_v, states, wts, big, batch, seq, prompt):
    depth = wts["norm_mix"].shape[0]
    gdn_w = wts["gdn_conv_w"].shape[-1] // 3
    gdn_heads = gdn_w // LANES
    hgrn_heads = wts["hgrn_lb_raw"].shape[1] // LANES
    lru_w = wts["lru_conv_w"].shape[-1]
    bf = {name: list(handles) for name, handles in big.items()}
    new = {"gdn_conv": [], "gdn": [], "lru_conv": [], "lru": [], "hgrn": []}

    def keep(name, idx, w_used):
        w, li = bf[name][idx]
        bf[name][idx] = (w_used, li if w_used is w else 0)

    for l in range(depth):
        g_mix = wts["norm_mix"][l]
        if l % 2 == 0:
            e = l // 2
            proj, _ = _norm_matmul(x, g_mix, wts["ab_head"], e, 0, 4 * gdn_w, TM, TN)
            ba, _ = _norm_matmul(x, g_mix, wts["ab_head"], e, 4 * gdn_w, LANES, TM, LANES)
            xy, _ = _norm_matmul(x, g_mix, wts["ab_lru"], e, 0, 2 * lru_w, TM, TN)
            gargs = (wts["gdn_conv_w"][e], wts["alog_pad"][e], wts["dtb_pad"][e], wts["gdn_norm_w"][e])
            largs = (wts["lru_conv_w"][e], wts["lru_conv_b"][e], wts["lru_w_a"][e], wts["lru_w_i"][e],
                     wts["lru_b_a"][e], wts["lru_b_i"][e], wts["lru_lam"][e])
            w_out, li = bf["ab_w_out"][e]
            if prompt:
                o_a, s_new = _gdn_prompt(proj, ba, *gargs, batch, seq, gdn_heads, 512)
                o_b, h_new = _lru_prompt(xy, *largs, batch, seq, 256)
                tail = lambda t, w: t.reshape(batch, seq, -1)[:, seq - (CONV_WIDTH - 1):, :w]
                new["gdn_conv"].append(tail(proj, 3 * gdn_w))
                new["lru_conv"].append(tail(xy, lru_w))
                x = _matmul2_res(o_a, o_b, w_out, li, x, TM, TN)
            else:
                gc_state, lc_state = states["gdn_conv"][e], states["lru_conv"][e]
                o_a, s_new = _gdn_step(proj, ba, jnp.swapaxes(gc_state, 0, 1), *gargs, states["gdn"], e,
                                       gdn_heads, 32)
                o_b, h_new = _lru_step(xy, jnp.swapaxes(lc_state, 0, 1), states["lru"][e], *largs, reset=False)
                new["gdn_conv"].append(jnp.concatenate([gc_state[:, 1:], proj[:, None, :3 * gdn_w]], axis=1))
                new["lru_conv"].append(jnp.concatenate([lc_state[:, 1:], xy[:, None, :lru_w]], axis=1))
                x, w_used = _matmul_res(jnp.concatenate([o_a, o_b], axis=1), w_out, li, x, TM, TN)
                keep("ab_w_out", e, w_used)
            new["gdn"].append(s_new)
            new["lru"].append(h_new)
        else:
            o_idx = l // 2
            w_in, li = bf["c_w_in"][o_idx]
            proj, w_used = _norm_matmul(x, g_mix, w_in, li, 0, w_in.shape[2], TM, TN)
            keep("c_w_in", o_idx, w_used)
            hargs = (wts["hgrn_lb_raw"], wts["hgrn_norm_w"][o_idx])
            if prompt:
                o_c, s_new = _hgrn_prompt(proj, *hargs, l, batch, seq, hgrn_heads, 512)
            else:
                o_c, s_new = _hgrn_step(proj, *hargs, states["hgrn"], o_idx, l, hgrn_heads, 32)
            new["hgrn"].append(s_new)
            w_out, li = bf["c_w_out"][o_idx]
            x, w_used = _matmul_res(o_c, w_out, li, x, TM, TN)
            keep("c_w_out", o_idx, w_used)
        (w_q, lq), (w_o, lo) = bf["mem_w_q"][l], bf["mem_w_o"][l]
        if prompt:
            assert lq == lo
            x = _mem_attn_prompt(x, wts["norm_mem"][l], w_q, mem_k, mem_v, w_o, l, lq, seq, 256)
        else:
            q, w_used = _norm_matmul(x, wts["norm_mem"][l], w_q, lq, 0, w_q.shape[2], TM, TN)
            keep("mem_w_q", l, w_used)
            o = _mem_attn_sample(q, mem_k, mem_v, l, 8)
            x, w_used = _matmul_res(o.reshape(batch, -1).astype(BF16), w_o, lo, x, TM, TN)
            keep("mem_w_o", l, w_used)
        (w_up, lu), (w_dn, ld) = bf["ffn_w_up"][l], bf["ffn_w_down"][l]
        assert lu == ld
        x, up_used, dn_used = _ffn(x, wts["norm_ffn"][l], w_up, w_dn, lu, wts["norm_final"], l == depth - 1,
                                   TM, TF)
        keep("ffn_w_up", l, up_used)
        keep("ffn_w_down", l, dn_used)
    return x, {n: jnp.stack(v) for n, v in new.items()}, bf


def kernel(x_prompt, x_sample, cache_mem_k, cache_mem_v, state_gdn_conv, state_gdn, state_lru_conv, state_lru, state_hgrn, mem_prompt, norm_mix, norm_mem, norm_mem_kv, norm_ffn, norm_final, ab_w_in, ab_w_out, gdn_conv_w, gdn_a_log, gdn_dt_bias, gdn_norm_w, lru_conv_w, lru_conv_b, lru_w_a, lru_b_a, lru_w_i, lru_b_i, lru_lam, c_w_in, c_w_out, hgrn_lb_raw, hgrn_norm_w, mem_w_q, mem_w_k, mem_w_v, mem_w_o, ffn_w_up, ffn_w_down):
    bp, lp, d = x_prompt.shape
    bs, ls, _ = x_sample.shape
    assert ls == 1, "the sample group advances one token per call"
    depth = norm_mix.shape[0]
    gdn_heads = gdn_a_log.shape[1]
    gdn_w = gdn_heads * LANES
    mem_tokens, mem_heads, mem_hd = cache_mem_k.shape[2:]
    mem_w = mem_heads * mem_hd
    n_ba = 2 * gdn_heads
    assert n_ba <= LANES

    pad_ba = lambda a: jnp.pad(a, ((0, 0), (gdn_heads, LANES - n_ba)))[:, None, :]
    ab_bf = ab_w_in.astype(BF16)
    wts = dict(
        norm_mix=norm_mix, norm_mem=norm_mem, norm_ffn=norm_ffn, norm_final=norm_final,
        ab_head=ab_bf, ab_lru=ab_bf[:, :, 4 * gdn_w + n_ba:],
        gdn_conv_w=gdn_conv_w, alog_pad=pad_ba(gdn_a_log), dtb_pad=pad_ba(gdn_dt_bias), gdn_norm_w=gdn_norm_w,
        lru_conv_w=lru_conv_w, lru_conv_b=lru_conv_b, lru_w_a=lru_w_a.astype(BF16), lru_w_i=lru_w_i.astype(BF16),
        lru_b_a=lru_b_a, lru_b_i=lru_b_i, lru_lam=lru_lam,
        hgrn_lb_raw=hgrn_lb_raw, hgrn_norm_w=hgrn_norm_w,
    )
    big = dict(ab_w_out=ab_w_out, c_w_in=c_w_in, c_w_out=c_w_out, mem_w_q=mem_w_q, mem_w_o=mem_w_o,
               ffn_w_up=ffn_w_up, ffn_w_down=ffn_w_down)
    big = {name: [(w, l) for l in range(w.shape[0])] for name, w in big.items()}

    states = dict(gdn_conv=state_gdn_conv, gdn=state_gdn, lru_conv=state_lru_conv, lru=state_lru,
                  hgrn=state_hgrn)
    y_s, new_s, big_bf = _trunk(x_sample.reshape(bs * ls, d), cache_mem_k, cache_mem_v, states, wts, big,
                                bs, ls, False)

    mem_rows = mem_prompt.reshape(bp * mem_tokens, d)
    w_kv = jnp.concatenate([mem_w_k, mem_w_v], axis=-1).astype(BF16)
    kv = jnp.stack([_norm_matmul(mem_rows, norm_mem_kv[l], w_kv, l, 0, 2 * mem_w, TM, TN)[0] for l in range(depth)])
    p_mem_k = kv[:, :, :mem_w].reshape(depth, bp, mem_tokens, mem_w)
    p_mem_v = kv[:, :, mem_w:].reshape(depth, bp, mem_tokens, mem_w)

    y_p, new_p, _ = _trunk(x_prompt.reshape(bp * lp, d), p_mem_k, p_mem_v, None, wts, big_bf, bp, lp, True)

    order = ("gdn_conv", "gdn", "lru_conv", "lru", "hgrn")
    mem5 = lambda t: t.reshape(depth, bp, mem_tokens, mem_heads, mem_hd)
    return (y_p.reshape(bp, lp, d), y_s.reshape(bs, ls, d), mem5(p_mem_k), mem5(p_mem_v),
            *(new_p[n] for n in order), *(new_s[n] for n in order))
```

```python
import functools
import math

import jax
import jax.numpy as jnp
import numpy as np
from jax import lax
from jax.experimental import pallas as pl
from jax.experimental.pallas import tpu as pltpu

F32 = jnp.float32
BF16 = jnp.bfloat16
EPS = 1e-6
LANES = 128
CONV_WIDTH = 4
LRU_C = 8.0
VMEM_LIMIT_BYTES = 56 * 1024 * 1024
TM = 512
TN = 1024
TF = 512
TN_CAST = 512
TF_CAST = 256
ROW_GROUPS = 4
CHUNK = 128
GDN_HEADS_PER_STEP = 4
HGRN_HEADS_PER_STEP = 4


def _params(sem):
    return pltpu.CompilerParams(dimension_semantics=sem, vmem_limit_bytes=VMEM_LIMIT_BYTES)


def _sigmoid(x):
    return jax.nn.sigmoid(x)


def _silu(x):
    return x * _sigmoid(x)


def _softplus(x):
    return jnp.maximum(x, 0.0) + jnp.log1p(jnp.exp(-jnp.abs(x)))


def _neg_expm1(x):
    t = jnp.tanh(0.5 * x)
    return -2.0 * t / (1.0 - t)


def _gelu_tanh(x):
    c = math.sqrt(2.0 / math.pi)
    return 0.5 * x * (1.0 + jnp.tanh(c * (x + 0.044715 * (x * x * x))))


def _rms(x, w):
    return x * lax.rsqrt(jnp.mean(x * x, axis=-1, keepdims=True) + EPS) * w


def _l2n(x):
    return x * lax.rsqrt(jnp.sum(x * x, axis=-1, keepdims=True) + EPS)


def _dot(a, b):
    return jnp.dot(a.astype(BF16), b.astype(BF16), preferred_element_type=F32)


def _dot_nt(a, b):
    return lax.dot_general(a.astype(BF16), b.astype(BF16), (((1,), (1,)), ((), ())),
                           preferred_element_type=F32)


def _dot_tn(a, b):
    return lax.dot_general(a.astype(BF16), b.astype(BF16), (((0,), (0,)), ((), ())),
                           preferred_element_type=F32)


def _split2(a):
    hi = a.astype(BF16)
    lo = (a - hi.astype(F32)).astype(BF16)
    return hi, lo


def _dot_hp(a_parts, b_parts):
    ah, al = a_parts
    bh, bl = b_parts
    return jnp.dot(jnp.concatenate([ah, ah, al], axis=1), jnp.concatenate([bh, bl, bh], axis=0),
                   preferred_element_type=F32)


def _dot_exact_lhs(t, b):
    b1 = b.astype(BF16)
    r1 = b - b1.astype(F32)
    b2 = r1.astype(BF16)
    b3 = (r1 - b2.astype(F32)).astype(BF16)
    return jnp.dot(jnp.concatenate([t, t, t], axis=1), jnp.concatenate([b1, b2, b3], axis=0),
                   preferred_element_type=F32)


def _iota2(n, axis):
    return lax.broadcasted_iota(jnp.int32, (n, n), axis)


def _col_from_row(row):
    n = row.shape[-1]
    return jnp.broadcast_to(row, (n, n)).T


def _row_groups(rows):
    n = ROW_GROUPS if rows % (8 * ROW_GROUPS) == 0 else 1
    return [slice(r * rows // n, (r + 1) * rows // n) for r in range(n)]


def _norm_matmul_kernel(x_ref, g_ref, w_ref, o_ref, *rest):
    xn_ref = rest[-1]
    if len(rest) == 2:
        rest[0][...] = w_ref[...].astype(BF16)
        w_ref = rest[0]
    first = pl.program_id(1) == 0

    @pl.when(first)
    def _():
        for rows in _row_groups(x_ref.shape[0]):
            xn = _rms(x_ref[rows, :], g_ref[...]).astype(BF16)
            xn_ref[rows, :] = xn
            o_ref[rows, :] = jnp.dot(xn, w_ref[...], preferred_element_type=F32)

    @pl.when(jnp.logical_not(first))
    def _():
        o_ref[...] = jnp.dot(xn_ref[...], w_ref[...], preferred_element_type=F32)


def _norm_matmul(x, g, w, layer, n0, n, tm, tn):
    m, k = x.shape
    cast = w.dtype != BF16
    tm = min(tm, m)
    tn = min(tn, n, TN_CAST) if cast else min(tn, n)
    j0 = n0 // tn
    out_specs = [pl.BlockSpec((tm, tn), lambda i, j: (i, j))]
    out_shape = [jax.ShapeDtypeStruct((m, n), F32)]
    if cast:
        assert m == tm and n0 == 0 and n == w.shape[2], "each weight tile must be visited exactly once"
        out_specs.append(pl.BlockSpec((None, k, tn), lambda i, j: (0, 0, j)))
        out_shape.append(jax.ShapeDtypeStruct((1, k, n), BF16))
    res = pl.pallas_call(
        _norm_matmul_kernel,
        grid=(m // tm, n // tn),
        in_specs=[pl.BlockSpec((tm, k), lambda i, j: (i, 0)),
                  pl.BlockSpec((1, k), lambda i, j: (0, 0)),
                  pl.BlockSpec((None, k, tn), lambda i, j: (layer, 0, j0 + j))],
        out_specs=out_specs,
        out_shape=out_shape,
        scratch_shapes=[pltpu.VMEM((tm, k), BF16)],
        compiler_params=_params(("arbitrary", "arbitrary")),
    )(x, g.reshape(1, k), w)
    return (res[0], res[1]) if cast else (res[0], w)


def _matmul2_res_kernel(a1_ref, a2_ref, w1_ref, w2_ref, r_ref, o_ref):
    acc = jnp.dot(a1_ref[...], w1_ref[...], preferred_element_type=F32)
    acc = acc + jnp.dot(a2_ref[...], w2_ref[...], preferred_element_type=F32)
    o_ref[...] = r_ref[...] + acc


def _matmul2_res(a1, a2, w, layer, res, tm, tn):
    m, kh = a1.shape
    n = w.shape[2]
    tm = min(tm, m)
    tn = min(tn, n)
    return pl.pallas_call(
        _matmul2_res_kernel,
        grid=(m // tm, n // tn),
        in_specs=[pl.BlockSpec((tm, kh), lambda i, j: (i, 0)),
                  pl.BlockSpec((tm, kh), lambda i, j: (i, 0)),
                  pl.BlockSpec((None, kh, tn), lambda i, j: (layer, 0, j)),
                  pl.BlockSpec((None, kh, tn), lambda i, j: (layer, 1, j)),
                  pl.BlockSpec((tm, tn), lambda i, j: (i, j))],
        out_specs=pl.BlockSpec((tm, tn), lambda i, j: (i, j)),
        out_shape=jax.ShapeDtypeStruct((m, n), F32),
        compiler_params=_params(("arbitrary", "arbitrary")),
    )(a1, a2, w, w, res)


def _matmul_res_kernel(a_ref, w_ref, r_ref, o_ref, *wb_ref):
    w = w_ref[...]
    if wb_ref:
        w = w.astype(BF16)
        wb_ref[0][...] = w
    o_ref[...] = r_ref[...] + jnp.dot(a_ref[...], w, preferred_element_type=F32)


def _matmul_res(a, w, layer, res, tm, tn):
    m, k = a.shape
    n = w.shape[2]
    cast = w.dtype != BF16
    tm = min(tm, m)
    tn = min(tn, n, TN_CAST) if cast else min(tn, n)
    out_specs = [pl.BlockSpec((tm, tn), lambda i, j: (i, j))]
    out_shape = [jax.ShapeDtypeStruct((m, n), F32)]
    if cast:
        assert m == tm, "each weight tile must be visited exactly once"
        out_specs.append(pl.BlockSpec((None, k, tn), lambda i, j: (0, 0, j)))
        out_shape.append(jax.ShapeDtypeStruct((1, k, n), BF16))
    out = pl.pallas_call(
        _matmul_res_kernel,
        grid=(m // tm, n // tn),
        in_specs=[pl.BlockSpec((tm, k), lambda i, j: (i, 0)),
                  pl.BlockSpec((None, k, tn), lambda i, j: (layer, 0, j)),
                  pl.BlockSpec((tm, tn), lambda i, j: (i, j))],
        out_specs=out_specs,
        out_shape=out_shape,
        compiler_params=_params(("arbitrary", "arbitrary")),
    )(a, w, res)
    return (out[0], out[1]) if cast else (out[0], w)


def _ffn_kernel(x_ref, g_ref, wu_ref, wd_ref, gf_ref, o_ref, *rest, final_norm):
    xn_ref = rest[-1]
    f = pl.program_id(1)
    if len(rest) == 3:
        rest[0][...] = wu_ref[...].astype(BF16)
        rest[1][...] = wd_ref[...].astype(BF16)
        wu_ref, wd_ref = rest[0], rest[1]

    def mlp(xn):
        h = jnp.dot(xn, wu_ref[...], preferred_element_type=F32)
        h = jnp.square(jnp.maximum(h, 0.0)).astype(BF16)
        return jnp.dot(h, wd_ref[...], preferred_element_type=F32)

    @pl.when(f == 0)
    def _():
        for rows in _row_groups(x_ref.shape[0]):
            x = x_ref[rows, :]
            xn = _rms(x, g_ref[...]).astype(BF16)
            xn_ref[rows, :] = xn
            o_ref[rows, :] = x + mlp(xn)

    @pl.when(f != 0)
    def _():
        o_ref[...] += mlp(xn_ref[...])

    if final_norm:
        @pl.when(f == pl.num_programs(1) - 1)
        def _():
            o_ref[...] = _rms(o_ref[...], gf_ref[...])


def _ffn(x, g, w_up, w_down, layer, g_final, final_norm, tm, tf):
    m, d = x.shape
    dff = w_up.shape[2]
    cast = w_up.dtype != BF16
    tm = min(tm, m)
    tf = min(tf, TF_CAST) if cast else tf
    out_specs = [pl.BlockSpec((tm, d), lambda i, f: (i, 0))]
    out_shape = [jax.ShapeDtypeStruct((m, d), F32)]
    if cast:
        assert m == tm, "each weight tile must be visited exactly once"
        out_specs += [pl.BlockSpec((None, d, tf), lambda i, f: (0, 0, f)),
                      pl.BlockSpec((None, tf, d), lambda i, f: (0, f, 0))]
        out_shape += [jax.ShapeDtypeStruct((1, d, dff), BF16), jax.ShapeDtypeStruct((1, dff, d), BF16)]
    out = pl.pallas_call(
        functools.partial(_ffn_kernel, final_norm=final_norm),
        grid=(m // tm, dff // tf),
        in_specs=[pl.BlockSpec((tm, d), lambda i, f: (i, 0), pipeline_mode=pl.Buffered(1)),
                  pl.BlockSpec((1, d), lambda i, f: (0, 0)),
                  pl.BlockSpec((None, d, tf), lambda i, f: (layer, 0, f)),
                  pl.BlockSpec((None, tf, d), lambda i, f: (layer, f, 0)),
                  pl.BlockSpec((1, d), lambda i, f: (0, 0))],
        out_specs=out_specs,
        out_shape=out_shape,
        scratch_shapes=[pltpu.VMEM((tm, d), BF16)],
        compiler_params=_params(("arbitrary", "arbitrary")),
    )(x, g.reshape(1, d), w_up, w_down, g_final.reshape(1, d))
    return (out[0], out[1], out[2]) if cast else (out[0], w_up, w_down)


def _mem_attn_prompt_kernel(x_ref, g_ref, wq_ref, k_ref, v_ref, wo_ref, o_ref, *, heads):
    x = x_ref[...]
    xn = _rms(x, g_ref[...]).astype(BF16)
    q = jnp.dot(xn, wq_ref[...], preferred_element_type=F32)
    k = k_ref[0].astype(BF16)
    v = v_ref[0].astype(BF16)
    scale = LANES ** -0.5
    outs = []
    for h in range(heads):
        sl = slice(h * LANES, (h + 1) * LANES)
        s = _dot_nt(q[:, sl], k[:, sl]) * scale
        e = jnp.exp(s - jnp.max(s, axis=-1, keepdims=True))
        p = e / jnp.sum(e, axis=-1, keepdims=True)
        outs.append(_dot(p, v[:, sl]))
    o = jnp.concatenate(outs, axis=-1).astype(BF16)
    o_ref[...] = x + jnp.dot(o, wo_ref[...], preferred_element_type=F32)


def _mem_attn_prompt(x, g, w_q, mem_k, mem_v, w_o, layer, w_layer, seq, tl):
    m, d = x.shape
    _, _, t, w = mem_k.shape
    nblk = seq // tl
    return pl.pallas_call(
        functools.partial(_mem_attn_prompt_kernel, heads=w // LANES),
        grid=(m // tl,),
        in_specs=[pl.BlockSpec((tl, d), lambda i: (i, 0)),
                  pl.BlockSpec((1, d), lambda i: (0, 0)),
                  pl.BlockSpec((None, d, w), lambda i: (w_layer, 0, 0), pipeline_mode=pl.Buffered(1)),
                  pl.BlockSpec((None, 1, t, w), lambda i: (layer, i // nblk, 0, 0)),
                  pl.BlockSpec((None, 1, t, w), lambda i: (layer, i // nblk, 0, 0)),
                  pl.BlockSpec((None, w, d), lambda i: (w_layer, 0, 0), pipeline_mode=pl.Buffered(1))],
        out_specs=pl.BlockSpec((tl, d), lambda i: (i, 0)),
        out_shape=jax.ShapeDtypeStruct((m, d), F32),
        compiler_params=_params(("arbitrary",)),
    )(x, g.reshape(1, d), w_q, mem_k, mem_v, w_o)


def _mem_attn_sample_kernel(q_ref, k_ref, v_ref, o_ref, *, tb, heads):
    scale = LANES ** -0.5
    for b in range(tb):
        q = jnp.concatenate([q_ref[b:b + 1, h * LANES:(h + 1) * LANES] for h in range(heads)], axis=0)
        s = jnp.sum(k_ref[b] * (q * scale)[None], axis=-1, keepdims=True)
        e = jnp.exp(s - jnp.max(s, axis=0, keepdims=True))
        o_ref[b] = jnp.sum(e * v_ref[b], axis=0) / jnp.sum(e, axis=0)


def _mem_attn_sample(q, mem_k, mem_v, layer, tb):
    b = q.shape[0]
    _, _, t, heads, hd = mem_k.shape
    kv_blk = pl.BlockSpec((None, tb, t, heads, hd), lambda i: (layer, i, 0, 0, 0))
    return pl.pallas_call(
        functools.partial(_mem_attn_sample_kernel, tb=tb, heads=heads),
        grid=(b // tb,),
        in_specs=[pl.BlockSpec((tb, heads * hd), lambda i: (i, 0)), kv_blk, kv_blk],
        out_specs=pl.BlockSpec((tb, heads, hd), lambda i: (i, 0, 0)),
        out_shape=jax.ShapeDtypeStruct((b, heads, hd), F32),
        compiler_params=_params(("arbitrary",)),
    )(q, mem_k, mem_v)


def _conv_block(x_ref, buf, w):
    cb = x_ref.shape[0]
    buf[8:8 + cb, :] = x_ref[...]
    y = buf[5:5 + cb, :] * w[0:1]
    y = y + buf[6:6 + cb, :] * w[1:2]
    y = y + buf[7:7 + cb, :] * w[2:3]
    y = y + buf[8:8 + cb, :] * w[3:4]
    buf[0:8, :] = buf[cb:cb + 8, :]
    return y


def _gdn_gates(ba, alog, dtb, h, heads):
    lane = lax.broadcasted_iota(jnp.int32, ba.shape, 1)
    beta_all = _sigmoid(ba)
    g_all = -jnp.exp(alog) * _softplus(ba + dtb)
    beta = jnp.sum(jnp.where(lane == h, beta_all, 0.0), axis=1, keepdims=True)
    g = jnp.sum(jnp.where(lane == heads + h, g_all, 0.0), axis=1, keepdims=True)
    return beta, g


def _gdn_prompt_kernel(q_ref, k_ref, v_ref, z_ref, ba_ref, cwq_ref, cwk_ref, cwv_ref, alog_ref, dtb_ref,
                       nw_ref, o_ref, s_ref, qbuf, kbuf, vbuf, s_scr, *, heads):
    hp = s_scr.shape[0]
    blk = pl.program_id(2)
    cb = q_ref.shape[0]
    c = CHUNK
    n_chunks = cb // c

    @pl.when(blk == 0)
    def _():
        s_scr[...] = jnp.zeros_like(s_scr)
        for buf in (qbuf, kbuf, vbuf):
            buf[0:8, :] = jnp.zeros((8, hp * LANES), F32)

    for x_ref, buf in ((q_ref, qbuf), (k_ref, kbuf), (v_ref, vbuf)):
        buf[8:8 + cb, :] = x_ref[...]
    conv_w = (cwq_ref[...], cwk_ref[...], cwv_ref[...])
    ba = ba_ref[...]
    gates = [_gdn_gates(ba, alog_ref[...], dtb_ref[...], pl.program_id(1) * hp + j, heads) for j in range(hp)]
    head_lanes = [slice(j * LANES, (j + 1) * LANES) for j in range(hp)]

    row = _iota2(c, 0)
    col = _iota2(c, 1)
    causal = col <= row
    strict = col < row
    tri = causal.astype(BF16)
    eye = (col == row).astype(F32)
    nw = nw_ref[...]

    def conv_rows(buf, w, r0):
        y = buf[r0 + 5:r0 + 5 + c, :] * w[0:1]
        y = y + buf[r0 + 6:r0 + 6 + c, :] * w[1:2]
        y = y + buf[r0 + 7:r0 + 7 + c, :] * w[2:3]
        return y + buf[r0 + 8:r0 + 8 + c, :] * w[3:4]

    conv = [[_silu(conv_rows(buf, w, ci * c)) for buf, w in zip((qbuf, kbuf, vbuf), conv_w)]
            for ci in range(n_chunks)]
    items = [(ci, j) for ci in range(n_chunks) for j in range(hp)]
    rows = lambda ci: slice(ci * c, (ci + 1) * c)
    qs = [_l2n(conv[ci][0][:, head_lanes[j]]) * (LANES ** -0.5) for ci, j in items]
    ks = [_l2n(conv[ci][1][:, head_lanes[j]]) for ci, j in items]
    vs = [conv[ci][2][:, head_lanes[j]] for ci, j in items]
    betas = [gates[j][0][rows(ci)] for ci, j in items]
    gcs = [_dot_exact_lhs(tri, jnp.broadcast_to(gates[j][1][rows(ci)], (c, c))) for ci, j in items]
    decays = [jnp.exp(jnp.where(causal, gc - gc.T, -jnp.inf)) for gc in gcs]
    kbs = [k * b for k, b in zip(ks, betas)]
    ms = [jnp.where(strict, _dot_nt(kb, k) * dec, 0.0) for kb, k, dec in zip(kbs, ks, decays)]
    xs = [_split2(-m) for m in ms]
    ts = [eye - m for m in ms]
    for _ in range(int(math.log2(c)) - 1):
        xs = [_split2(_dot_hp(xp, xp)) for xp in xs]
        ts = [t + _dot_hp(_split2(t), xp) for t, xp in zip(ts, xs)]
    egs = [jnp.exp(gc) for gc in gcs]
    uws = [_dot(t, jnp.concatenate([v * b, kb * eg], axis=1))
           for t, v, b, kb, eg in zip(ts, vs, betas, kbs, egs)]
    attns = [jnp.where(causal, _dot_nt(q, k) * dec, 0.0) for q, k, dec in zip(qs, ks, decays)]
    g_lasts = [gc[c - 1:c, :] for gc in gcs]
    kd_uws = [_dot_tn(k * jnp.exp(gl - gc), uw) for k, gl, gc, uw in zip(ks, g_lasts, gcs, uws)]
    at_uws = [_dot(attn, uw) for attn, uw in zip(attns, uws)]
    lhs = [jnp.concatenate([q * eg - at[:, LANES:], eye * jnp.exp(gl) - kd[:, LANES:]], axis=0)
           for q, eg, at, gl, kd in zip(qs, egs, at_uws, g_lasts, kd_uws)]

    states = [s_scr[j] for j in range(hp)]
    outs = []
    for it, (ci, j) in enumerate(items):
        both = _dot(lhs[it], states[j])
        outs.append(at_uws[it][:, :LANES] + both[:c])
        states[j] = both[c:] + kd_uws[it][:, :LANES]
    for j in range(hp):
        s_scr[j] = states[j]
    for buf in (qbuf, kbuf, vbuf):
        buf[0:8, :] = buf[cb:cb + 8, :]
    for (ci, j), o in zip(items, outs):
        o_ref[rows(ci), head_lanes[j]] = (_rms(o, nw) * _silu(z_ref[rows(ci), head_lanes[j]])).astype(BF16)

    @pl.when(blk == pl.num_programs(2) - 1)
    def _():
        s_ref[0] = s_scr[...]


def _gdn_prompt(proj, ba, conv_w, alog_pad, dtb_pad, norm_w, batch, seq, heads, cb):
    m = proj.shape[0]
    nblk = seq // cb
    hp = GDN_HEADS_PER_STEP
    assert heads % hp == 0
    groups = heads // hp
    rows = lambda b, h, c: b * nblk + c
    head_blk = lambda part: pl.BlockSpec((cb, hp * LANES), lambda b, h, c: (rows(b, h, c), part * groups + h))
    cw_blk = lambda part: pl.BlockSpec((CONV_WIDTH, hp * LANES), lambda b, h, c: (0, part * groups + h))
    row128 = pl.BlockSpec((1, LANES), lambda b, h, c: (0, 0))
    return pl.pallas_call(
        functools.partial(_gdn_prompt_kernel, heads=heads),
        grid=(batch, groups, nblk),
        in_specs=[head_blk(0), head_blk(1), head_blk(2), head_blk(3),
                  pl.BlockSpec((cb, LANES), lambda b, h, c: (rows(b, h, c), 0)),
                  cw_blk(0), cw_blk(1), cw_blk(2), row128, row128, row128],
        out_specs=[pl.BlockSpec((cb, hp * LANES), lambda b, h, c: (rows(b, h, c), h)),
                   pl.BlockSpec((1, hp, LANES, LANES), lambda b, h, c: (b, h, 0, 0))],
        out_shape=[jax.ShapeDtypeStruct((m, heads * LANES), BF16),
                   jax.ShapeDtypeStruct((batch, heads, LANES, LANES), F32)],
        scratch_shapes=[pltpu.VMEM((cb + 8, hp * LANES), F32)] * 3 + [pltpu.VMEM((hp, LANES, LANES), F32)],
        compiler_params=_params(("arbitrary", "arbitrary", "arbitrary")),
    )(proj, proj, proj, proj, ba, conv_w, conv_w, conv_w, alog_pad, dtb_pad, norm_w.reshape(1, LANES))


def _gdn_step_kernel(q_ref, k_ref, v_ref, z_ref, ba_ref, cq_ref, ck_ref, cv_ref, cwq_ref, cwk_ref, cwv_ref,
                     alog_ref, dtb_ref, nw_ref, s_ref, o_ref, so_ref, *, heads):
    h = pl.program_id(1)
    tb = q_ref.shape[0]

    def conv(x_ref, c_ref, w_ref):
        w = w_ref[...]
        y = c_ref[0] * w[0:1]
        y = y + c_ref[1] * w[1:2]
        y = y + c_ref[2] * w[2:3]
        return y + x_ref[...] * w[3:4]

    q = _l2n(_silu(conv(q_ref, cq_ref, cwq_ref))) * (LANES ** -0.5)
    k = _l2n(_silu(conv(k_ref, ck_ref, cwk_ref)))
    v = _silu(conv(v_ref, cv_ref, cwv_ref))
    beta, g = _gdn_gates(ba_ref[...], alog_ref[...], dtb_ref[...], h, heads)
    eg = jnp.exp(g)
    outs = []
    group = 4
    for b0 in range(0, tb, group):
        rows = range(b0, min(b0 + group, tb))
        kcols = [_col_from_row(k[b:b + 1]) for b in rows]
        for b, kcol in zip(rows, kcols):
            r = slice(b, b + 1)
            s = s_ref[b, 0] * eg[r]
            v_new = beta[r] * (v[r] - jnp.sum(kcol * s, axis=0, keepdims=True))
            s = s + kcol * v_new
            so_ref[b, 0] = s
            outs.append(_dot(q[r], s))
    o = jnp.concatenate(outs, axis=0)
    o_ref[...] = (_rms(o, nw_ref[...]) * _silu(z_ref[...])).astype(BF16)


def _gdn_step(proj, ba, conv_state_t, conv_w, alog_pad, dtb_pad, norm_w, state, layer, heads, tb):
    bsz = proj.shape[0]
    head_blk = lambda off: pl.BlockSpec((tb, LANES), lambda i, h: (i, off + h))
    cs_blk = lambda off: pl.BlockSpec((CONV_WIDTH - 1, tb, LANES), lambda i, h: (0, i, off + h))
    cw_blk = lambda off: pl.BlockSpec((CONV_WIDTH, LANES), lambda i, h: (0, off + h))
    row128 = pl.BlockSpec((1, LANES), lambda i, h: (0, 0))
    st_in = pl.BlockSpec((None, tb, 1, LANES, LANES), lambda i, h: (layer, i, h, 0, 0))
    st_out = pl.BlockSpec((tb, 1, LANES, LANES), lambda i, h: (i, h, 0, 0))
    return pl.pallas_call(
        functools.partial(_gdn_step_kernel, heads=heads),
        grid=(bsz // tb, heads),
        in_specs=[head_blk(0), head_blk(heads), head_blk(2 * heads), head_blk(3 * heads),
                  pl.BlockSpec((tb, LANES), lambda i, h: (i, 0)),
                  cs_blk(0), cs_blk(heads), cs_blk(2 * heads),
                  cw_blk(0), cw_blk(heads), cw_blk(2 * heads), row128, row128, row128, st_in],
        out_specs=[pl.BlockSpec((tb, LANES), lambda i, h: (i, h)), st_out],
        out_shape=[jax.ShapeDtypeStruct((bsz, heads * LANES), BF16),
                   jax.ShapeDtypeStruct(state.shape[1:], F32)],
        compiler_params=_params(("arbitrary", "arbitrary")),
    )(proj, proj, proj, proj, ba, conv_state_t, conv_state_t, conv_state_t, conv_w, conv_w, conv_w,
      alog_pad, dtb_pad, norm_w.reshape(1, LANES), state)


def _lru_gates(x, wa_ref, wi_ref, ba, bi, lam):
    nb = wa_ref.shape[0]
    ga, gi = [], []
    for s in range(nb):
        xs = x[:, s * LANES:(s + 1) * LANES].astype(BF16)
        ga.append(jnp.dot(xs, wa_ref[s], preferred_element_type=F32))
        gi.append(jnp.dot(xs, wi_ref[s], preferred_element_type=F32))
    gate_a = _sigmoid(jnp.concatenate(ga, axis=-1) + ba)
    gate_i = _sigmoid(jnp.concatenate(gi, axis=-1) + bi)
    log_a = -LRU_C * gate_a * _softplus(-lam)
    return log_a, gate_i


def _lru_prompt_kernel(xl_ref, yl_ref, cw_ref, cb_ref, wa_ref, wi_ref, ba_ref, bi_ref, lam_ref,
                       o_ref, hl_ref, xbuf, abuf, bbuf, h_scr):
    blk = pl.program_id(1)
    cb = xl_ref.shape[0]
    width = xl_ref.shape[1]

    @pl.when(blk == 0)
    def _():
        h_scr[...] = jnp.zeros_like(h_scr)
        xbuf[0:8, :] = jnp.zeros((8, width), F32)

    x = _conv_block(xl_ref, xbuf, cw_ref[...]) + cb_ref[...]
    log_a, gate_i = _lru_gates(x, wa_ref, wi_ref, ba_ref[...], bi_ref[...], lam_ref[...])
    mult = jnp.sqrt(_neg_expm1(2.0 * log_a))
    first = (lax.broadcasted_iota(jnp.int32, (cb, 1), 0) == 0) & (blk == 0)
    mult = jnp.where(first, 1.0, mult)
    abuf[...] = jnp.exp(log_a)
    bbuf[...] = mult * gate_i * x

    def step(t, h):
        h = abuf[pl.ds(t, 1), :] * h + bbuf[pl.ds(t, 1), :]
        bbuf[pl.ds(t, 1), :] = h
        return h

    h_last = lax.fori_loop(0, cb, step, h_scr[...], unroll=8)
    h_scr[...] = h_last
    o_ref[...] = (bbuf[...] * _gelu_tanh(yl_ref[...])).astype(BF16)

    @pl.when(blk == pl.num_programs(1) - 1)
    def _():
        hl_ref[0] = h_last


def _lru_prompt(xy, conv_w, conv_b, w_a, w_i, b_a, b_i, lam, batch, seq, cb):
    m = xy.shape[0]
    width = xy.shape[1] // 2
    nblk = seq // cb
    nb = w_a.shape[0]
    vec = pl.BlockSpec((1, width), lambda b, c: (0, 0))
    wblk = pl.BlockSpec((nb, LANES, LANES), lambda b, c: (0, 0, 0))
    out, h_last = pl.pallas_call(
        _lru_prompt_kernel,
        grid=(batch, nblk),
        in_specs=[pl.BlockSpec((cb, width), lambda b, c: (b * nblk + c, 0)),
                  pl.BlockSpec((cb, width), lambda b, c: (b * nblk + c, 1)),
                  pl.BlockSpec((CONV_WIDTH, width), lambda b, c: (0, 0)),
                  vec, wblk, wblk, vec, vec, vec],
        out_specs=[pl.BlockSpec((cb, width), lambda b, c: (b * nblk + c, 0)),
                   pl.BlockSpec((1, 1, width), lambda b, c: (b, 0, 0))],
        out_shape=[jax.ShapeDtypeStruct((m, width), BF16),
                   jax.ShapeDtypeStruct((batch, 1, width), F32)],
        scratch_shapes=[pltpu.VMEM((cb + 8, width), F32), pltpu.VMEM((cb, width), F32),
                        pltpu.VMEM((cb, width), F32), pltpu.VMEM((1, width), F32)],
        compiler_params=_params(("arbitrary", "arbitrary")),
    )(xy, xy, conv_w, conv_b.reshape(1, width), w_a, w_i, b_a.reshape(1, width), b_i.reshape(1, width),
      lam.reshape(1, width))
    return out, h_last.reshape(batch, width)


def _lru_step_kernel(xl_ref, yl_ref, cs_ref, h0_ref, cw_ref, cb_ref, wa_ref, wi_ref, ba_ref, bi_ref, lam_ref,
                     o_ref, h_ref, *, reset):
    w = cw_ref[...]
    x = cs_ref[0] * w[0:1]
    x = x + cs_ref[1] * w[1:2]
    x = x + cs_ref[2] * w[2:3]
    x = x + xl_ref[...] * w[3:4] + cb_ref[...]
    log_a, gate_i = _lru_gates(x, wa_ref, wi_ref, ba_ref[...], bi_ref[...], lam_ref[...])
    mult = 1.0 if reset else jnp.sqrt(_neg_expm1(2.0 * log_a))
    h = jnp.exp(log_a) * h0_ref[...] + mult * gate_i * x
    h_ref[...] = h
    o_ref[...] = (h * _gelu_tanh(yl_ref[...])).astype(BF16)


def _lru_step(xy, conv_state_t, h0, conv_w, conv_b, w_a, w_i, b_a, b_i, lam, reset):
    bsz = xy.shape[0]
    width = xy.shape[1] // 2
    nb = w_a.shape[0]
    vec = pl.BlockSpec((1, width), lambda i: (0, 0))
    wblk = pl.BlockSpec((nb, LANES, LANES), lambda i: (0, 0, 0))
    full = pl.BlockSpec((bsz, width), lambda i: (0, 0))
    return pl.pallas_call(
        functools.partial(_lru_step_kernel, reset=reset),
        grid=(1,),
        in_specs=[full, pl.BlockSpec((bsz, width), lambda i: (0, 1)),
                  pl.BlockSpec((CONV_WIDTH - 1, bsz, width), lambda i: (0, 0, 0)), full,
                  pl.BlockSpec((CONV_WIDTH, width), lambda i: (0, 0)), vec, wblk, wblk, vec, vec, vec],
        out_specs=[full, full],
        out_shape=[jax.ShapeDtypeStruct((bsz, width), BF16), jax.ShapeDtypeStruct((bsz, width), F32)],
        compiler_params=_params(("arbitrary",)),
    )(xy, xy, conv_state_t, h0, conv_w, conv_b.reshape(1, width), w_a, w_i, b_a.reshape(1, width),
      b_i.reshape(1, width), lam.reshape(1, width))


def _hgrn_lower_bound(lb_ref, layer, j=0):
    depth = lb_ref.shape[0]
    raw = [lb_ref[l, j] for l in range(depth)]
    mx = raw[0]
    for r in raw[1:]:
        mx = jnp.maximum(mx, r)
    ex = [jnp.exp(r - mx) for r in raw]
    tot = ex[0]
    for e in ex[1:]:
        tot = tot + e
    wts = [e / tot for e in ex]
    cum = wts[0]
    for w in wts[1:layer + 1]:
        cum = cum + w
    return cum - wts[0]


def _hgrn_inputs(q_raw, f_raw, lb):
    q = _silu(q_raw) * (LANES ** -0.5)
    f = lb + (1.0 - lb) * _sigmoid(f_raw)
    return q, f


def _level_ref(g, half):
    c = g.shape[0]
    sub = 8
    if 2 * half >= sub:
        g3 = g.reshape(c // (2 * half), 2 * half, LANES)
        return jnp.broadcast_to(g3[:, half - 1:half, :], g3.shape).reshape(c, LANES)
    g3 = g.reshape(c // sub, sub, LANES)
    rin = lax.broadcasted_iota(jnp.int32, g3.shape, 1)
    out = jnp.broadcast_to(g3[:, sub - half - 1:sub - half, :], g3.shape)
    for start in range(sub - 4 * half, -1, -2 * half):
        out = jnp.where(rin < start + 2 * half, jnp.broadcast_to(g3[:, start + half - 1:start + half, :], g3.shape),
                        out)
    return out.reshape(c, LANES)


def _hgrn_level_table(c):
    i = np.arange(c)[:, None]
    j = np.arange(c)[None, :]
    nlev = int(math.log2(c))
    top_bit = sum(((i ^ j) >> b > 0).astype(np.int32) for b in range(1, nlev))
    return jnp.asarray(np.where(j < i, nlev - 1 - top_bit, np.where(j == i, nlev, -1)), jnp.int32)


def _hgrn_prompt_kernel(q_ref, f_ref, i_ref, gz_ref, lb_ref, nw_ref, lvl_ref, o_ref, s_ref, s_scr, *, layer):
    hp = s_scr.shape[0]
    blk = pl.program_id(2)
    cb = q_ref.shape[0]
    c = CHUNK

    @pl.when(blk == 0)
    def _():
        s_scr[...] = jnp.zeros_like(s_scr)

    lbs = [_hgrn_lower_bound(lb_ref, layer, j) for j in range(hp)]
    head_lanes = [slice(j * LANES, (j + 1) * LANES) for j in range(hp)]
    tri = (_iota2(c, 1) <= _iota2(c, 0)).astype(BF16)
    nw = nw_ref[...]
    lvl = lvl_ref[...]
    halves = [c >> (i + 1) for i in range(int(math.log2(c)))]

    items = [(slice(ci * c, (ci + 1) * c), j) for ci in range(cb // c) for j in range(hp)]
    qf = [_hgrn_inputs(q_ref[sl, head_lanes[j]], f_ref[sl, head_lanes[j]], lbs[j]) for sl, j in items]
    qs = [q for q, _ in qf]
    ks = [1.0 - f for _, f in qf]
    vs = [i_ref[sl, head_lanes[j]] for sl, j in items]
    gs = [_dot_exact_lhs(tri, jnp.log(f)) for _, f in qf]

    amats = [_dot_nt(q, k) for q, k in zip(qs, ks)]
    for li, half in enumerate(halves):
        es = [jnp.exp(-jnp.abs(g - _level_ref(g, half))) for g in gs]
        ps = [_dot_nt(q * e, k * e) for q, k, e in zip(qs, ks, es)]
        amats = [jnp.where(lvl == li, p, a) for p, a in zip(ps, amats)]
    intra = [_dot(jnp.where(lvl >= 0, a, 0.0), v) for a, v in zip(amats, vs)]

    g_lasts = [g[c - 1:c, :] for g in gs]
    qdec = [q * jnp.exp(g) for q, g in zip(qs, gs)]
    sdec = [_col_from_row(jnp.exp(gl)) for gl in g_lasts]
    sadd = [_dot_tn(k * jnp.exp(gl - g), v) for k, gl, g, v in zip(ks, g_lasts, gs, vs)]

    states = [s_scr[j] for j in range(hp)]
    outs = []
    for (sl, j), o, qd, dec, add in zip(items, intra, qdec, sdec, sadd):
        outs.append(o + _dot(qd, states[j]))
        states[j] = states[j] * dec + add
    for j in range(hp):
        s_scr[j] = states[j]
    for (sl, j), o in zip(items, outs):
        o_ref[sl, head_lanes[j]] = (_rms(o, nw) * _silu(gz_ref[sl, head_lanes[j]])).astype(BF16)

    @pl.when(blk == pl.num_programs(2) - 1)
    def _():
        s_ref[0] = s_scr[...]


def _hgrn_prompt(proj, lb_raw, norm_w, layer, batch, seq, heads, cb):
    m = proj.shape[0]
    nblk = seq // cb
    depth = lb_raw.shape[0]
    hp = HGRN_HEADS_PER_STEP
    assert heads % hp == 0
    groups = heads // hp
    head_blk = lambda part: pl.BlockSpec((cb, hp * LANES), lambda b, h, c: (b * nblk + c, part * groups + h))
    return pl.pallas_call(
        functools.partial(_hgrn_prompt_kernel, layer=layer),
        grid=(batch, groups, nblk),
        in_specs=[head_blk(0), head_blk(1), head_blk(2), head_blk(3),
                  pl.BlockSpec((depth, hp, 1, LANES), lambda b, h, c: (0, h, 0, 0)),
                  pl.BlockSpec((1, LANES), lambda b, h, c: (0, 0)),
                  pl.BlockSpec((CHUNK, CHUNK), lambda b, h, c: (0, 0))],
        out_specs=[pl.BlockSpec((cb, hp * LANES), lambda b, h, c: (b * nblk + c, h)),
                   pl.BlockSpec((1, hp, LANES, LANES), lambda b, h, c: (b, h, 0, 0))],
        out_shape=[jax.ShapeDtypeStruct((m, heads * LANES), BF16),
                   jax.ShapeDtypeStruct((batch, heads, LANES, LANES), F32)],
        scratch_shapes=[pltpu.VMEM((hp, LANES, LANES), F32)],
        compiler_params=_params(("arbitrary", "arbitrary", "arbitrary")),
    )(proj, proj, proj, proj, lb_raw.reshape(depth, heads, 1, LANES), norm_w.reshape(1, LANES),
      _hgrn_level_table(CHUNK))


def _hgrn_step_kernel(q_ref, f_ref, i_ref, gz_ref, lb_ref, nw_ref, s_ref, o_ref, so_ref, *, layer):
    tb = q_ref.shape[0]
    lb = _hgrn_lower_bound(lb_ref, layer)
    q, f = _hgrn_inputs(q_ref[...], f_ref[...], lb)
    v = i_ref[...]
    outs = []
    group = 4
    for b0 in range(0, tb, group):
        rows = range(b0, min(b0 + group, tb))
        fcols = [_col_from_row(f[b:b + 1]) for b in rows]
        for b, fcol in zip(rows, fcols):
            s = fcol * (s_ref[b, 0] - v[b:b + 1]) + v[b:b + 1]
            so_ref[b, 0] = s
            outs.append(_dot(q[b:b + 1], s))
    o = jnp.concatenate(outs, axis=0)
    o_ref[...] = (_rms(o, nw_ref[...]) * _silu(gz_ref[...])).astype(BF16)


def _hgrn_step(proj, lb_raw, norm_w, state, state_idx, layer, heads, tb):
    bsz = proj.shape[0]
    depth = lb_raw.shape[0]
    head_blk = lambda off: pl.BlockSpec((tb, LANES), lambda i, h: (i, off + h))
    st_in = pl.BlockSpec((None, tb, 1, LANES, LANES), lambda i, h: (state_idx, i, h, 0, 0))
    st_out = pl.BlockSpec((tb, 1, LANES, LANES), lambda i, h: (i, h, 0, 0))
    return pl.pallas_call(
        functools.partial(_hgrn_step_kernel, layer=layer),
        grid=(bsz // tb, heads),
        in_specs=[head_blk(0), head_blk(heads), head_blk(2 * heads), head_blk(3 * heads),
                  pl.BlockSpec((depth, 1, 1, LANES), lambda i, h: (0, h, 0, 0)),
                  pl.BlockSpec((1, LANES), lambda i, h: (0, 0)), st_in],
        out_specs=[pl.BlockSpec((tb, LANES), lambda i, h: (i, h)), st_out],
        out_shape=[jax.ShapeDtypeStruct((bsz, heads * LANES), BF16),
                   jax.ShapeDtypeStruct(state.shape[1:], F32)],
        compiler_params=_params(("arbitrary", "arbitrary")),
    )(proj, proj, proj, proj, lb_raw.reshape(depth, heads, 1, LANES), norm_w.reshape(1, LANES), state)


def _trunk(x, mem_k, mem_v, states, wts, big, batch, seq, prompt):
    depth = wts["norm_mix"].shape[0]
    gdn_w = wts["gdn_conv_w"].shape[-1] // 3
    gdn_heads = gdn_w // LANES
    hgrn_heads = wts["hgrn_lb_raw"].shape[1] // LANES
    lru_w = wts["lru_conv_w"].shape[-1]
    bf = {name: list(handles) for name, handles in big.items()}
    new = {"gdn_conv": [], "gdn": [], "lru_conv": [], "lru": [], "hgrn": []}

    def keep(name, idx, w_used):
        w, li = bf[name][idx]
        bf[name][idx] = (w_used, li if w_used is w else 0)

    for l in range(depth):
        g_mix = wts["norm_mix"][l]
        if l % 2 == 0:
            e = l // 2
            proj, _ = _norm_matmul(x, g_mix, wts["ab_head"], e, 0, 4 * gdn_w, TM, TN)
            ba, _ = _norm_matmul(x, g_mix, wts["ab_head"], e, 4 * gdn_w, LANES, TM, LANES)
            xy, _ = _norm_matmul(x, g_mix, wts["ab_lru"], e, 0, 2 * lru_w, TM, TN)
            gargs = (wts["gdn_conv_w"][e], wts["alog_pad"][e], wts["dtb_pad"][e], wts["gdn_norm_w"][e])
            largs = (wts["lru_conv_w"][e], wts["lru_conv_b"][e], wts["lru_w_a"][e], wts["lru_w_i"][e],
                     wts["lru_b_a"][e], wts["lru_b_i"][e], wts["lru_lam"][e])
            w_out, li = bf["ab_w_out"][e]
            if prompt:
                o_a, s_new = _gdn_prompt(proj, ba, *gargs, batch, seq, gdn_heads, 512)
                o_b, h_new = _lru_prompt(xy, *largs, batch, seq, 256)
                tail = lambda t, w: t.reshape(batch, seq, -1)[:, seq - (CONV_WIDTH - 1):, :w]
                new["gdn_conv"].append(tail(proj, 3 * gdn_w))
                new["lru_conv"].append(tail(xy, lru_w))
                x = _matmul2_res(o_a, o_b, w_out, li, x, TM, TN)
            else:
                gc_state, lc_state = states["gdn_conv"][e], states["lru_conv"][e]
                o_a, s_new = _gdn_step(proj, ba, jnp.swapaxes(gc_state, 0, 1), *gargs, states["gdn"], e,
                                       gdn_heads, 64)
                o_b, h_new = _lru_step(xy, jnp.swapaxes(lc_state, 0, 1), states["lru"][e], *largs, reset=False)
                new["gdn_conv"].append(jnp.concatenate([gc_state[:, 1:], proj[:, None, :3 * gdn_w]], axis=1))
                new["lru_conv"].append(jnp.concatenate([lc_state[:, 1:], xy[:, None, :lru_w]], axis=1))
                x, w_used = _matmul_res(jnp.concatenate([o_a, o_b], axis=1), w_out, li, x, TM, TN)
                keep("ab_w_out", e, w_used)
            new["gdn"].append(s_new)
            new["lru"].append(h_new)
        else:
            o_idx = l // 2
            w_in, li = bf["c_w_in"][o_idx]
            proj, w_used = _norm_matmul(x, g_mix, w_in, li, 0, w_in.shape[2], TM, TN)
            keep("c_w_in", o_idx, w_used)
            hargs = (wts["hgrn_lb_raw"], wts["hgrn_norm_w"][o_idx])
            if prompt:
                o_c, s_new = _hgrn_prompt(proj, *hargs, l, batch, seq, hgrn_heads, 512)
            else:
                o_c, s_new = _hgrn_step(proj, *hargs, states["hgrn"], o_idx, l, hgrn_heads, 64)
            new["hgrn"].append(s_new)
            w_out, li = bf["c_w_out"][o_idx]
            x, w_used = _matmul_res(o_c, w_out, li, x, TM, TN)
            keep("c_w_out", o_idx, w_used)
        (w_q, lq), (w_o, lo) = bf["mem_w_q"][l], bf["mem_w_o"][l]
        if prompt:
            assert lq == lo
            x = _mem_attn_prompt(x, wts["norm_mem"][l], w_q, mem_k, mem_v, w_o, l, lq, seq, 512)
        else:
            q, w_used = _norm_matmul(x, wts["norm_mem"][l], w_q, lq, 0, w_q.shape[2], TM, TN)
            keep("mem_w_q", l, w_used)
            o = _mem_attn_sample(q, mem_k, mem_v, l, 8)
            x, w_used = _matmul_res(o.reshape(batch, -1).astype(BF16), w_o, lo, x, TM, TN)
            keep("mem_w_o", l, w_used)
        (w_up, lu), (w_dn, ld) = bf["ffn_w_up"][l], bf["ffn_w_down"][l]
        assert lu == ld
        x, up_used, dn_used = _ffn(x, wts["norm_ffn"][l], w_up, w_dn, lu, wts["norm_final"], l == depth - 1,
                                   TM, TF)
        keep("ffn_w_up", l, up_used)
        keep("ffn_w_down", l, dn_used)
    return x, {n: jnp.stack(v) for n, v in new.items()}, bf


def kernel(x_prompt, x_sample, cache_mem_k, cache_mem_v, state_gdn_conv, state_gdn, state_lru_conv, state_lru, state_hgrn, mem_prompt, norm_mix, norm_mem, norm_mem_kv, norm_ffn, norm_final, ab_w_in, ab_w_out, gdn_conv_w, gdn_a_log, gdn_dt_bias, gdn_norm_w, lru_conv_w, lru_conv_b, lru_w_a, lru_b_a, lru_w_i, lru_b_i, lru_lam, c_w_in, c_w_out, hgrn_lb_raw, hgrn_norm_w, mem_w_q, mem_w_k, mem_w_v, mem_w_o, ffn_w_up, ffn_w_down):
    bp, lp, d = x_prompt.shape
    bs, ls, _ = x_sample.shape
    assert ls == 1, "the sample group advances one token per call"
    depth = norm_mix.shape[0]
    gdn_heads = gdn_a_log.shape[1]
    gdn_w = gdn_heads * LANES
    mem_tokens, mem_heads, mem_hd = cache_mem_k.shape[2:]
    mem_w = mem_heads * mem_hd
    n_ba = 2 * gdn_heads
    assert n_ba <= LANES

    pad_ba = lambda a: jnp.pad(a, ((0, 0), (gdn_heads, LANES - n_ba)))[:, None, :]
    ab_bf = ab_w_in.astype(BF16)
    wts = dict(
        norm_mix=norm_mix, norm_mem=norm_mem, norm_ffn=norm_ffn, norm_final=norm_final,
        ab_head=ab_bf, ab_lru=ab_bf[:, :, 4 * gdn_w + n_ba:],
        gdn_conv_w=gdn_conv_w, alog_pad=pad_ba(gdn_a_log), dtb_pad=pad_ba(gdn_dt_bias), gdn_norm_w=gdn_norm_w,
        lru_conv_w=lru_conv_w, lru_conv_b=lru_conv_b, lru_w_a=lru_w_a.astype(BF16), lru_w_i=lru_w_i.astype(BF16),
        lru_b_a=lru_b_a, lru_b_i=lru_b_i, lru_lam=lru_lam,
        hgrn_lb_raw=hgrn_lb_raw, hgrn_norm_w=hgrn_norm_w,
    )
    big = dict(ab_w_out=ab_w_out, c_w_in=c_w_in, c_w_out=c_w_out, mem_w_q=mem_w_q, mem_w_o=mem_w_o,
               ffn_w_up=ffn_w_up, ffn_w_down=ffn_w_down)
    big = {name: [(w, l) for l in range(w.shape[0])] for name, w in big.items()}

    states = dict(gdn_conv=state_gdn_conv, gdn=state_gdn, lru_conv=state_lru_conv, lru=state_lru,
                  hgrn=state_hgrn)
    y_s, new_s, big_bf = _trunk(x_sample.reshape(bs * ls, d), cache_mem_k, cache_mem_v, states, wts, big,
                                bs, ls, False)

    mem_rows = mem_prompt.reshape(bp * mem_tokens, d)
    w_kv = jnp.concatenate([mem_w_k, mem_w_v], axis=-1).astype(BF16)
    kv = jnp.stack([_norm_matmul(mem_rows, norm_mem_kv[l], w_kv, l, 0, 2 * mem_w, TM, TN)[0] for l in range(depth)])
    p_mem_k = kv[:, :, :mem_w].reshape(depth, bp, mem_tokens, mem_w)
    p_mem_v = kv[:, :, mem_w:].reshape(depth, bp, mem_tokens, mem_w)

    y_p, new_p, _ = _trunk(x_prompt.reshape(bp * lp, d), p_mem_k, p_mem_v, None, wts, big_bf, bp, lp, True)

    order = ("gdn_conv", "gdn", "lru_conv", "lru", "hgrn")
    mem5 = lambda t: t.reshape(depth, bp, mem_tokens, mem_heads, mem_hd)
    return (y_p.reshape(bp, lp, d), y_s.reshape(bs, ls, d), mem5(p_mem_k), mem5(p_mem_v),
            *(new_p[n] for n in order), *(new_s[n] for n in order))
```

```python
import functools
import math

import jax
import jax.numpy as jnp
import numpy as np
from jax import lax
from jax.experimental import pallas as pl
from jax.experimental.pallas import tpu as pltpu

F32 = jnp.float32
BF16 = jnp.bfloat16
EPS = 1e-6
LANES = 128
CONV_WIDTH = 4
LRU_C = 8.0
VMEM_LIMIT_BYTES = 56 * 1024 * 1024
TM = 512
TN = 1024
TF = 512
TN_CAST = 512
TF_CAST = 256
ROW_GROUPS = 4
CHUNK = 128
GDN_HEADS_PER_STEP = 4
HGRN_HEADS_PER_STEP = 8


def _params(sem):
    return pltpu.CompilerParams(dimension_semantics=sem, vmem_limit_bytes=VMEM_LIMIT_BYTES)


def _sigmoid(x):
    return jax.nn.sigmoid(x)


def _silu(x):
    return x * _sigmoid(x)


def _softplus(x):
    return jnp.maximum(x, 0.0) + jnp.log1p(jnp.exp(-jnp.abs(x)))


def _neg_expm1(x):
    t = jnp.tanh(0.5 * x)
    return -2.0 * t / (1.0 - t)


def _gelu_tanh(x):
    c = math.sqrt(2.0 / math.pi)
    return 0.5 * x * (1.0 + jnp.tanh(c * (x + 0.044715 * (x * x * x))))


def _rms(x, w):
    return x * lax.rsqrt(jnp.mean(x * x, axis=-1, keepdims=True) + EPS) * w


def _l2n(x):
    return x * lax.rsqrt(jnp.sum(x * x, axis=-1, keepdims=True) + EPS)


def _dot(a, b):
    return jnp.dot(a.astype(BF16), b.astype(BF16), preferred_element_type=F32)


def _dot_nt(a, b):
    return lax.dot_general(a.astype(BF16), b.astype(BF16), (((1,), (1,)), ((), ())),
                           preferred_element_type=F32)


def _dot_tn(a, b):
    return lax.dot_general(a.astype(BF16), b.astype(BF16), (((0,), (0,)), ((), ())),
                           preferred_element_type=F32)


def _split2(a):
    hi = a.astype(BF16)
    lo = (a - hi.astype(F32)).astype(BF16)
    return hi, lo


def _dot_hp(a_parts, b_parts):
    ah, al = a_parts
    bh, bl = b_parts
    return jnp.dot(jnp.concatenate([ah, ah, al], axis=1), jnp.concatenate([bh, bl, bh], axis=0),
                   preferred_element_type=F32)


def _dot_exact_lhs(t, b):
    b1 = b.astype(BF16)
    r1 = b - b1.astype(F32)
    b2 = r1.astype(BF16)
    b3 = (r1 - b2.astype(F32)).astype(BF16)
    return jnp.dot(jnp.concatenate([t, t, t], axis=1), jnp.concatenate([b1, b2, b3], axis=0),
                   preferred_element_type=F32)


def _iota2(n, axis):
    return lax.broadcasted_iota(jnp.int32, (n, n), axis)


def _col_from_row(row):
    n = row.shape[-1]
    return jnp.broadcast_to(row, (n, n)).T


def _row_groups(rows):
    n = ROW_GROUPS if rows % (8 * ROW_GROUPS) == 0 else 1
    return [slice(r * rows // n, (r + 1) * rows // n) for r in range(n)]


def _norm_matmul_kernel(x_ref, g_ref, w_ref, o_ref, *rest):
    xn_ref = rest[-1]
    if len(rest) == 2:
        rest[0][...] = w_ref[...].astype(BF16)
        w_ref = rest[0]
    first = pl.program_id(1) == 0

    @pl.when(first)
    def _():
        for rows in _row_groups(x_ref.shape[0]):
            xn = _rms(x_ref[rows, :], g_ref[...]).astype(BF16)
            xn_ref[rows, :] = xn
            o_ref[rows, :] = jnp.dot(xn, w_ref[...], preferred_element_type=F32)

    @pl.when(jnp.logical_not(first))
    def _():
        o_ref[...] = jnp.dot(xn_ref[...], w_ref[...], preferred_element_type=F32)


def _norm_matmul(x, g, w, layer, n0, n, tm, tn):
    m, k = x.shape
    cast = w.dtype != BF16
    tm = min(tm, m)
    tn = min(tn, n, TN_CAST) if cast else min(tn, n)
    j0 = n0 // tn
    out_specs = [pl.BlockSpec((tm, tn), lambda i, j: (i, j))]
    out_shape = [jax.ShapeDtypeStruct((m, n), F32)]
    if cast:
        assert m == tm and n0 == 0 and n == w.shape[2], "each weight tile must be visited exactly once"
        out_specs.append(pl.BlockSpec((None, k, tn), lambda i, j: (0, 0, j)))
        out_shape.append(jax.ShapeDtypeStruct((1, k, n), BF16))
    res = pl.pallas_call(
        _norm_matmul_kernel,
        grid=(m // tm, n // tn),
        in_specs=[pl.BlockSpec((tm, k), lambda i, j: (i, 0)),
                  pl.BlockSpec((1, k), lambda i, j: (0, 0)),
                  pl.BlockSpec((None, k, tn), lambda i, j: (layer, 0, j0 + j))],
        out_specs=out_specs,
        out_shape=out_shape,
        scratch_shapes=[pltpu.VMEM((tm, k), BF16)],
        compiler_params=_params(("arbitrary", "arbitrary")),
    )(x, g.reshape(1, k), w)
    return (res[0], res[1]) if cast else (res[0], w)


def _matmul2_res_kernel(a1_ref, a2_ref, w1_ref, w2_ref, r_ref, o_ref):
    acc = jnp.dot(a1_ref[...], w1_ref[...], preferred_element_type=F32)
    acc = acc + jnp.dot(a2_ref[...], w2_ref[...], preferred_element_type=F32)
    o_ref[...] = r_ref[...] + acc


def _matmul2_res(a1, a2, w, layer, res, tm, tn):
    m, kh = a1.shape
    n = w.shape[2]
    tm = min(tm, m)
    tn = min(tn, n)
    return pl.pallas_call(
        _matmul2_res_kernel,
        grid=(m // tm, n // tn),
        in_specs=[pl.BlockSpec((tm, kh), lambda i, j: (i, 0)),
                  pl.BlockSpec((tm, kh), lambda i, j: (i, 0)),
                  pl.BlockSpec((None, kh, tn), lambda i, j: (layer, 0, j)),
                  pl.BlockSpec((None, kh, tn), lambda i, j: (layer, 1, j)),
                  pl.BlockSpec((tm, tn), lambda i, j: (i, j))],
        out_specs=pl.BlockSpec((tm, tn), lambda i, j: (i, j)),
        out_shape=jax.ShapeDtypeStruct((m, n), F32),
        compiler_params=_params(("arbitrary", "arbitrary")),
    )(a1, a2, w, w, res)


def _matmul_res_kernel(a_ref, w_ref, r_ref, o_ref, *wb_ref):
    w = w_ref[...]
    if wb_ref:
        w = w.astype(BF16)
        wb_ref[0][...] = w
    o_ref[...] = r_ref[...] + jnp.dot(a_ref[...], w, preferred_element_type=F32)


def _matmul_res(a, w, layer, res, tm, tn):
    m, k = a.shape
    n = w.shape[2]
    cast = w.dtype != BF16
    tm = min(tm, m)
    tn = min(tn, n, TN_CAST) if cast else min(tn, n)
    out_specs = [pl.BlockSpec((tm, tn), lambda i, j: (i, j))]
    out_shape = [jax.ShapeDtypeStruct((m, n), F32)]
    if cast:
        assert m == tm, "each weight tile must be visited exactly once"
        out_specs.append(pl.BlockSpec((None, k, tn), lambda i, j: (0, 0, j)))
        out_shape.append(jax.ShapeDtypeStruct((1, k, n), BF16))
    out = pl.pallas_call(
        _matmul_res_kernel,
        grid=(m // tm, n // tn),
        in_specs=[pl.BlockSpec((tm, k), lambda i, j: (i, 0)),
                  pl.BlockSpec((None, k, tn), lambda i, j: (layer, 0, j)),
                  pl.BlockSpec((tm, tn), lambda i, j: (i, j))],
        out_specs=out_specs,
        out_shape=out_shape,
        compiler_params=_params(("arbitrary", "arbitrary")),
    )(a, w, res)
    return (out[0], out[1]) if cast else (out[0], w)


def _ffn_kernel(x_ref, g_ref, wu_ref, wd_ref, gf_ref, o_ref, *rest, final_norm):
    xn_ref = rest[-1]
    f = pl.program_id(1)
    if len(rest) == 3:
        rest[0][...] = wu_ref[...].astype(BF16)
        rest[1][...] = wd_ref[...].astype(BF16)
        wu_ref, wd_ref = rest[0], rest[1]

    def mlp(xn):
        h = jnp.dot(xn, wu_ref[...], preferred_element_type=F32)
        h = jnp.square(jnp.maximum(h, 0.0)).astype(BF16)
        return jnp.dot(h, wd_ref[...], preferred_element_type=F32)

    @pl.when(f == 0)
    def _():
        for rows in _row_groups(x_ref.shape[0]):
            x = x_ref[rows, :]
            xn = _rms(x, g_ref[...]).astype(BF16)
            xn_ref[rows, :] = xn
            o_ref[rows, :] = x + mlp(xn)

    @pl.when(f != 0)
    def _():
        o_ref[...] += mlp(xn_ref[...])

    if final_norm:
        @pl.when(f == pl.num_programs(1) - 1)
        def _():
            o_ref[...] = _rms(o_ref[...], gf_ref[...])


def _ffn(x, g, w_up, w_down, layer, g_final, final_norm, tm, tf):
    m, d = x.shape
    dff = w_up.shape[2]
    cast = w_up.dtype != BF16
    tm = min(tm, m)
    tf = min(tf, TF_CAST) if cast else tf
    out_specs = [pl.BlockSpec((tm, d), lambda i, f: (i, 0))]
    out_shape = [jax.ShapeDtypeStruct((m, d), F32)]
    if cast:
        assert m == tm, "each weight tile must be visited exactly once"
        out_specs += [pl.BlockSpec((None, d, tf), lambda i, f: (0, 0, f)),
                      pl.BlockSpec((None, tf, d), lambda i, f: (0, f, 0))]
        out_shape += [jax.ShapeDtypeStruct((1, d, dff), BF16), jax.ShapeDtypeStruct((1, dff, d), BF16)]
    out = pl.pallas_call(
        functools.partial(_ffn_kernel, final_norm=final_norm),
        grid=(m // tm, dff // tf),
        in_specs=[pl.BlockSpec((tm, d), lambda i, f: (i, 0), pipeline_mode=pl.Buffered(1)),
                  pl.BlockSpec((1, d), lambda i, f: (0, 0)),
                  pl.BlockSpec((None, d, tf), lambda i, f: (layer, 0, f)),
                  pl.BlockSpec((None, tf, d), lambda i, f: (layer, f, 0)),
                  pl.BlockSpec((1, d), lambda i, f: (0, 0))],
        out_specs=out_specs,
        out_shape=out_shape,
        scratch_shapes=[pltpu.VMEM((tm, d), BF16)],
        compiler_params=_params(("arbitrary", "arbitrary")),
    )(x, g.reshape(1, d), w_up, w_down, g_final.reshape(1, d))
    return (out[0], out[1], out[2]) if cast else (out[0], w_up, w_down)


def _mem_attn_prompt_kernel(x_ref, g_ref, wq_ref, k_ref, v_ref, wo_ref, o_ref, *, heads):
    x = x_ref[...]
    xn = _rms(x, g_ref[...]).astype(BF16)
    q = jnp.dot(xn, wq_ref[...], preferred_element_type=F32)
    k = k_ref[0].astype(BF16)
    v = v_ref[0].astype(BF16)
    scale = LANES ** -0.5
    outs = []
    for h in range(heads):
        sl = slice(h * LANES, (h + 1) * LANES)
        s = _dot_nt(q[:, sl], k[:, sl]) * scale
        e = jnp.exp(s - jnp.max(s, axis=-1, keepdims=True))
        p = e / jnp.sum(e, axis=-1, keepdims=True)
        outs.append(_dot(p, v[:, sl]))
    o = jnp.concatenate(outs, axis=-1).astype(BF16)
    o_ref[...] = x + jnp.dot(o, wo_ref[...], preferred_element_type=F32)


def _mem_attn_prompt(x, g, w_q, mem_k, mem_v, w_o, layer, w_layer, seq, tl):
    m, d = x.shape
    _, _, t, w = mem_k.shape
    nblk = seq // tl
    return pl.pallas_call(
        functools.partial(_mem_attn_prompt_kernel, heads=w // LANES),
        grid=(m // tl,),
        in_specs=[pl.BlockSpec((tl, d), lambda i: (i, 0)),
                  pl.BlockSpec((1, d), lambda i: (0, 0)),
                  pl.BlockSpec((None, d, w), lambda i: (w_layer, 0, 0), pipeline_mode=pl.Buffered(1)),
                  pl.BlockSpec((None, 1, t, w), lambda i: (layer, i // nblk, 0, 0)),
                  pl.BlockSpec((None, 1, t, w), lambda i: (layer, i // nblk, 0, 0)),
                  pl.BlockSpec((None, w, d), lambda i: (w_layer, 0, 0), pipeline_mode=pl.Buffered(1))],
        out_specs=pl.BlockSpec((tl, d), lambda i: (i, 0)),
        out_shape=jax.ShapeDtypeStruct((m, d), F32),
        compiler_params=_params(("arbitrary",)),
    )(x, g.reshape(1, d), w_q, mem_k, mem_v, w_o)


def _mem_attn_sample_kernel(q_ref, k_ref, v_ref, o_ref, *, tb, heads):
    scale = LANES ** -0.5
    for b in range(tb):
        q = jnp.concatenate([q_ref[b:b + 1, h * LANES:(h + 1) * LANES] for h in range(heads)], axis=0)
        s = jnp.sum(k_ref[b] * (q * scale)[None], axis=-1, keepdims=True)
        e = jnp.exp(s - jnp.max(s, axis=0, keepdims=True))
        o_ref[b] = jnp.sum(e * v_ref[b], axis=0) / jnp.sum(e, axis=0)


def _mem_attn_sample(q, mem_k, mem_v, layer, tb):
    b = q.shape[0]
    _, _, t, heads, hd = mem_k.shape
    kv_blk = pl.BlockSpec((None, tb, t, heads, hd), lambda i: (layer, i, 0, 0, 0))
    return pl.pallas_call(
        functools.partial(_mem_attn_sample_kernel, tb=tb, heads=heads),
        grid=(b // tb,),
        in_specs=[pl.BlockSpec((tb, heads * hd), lambda i: (i, 0)), kv_blk, kv_blk],
        out_specs=pl.BlockSpec((tb, heads, hd), lambda i: (i, 0, 0)),
        out_shape=jax.ShapeDtypeStruct((b, heads, hd), F32),
        compiler_params=_params(("arbitrary",)),
    )(q, mem_k, mem_v)


def _conv_block(x_ref, buf, w):
    cb = x_ref.shape[0]
    buf[8:8 + cb, :] = x_ref[...]
    y = buf[5:5 + cb, :] * w[0:1]
    y = y + buf[6:6 + cb, :] * w[1:2]
    y = y + buf[7:7 + cb, :] * w[2:3]
    y = y + buf[8:8 + cb, :] * w[3:4]
    buf[0:8, :] = buf[cb:cb + 8, :]
    return y


def _gdn_gates(ba, alog, dtb, h, heads):
    lane = lax.broadcasted_iota(jnp.int32, ba.shape, 1)
    beta_all = _sigmoid(ba)
    g_all = -jnp.exp(alog) * _softplus(ba + dtb)
    beta = jnp.sum(jnp.where(lane == h, beta_all, 0.0), axis=1, keepdims=True)
    g = jnp.sum(jnp.where(lane == heads + h, g_all, 0.0), axis=1, keepdims=True)
    return beta, g


def _gdn_prompt_kernel(q_ref, k_ref, v_ref, z_ref, ba_ref, cwq_ref, cwk_ref, cwv_ref, alog_ref, dtb_ref,
                       nw_ref, o_ref, s_ref, qbuf, kbuf, vbuf, s_scr, *, heads):
    hp = s_scr.shape[0]
    blk = pl.program_id(2)
    cb = q_ref.shape[0]
    c = CHUNK
    n_chunks = cb // c

    @pl.when(blk == 0)
    def _():
        s_scr[...] = jnp.zeros_like(s_scr)
        for buf in (qbuf, kbuf, vbuf):
            buf[0:8, :] = jnp.zeros((8, hp * LANES), F32)

    for x_ref, buf in ((q_ref, qbuf), (k_ref, kbuf), (v_ref, vbuf)):
        buf[8:8 + cb, :] = x_ref[...]
    conv_w = (cwq_ref[...], cwk_ref[...], cwv_ref[...])
    ba = ba_ref[...]
    gates = [_gdn_gates(ba, alog_ref[...], dtb_ref[...], pl.program_id(1) * hp + j, heads) for j in range(hp)]
    head_lanes = [slice(j * LANES, (j + 1) * LANES) for j in range(hp)]

    row = _iota2(c, 0)
    col = _iota2(c, 1)
    causal = col <= row
    strict = col < row
    tri = causal.astype(BF16)
    eye = (col == row).astype(F32)
    nw = nw_ref[...]

    def conv_rows(buf, w, r0):
        y = buf[r0 + 5:r0 + 5 + c, :] * w[0:1]
        y = y + buf[r0 + 6:r0 + 6 + c, :] * w[1:2]
        y = y + buf[r0 + 7:r0 + 7 + c, :] * w[2:3]
        return y + buf[r0 + 8:r0 + 8 + c, :] * w[3:4]

    conv = [[_silu(conv_rows(buf, w, ci * c)) for buf, w in zip((qbuf, kbuf, vbuf), conv_w)]
            for ci in range(n_chunks)]
    items = [(ci, j) for ci in range(n_chunks) for j in range(hp)]
    rows = lambda ci: slice(ci * c, (ci + 1) * c)
    qs = [_l2n(conv[ci][0][:, head_lanes[j]]) * (LANES ** -0.5) for ci, j in items]
    ks = [_l2n(conv[ci][1][:, head_lanes[j]]) for ci, j in items]
    vs = [conv[ci][2][:, head_lanes[j]] for ci, j in items]
    betas = [gates[j][0][rows(ci)] for ci, j in items]
    gcs = [_dot_exact_lhs(tri, jnp.broadcast_to(gates[j][1][rows(ci)], (c, c))) for ci, j in items]
    decays = [jnp.exp(jnp.where(causal, gc - gc.T, -jnp.inf)) for gc in gcs]
    kbs = [k * b for k, b in zip(ks, betas)]
    ms = [jnp.where(strict, _dot_nt(kb, k) * dec, 0.0) for kb, k, dec in zip(kbs, ks, decays)]
    xs = [_split2(-m) for m in ms]
    ts = [eye - m for m in ms]
    for _ in range(int(math.log2(c)) - 1):
        xs = [_split2(_dot_hp(xp, xp)) for xp in xs]
        ts = [t + _dot_hp(_split2(t), xp) for t, xp in zip(ts, xs)]
    egs = [jnp.exp(gc) for gc in gcs]
    uws = [_dot(t, jnp.concatenate([v * b, kb * eg], axis=1))
           for t, v, b, kb, eg in zip(ts, vs, betas, kbs, egs)]
    attns = [jnp.where(causal, _dot_nt(q, k) * dec, 0.0) for q, k, dec in zip(qs, ks, decays)]
    g_lasts = [gc[c - 1:c, :] for gc in gcs]
    kd_uws = [_dot_tn(k * jnp.exp(gl - gc), uw) for k, gl, gc, uw in zip(ks, g_lasts, gcs, uws)]
    at_uws = [_dot(attn, uw) for attn, uw in zip(attns, uws)]
    lhs = [jnp.concatenate([q * eg - at[:, LANES:], eye * jnp.exp(gl) - kd[:, LANES:]], axis=0)
           for q, eg, at, gl, kd in zip(qs, egs, at_uws, g_lasts, kd_uws)]

    states = [s_scr[j] for j in range(hp)]
    outs = []
    for it, (ci, j) in enumerate(items):
        both = _dot(lhs[it], states[j])
        outs.append(at_uws[it][:, :LANES] + both[:c])
        states[j] = both[c:] + kd_uws[it][:, :LANES]
    for j in range(hp):
        s_scr[j] = states[j]
    for buf in (qbuf, kbuf, vbuf):
        buf[0:8, :] = buf[cb:cb + 8, :]
    for (ci, j), o in zip(items, outs):
        o_ref[rows(ci), head_lanes[j]] = (_rms(o, nw) * _silu(z_ref[rows(ci), head_lanes[j]])).astype(BF16)

    @pl.when(blk == pl.num_programs(2) - 1)
    def _():
        s_ref[0] = s_scr[...]


def _gdn_prompt(proj, ba, conv_w, alog_pad, dtb_pad, norm_w, batch, seq, heads, cb):
    m = proj.shape[0]
    nblk = seq // cb
    hp = GDN_HEADS_PER_STEP
    assert heads % hp == 0
    groups = heads // hp
    rows = lambda b, h, c: b * nblk + c
    head_blk = lambda part: pl.BlockSpec((cb, hp * LANES), lambda b, h, c: (rows(b, h, c), part * groups + h))
    cw_blk = lambda part: pl.BlockSpec((CONV_WIDTH, hp * LANES), lambda b, h, c: (0, part * groups + h))
    row128 = pl.BlockSpec((1, LANES), lambda b, h, c: (0, 0))
    return pl.pallas_call(
        functools.partial(_gdn_prompt_kernel, heads=heads),
        grid=(batch, groups, nblk),
        in_specs=[head_blk(0), head_blk(1), head_blk(2), head_blk(3),
                  pl.BlockSpec((cb, LANES), lambda b, h, c: (rows(b, h, c), 0)),
                  cw_blk(0), cw_blk(1), cw_blk(2), row128, row128, row128],
        out_specs=[pl.BlockSpec((cb, hp * LANES), lambda b, h, c: (rows(b, h, c), h)),
                   pl.BlockSpec((1, hp, LANES, LANES), lambda b, h, c: (b, h, 0, 0))],
        out_shape=[jax.ShapeDtypeStruct((m, heads * LANES), BF16),
                   jax.ShapeDtypeStruct((batch, heads, LANES, LANES), F32)],
        scratch_shapes=[pltpu.VMEM((cb + 8, hp * LANES), F32)] * 3 + [pltpu.VMEM((hp, LANES, LANES), F32)],
        compiler_params=_params(("arbitrary", "arbitrary", "arbitrary")),
    )(proj, proj, proj, proj, ba, conv_w, conv_w, conv_w, alog_pad, dtb_pad, norm_w.reshape(1, LANES))


def _gdn_step_kernel(q_ref, k_ref, v_ref, z_ref, ba_ref, cq_ref, ck_ref, cv_ref, cwq_ref, cwk_ref, cwv_ref,
                     alog_ref, dtb_ref, nw_ref, s_ref, o_ref, so_ref, *, heads):
    h = pl.program_id(1)
    tb = q_ref.shape[0]

    def conv(x_ref, c_ref, w_ref):
        w = w_ref[...]
        y = c_ref[0] * w[0:1]
        y = y + c_ref[1] * w[1:2]
        y = y + c_ref[2] * w[2:3]
        return y + x_ref[...] * w[3:4]

    q = _l2n(_silu(conv(q_ref, cq_ref, cwq_ref))) * (LANES ** -0.5)
    k = _l2n(_silu(conv(k_ref, ck_ref, cwk_ref)))
    v = _silu(conv(v_ref, cv_ref, cwv_ref))
    beta, g = _gdn_gates(ba_ref[...], alog_ref[...], dtb_ref[...], h, heads)
    eg = jnp.exp(g)
    outs = []
    group = 4
    for b0 in range(0, tb, group):
        rows = range(b0, min(b0 + group, tb))
        kcols = [_col_from_row(k[b:b + 1]) for b in rows]
        for b, kcol in zip(rows, kcols):
            r = slice(b, b + 1)
            s = s_ref[b, 0] * eg[r]
            v_new = beta[r] * (v[r] - jnp.sum(kcol * s, axis=0, keepdims=True))
            s = s + kcol * v_new
            so_ref[b, 0] = s
            outs.append(_dot(q[r], s))
    o = jnp.concatenate(outs, axis=0)
    o_ref[...] = (_rms(o, nw_ref[...]) * _silu(z_ref[...])).astype(BF16)


def _gdn_step(proj, ba, conv_state_t, conv_w, alog_pad, dtb_pad, norm_w, state, layer, heads, tb):
    bsz = proj.shape[0]
    head_blk = lambda off: pl.BlockSpec((tb, LANES), lambda i, h: (i, off + h))
    cs_blk = lambda off: pl.BlockSpec((CONV_WIDTH - 1, tb, LANES), lambda i, h: (0, i, off + h))
    cw_blk = lambda off: pl.BlockSpec((CONV_WIDTH, LANES), lambda i, h: (0, off + h))
    row128 = pl.BlockSpec((1, LANES), lambda i, h: (0, 0))
    st_in = pl.BlockSpec((None, tb, 1, LANES, LANES), lambda i, h: (layer, i, h, 0, 0))
    st_out = pl.BlockSpec((tb, 1, LANES, LANES), lambda i, h: (i, h, 0, 0))
    return pl.pallas_call(
        functools.partial(_gdn_step_kernel, heads=heads),
        grid=(bsz // tb, heads),
        in_specs=[head_blk(0), head_blk(heads), head_blk(2 * heads), head_blk(3 * heads),
                  pl.BlockSpec((tb, LANES), lambda i, h: (i, 0)),
                  cs_blk(0), cs_blk(heads), cs_blk(2 * heads),
                  cw_blk(0), cw_blk(heads), cw_blk(2 * heads), row128, row128, row128, st_in],
        out_specs=[pl.BlockSpec((tb, LANES), lambda i, h: (i, h)), st_out],
        out_shape=[jax.ShapeDtypeStruct((bsz, heads * LANES), BF16),
                   jax.ShapeDtypeStruct(state.shape[1:], F32)],
        compiler_params=_params(("arbitrary", "arbitrary")),
    )(proj, proj, proj, proj, ba, conv_state_t, conv_state_t, conv_state_t, conv_w, conv_w, conv_w,
      alog_pad, dtb_pad, norm_w.reshape(1, LANES), state)


def _lru_gates(x, wa_ref, wi_ref, ba, bi, lam):
    nb = wa_ref.shape[0]
    ga, gi = [], []
    for s in range(nb):
        xs = x[:, s * LANES:(s + 1) * LANES].astype(BF16)
        ga.append(jnp.dot(xs, wa_ref[s], preferred_element_type=F32))
        gi.append(jnp.dot(xs, wi_ref[s], preferred_element_type=F32))
    gate_a = _sigmoid(jnp.concatenate(ga, axis=-1) + ba)
    gate_i = _sigmoid(jnp.concatenate(gi, axis=-1) + bi)
    log_a = -LRU_C * gate_a * _softplus(-lam)
    return log_a, gate_i


def _lru_prompt_kernel(xl_ref, yl_ref, cw_ref, cb_ref, wa_ref, wi_ref, ba_ref, bi_ref, lam_ref,
                       o_ref, hl_ref, xbuf, abuf, bbuf, h_scr):
    blk = pl.program_id(1)
    cb = xl_ref.shape[0]
    width = xl_ref.shape[1]

    @pl.when(blk == 0)
    def _():
        h_scr[...] = jnp.zeros_like(h_scr)
        xbuf[0:8, :] = jnp.zeros((8, width), F32)

    x = _conv_block(xl_ref, xbuf, cw_ref[...]) + cb_ref[...]
    log_a, gate_i = _lru_gates(x, wa_ref, wi_ref, ba_ref[...], bi_ref[...], lam_ref[...])
    mult = jnp.sqrt(_neg_expm1(2.0 * log_a))
    first = (lax.broadcasted_iota(jnp.int32, (cb, 1), 0) == 0) & (blk == 0)
    mult = jnp.where(first, 1.0, mult)
    abuf[...] = jnp.exp(log_a)
    bbuf[...] = mult * gate_i * x

    def step(t, h):
        h = abuf[pl.ds(t, 1), :] * h + bbuf[pl.ds(t, 1), :]
        bbuf[pl.ds(t, 1), :] = h
        return h

    h_last = lax.fori_loop(0, cb, step, h_scr[...], unroll=8)
    h_scr[...] = h_last
    o_ref[...] = (bbuf[...] * _gelu_tanh(yl_ref[...])).astype(BF16)

    @pl.when(blk == pl.num_programs(1) - 1)
    def _():
        hl_ref[0] = h_last


def _lru_prompt(xy, conv_w, conv_b, w_a, w_i, b_a, b_i, lam, batch, seq, cb):
    m = xy.shape[0]
    width = xy.shape[1] // 2
    nblk = seq // cb
    nb = w_a.shape[0]
    vec = pl.BlockSpec((1, width), lambda b, c: (0, 0))
    wblk = pl.BlockSpec((nb, LANES, LANES), lambda b, c: (0, 0, 0))
    out, h_last = pl.pallas_call(
        _lru_prompt_kernel,
        grid=(batch, nblk),
        in_specs=[pl.BlockSpec((cb, width), lambda b, c: (b * nblk + c, 0)),
                  pl.BlockSpec((cb, width), lambda b, c: (b * nblk + c, 1)),
                  pl.BlockSpec((CONV_WIDTH, width), lambda b, c: (0, 0)),
                  vec, wblk, wblk, vec, vec, vec],
        out_specs=[pl.BlockSpec((cb, width), lambda b, c: (b * nblk + c, 0)),
                   pl.BlockSpec((1, 1, width), lambda b, c: (b, 0, 0))],
        out_shape=[jax.ShapeDtypeStruct((m, width), BF16),
                   jax.ShapeDtypeStruct((batch, 1, width), F32)],
        scratch_shapes=[pltpu.VMEM((cb + 8, width), F32), pltpu.VMEM((cb, width), F32),
                        pltpu.VMEM((cb, width), F32), pltpu.VMEM((1, width), F32)],
        compiler_params=_params(("arbitrary", "arbitrary")),
    )(xy, xy, conv_w, conv_b.reshape(1, width), w_a, w_i, b_a.reshape(1, width), b_i.reshape(1, width),
      lam.reshape(1, width))
    return out, h_last.reshape(batch, width)


def _lru_step_kernel(xl_ref, yl_ref, cs_ref, h0_ref, cw_ref, cb_ref, wa_ref, wi_ref, ba_ref, bi_ref, lam_ref,
                     o_ref, h_ref, *, reset):
    w = cw_ref[...]
    x = cs_ref[0] * w[0:1]
    x = x + cs_ref[1] * w[1:2]
    x = x + cs_ref[2] * w[2:3]
    x = x + xl_ref[...] * w[3:4] + cb_ref[...]
    log_a, gate_i = _lru_gates(x, wa_ref, wi_ref, ba_ref[...], bi_ref[...], lam_ref[...])
    mult = 1.0 if reset else jnp.sqrt(_neg_expm1(2.0 * log_a))
    h = jnp.exp(log_a) * h0_ref[...] + mult * gate_i * x
    h_ref[...] = h
    o_ref[...] = (h * _gelu_tanh(yl_ref[...])).astype(BF16)


def _lru_step(xy, conv_state_t, h0, conv_w, conv_b, w_a, w_i, b_a, b_i, lam, reset):
    bsz = xy.shape[0]
    width = xy.shape[1] // 2
    nb = w_a.shape[0]
    vec = pl.BlockSpec((1, width), lambda i: (0, 0))
    wblk = pl.BlockSpec((nb, LANES, LANES), lambda i: (0, 0, 0))
    full = pl.BlockSpec((bsz, width), lambda i: (0, 0))
    return pl.pallas_call(
        functools.partial(_lru_step_kernel, reset=reset),
        grid=(1,),
        in_specs=[full, pl.BlockSpec((bsz, width), lambda i: (0, 1)),
                  pl.BlockSpec((CONV_WIDTH - 1, bsz, width), lambda i: (0, 0, 0)), full,
                  pl.BlockSpec((CONV_WIDTH, width), lambda i: (0, 0)), vec, wblk, wblk, vec, vec, vec],
        out_specs=[full, full],
        out_shape=[jax.ShapeDtypeStruct((bsz, width), BF16), jax.ShapeDtypeStruct((bsz, width), F32)],
        compiler_params=_params(("arbitrary",)),
    )(xy, xy, conv_state_t, h0, conv_w, conv_b.reshape(1, width), w_a, w_i, b_a.reshape(1, width),
      b_i.reshape(1, width), lam.reshape(1, width))


def _hgrn_lower_bound(lb_ref, layer, j=0):
    depth = lb_ref.shape[0]
    raw = [lb_ref[l, j] for l in range(depth)]
    mx = raw[0]
    for r in raw[1:]:
        mx = jnp.maximum(mx, r)
    ex = [jnp.exp(r - mx) for r in raw]
    tot = ex[0]
    for e in ex[1:]:
        tot = tot + e
    wts = [e / tot for e in ex]
    cum = wts[0]
    for w in wts[1:layer + 1]:
        cum = cum + w
    return cum - wts[0]


def _hgrn_inputs(q_raw, f_raw, lb):
    q = _silu(q_raw) * (LANES ** -0.5)
    f = lb + (1.0 - lb) * _sigmoid(f_raw)
    return q, f


def _level_ref(g, half):
    c = g.shape[0]
    sub = 8
    if 2 * half >= sub:
        g3 = g.reshape(c // (2 * half), 2 * half, LANES)
        return jnp.broadcast_to(g3[:, half - 1:half, :], g3.shape).reshape(c, LANES)
    g3 = g.reshape(c // sub, sub, LANES)
    rin = lax.broadcasted_iota(jnp.int32, g3.shape, 1)
    out = jnp.broadcast_to(g3[:, sub - half - 1:sub - half, :], g3.shape)
    for start in range(sub - 4 * half, -1, -2 * half):
        out = jnp.where(rin < start + 2 * half, jnp.broadcast_to(g3[:, start + half - 1:start + half, :], g3.shape),
                        out)
    return out.reshape(c, LANES)


def _hgrn_level_table(c):
    i = np.arange(c)[:, None]
    j = np.arange(c)[None, :]
    nlev = int(math.log2(c))
    top_bit = sum(((i ^ j) >> b > 0).astype(np.int32) for b in range(1, nlev))
    return jnp.asarray(np.where(j < i, nlev - 1 - top_bit, np.where(j == i, nlev, -1)), jnp.int32)


def _hgrn_prompt_kernel(q_ref, f_ref, i_ref, gz_ref, lb_ref, nw_ref, lvl_ref, o_ref, s_ref, s_scr, *, layer):
    hp = s_scr.shape[0]
    blk = pl.program_id(2)
    cb = q_ref.shape[0]
    c = CHUNK

    @pl.when(blk == 0)
    def _():
        s_scr[...] = jnp.zeros_like(s_scr)

    lbs = [_hgrn_lower_bound(lb_ref, layer, j) for j in range(hp)]
    head_lanes = [slice(j * LANES, (j + 1) * LANES) for j in range(hp)]
    tri = (_iota2(c, 1) <= _iota2(c, 0)).astype(BF16)
    nw = nw_ref[...]
    lvl = lvl_ref[...]
    halves = [c >> (i + 1) for i in range(int(math.log2(c)))]

    items = [(slice(ci * c, (ci + 1) * c), j) for ci in range(cb // c) for j in range(hp)]
    qf = [_hgrn_inputs(q_ref[sl, head_lanes[j]], f_ref[sl, head_lanes[j]], lbs[j]) for sl, j in items]
    qs = [q for q, _ in qf]
    ks = [1.0 - f for _, f in qf]
    vs = [i_ref[sl, head_lanes[j]] for sl, j in items]
    gs = [_dot_exact_lhs(tri, jnp.log(f)) for _, f in qf]

    amats = [_dot_nt(q, k) for q, k in zip(qs, ks)]
    for li, half in enumerate(halves):
        es = [jnp.exp(-jnp.abs(g - _level_ref(g, half))) for g in gs]
        ps = [_dot_nt(q * e, k * e) for q, k, e in zip(qs, ks, es)]
        amats = [jnp.where(lvl == li, p, a) for p, a in zip(ps, amats)]
    intra = [_dot(jnp.where(lvl >= 0, a, 0.0), v) for a, v in zip(amats, vs)]

    g_lasts = [g[c - 1:c, :] for g in gs]
    qdec = [q * jnp.exp(g) for q, g in zip(qs, gs)]
    sdec = [_col_from_row(jnp.exp(gl)) for gl in g_lasts]
    sadd = [_dot_tn(k * jnp.exp(gl - g), v) for k, gl, g, v in zip(ks, g_lasts, gs, vs)]

    states = [s_scr[j] for j in range(hp)]
    outs = []
    for (sl, j), o, qd, dec, add in zip(items, intra, qdec, sdec, sadd):
        outs.append(o + _dot(qd, states[j]))
        states[j] = states[j] * dec + add
    for j in range(hp):
        s_scr[j] = states[j]
    for (sl, j), o in zip(items, outs):
        o_ref[sl, head_lanes[j]] = (_rms(o, nw) * _silu(gz_ref[sl, head_lanes[j]])).astype(BF16)

    @pl.when(blk == pl.num_programs(2) - 1)
    def _():
        s_ref[0] = s_scr[...]


def _hgrn_prompt(proj, lb_raw, norm_w, layer, batch, seq, heads, cb):
    m = proj.shape[0]
    nblk = seq // cb
    depth = lb_raw.shape[0]
    hp = HGRN_HEADS_PER_STEP
    assert heads % hp == 0
    groups = heads // hp
    head_blk = lambda part: pl.BlockSpec((cb, hp * LANES), lambda b, h, c: (b * nblk + c, part * groups + h))
    return pl.pallas_call(
        functools.partial(_hgrn_prompt_kernel, layer=layer),
        grid=(batch, groups, nblk),
        in_specs=[head_blk(0), head_blk(1), head_blk(2), head_blk(3),
                  pl.BlockSpec((depth, hp, 1, LANES), lambda b, h, c: (0, h, 0, 0)),
                  pl.BlockSpec((1, LANES), lambda b, h, c: (0, 0)),
                  pl.BlockSpec((CHUNK, CHUNK), lambda b, h, c: (0, 0))],
        out_specs=[pl.BlockSpec((cb, hp * LANES), lambda b, h, c: (b * nblk + c, h)),
                   pl.BlockSpec((1, hp, LANES, LANES), lambda b, h, c: (b, h, 0, 0))],
        out_shape=[jax.ShapeDtypeStruct((m, heads * LANES), BF16),
                   jax.ShapeDtypeStruct((batch, heads, LANES, LANES), F32)],
        scratch_shapes=[pltpu.VMEM((hp, LANES, LANES), F32)],
        compiler_params=_params(("arbitrary", "arbitrary", "arbitrary")),
    )(proj, proj, proj, proj, lb_raw.reshape(depth, heads, 1, LANES), norm_w.reshape(1, LANES),
      _hgrn_level_table(CHUNK))


def _hgrn_step_kernel(q_ref, f_ref, i_ref, gz_ref, lb_ref, nw_ref, s_ref, o_ref, so_ref, *, layer):
    tb = q_ref.shape[0]
    lb = _hgrn_lower_bound(lb_ref, layer)
    q, f = _hgrn_inputs(q_ref[...], f_ref[...], lb)
    v = i_ref[...]
    outs = []
    group = 4
    for b0 in range(0, tb, group):
        rows = range(b0, min(b0 + group, tb))
        fcols = [_col_from_row(f[b:b + 1]) for b in rows]
        for b, fcol in zip(rows, fcols):
            s = fcol * (s_ref[b, 0] - v[b:b + 1]) + v[b:b + 1]
            so_ref[b, 0] = s
            outs.append(_dot(q[b:b + 1], s))
    o = jnp.concatenate(outs, axis=0)
    o_ref[...] = (_rms(o, nw_ref[...]) * _silu(gz_ref[...])).astype(BF16)


def _hgrn_step(proj, lb_raw, norm_w, state, state_idx, layer, heads, tb):
    bsz = proj.shape[0]
    depth = lb_raw.shape[0]
    head_blk = lambda off: pl.BlockSpec((tb, LANES), lambda i, h: (i, off + h))
    st_in = pl.BlockSpec((None, tb, 1, LANES, LANES), lambda i, h: (state_idx, i, h, 0, 0))
    st_out = pl.BlockSpec((tb, 1, LANES, LANES), lambda i, h: (i, h, 0, 0))
    return pl.pallas_call(
        functools.partial(_hgrn_step_kernel, layer=layer),
        grid=(bsz // tb, heads),
        in_specs=[head_blk(0), head_blk(heads), head_blk(2 * heads), head_blk(3 * heads),
                  pl.BlockSpec((depth, 1, 1, LANES), lambda i, h: (0, h, 0, 0)),
                  pl.BlockSpec((1, LANES), lambda i, h: (0, 0)), st_in],
        out_specs=[pl.BlockSpec((tb, LANES), lambda i, h: (i, h)), st_out],
        out_shape=[jax.ShapeDtypeStruct((bsz, heads * LANES), BF16),
                   jax.ShapeDtypeStruct(state.shape[1:], F32)],
        compiler_params=_params(("arbitrary", "arbitrary")),
    )(proj, proj, proj, proj, lb_raw.reshape(depth, heads, 1, LANES), norm_w.reshape(1, LANES), state)


def _trunk(x, mem_k, mem_v, states, wts, big, batch, seq, prompt):
    depth = wts["norm_mix"].shape[0]
    gdn_w = wts["gdn_conv_w"].shape[-1] // 3
    gdn_heads = gdn_w // LANES
    hgrn_heads = wts["hgrn_lb_raw"].shape[1] // LANES
    lru_w = wts["lru_conv_w"].shape[-1]
    bf = {name: list(handles) for name, handles in big.items()}
    new = {"gdn_conv": [], "gdn": [], "lru_conv": [], "lru": [], "hgrn": []}

    def keep(name, idx, w_used):
        w, li = bf[name][idx]
        bf[name][idx] = (w_used, li if w_used is w else 0)

    for l in range(depth):
        g_mix = wts["norm_mix"][l]
        if l % 2 == 0:
            e = l // 2
            proj, _ = _norm_matmul(x, g_mix, wts["ab_head"], e, 0, 4 * gdn_w, TM, TN)
            ba, _ = _norm_matmul(x, g_mix, wts["ab_head"], e, 4 * gdn_w, LANES, TM, LANES)
            xy, _ = _norm_matmul(x, g_mix, wts["ab_lru"], e, 0, 2 * lru_w, TM, TN)
            gargs = (wts["gdn_conv_w"][e], wts["alog_pad"][e], wts["dtb_pad"][e], wts["gdn_norm_w"][e])
            largs = (wts["lru_conv_w"][e], wts["lru_conv_b"][e], wts["lru_w_a"][e], wts["lru_w_i"][e],
                     wts["lru_b_a"][e], wts["lru_b_i"][e], wts["lru_lam"][e])
            w_out, li = bf["ab_w_out"][e]
            if prompt:
                o_a, s_new = _gdn_prompt(proj, ba, *gargs, batch, seq, gdn_heads, 512)
                o_b, h_new = _lru_prompt(xy, *largs, batch, seq, 256)
                tail = lambda t, w: t.reshape(batch, seq, -1)[:, seq - (CONV_WIDTH - 1):, :w]
                new["gdn_conv"].append(tail(proj, 3 * gdn_w))
                new["lru_conv"].append(tail(xy, lru_w))
                x = _matmul2_res(o_a, o_b, w_out, li, x, TM, TN)
            else:
                gc_state, lc_state = states["gdn_conv"][e], states["lru_conv"][e]
                o_a, s_new = _gdn_step(proj, ba, jnp.swapaxes(gc_state, 0, 1), *gargs, states["gdn"], e,
                                       gdn_heads, 128)
                o_b, h_new = _lru_step(xy, jnp.swapaxes(lc_state, 0, 1), states["lru"][e], *largs, reset=False)
                new["gdn_conv"].append(jnp.concatenate([gc_state[:, 1:], proj[:, None, :3 * gdn_w]], axis=1))
                new["lru_conv"].append(jnp.concatenate([lc_state[:, 1:], xy[:, None, :lru_w]], axis=1))
                x, w_used = _matmul_res(jnp.concatenate([o_a, o_b], axis=1), w_out, li, x, TM, TN)
                keep("ab_w_out", e, w_used)
            new["gdn"].append(s_new)
            new["lru"].append(h_new)
        else:
            o_idx = l // 2
            w_in, li = bf["c_w_in"][o_idx]
            proj, w_used = _norm_matmul(x, g_mix, w_in, li, 0, w_in.shape[2], TM, TN)
            keep("c_w_in", o_idx, w_used)
            hargs = (wts["hgrn_lb_raw"], wts["hgrn_norm_w"][o_idx])
            if prompt:
                o_c, s_new = _hgrn_prompt(proj, *hargs, l, batch, seq, hgrn_heads, 512)
            else:
                o_c, s_new = _hgrn_step(proj, *hargs, states["hgrn"], o_idx, l, hgrn_heads, 128)
            new["hgrn"].append(s_new)
            w_out, li = bf["c_w_out"][o_idx]
            x, w_used = _matmul_res(o_c, w_out, li, x, TM, TN)
            keep("c_w_out", o_idx, w_used)
        (w_q, lq), (w_o, lo) = bf["mem_w_q"][l], bf["mem_w_o"][l]
        if prompt:
            assert lq == lo
            x = _mem_attn_prompt(x, wts["norm_mem"][l], w_q, mem_k, mem_v, w_o, l, lq, seq, 512)
        else:
            q, w_used = _norm_matmul(x, wts["norm_mem"][l], w_q, lq, 0, w_q.shape[2], TM, TN)
            keep("mem_w_q", l, w_used)
            o = _mem_attn_sample(q, mem_k, mem_v, l, 8)
            x, w_used = _matmul_res(o.reshape(batch, -1).astype(BF16), w_o, lo, x, TM, TN)
            keep("mem_w_o", l, w_used)
        (w_up, lu), (w_dn, ld) = bf["ffn_w_up"][l], bf["ffn_w_down"][l]
        assert lu == ld
        x, up_used, dn_used = _ffn(x, wts["norm_ffn"][l], w_up, w_dn, lu, wts["norm_final"], l == depth - 1,
                                   TM, TF)
        keep("ffn_w_up", l, up_used)
        keep("ffn_w_down", l, dn_used)
    return x, {n: jnp.stack(v) for n, v in new.items()}, bf


def kernel(x_prompt, x_sample, cache_mem_k, cache_mem_v, state_gdn_conv, state_gdn, state_lru_conv, state_lru, state_hgrn, mem_prompt, norm_mix, norm_mem, norm_mem_kv, norm_ffn, norm_final, ab_w_in, ab_w_out, gdn_conv_w, gdn_a_log, gdn_dt_bias, gdn_norm_w, lru_conv_w, lru_conv_b, lru_w_a, lru_b_a, lru_w_i, lru_b_i, lru_lam, c_w_in, c_w_out, hgrn_lb_raw, hgrn_norm_w, mem_w_q, mem_w_k, mem_w_v, mem_w_o, ffn_w_up, ffn_w_down):
    bp, lp, d = x_prompt.shape
    bs, ls, _ = x_sample.shape
    assert ls == 1, "the sample group advances one token per call"
    depth = norm_mix.shape[0]
    gdn_heads = gdn_a_log.shape[1]
    gdn_w = gdn_heads * LANES
    mem_tokens, mem_heads, mem_hd = cache_mem_k.shape[2:]
    mem_w = mem_heads * mem_hd
    n_ba = 2 * gdn_heads
    assert n_ba <= LANES

    pad_ba = lambda a: jnp.pad(a, ((0, 0), (gdn_heads, LANES - n_ba)))[:, None, :]
    ab_bf = ab_w_in.astype(BF16)
    wts = dict(
        norm_mix=norm_mix, norm_mem=norm_mem, norm_ffn=norm_ffn, norm_final=norm_final,
        ab_head=ab_bf, ab_lru=ab_bf[:, :, 4 * gdn_w + n_ba:],
        gdn_conv_w=gdn_conv_w, alog_pad=pad_ba(gdn_a_log), dtb_pad=pad_ba(gdn_dt_bias), gdn_norm_w=gdn_norm_w,
        lru_conv_w=lru_conv_w, lru_conv_b=lru_conv_b, lru_w_a=lru_w_a.astype(BF16), lru_w_i=lru_w_i.astype(BF16),
        lru_b_a=lru_b_a, lru_b_i=lru_b_i, lru_lam=lru_lam,
        hgrn_lb_raw=hgrn_lb_raw, hgrn_norm_w=hgrn_norm_w,
    )
    big = dict(ab_w_out=ab_w_out, c_w_in=c_w_in, c_w_out=c_w_out, mem_w_q=mem_w_q, mem_w_o=mem_w_o,
               ffn_w_up=ffn_w_up, ffn_w_down=ffn_w_down)
    big = {name: [(w, l) for l in range(w.shape[0])] for name, w in big.items()}

    states = dict(gdn_conv=state_gdn_conv, gdn=state_gdn, lru_conv=state_lru_conv, lru=state_lru,
                  hgrn=state_hgrn)
    y_s, new_s, big_bf = _trunk(x_sample.reshape(bs * ls, d), cache_mem_k, cache_mem_v, states, wts, big,
                                bs, ls, False)

    mem_rows = mem_prompt.reshape(bp * mem_tokens, d)
    w_kv = jnp.concatenate([mem_w_k, mem_w_v], axis=-1).astype(BF16)
    kv = jnp.stack([_norm_matmul(mem_rows, norm_mem_kv[l], w_kv, l, 0, 2 * mem_w, TM, TN)[0] for l in range(depth)])
    p_mem_k = kv[:, :, :mem_w].reshape(depth, bp, mem_tokens, mem_w)
    p_mem_v = kv[:, :, mem_w:].reshape(depth, bp, mem_tokens, mem_w)

    y_p, new_p, _ = _trunk(x_prompt.reshape(bp * lp, d), p_mem_k, p_mem_v, None, wts, big_bf, bp, lp, True)

    order = ("gdn_conv", "gdn", "lru_conv", "lru", "hgrn")
    mem5 = lambda t: t.reshape(depth, bp, mem_tokens, mem_heads, mem_hd)
    return (y_p.reshape(bp, lp, d), y_s.reshape(bs, ls, d), mem5(p_mem_k), mem5(p_mem_v),
            *(new_p[n] for n in order), *(new_s[n] for n in order))
```

```python
import functools
import math

import jax
import jax.numpy as jnp
import numpy as np
from jax import lax
from jax.experimental import pallas as pl
from jax.experimental.pallas import tpu as pltpu

F32 = jnp.float32
BF16 = jnp.bfloat16
EPS = 1e-6
LANES = 128
CONV_WIDTH = 4
LRU_C = 8.0
VMEM_LIMIT_BYTES = 56 * 1024 * 1024
TM = 512
TM_BF16 = 1024
TN = 1024
TF = 512
TN_CAST = 512
TF_CAST = 256
ROW_GROUPS = 4
CHUNK = 128
GDN_HEADS_PER_STEP = 4
HGRN_HEADS_PER_STEP = 8


def _params(sem):
    return pltpu.CompilerParams(dimension_semantics=sem, vmem_limit_bytes=VMEM_LIMIT_BYTES)


def _sigmoid(x):
    return jax.nn.sigmoid(x)


def _silu(x):
    return x * _sigmoid(x)


def _softplus(x):
    return jnp.maximum(x, 0.0) + jnp.log1p(jnp.exp(-jnp.abs(x)))


def _neg_expm1(x):
    t = jnp.tanh(0.5 * x)
    return -2.0 * t / (1.0 - t)


def _gelu_tanh(x):
    c = math.sqrt(2.0 / math.pi)
    return 0.5 * x * (1.0 + jnp.tanh(c * (x + 0.044715 * (x * x * x))))


def _rms(x, w):
    return x * lax.rsqrt(jnp.mean(x * x, axis=-1, keepdims=True) + EPS) * w


def _l2n(x):
    return x * lax.rsqrt(jnp.sum(x * x, axis=-1, keepdims=True) + EPS)


def _dot(a, b):
    return jnp.dot(a.astype(BF16), b.astype(BF16), preferred_element_type=F32)


def _dot_nt(a, b):
    return lax.dot_general(a.astype(BF16), b.astype(BF16), (((1,), (1,)), ((), ())),
                           preferred_element_type=F32)


def _dot_tn(a, b):
    return lax.dot_general(a.astype(BF16), b.astype(BF16), (((0,), (0,)), ((), ())),
                           preferred_element_type=F32)


def _split2(a):
    hi = a.astype(BF16)
    lo = (a - hi.astype(F32)).astype(BF16)
    return hi, lo


def _dot_hp(a_parts, b_parts):
    ah, al = a_parts
    bh, bl = b_parts
    return jnp.dot(jnp.concatenate([ah, ah, al], axis=1), jnp.concatenate([bh, bl, bh], axis=0),
                   preferred_element_type=F32)


def _dot_exact_lhs(t, b):
    b1 = b.astype(BF16)
    r1 = b - b1.astype(F32)
    b2 = r1.astype(BF16)
    b3 = (r1 - b2.astype(F32)).astype(BF16)
    return jnp.dot(jnp.concatenate([t, t, t], axis=1), jnp.concatenate([b1, b2, b3], axis=0),
                   preferred_element_type=F32)


def _iota2(n, axis):
    return lax.broadcasted_iota(jnp.int32, (n, n), axis)


def _col_from_row(row):
    n = row.shape[-1]
    return jnp.broadcast_to(row, (n, n)).T


def _row_groups(rows):
    n = ROW_GROUPS if rows % (8 * ROW_GROUPS) == 0 else 1
    return [slice(r * rows // n, (r + 1) * rows // n) for r in range(n)]


def _norm_matmul_kernel(x_ref, g_ref, w_ref, o_ref, *rest):
    xn_ref = rest[-1]
    if len(rest) == 2:
        rest[0][...] = w_ref[...].astype(BF16)
        w_ref = rest[0]
    first = pl.program_id(1) == 0

    @pl.when(first)
    def _():
        for rows in _row_groups(x_ref.shape[0]):
            xn = _rms(x_ref[rows, :], g_ref[...]).astype(BF16)
            xn_ref[rows, :] = xn
            o_ref[rows, :] = jnp.dot(xn, w_ref[...], preferred_element_type=F32)

    @pl.when(jnp.logical_not(first))
    def _():
        o_ref[...] = jnp.dot(xn_ref[...], w_ref[...], preferred_element_type=F32)


def _norm_matmul(x, g, w, layer, n0, n, tm, tn):
    m, k = x.shape
    cast = w.dtype != BF16
    tm = min(tm, m)
    tn = min(tn, n, TN_CAST) if cast else min(tn, n)
    j0 = n0 // tn
    out_specs = [pl.BlockSpec((tm, tn), lambda i, j: (i, j))]
    out_shape = [jax.ShapeDtypeStruct((m, n), F32)]
    if cast:
        assert m == tm and n0 == 0 and n == w.shape[2], "each weight tile must be visited exactly once"
        out_specs.append(pl.BlockSpec((None, k, tn), lambda i, j: (0, 0, j)))
        out_shape.append(jax.ShapeDtypeStruct((1, k, n), BF16))
    res = pl.pallas_call(
        _norm_matmul_kernel,
        grid=(m // tm, n // tn),
        in_specs=[pl.BlockSpec((tm, k), lambda i, j: (i, 0)),
                  pl.BlockSpec((1, k), lambda i, j: (0, 0)),
                  pl.BlockSpec((None, k, tn), lambda i, j: (layer, 0, j0 + j))],
        out_specs=out_specs,
        out_shape=out_shape,
        scratch_shapes=[pltpu.VMEM((tm, k), BF16)],
        compiler_params=_params(("arbitrary", "arbitrary")),
    )(x, g.reshape(1, k), w)
    return (res[0], res[1]) if cast else (res[0], w)


def _rms_cast_kernel(x_ref, g_ref, o_ref):
    o_ref[...] = _rms(x_ref[...], g_ref[...]).astype(BF16)


def _rms_cast(x, g, tm):
    m, k = x.shape
    tm = min(tm, m)
    return pl.pallas_call(
        _rms_cast_kernel,
        grid=(m // tm,),
        in_specs=[pl.BlockSpec((tm, k), lambda i: (i, 0)), pl.BlockSpec((1, k), lambda i: (0, 0))],
        out_specs=pl.BlockSpec((tm, k), lambda i: (i, 0)),
        out_shape=jax.ShapeDtypeStruct((m, k), BF16),
        compiler_params=_params(("arbitrary",)),
    )(x, g.reshape(1, k))


def _matmul_kernel(a_ref, w_ref, o_ref):
    o_ref[...] = jnp.dot(a_ref[...], w_ref[...], preferred_element_type=F32)


def _matmul(a, w, layer, n0, n, tm, tn):
    m, k = a.shape
    tm = min(tm, m)
    tn = min(tn, n)
    j0 = n0 // tn
    return pl.pallas_call(
        _matmul_kernel,
        grid=(m // tm, n // tn),
        in_specs=[pl.BlockSpec((tm, k), lambda i, j: (i, 0)),
                  pl.BlockSpec((None, k, tn), lambda i, j: (layer, 0, j0 + j))],
        out_specs=pl.BlockSpec((tm, tn), lambda i, j: (i, j)),
        out_shape=jax.ShapeDtypeStruct((m, n), F32),
        compiler_params=_params(("arbitrary", "arbitrary")),
    )(a, w)


def _matmul2_res_kernel(a1_ref, a2_ref, w1_ref, w2_ref, r_ref, o_ref):
    acc = jnp.dot(a1_ref[...], w1_ref[...], preferred_element_type=F32)
    acc = acc + jnp.dot(a2_ref[...], w2_ref[...], preferred_element_type=F32)
    o_ref[...] = r_ref[...] + acc


def _matmul2_res(a1, a2, w, layer, res, tm, tn):
    m, kh = a1.shape
    n = w.shape[2]
    tm = min(tm, m)
    tn = min(tn, n)
    return pl.pallas_call(
        _matmul2_res_kernel,
        grid=(m // tm, n // tn),
        in_specs=[pl.BlockSpec((tm, kh), lambda i, j: (i, 0)),
                  pl.BlockSpec((tm, kh), lambda i, j: (i, 0)),
                  pl.BlockSpec((None, kh, tn), lambda i, j: (layer, 0, j)),
                  pl.BlockSpec((None, kh, tn), lambda i, j: (layer, 1, j)),
                  pl.BlockSpec((tm, tn), lambda i, j: (i, j))],
        out_specs=pl.BlockSpec((tm, tn), lambda i, j: (i, j)),
        out_shape=jax.ShapeDtypeStruct((m, n), F32),
        compiler_params=_params(("arbitrary", "arbitrary")),
    )(a1, a2, w, w, res)


def _matmul_res_kernel(a_ref, w_ref, r_ref, o_ref, *wb_ref):
    w = w_ref[...]
    if wb_ref:
        w = w.astype(BF16)
        wb_ref[0][...] = w
    o_ref[...] = r_ref[...] + jnp.dot(a_ref[...], w, preferred_element_type=F32)


def _matmul_res(a, w, layer, res, tm, tn):
    m, k = a.shape
    n = w.shape[2]
    cast = w.dtype != BF16
    tm = min(tm, m)
    tn = min(tn, n, TN_CAST) if cast else min(tn, n)
    out_specs = [pl.BlockSpec((tm, tn), lambda i, j: (i, j))]
    out_shape = [jax.ShapeDtypeStruct((m, n), F32)]
    if cast:
        assert m == tm, "each weight tile must be visited exactly once"
        out_specs.append(pl.BlockSpec((None, k, tn), lambda i, j: (0, 0, j)))
        out_shape.append(jax.ShapeDtypeStruct((1, k, n), BF16))
    out = pl.pallas_call(
        _matmul_res_kernel,
        grid=(m // tm, n // tn),
        in_specs=[pl.BlockSpec((tm, k), lambda i, j: (i, 0)),
                  pl.BlockSpec((None, k, tn), lambda i, j: (layer, 0, j)),
                  pl.BlockSpec((tm, tn), lambda i, j: (i, j))],
        out_specs=out_specs,
        out_shape=out_shape,
        compiler_params=_params(("arbitrary", "arbitrary")),
    )(a, w, res)
    return (out[0], out[1]) if cast else (out[0], w)


def _ffn_kernel(x_ref, g_ref, wu_ref, wd_ref, gf_ref, o_ref, *rest, final_norm):
    xn_ref = rest[-1]
    f = pl.program_id(1)
    if len(rest) == 3:
        rest[0][...] = wu_ref[...].astype(BF16)
        rest[1][...] = wd_ref[...].astype(BF16)
        wu_ref, wd_ref = rest[0], rest[1]

    def mlp(xn):
        h = jnp.dot(xn, wu_ref[...], preferred_element_type=F32)
        h = jnp.square(jnp.maximum(h, 0.0)).astype(BF16)
        return jnp.dot(h, wd_ref[...], preferred_element_type=F32)

    @pl.when(f == 0)
    def _():
        for rows in _row_groups(x_ref.shape[0]):
            x = x_ref[rows, :]
            xn = _rms(x, g_ref[...]).astype(BF16)
            xn_ref[rows, :] = xn
            o_ref[rows, :] = x + mlp(xn)

    @pl.when(f != 0)
    def _():
        o_ref[...] += mlp(xn_ref[...])

    if final_norm:
        @pl.when(f == pl.num_programs(1) - 1)
        def _():
            o_ref[...] = _rms(o_ref[...], gf_ref[...])


def _ffn(x, g, w_up, w_down, layer, g_final, final_norm, tm, tf):
    m, d = x.shape
    dff = w_up.shape[2]
    cast = w_up.dtype != BF16
    tm = min(tm, m)
    tf = min(tf, TF_CAST) if cast else tf
    out_specs = [pl.BlockSpec((tm, d), lambda i, f: (i, 0))]
    out_shape = [jax.ShapeDtypeStruct((m, d), F32)]
    if cast:
        assert m == tm, "each weight tile must be visited exactly once"
        out_specs += [pl.BlockSpec((None, d, tf), lambda i, f: (0, 0, f)),
                      pl.BlockSpec((None, tf, d), lambda i, f: (0, f, 0))]
        out_shape += [jax.ShapeDtypeStruct((1, d, dff), BF16), jax.ShapeDtypeStruct((1, dff, d), BF16)]
    out = pl.pallas_call(
        functools.partial(_ffn_kernel, final_norm=final_norm),
        grid=(m // tm, dff // tf),
        in_specs=[pl.BlockSpec((tm, d), lambda i, f: (i, 0), pipeline_mode=pl.Buffered(1)),
                  pl.BlockSpec((1, d), lambda i, f: (0, 0)),
                  pl.BlockSpec((None, d, tf), lambda i, f: (layer, 0, f)),
                  pl.BlockSpec((None, tf, d), lambda i, f: (layer, f, 0)),
                  pl.BlockSpec((1, d), lambda i, f: (0, 0))],
        out_specs=out_specs,
        out_shape=out_shape,
        scratch_shapes=[pltpu.VMEM((tm, d), BF16)],
        compiler_params=_params(("arbitrary", "arbitrary")),
    )(x, g.reshape(1, d), w_up, w_down, g_final.reshape(1, d))
    return (out[0], out[1], out[2]) if cast else (out[0], w_up, w_down)


def _mem_attn_prompt_kernel(x_ref, g_ref, wq_ref, k_ref, v_ref, wo_ref, o_ref, *, heads):
    x = x_ref[...]
    xn = _rms(x, g_ref[...]).astype(BF16)
    q = jnp.dot(xn, wq_ref[...], preferred_element_type=F32)
    k = k_ref[0].astype(BF16)
    v = v_ref[0].astype(BF16)
    scale = LANES ** -0.5
    outs = []
    for h in range(heads):
        sl = slice(h * LANES, (h + 1) * LANES)
        s = _dot_nt(q[:, sl], k[:, sl]) * scale
        e = jnp.exp(s - jnp.max(s, axis=-1, keepdims=True))
        p = e / jnp.sum(e, axis=-1, keepdims=True)
        outs.append(_dot(p, v[:, sl]))
    o = jnp.concatenate(outs, axis=-1).astype(BF16)
    o_ref[...] = x + jnp.dot(o, wo_ref[...], preferred_element_type=F32)


def _mem_attn_prompt(x, g, w_q, mem_k, mem_v, w_o, layer, w_layer, seq, tl):
    m, d = x.shape
    _, _, t, w = mem_k.shape
    nblk = seq // tl
    return pl.pallas_call(
        functools.partial(_mem_attn_prompt_kernel, heads=w // LANES),
        grid=(m // tl,),
        in_specs=[pl.BlockSpec((tl, d), lambda i: (i, 0)),
                  pl.BlockSpec((1, d), lambda i: (0, 0)),
                  pl.BlockSpec((None, d, w), lambda i: (w_layer, 0, 0), pipeline_mode=pl.Buffered(1)),
                  pl.BlockSpec((None, 1, t, w), lambda i: (layer, i // nblk, 0, 0)),
                  pl.BlockSpec((None, 1, t, w), lambda i: (layer, i // nblk, 0, 0)),
                  pl.BlockSpec((None, w, d), lambda i: (w_layer, 0, 0), pipeline_mode=pl.Buffered(1))],
        out_specs=pl.BlockSpec((tl, d), lambda i: (i, 0)),
        out_shape=jax.ShapeDtypeStruct((m, d), F32),
        compiler_params=_params(("arbitrary",)),
    )(x, g.reshape(1, d), w_q, mem_k, mem_v, w_o)


def _mem_attn_sample_kernel(q_ref, k_ref, v_ref, o_ref, *, tb, heads):
    scale = LANES ** -0.5
    for b in range(tb):
        q = jnp.concatenate([q_ref[b:b + 1, h * LANES:(h + 1) * LANES] for h in range(heads)], axis=0)
        s = jnp.sum(k_ref[b] * (q * scale)[None], axis=-1, keepdims=True)
        e = jnp.exp(s - jnp.max(s, axis=0, keepdims=True))
        o_ref[b] = jnp.sum(e * v_ref[b], axis=0) / jnp.sum(e, axis=0)


def _mem_attn_sample(q, mem_k, mem_v, layer, tb):
    b = q.shape[0]
    _, _, t, heads, hd = mem_k.shape
    kv_blk = pl.BlockSpec((None, tb, t, heads, hd), lambda i: (layer, i, 0, 0, 0))
    return pl.pallas_call(
        functools.partial(_mem_attn_sample_kernel, tb=tb, heads=heads),
        grid=(b // tb,),
        in_specs=[pl.BlockSpec((tb, heads * hd), lambda i: (i, 0)), kv_blk, kv_blk],
        out_specs=pl.BlockSpec((tb, heads, hd), lambda i: (i, 0, 0)),
        out_shape=jax.ShapeDtypeStruct((b, heads, hd), F32),
        compiler_params=_params(("arbitrary",)),
    )(q, mem_k, mem_v)


def _conv_block(x_ref, buf, w):
    cb = x_ref.shape[0]
    buf[8:8 + cb, :] = x_ref[...]
    y = buf[5:5 + cb, :] * w[0:1]
    y = y + buf[6:6 + cb, :] * w[1:2]
    y = y + buf[7:7 + cb, :] * w[2:3]
    y = y + buf[8:8 + cb, :] * w[3:4]
    buf[0:8, :] = buf[cb:cb + 8, :]
    return y


def _gdn_gates(ba, alog, dtb, h, heads):
    lane = lax.broadcasted_iota(jnp.int32, ba.shape, 1)
    beta_all = _sigmoid(ba)
    g_all = -jnp.exp(alog) * _softplus(ba + dtb)
    beta = jnp.sum(jnp.where(lane == h, beta_all, 0.0), axis=1, keepdims=True)
    g = jnp.sum(jnp.where(lane == heads + h, g_all, 0.0), axis=1, keepdims=True)
    return beta, g


def _gdn_prompt_kernel(q_ref, k_ref, v_ref, z_ref, ba_ref, cwq_ref, cwk_ref, cwv_ref, alog_ref, dtb_ref,
                       nw_ref, o_ref, s_ref, qbuf, kbuf, vbuf, s_scr, *, heads):
    hp = s_scr.shape[0]
    blk = pl.program_id(2)
    cb = q_ref.shape[0]
    c = CHUNK
    n_chunks = cb // c

    @pl.when(blk == 0)
    def _():
        s_scr[...] = jnp.zeros_like(s_scr)
        for buf in (qbuf, kbuf, vbuf):
            buf[0:8, :] = jnp.zeros((8, hp * LANES), F32)

    for x_ref, buf in ((q_ref, qbuf), (k_ref, kbuf), (v_ref, vbuf)):
        buf[8:8 + cb, :] = x_ref[...]
    conv_w = (cwq_ref[...], cwk_ref[...], cwv_ref[...])
    ba = ba_ref[...]
    gates = [_gdn_gates(ba, alog_ref[...], dtb_ref[...], pl.program_id(1) * hp + j, heads) for j in range(hp)]
    head_lanes = [slice(j * LANES, (j + 1) * LANES) for j in range(hp)]

    row = _iota2(c, 0)
    col = _iota2(c, 1)
    causal = col <= row
    strict = col < row
    tri = causal.astype(BF16)
    eye = (col == row).astype(F32)
    nw = nw_ref[...]

    def conv_rows(buf, w, r0):
        y = buf[r0 + 5:r0 + 5 + c, :] * w[0:1]
        y = y + buf[r0 + 6:r0 + 6 + c, :] * w[1:2]
        y = y + buf[r0 + 7:r0 + 7 + c, :] * w[2:3]
        return y + buf[r0 + 8:r0 + 8 + c, :] * w[3:4]

    conv = [[_silu(conv_rows(buf, w, ci * c)) for buf, w in zip((qbuf, kbuf, vbuf), conv_w)]
            for ci in range(n_chunks)]
    items = [(ci, j) for ci in range(n_chunks) for j in range(hp)]
    rows = lambda ci: slice(ci * c, (ci + 1) * c)
    qs = [_l2n(conv[ci][0][:, head_lanes[j]]) * (LANES ** -0.5) for ci, j in items]
    ks = [_l2n(conv[ci][1][:, head_lanes[j]]) for ci, j in items]
    vs = [conv[ci][2][:, head_lanes[j]] for ci, j in items]
    betas = [gates[j][0][rows(ci)] for ci, j in items]
    gcs = [_dot_exact_lhs(tri, jnp.broadcast_to(gates[j][1][rows(ci)], (c, c))) for ci, j in items]
    decays = [jnp.exp(jnp.where(causal, gc - gc.T, -jnp.inf)) for gc in gcs]
    kbs = [k * b for k, b in zip(ks, betas)]
    ms = [jnp.where(strict, _dot_nt(kb, k) * dec, 0.0) for kb, k, dec in zip(kbs, ks, decays)]
    xs = [_split2(-m) for m in ms]
    ts = [eye - m for m in ms]
    for _ in range(int(math.log2(c)) - 1):
        xs = [_split2(_dot_hp(xp, xp)) for xp in xs]
        ts = [t + _dot_hp(_split2(t), xp) for t, xp in zip(ts, xs)]
    egs = [jnp.exp(gc) for gc in gcs]
    uws = [_dot(t, jnp.concatenate([v * b, kb * eg], axis=1))
           for t, v, b, kb, eg in zip(ts, vs, betas, kbs, egs)]
    attns = [jnp.where(causal, _dot_nt(q, k) * dec, 0.0) for q, k, dec in zip(qs, ks, decays)]
    g_lasts = [gc[c - 1:c, :] for gc in gcs]
    kd_uws = [_dot_tn(k * jnp.exp(gl - gc), uw) for k, gl, gc, uw in zip(ks, g_lasts, gcs, uws)]
    at_uws = [_dot(attn, uw) for attn, uw in zip(attns, uws)]
    lhs = [jnp.concatenate([q * eg - at[:, LANES:], eye * jnp.exp(gl) - kd[:, LANES:]], axis=0)
           for q, eg, at, gl, kd in zip(qs, egs, at_uws, g_lasts, kd_uws)]

    states = [s_scr[j] for j in range(hp)]
    outs = []
    for it, (ci, j) in enumerate(items):
        both = _dot(lhs[it], states[j])
        outs.append(at_uws[it][:, :LANES] + both[:c])
        states[j] = both[c:] + kd_uws[it][:, :LANES]
    for j in range(hp):
        s_scr[j] = states[j]
    for buf in (qbuf, kbuf, vbuf):
        buf[0:8, :] = buf[cb:cb + 8, :]
    for (ci, j), o in zip(items, outs):
        o_ref[rows(ci), head_lanes[j]] = (_rms(o, nw) * _silu(z_ref[rows(ci), head_lanes[j]])).astype(BF16)

    @pl.when(blk == pl.num_programs(2) - 1)
    def _():
        s_ref[0] = s_scr[...]


def _gdn_prompt(proj, ba, conv_w, alog_pad, dtb_pad, norm_w, batch, seq, heads, cb):
    m = proj.shape[0]
    nblk = seq // cb
    hp = GDN_HEADS_PER_STEP
    assert heads % hp == 0
    groups = heads // hp
    rows = lambda b, h, c: b * nblk + c
    head_blk = lambda part: pl.BlockSpec((cb, hp * LANES), lambda b, h, c: (rows(b, h, c), part * groups + h))
    cw_blk = lambda part: pl.BlockSpec((CONV_WIDTH, hp * LANES), lambda b, h, c: (0, part * groups + h))
    row128 = pl.BlockSpec((1, LANES), lambda b, h, c: (0, 0))
    return pl.pallas_call(
        functools.partial(_gdn_prompt_kernel, heads=heads),
        grid=(batch, groups, nblk),
        in_specs=[head_blk(0), head_blk(1), head_blk(2), head_blk(3),
                  pl.BlockSpec((cb, LANES), lambda b, h, c: (rows(b, h, c), 0)),
                  cw_blk(0), cw_blk(1), cw_blk(2), row128, row128, row128],
        out_specs=[pl.BlockSpec((cb, hp * LANES), lambda b, h, c: (rows(b, h, c), h)),
                   pl.BlockSpec((1, hp, LANES, LANES), lambda b, h, c: (b, h, 0, 0))],
        out_shape=[jax.ShapeDtypeStruct((m, heads * LANES), BF16),
                   jax.ShapeDtypeStruct((batch, heads, LANES, LANES), F32)],
        scratch_shapes=[pltpu.VMEM((cb + 8, hp * LANES), F32)] * 3 + [pltpu.VMEM((hp, LANES, LANES), F32)],
        compiler_params=_params(("arbitrary", "arbitrary", "arbitrary")),
    )(proj, proj, proj, proj, ba, conv_w, conv_w, conv_w, alog_pad, dtb_pad, norm_w.reshape(1, LANES))


def _gdn_step_kernel(q_ref, k_ref, v_ref, z_ref, ba_ref, cq_ref, ck_ref, cv_ref, cwq_ref, cwk_ref, cwv_ref,
                     alog_ref, dtb_ref, nw_ref, s_ref, o_ref, so_ref, *, heads):
    h = pl.program_id(1)
    tb = q_ref.shape[0]

    def conv(x_ref, c_ref, w_ref):
        w = w_ref[...]
        y = c_ref[0] * w[0:1]
        y = y + c_ref[1] * w[1:2]
        y = y + c_ref[2] * w[2:3]
        return y + x_ref[...] * w[3:4]

    q = _l2n(_silu(conv(q_ref, cq_ref, cwq_ref))) * (LANES ** -0.5)
    k = _l2n(_silu(conv(k_ref, ck_ref, cwk_ref)))
    v = _silu(conv(v_ref, cv_ref, cwv_ref))
    beta, g = _gdn_gates(ba_ref[...], alog_ref[...], dtb_ref[...], h, heads)
    eg = jnp.exp(g)
    outs = []
    group = 4
    for b0 in range(0, tb, group):
        rows = range(b0, min(b0 + group, tb))
        kcols = [_col_from_row(k[b:b + 1]) for b in rows]
        for b, kcol in zip(rows, kcols):
            r = slice(b, b + 1)
            s = s_ref[b, 0] * eg[r]
            v_new = beta[r] * (v[r] - jnp.sum(kcol * s, axis=0, keepdims=True))
            s = s + kcol * v_new
            so_ref[b, 0] = s
            outs.append(_dot(q[r], s))
    o = jnp.concatenate(outs, axis=0)
    o_ref[...] = (_rms(o, nw_ref[...]) * _silu(z_ref[...])).astype(BF16)


def _gdn_step(proj, ba, conv_state_t, conv_w, alog_pad, dtb_pad, norm_w, state, layer, heads, tb):
    bsz = proj.shape[0]
    head_blk = lambda off: pl.BlockSpec((tb, LANES), lambda i, h: (i, off + h))
    cs_blk = lambda off: pl.BlockSpec((CONV_WIDTH - 1, tb, LANES), lambda i, h: (0, i, off + h))
    cw_blk = lambda off: pl.BlockSpec((CONV_WIDTH, LANES), lambda i, h: (0, off + h))
    row128 = pl.BlockSpec((1, LANES), lambda i, h: (0, 0))
    st_in = pl.BlockSpec((None, tb, 1, LANES, LANES), lambda i, h: (layer, i, h, 0, 0))
    st_out = pl.BlockSpec((tb, 1, LANES, LANES), lambda i, h: (i, h, 0, 0))
    return pl.pallas_call(
        functools.partial(_gdn_step_kernel, heads=heads),
        grid=(bsz // tb, heads),
        in_specs=[head_blk(0), head_blk(heads), head_blk(2 * heads), head_blk(3 * heads),
                  pl.BlockSpec((tb, LANES), lambda i, h: (i, 0)),
                  cs_blk(0), cs_blk(heads), cs_blk(2 * heads),
                  cw_blk(0), cw_blk(heads), cw_blk(2 * heads), row128, row128, row128, st_in],
        out_specs=[pl.BlockSpec((tb, LANES), lambda i, h: (i, h)), st_out],
        out_shape=[jax.ShapeDtypeStruct((bsz, heads * LANES), BF16),
                   jax.ShapeDtypeStruct(state.shape[1:], F32)],
        compiler_params=_params(("arbitrary", "arbitrary")),
    )(proj, proj, proj, proj, ba, conv_state_t, conv_state_t, conv_state_t, conv_w, conv_w, conv_w,
      alog_pad, dtb_pad, norm_w.reshape(1, LANES), state)


def _lru_gates(x, wa_ref, wi_ref, ba, bi, lam):
    nb = wa_ref.shape[0]
    ga, gi = [], []
    for s in range(nb):
        xs = x[:, s * LANES:(s + 1) * LANES].astype(BF16)
        ga.append(jnp.dot(xs, wa_ref[s], preferred_element_type=F32))
        gi.append(jnp.dot(xs, wi_ref[s], preferred_element_type=F32))
    gate_a = _sigmoid(jnp.concatenate(ga, axis=-1) + ba)
    gate_i = _sigmoid(jnp.concatenate(gi, axis=-1) + bi)
    log_a = -LRU_C * gate_a * _softplus(-lam)
    return log_a, gate_i


def _lru_prompt_kernel(xl_ref, yl_ref, cw_ref, cb_ref, wa_ref, wi_ref, ba_ref, bi_ref, lam_ref,
                       o_ref, hl_ref, xbuf, abuf, bbuf, h_scr):
    blk = pl.program_id(1)
    cb = xl_ref.shape[0]
    width = xl_ref.shape[1]

    @pl.when(blk == 0)
    def _():
        h_scr[...] = jnp.zeros_like(h_scr)
        xbuf[0:8, :] = jnp.zeros((8, width), F32)

    x = _conv_block(xl_ref, xbuf, cw_ref[...]) + cb_ref[...]
    log_a, gate_i = _lru_gates(x, wa_ref, wi_ref, ba_ref[...], bi_ref[...], lam_ref[...])
    mult = jnp.sqrt(_neg_expm1(2.0 * log_a))
    first = (lax.broadcasted_iota(jnp.int32, (cb, 1), 0) == 0) & (blk == 0)
    mult = jnp.where(first, 1.0, mult)
    abuf[...] = jnp.exp(log_a)
    bbuf[...] = mult * gate_i * x

    def step(t, h):
        h = abuf[pl.ds(t, 1), :] * h + bbuf[pl.ds(t, 1), :]
        bbuf[pl.ds(t, 1), :] = h
        return h

    h_last = lax.fori_loop(0, cb, step, h_scr[...], unroll=8)
    h_scr[...] = h_last
    o_ref[...] = (bbuf[...] * _gelu_tanh(yl_ref[...])).astype(BF16)

    @pl.when(blk == pl.num_programs(1) - 1)
    def _():
        hl_ref[0] = h_last


def _lru_prompt(xy, conv_w, conv_b, w_a, w_i, b_a, b_i, lam, batch, seq, cb):
    m = xy.shape[0]
    width = xy.shape[1] // 2
    nblk = seq // cb
    nb = w_a.shape[0]
    vec = pl.BlockSpec((1, width), lambda b, c: (0, 0))
    wblk = pl.BlockSpec((nb, LANES, LANES), lambda b, c: (0, 0, 0))
    out, h_last = pl.pallas_call(
        _lru_prompt_kernel,
        grid=(batch, nblk),
        in_specs=[pl.BlockSpec((cb, width), lambda b, c: (b * nblk + c, 0)),
                  pl.BlockSpec((cb, width), lambda b, c: (b * nblk + c, 1)),
                  pl.BlockSpec((CONV_WIDTH, width), lambda b, c: (0, 0)),
                  vec, wblk, wblk, vec, vec, vec],
        out_specs=[pl.BlockSpec((cb, width), lambda b, c: (b * nblk + c, 0)),
                   pl.BlockSpec((1, 1, width), lambda b, c: (b, 0, 0))],
        out_shape=[jax.ShapeDtypeStruct((m, width), BF16),
                   jax.ShapeDtypeStruct((batch, 1, width), F32)],
        scratch_shapes=[pltpu.VMEM((cb + 8, width), F32), pltpu.VMEM((cb, width), F32),
                        pltpu.VMEM((cb, width), F32), pltpu.VMEM((1, width), F32)],
        compiler_params=_params(("arbitrary", "arbitrary")),
    )(xy, xy, conv_w, conv_b.reshape(1, width), w_a, w_i, b_a.reshape(1, width), b_i.reshape(1, width),
      lam.reshape(1, width))
    return out, h_last.reshape(batch, width)


def _lru_step_kernel(xl_ref, yl_ref, cs_ref, h0_ref, cw_ref, cb_ref, wa_ref, wi_ref, ba_ref, bi_ref, lam_ref,
                     o_ref, h_ref, *, reset):
    w = cw_ref[...]
    x = cs_ref[0] * w[0:1]
    x = x + cs_ref[1] * w[1:2]
    x = x + cs_ref[2] * w[2:3]
    x = x + xl_ref[...] * w[3:4] + cb_ref[...]
    log_a, gate_i = _lru_gates(x, wa_ref, wi_ref, ba_ref[...], bi_ref[...], lam_ref[...])
    mult = 1.0 if reset else jnp.sqrt(_neg_expm1(2.0 * log_a))
    h = jnp.exp(log_a) * h0_ref[...] + mult * gate_i * x
    h_ref[...] = h
    o_ref[...] = (h * _gelu_tanh(yl_ref[...])).astype(BF16)


def _lru_step(xy, conv_state_t, h0, conv_w, conv_b, w_a, w_i, b_a, b_i, lam, reset):
    bsz = xy.shape[0]
    width = xy.shape[1] // 2
    nb = w_a.shape[0]
    vec = pl.BlockSpec((1, width), lambda i: (0, 0))
    wblk = pl.BlockSpec((nb, LANES, LANES), lambda i: (0, 0, 0))
    full = pl.BlockSpec((bsz, width), lambda i: (0, 0))
    return pl.pallas_call(
        functools.partial(_lru_step_kernel, reset=reset),
        grid=(1,),
        in_specs=[full, pl.BlockSpec((bsz, width), lambda i: (0, 1)),
                  pl.BlockSpec((CONV_WIDTH - 1, bsz, width), lambda i: (0, 0, 0)), full,
                  pl.BlockSpec((CONV_WIDTH, width), lambda i: (0, 0)), vec, wblk, wblk, vec, vec, vec],
        out_specs=[full, full],
        out_shape=[jax.ShapeDtypeStruct((bsz, width), BF16), jax.ShapeDtypeStruct((bsz, width), F32)],
        compiler_params=_params(("arbitrary",)),
    )(xy, xy, conv_state_t, h0, conv_w, conv_b.reshape(1, width), w_a, w_i, b_a.reshape(1, width),
      b_i.reshape(1, width), lam.reshape(1, width))


def _hgrn_lower_bound(lb_ref, layer, j=0):
    depth = lb_ref.shape[0]
    raw = [lb_ref[l, j] for l in range(depth)]
    mx = raw[0]
    for r in raw[1:]:
        mx = jnp.maximum(mx, r)
    ex = [jnp.exp(r - mx) for r in raw]
    tot = ex[0]
    for e in ex[1:]:
        tot = tot + e
    wts = [e / tot for e in ex]
    cum = wts[0]
    for w in wts[1:layer + 1]:
        cum = cum + w
    return cum - wts[0]


def _hgrn_inputs(q_raw, f_raw, lb):
    q = _silu(q_raw) * (LANES ** -0.5)
    f = lb + (1.0 - lb) * _sigmoid(f_raw)
    return q, f


def _level_ref(g, half):
    c = g.shape[0]
    sub = 8
    if 2 * half >= sub:
        g3 = g.reshape(c // (2 * half), 2 * half, LANES)
        return jnp.broadcast_to(g3[:, half - 1:half, :], g3.shape).reshape(c, LANES)
    g3 = g.reshape(c // sub, sub, LANES)
    rin = lax.broadcasted_iota(jnp.int32, g3.shape, 1)
    out = jnp.broadcast_to(g3[:, sub - half - 1:sub - half, :], g3.shape)
    for start in range(sub - 4 * half, -1, -2 * half):
        out = jnp.where(rin < start + 2 * half, jnp.broadcast_to(g3[:, start + half - 1:start + half, :], g3.shape),
                        out)
    return out.reshape(c, LANES)


def _hgrn_level_table(c):
    i = np.arange(c)[:, None]
    j = np.arange(c)[None, :]
    nlev = int(math.log2(c))
    top_bit = sum(((i ^ j) >> b > 0).astype(np.int32) for b in range(1, nlev))
    return jnp.asarray(np.where(j < i, nlev - 1 - top_bit, np.where(j == i, nlev, -1)), jnp.int32)


def _hgrn_prompt_kernel(q_ref, f_ref, i_ref, gz_ref, lb_ref, nw_ref, lvl_ref, o_ref, s_ref, s_scr, *, layer):
    hp = s_scr.shape[0]
    blk = pl.program_id(2)
    cb = q_ref.shape[0]
    c = CHUNK

    @pl.when(blk == 0)
    def _():
        s_scr[...] = jnp.zeros_like(s_scr)

    lbs = [_hgrn_lower_bound(lb_ref, layer, j) for j in range(hp)]
    head_lanes = [slice(j * LANES, (j + 1) * LANES) for j in range(hp)]
    tri = (_iota2(c, 1) <= _iota2(c, 0)).astype(BF16)
    nw = nw_ref[...]
    lvl = lvl_ref[...]
    halves = [c >> (i + 1) for i in range(int(math.log2(c)))]

    items = [(slice(ci * c, (ci + 1) * c), j) for ci in range(cb // c) for j in range(hp)]
    qf = [_hgrn_inputs(q_ref[sl, head_lanes[j]], f_ref[sl, head_lanes[j]], lbs[j]) for sl, j in items]
    qs = [q for q, _ in qf]
    ks = [1.0 - f for _, f in qf]
    vs = [i_ref[sl, head_lanes[j]] for sl, j in items]
    gs = [_dot_exact_lhs(tri, jnp.log(f)) for _, f in qf]

    amats = [_dot_nt(q, k) for q, k in zip(qs, ks)]
    for li, half in enumerate(halves):
        es = [jnp.exp(-jnp.abs(g - _level_ref(g, half))) for g in gs]
        ps = [_dot_nt(q * e, k * e) for q, k, e in zip(qs, ks, es)]
        amats = [jnp.where(lvl == li, p, a) for p, a in zip(ps, amats)]
    intra = [_dot(jnp.where(lvl >= 0, a, 0.0), v) for a, v in zip(amats, vs)]

    g_lasts = [g[c - 1:c, :] for g in gs]
    qdec = [q * jnp.exp(g) for q, g in zip(qs, gs)]
    sdec = [_col_from_row(jnp.exp(gl)) for gl in g_lasts]
    sadd = [_dot_tn(k * jnp.exp(gl - g), v) for k, gl, g, v in zip(ks, g_lasts, gs, vs)]

    states = [s_scr[j] for j in range(hp)]
    outs = []
    for (sl, j), o, qd, dec, add in zip(items, intra, qdec, sdec, sadd):
        outs.append(o + _dot(qd, states[j]))
        states[j] = states[j] * dec + add
    for j in range(hp):
        s_scr[j] = states[j]
    for (sl, j), o in zip(items, outs):
        o_ref[sl, head_lanes[j]] = (_rms(o, nw) * _silu(gz_ref[sl, head_lanes[j]])).astype(BF16)

    @pl.when(blk == pl.num_programs(2) - 1)
    def _():
        s_ref[0] = s_scr[...]


def _hgrn_prompt(proj, lb_raw, norm_w, layer, batch, seq, heads, cb):
    m = proj.shape[0]
    nblk = seq // cb
    depth = lb_raw.shape[0]
    hp = HGRN_HEADS_PER_STEP
    assert heads % hp == 0
    groups = heads // hp
    head_blk = lambda part: pl.BlockSpec((cb, hp * LANES), lambda b, h, c: (b * nblk + c, part * groups + h))
    return pl.pallas_call(
        functools.partial(_hgrn_prompt_kernel, layer=layer),
        grid=(batch, groups, nblk),
        in_specs=[head_blk(0), head_blk(1), head_blk(2), head_blk(3),
                  pl.BlockSpec((depth, hp, 1, LANES), lambda b, h, c: (0, h, 0, 0)),
                  pl.BlockSpec((1, LANES), lambda b, h, c: (0, 0)),
                  pl.BlockSpec((CHUNK, CHUNK), lambda b, h, c: (0, 0))],
        out_specs=[pl.BlockSpec((cb, hp * LANES), lambda b, h, c: (b * nblk + c, h)),
                   pl.BlockSpec((1, hp, LANES, LANES), lambda b, h, c: (b, h, 0, 0))],
        out_shape=[jax.ShapeDtypeStruct((m, heads * LANES), BF16),
                   jax.ShapeDtypeStruct((batch, heads, LANES, LANES), F32)],
        scratch_shapes=[pltpu.VMEM((hp, LANES, LANES), F32)],
        compiler_params=_params(("arbitrary", "arbitrary", "arbitrary")),
    )(proj, proj, proj, proj, lb_raw.reshape(depth, heads, 1, LANES), norm_w.reshape(1, LANES),
      _hgrn_level_table(CHUNK))


def _hgrn_step_kernel(q_ref, f_ref, i_ref, gz_ref, lb_ref, nw_ref, s_ref, o_ref, so_ref, *, layer):
    tb = q_ref.shape[0]
    lb = _hgrn_lower_bound(lb_ref, layer)
    q, f = _hgrn_inputs(q_ref[...], f_ref[...], lb)
    v = i_ref[...]
    outs = []
    group = 4
    for b0 in range(0, tb, group):
        rows = range(b0, min(b0 + group, tb))
        fcols = [_col_from_row(f[b:b + 1]) for b in rows]
        for b, fcol in zip(rows, fcols):
            s = fcol * (s_ref[b, 0] - v[b:b + 1]) + v[b:b + 1]
            so_ref[b, 0] = s
            outs.append(_dot(q[b:b + 1], s))
    o = jnp.concatenate(outs, axis=0)
    o_ref[...] = (_rms(o, nw_ref[...]) * _silu(gz_ref[...])).astype(BF16)


def _hgrn_step(proj, lb_raw, norm_w, state, state_idx, layer, heads, tb):
    bsz = proj.shape[0]
    depth = lb_raw.shape[0]
    head_blk = lambda off: pl.BlockSpec((tb, LANES), lambda i, h: (i, off + h))
    st_in = pl.BlockSpec((None, tb, 1, LANES, LANES), lambda i, h: (state_idx, i, h, 0, 0))
    st_out = pl.BlockSpec((tb, 1, LANES, LANES), lambda i, h: (i, h, 0, 0))
    return pl.pallas_call(
        functools.partial(_hgrn_step_kernel, layer=layer),
        grid=(bsz // tb, heads),
        in_specs=[head_blk(0), head_blk(heads), head_blk(2 * heads), head_blk(3 * heads),
                  pl.BlockSpec((depth, 1, 1, LANES), lambda i, h: (0, h, 0, 0)),
                  pl.BlockSpec((1, LANES), lambda i, h: (0, 0)), st_in],
        out_specs=[pl.BlockSpec((tb, LANES), lambda i, h: (i, h)), st_out],
        out_shape=[jax.ShapeDtypeStruct((bsz, heads * LANES), BF16),
                   jax.ShapeDtypeStruct(state.shape[1:], F32)],
        compiler_params=_params(("arbitrary", "arbitrary")),
    )(proj, proj, proj, proj, lb_raw.reshape(depth, heads, 1, LANES), norm_w.reshape(1, LANES), state)


def _trunk(x, mem_k, mem_v, states, wts, big, batch, seq, prompt):
    depth = wts["norm_mix"].shape[0]
    gdn_w = wts["gdn_conv_w"].shape[-1] // 3
    gdn_heads = gdn_w // LANES
    hgrn_heads = wts["hgrn_lb_raw"].shape[1] // LANES
    lru_w = wts["lru_conv_w"].shape[-1]
    bf = {name: list(handles) for name, handles in big.items()}
    new = {"gdn_conv": [], "gdn": [], "lru_conv": [], "lru": [], "hgrn": []}

    def keep(name, idx, w_used):
        w, li = bf[name][idx]
        bf[name][idx] = (w_used, li if w_used is w else 0)

    for l in range(depth):
        g_mix = wts["norm_mix"][l]
        if l % 2 == 0:
            e = l // 2
            if prompt:
                hn = _rms_cast(x, g_mix, TM)
                proj = _matmul(hn, wts["ab_head"], e, 0, 4 * gdn_w, TM_BF16, TN)
                ba = _matmul(hn, wts["ab_head"], e, 4 * gdn_w, LANES, TM_BF16, LANES)
                xy = _matmul(hn, wts["ab_lru"], e, 0, 2 * lru_w, TM_BF16, TN)
            else:
                proj, _ = _norm_matmul(x, g_mix, wts["ab_head"], e, 0, 4 * gdn_w, TM, TN)
                ba, _ = _norm_matmul(x, g_mix, wts["ab_head"], e, 4 * gdn_w, LANES, TM, LANES)
                xy, _ = _norm_matmul(x, g_mix, wts["ab_lru"], e, 0, 2 * lru_w, TM, TN)
            gargs = (wts["gdn_conv_w"][e], wts["alog_pad"][e], wts["dtb_pad"][e], wts["gdn_norm_w"][e])
            largs = (wts["lru_conv_w"][e], wts["lru_conv_b"][e], wts["lru_w_a"][e], wts["lru_w_i"][e],
                     wts["lru_b_a"][e], wts["lru_b_i"][e], wts["lru_lam"][e])
            w_out, li = bf["ab_w_out"][e]
            if prompt:
                o_a, s_new = _gdn_prompt(proj, ba, *gargs, batch, seq, gdn_heads, 512)
                o_b, h_new = _lru_prompt(xy, *largs, batch, seq, 256)
                tail = lambda t, w: t.reshape(batch, seq, -1)[:, seq - (CONV_WIDTH - 1):, :w]
                new["gdn_conv"].append(tail(proj, 3 * gdn_w))
                new["lru_conv"].append(tail(xy, lru_w))
                x = _matmul2_res(o_a, o_b, w_out, li, x, TM, TN)
            else:
                gc_state, lc_state = states["gdn_conv"][e], states["lru_conv"][e]
                o_a, s_new = _gdn_step(proj, ba, jnp.swapaxes(gc_state, 0, 1), *gargs, states["gdn"], e,
                                       gdn_heads, 128)
                o_b, h_new = _lru_step(xy, jnp.swapaxes(lc_state, 0, 1), states["lru"][e], *largs, reset=False)
                new["gdn_conv"].append(jnp.concatenate([gc_state[:, 1:], proj[:, None, :3 * gdn_w]], axis=1))
                new["lru_conv"].append(jnp.concatenate([lc_state[:, 1:], xy[:, None, :lru_w]], axis=1))
                x, w_used = _matmul_res(jnp.concatenate([o_a, o_b], axis=1), w_out, li, x, TM, TN)
                keep("ab_w_out", e, w_used)
            new["gdn"].append(s_new)
            new["lru"].append(h_new)
        else:
            o_idx = l // 2
            w_in, li = bf["c_w_in"][o_idx]
            proj, w_used = _norm_matmul(x, g_mix, w_in, li, 0, w_in.shape[2], TM, TN)
            keep("c_w_in", o_idx, w_used)
            hargs = (wts["hgrn_lb_raw"], wts["hgrn_norm_w"][o_idx])
            if prompt:
                o_c, s_new = _hgrn_prompt(proj, *hargs, l, batch, seq, hgrn_heads, 512)
            else:
                o_c, s_new = _hgrn_step(proj, *hargs, states["hgrn"], o_idx, l, hgrn_heads, 128)
            new["hgrn"].append(s_new)
            w_out, li = bf["c_w_out"][o_idx]
            x, w_used = _matmul_res(o_c, w_out, li, x, TM, TN)
            keep("c_w_out", o_idx, w_used)
        (w_q, lq), (w_o, lo) = bf["mem_w_q"][l], bf["mem_w_o"][l]
        if prompt:
            assert lq == lo
            x = _mem_attn_prompt(x, wts["norm_mem"][l], w_q, mem_k, mem_v, w_o, l, lq, seq, 512)
        else:
            q, w_used = _norm_matmul(x, wts["norm_mem"][l], w_q, lq, 0, w_q.shape[2], TM, TN)
            keep("mem_w_q", l, w_used)
            o = _mem_attn_sample(q, mem_k, mem_v, l, 8)
            x, w_used = _matmul_res(o.reshape(batch, -1).astype(BF16), w_o, lo, x, TM, TN)
            keep("mem_w_o", l, w_used)
        (w_up, lu), (w_dn, ld) = bf["ffn_w_up"][l], bf["ffn_w_down"][l]
        assert lu == ld
        x, up_used, dn_used = _ffn(x, wts["norm_ffn"][l], w_up, w_dn, lu, wts["norm_final"], l == depth - 1,
                                   TM, TF)
        keep("ffn_w_up", l, up_used)
        keep("ffn_w_down", l, dn_used)
    return x, {n: jnp.stack(v) for n, v in new.items()}, bf


def kernel(x_prompt, x_sample, cache_mem_k, cache_mem_v, state_gdn_conv, state_gdn, state_lru_conv, state_lru, state_hgrn, mem_prompt, norm_mix, norm_mem, norm_mem_kv, norm_ffn, norm_final, ab_w_in, ab_w_out, gdn_conv_w, gdn_a_log, gdn_dt_bias, gdn_norm_w, lru_conv_w, lru_conv_b, lru_w_a, lru_b_a, lru_w_i, lru_b_i, lru_lam, c_w_in, c_w_out, hgrn_lb_raw, hgrn_norm_w, mem_w_q, mem_w_k, mem_w_v, mem_w_o, ffn_w_up, ffn_w_down):
    bp, lp, d = x_prompt.shape
    bs, ls, _ = x_sample.shape
    assert ls == 1, "the sample group advances one token per call"
    depth = norm_mix.shape[0]
    gdn_heads = gdn_a_log.shape[1]
    gdn_w = gdn_heads * LANES
    mem_tokens, mem_heads, mem_hd = cache_mem_k.shape[2:]
    mem_w = mem_heads * mem_hd
    n_ba = 2 * gdn_heads
    assert n_ba <= LANES

    pad_ba = lambda a: jnp.pad(a, ((0, 0), (gdn_heads, LANES - n_ba)))[:, None, :]
    ab_bf = ab_w_in.astype(BF16)
    wts = dict(
        norm_mix=norm_mix, norm_mem=norm_mem, norm_ffn=norm_ffn, norm_final=norm_final,
        ab_head=ab_bf, ab_lru=ab_bf[:, :, 4 * gdn_w + n_ba:],
        gdn_conv_w=gdn_conv_w, alog_pad=pad_ba(gdn_a_log), dtb_pad=pad_ba(gdn_dt_bias), gdn_norm_w=gdn_norm_w,
        lru_conv_w=lru_conv_w, lru_conv_b=lru_conv_b, lru_w_a=lru_w_a.astype(BF16), lru_w_i=lru_w_i.astype(BF16),
        lru_b_a=lru_b_a, lru_b_i=lru_b_i, lru_lam=lru_lam,
        hgrn_lb_raw=hgrn_lb_raw, hgrn_norm_w=hgrn_norm_w,
    )
    big = dict(ab_w_out=ab_w_out, c_w_in=c_w_in, c_w_out=c_w_out, mem_w_q=mem_w_q, mem_w_o=mem_w_o,
               ffn_w_up=ffn_w_up, ffn_w_down=ffn_w_down)
    big = {name: [(w, l) for l in range(w.shape[0])] for name, w in big.items()}

    states = dict(gdn_conv=state_gdn_conv, gdn=state_gdn, lru_conv=state_lru_conv, lru=state_lru,
                  hgrn=state_hgrn)
    y_s, new_s, big_bf = _trunk(x_sample.reshape(bs * ls, d), cache_mem_k, cache_mem_v, states, wts, big,
                                bs, ls, False)

    mem_rows = mem_prompt.reshape(bp * mem_tokens, d)
    w_kv = jnp.concatenate([mem_w_k, mem_w_v], axis=-1).astype(BF16)
    kv = jnp.stack([_norm_matmul(mem_rows, norm_mem_kv[l], w_kv, l, 0, 2 * mem_w, TM, TN)[0] for l in range(depth)])
    p_mem_k = kv[:, :, :mem_w].reshape(depth, bp, mem_tokens, mem_w)
    p_mem_v = kv[:, :, mem_w:].reshape(depth, bp, mem_tokens, mem_w)

    y_p, new_p, _ = _trunk(x_prompt.reshape(bp * lp, d), p_mem_k, p_mem_v, None, wts, big_bf, bp, lp, True)

    order = ("gdn_conv", "gdn", "lru_conv", "lru", "hgrn")
    mem5 = lambda t: t.reshape(depth, bp, mem_tokens, mem_heads, mem_hd)
    return (y_p.reshape(bp, lp, d), y_s.reshape(bs, ls, d), mem5(p_mem_k), mem5(p_mem_v),
            *(new_p[n] for n in order), *(new_s[n] for n in order))
```

```python
import functools
import math

import jax
import jax.numpy as jnp
import numpy as np
from jax import lax
from jax.experimental import pallas as pl
from jax.experimental.pallas import tpu as pltpu

F32 = jnp.float32
BF16 = jnp.bfloat16
EPS = 1e-6
LANES = 128
CONV_WIDTH = 4
LRU_C = 8.0
VMEM_LIMIT_BYTES = 56 * 1024 * 1024
TM = 512
TM_BF16 = 1024
TN = 1024
TF = 512
TN_CAST = 512
TF_CAST = 256
ROW_GROUPS = 4
CHUNK = 128
GDN_HEADS_PER_STEP = 4
HGRN_HEADS_PER_STEP = 8


def _params(sem):
    return pltpu.CompilerParams(dimension_semantics=sem, vmem_limit_bytes=VMEM_LIMIT_BYTES)


def _sigmoid(x):
    return jax.nn.sigmoid(x)


def _silu(x):
    return x * _sigmoid(x)


def _softplus(x):
    return jnp.maximum(x, 0.0) + jnp.log1p(jnp.exp(-jnp.abs(x)))


def _neg_expm1(x):
    t = jnp.tanh(0.5 * x)
    return -2.0 * t / (1.0 - t)


def _gelu_tanh(x):
    c = math.sqrt(2.0 / math.pi)
    return 0.5 * x * (1.0 + jnp.tanh(c * (x + 0.044715 * (x * x * x))))


def _rms(x, w):
    return x * lax.rsqrt(jnp.mean(x * x, axis=-1, keepdims=True) + EPS) * w


def _l2n(x):
    return x * lax.rsqrt(jnp.sum(x * x, axis=-1, keepdims=True) + EPS)


def _dot(a, b):
    return jnp.dot(a.astype(BF16), b.astype(BF16), preferred_element_type=F32)


def _dot_nt(a, b):
    return lax.dot_general(a.astype(BF16), b.astype(BF16), (((1,), (1,)), ((), ())),
                           preferred_element_type=F32)


def _dot_tn(a, b):
    return lax.dot_general(a.astype(BF16), b.astype(BF16), (((0,), (0,)), ((), ())),
                           preferred_element_type=F32)


def _split2(a):
    hi = a.astype(BF16)
    lo = (a - hi.astype(F32)).astype(BF16)
    return hi, lo


def _dot_hp(a_parts, b_parts):
    ah, al = a_parts
    bh, bl = b_parts
    return jnp.dot(jnp.concatenate([ah, ah, al], axis=1), jnp.concatenate([bh, bl, bh], axis=0),
                   preferred_element_type=F32)


def _dot_exact_lhs(t, b):
    b1 = b.astype(BF16)
    r1 = b - b1.astype(F32)
    b2 = r1.astype(BF16)
    b3 = (r1 - b2.astype(F32)).astype(BF16)
    return jnp.dot(jnp.concatenate([t, t, t], axis=1), jnp.concatenate([b1, b2, b3], axis=0),
                   preferred_element_type=F32)


def _iota2(n, axis):
    return lax.broadcasted_iota(jnp.int32, (n, n), axis)


def _col_from_row(row):
    n = row.shape[-1]
    return jnp.broadcast_to(row, (n, n)).T


def _row_groups(rows):
    n = ROW_GROUPS if rows % (8 * ROW_GROUPS) == 0 else 1
    return [slice(r * rows // n, (r + 1) * rows // n) for r in range(n)]


def _norm_matmul_kernel(x_ref, g_ref, w_ref, o_ref, *rest):
    xn_ref = rest[-1]
    if len(rest) == 2:
        rest[0][...] = w_ref[...].astype(BF16)
        w_ref = rest[0]
    first = pl.program_id(1) == 0

    @pl.when(first)
    def _():
        for rows in _row_groups(x_ref.shape[0]):
            xn = _rms(x_ref[rows, :], g_ref[...]).astype(BF16)
            xn_ref[rows, :] = xn
            o_ref[rows, :] = jnp.dot(xn, w_ref[...], preferred_element_type=F32)

    @pl.when(jnp.logical_not(first))
    def _():
        o_ref[...] = jnp.dot(xn_ref[...], w_ref[...], preferred_element_type=F32)


def _norm_matmul(x, g, w, layer, n0, n, tm, tn):
    m, k = x.shape
    cast = w.dtype != BF16
    tm = min(tm, m)
    tn = min(tn, n, TN_CAST) if cast else min(tn, n)
    j0 = n0 // tn
    out_specs = [pl.BlockSpec((tm, tn), lambda i, j: (i, j))]
    out_shape = [jax.ShapeDtypeStruct((m, n), F32)]
    if cast:
        assert m == tm and n0 == 0 and n == w.shape[2], "each weight tile must be visited exactly once"
        out_specs.append(pl.BlockSpec((None, k, tn), lambda i, j: (0, 0, j)))
        out_shape.append(jax.ShapeDtypeStruct((1, k, n), BF16))
    res = pl.pallas_call(
        _norm_matmul_kernel,
        grid=(m // tm, n // tn),
        in_specs=[pl.BlockSpec((tm, k), lambda i, j: (i, 0)),
                  pl.BlockSpec((1, k), lambda i, j: (0, 0)),
                  pl.BlockSpec((None, k, tn), lambda i, j: (layer, 0, j0 + j))],
        out_specs=out_specs,
        out_shape=out_shape,
        scratch_shapes=[pltpu.VMEM((tm, k), BF16)],
        compiler_params=_params(("arbitrary", "arbitrary")),
    )(x, g.reshape(1, k), w)
    return (res[0], res[1]) if cast else (res[0], w)


def _rms_cast_kernel(x_ref, g_ref, o_ref):
    o_ref[...] = _rms(x_ref[...], g_ref[...]).astype(BF16)


def _rms_cast(x, g, tm):
    m, k = x.shape
    tm = min(tm, m)
    return pl.pallas_call(
        _rms_cast_kernel,
        grid=(m // tm,),
        in_specs=[pl.BlockSpec((tm, k), lambda i: (i, 0)), pl.BlockSpec((1, k), lambda i: (0, 0))],
        out_specs=pl.BlockSpec((tm, k), lambda i: (i, 0)),
        out_shape=jax.ShapeDtypeStruct((m, k), BF16),
        compiler_params=_params(("arbitrary",)),
    )(x, g.reshape(1, k))


def _matmul_kernel(a_ref, w_ref, o_ref):
    o_ref[...] = jnp.dot(a_ref[...], w_ref[...], preferred_element_type=F32)


def _matmul(a, w, layer, n0, n, tm, tn):
    m, k = a.shape
    tm = min(tm, m)
    tn = min(tn, n)
    j0 = n0 // tn
    return pl.pallas_call(
        _matmul_kernel,
        grid=(m // tm, n // tn),
        in_specs=[pl.BlockSpec((tm, k), lambda i, j: (i, 0)),
                  pl.BlockSpec((None, k, tn), lambda i, j: (layer, 0, j0 + j))],
        out_specs=pl.BlockSpec((tm, tn), lambda i, j: (i, j)),
        out_shape=jax.ShapeDtypeStruct((m, n), F32),
        compiler_params=_params(("arbitrary", "arbitrary")),
    )(a, w)


def _matmul2_res_kernel(a1_ref, a2_ref, w1_ref, w2_ref, r_ref, o_ref):
    acc = jnp.dot(a1_ref[...], w1_ref[...], preferred_element_type=F32)
    acc = acc + jnp.dot(a2_ref[...], w2_ref[...], preferred_element_type=F32)
    o_ref[...] = r_ref[...] + acc


def _matmul2_res(a1, a2, w, layer, res, tm, tn):
    m, kh = a1.shape
    n = w.shape[2]
    tm = min(tm, m)
    tn = min(tn, n)
    return pl.pallas_call(
        _matmul2_res_kernel,
        grid=(m // tm, n // tn),
        in_specs=[pl.BlockSpec((tm, kh), lambda i, j: (i, 0), pipeline_mode=pl.Buffered(1)),
                  pl.BlockSpec((tm, kh), lambda i, j: (i, 0), pipeline_mode=pl.Buffered(1)),
                  pl.BlockSpec((None, kh, tn), lambda i, j: (layer, 0, j)),
                  pl.BlockSpec((None, kh, tn), lambda i, j: (layer, 1, j)),
                  pl.BlockSpec((tm, tn), lambda i, j: (i, j))],
        out_specs=pl.BlockSpec((tm, tn), lambda i, j: (i, j)),
        out_shape=jax.ShapeDtypeStruct((m, n), F32),
        compiler_params=_params(("arbitrary", "arbitrary")),
    )(a1, a2, w, w, res)


def _matmul_res_kernel(a_ref, w_ref, r_ref, o_ref, *wb_ref):
    w = w_ref[...]
    if wb_ref:
        w = w.astype(BF16)
        wb_ref[0][...] = w
    o_ref[...] = r_ref[...] + jnp.dot(a_ref[...], w, preferred_element_type=F32)


def _matmul_res(a, w, layer, res, tm, tn):
    m, k = a.shape
    n = w.shape[2]
    cast = w.dtype != BF16
    tm = min(tm, m)
    tn = min(tn, n, TN_CAST) if cast else min(tn, n)
    out_specs = [pl.BlockSpec((tm, tn), lambda i, j: (i, j))]
    out_shape = [jax.ShapeDtypeStruct((m, n), F32)]
    if cast:
        assert m == tm, "each weight tile must be visited exactly once"
        out_specs.append(pl.BlockSpec((None, k, tn), lambda i, j: (0, 0, j)))
        out_shape.append(jax.ShapeDtypeStruct((1, k, n), BF16))
    out = pl.pallas_call(
        _matmul_res_kernel,
        grid=(m // tm, n // tn),
        in_specs=[pl.BlockSpec((tm, k), lambda i, j: (i, 0), pipeline_mode=pl.Buffered(1)),
                  pl.BlockSpec((None, k, tn), lambda i, j: (layer, 0, j)),
                  pl.BlockSpec((tm, tn), lambda i, j: (i, j))],
        out_specs=out_specs,
        out_shape=out_shape,
        compiler_params=_params(("arbitrary", "arbitrary")),
    )(a, w, res)
    return (out[0], out[1]) if cast else (out[0], w)


def _ffn_kernel(x_ref, g_ref, wu_ref, wd_ref, gf_ref, o_ref, *rest, final_norm):
    xn_ref = rest[-1]
    f = pl.program_id(1)
    if len(rest) == 3:
        rest[0][...] = wu_ref[...].astype(BF16)
        rest[1][...] = wd_ref[...].astype(BF16)
        wu_ref, wd_ref = rest[0], rest[1]

    def mlp(xn):
        h = jnp.dot(xn, wu_ref[...], preferred_element_type=F32)
        h = jnp.square(jnp.maximum(h, 0.0)).astype(BF16)
        return jnp.dot(h, wd_ref[...], preferred_element_type=F32)

    @pl.when(f == 0)
    def _():
        for rows in _row_groups(x_ref.shape[0]):
            x = x_ref[rows, :]
            xn = _rms(x, g_ref[...]).astype(BF16)
            xn_ref[rows, :] = xn
            o_ref[rows, :] = x + mlp(xn)

    @pl.when(f != 0)
    def _():
        o_ref[...] += mlp(xn_ref[...])

    if final_norm:
        @pl.when(f == pl.num_programs(1) - 1)
        def _():
            o_ref[...] = _rms(o_ref[...], gf_ref[...])


def _ffn(x, g, w_up, w_down, layer, g_final, final_norm, tm, tf):
    m, d = x.shape
    dff = w_up.shape[2]
    cast = w_up.dtype != BF16
    tm = min(tm, m)
    tf = min(tf, TF_CAST) if cast else tf
    out_specs = [pl.BlockSpec((tm, d), lambda i, f: (i, 0))]
    out_shape = [jax.ShapeDtypeStruct((m, d), F32)]
    if cast:
        assert m == tm, "each weight tile must be visited exactly once"
        out_specs += [pl.BlockSpec((None, d, tf), lambda i, f: (0, 0, f)),
                      pl.BlockSpec((None, tf, d), lambda i, f: (0, f, 0))]
        out_shape += [jax.ShapeDtypeStruct((1, d, dff), BF16), jax.ShapeDtypeStruct((1, dff, d), BF16)]
    out = pl.pallas_call(
        functools.partial(_ffn_kernel, final_norm=final_norm),
        grid=(m // tm, dff // tf),
        in_specs=[pl.BlockSpec((tm, d), lambda i, f: (i, 0), pipeline_mode=pl.Buffered(1)),
                  pl.BlockSpec((1, d), lambda i, f: (0, 0)),
                  pl.BlockSpec((None, d, tf), lambda i, f: (layer, 0, f)),
                  pl.BlockSpec((None, tf, d), lambda i, f: (layer, f, 0)),
                  pl.BlockSpec((1, d), lambda i, f: (0, 0))],
        out_specs=out_specs,
        out_shape=out_shape,
        scratch_shapes=[pltpu.VMEM((tm, d), BF16)],
        compiler_params=_params(("arbitrary", "arbitrary")),
    )(x, g.reshape(1, d), w_up, w_down, g_final.reshape(1, d))
    return (out[0], out[1], out[2]) if cast else (out[0], w_up, w_down)


def _mem_attn_prompt_kernel(x_ref, g_ref, wq_ref, k_ref, v_ref, wo_ref, o_ref, *, heads):
    x = x_ref[...]
    xn = _rms(x, g_ref[...]).astype(BF16)
    q = jnp.dot(xn, wq_ref[...], preferred_element_type=F32)
    k = k_ref[0].astype(BF16)
    v = v_ref[0].astype(BF16)
    scale = LANES ** -0.5
    outs = []
    for h in range(heads):
        sl = slice(h * LANES, (h + 1) * LANES)
        s = _dot_nt(q[:, sl], k[:, sl]) * scale
        e = jnp.exp(s - jnp.max(s, axis=-1, keepdims=True))
        p = e / jnp.sum(e, axis=-1, keepdims=True)
        outs.append(_dot(p, v[:, sl]))
    o = jnp.concatenate(outs, axis=-1).astype(BF16)
    o_ref[...] = x + jnp.dot(o, wo_ref[...], preferred_element_type=F32)


def _mem_attn_prompt(x, g, w_q, mem_k, mem_v, w_o, layer, w_layer, seq, tl):
    m, d = x.shape
    _, _, t, w = mem_k.shape
    nblk = seq // tl
    return pl.pallas_call(
        functools.partial(_mem_attn_prompt_kernel, heads=w // LANES),
        grid=(m // tl,),
        in_specs=[pl.BlockSpec((tl, d), lambda i: (i, 0)),
                  pl.BlockSpec((1, d), lambda i: (0, 0)),
                  pl.BlockSpec((None, d, w), lambda i: (w_layer, 0, 0), pipeline_mode=pl.Buffered(1)),
                  pl.BlockSpec((None, 1, t, w), lambda i: (layer, i // nblk, 0, 0)),
                  pl.BlockSpec((None, 1, t, w), lambda i: (layer, i // nblk, 0, 0)),
                  pl.BlockSpec((None, w, d), lambda i: (w_layer, 0, 0), pipeline_mode=pl.Buffered(1))],
        out_specs=pl.BlockSpec((tl, d), lambda i: (i, 0)),
        out_shape=jax.ShapeDtypeStruct((m, d), F32),
        compiler_params=_params(("arbitrary",)),
    )(x, g.reshape(1, d), w_q, mem_k, mem_v, w_o)


def _mem_attn_sample_kernel(q_ref, k_ref, v_ref, o_ref, *, tb, heads):
    scale = LANES ** -0.5
    for b in range(tb):
        q = jnp.concatenate([q_ref[b:b + 1, h * LANES:(h + 1) * LANES] for h in range(heads)], axis=0)
        s = jnp.sum(k_ref[b] * (q * scale)[None], axis=-1, keepdims=True)
        e = jnp.exp(s - jnp.max(s, axis=0, keepdims=True))
        o_ref[b] = jnp.sum(e * v_ref[b], axis=0) / jnp.sum(e, axis=0)


def _mem_attn_sample(q, mem_k, mem_v, layer, tb):
    b = q.shape[0]
    _, _, t, heads, hd = mem_k.shape
    kv_blk = pl.BlockSpec((None, tb, t, heads, hd), lambda i: (layer, i, 0, 0, 0))
    return pl.pallas_call(
        functools.partial(_mem_attn_sample_kernel, tb=tb, heads=heads),
        grid=(b // tb,),
        in_specs=[pl.BlockSpec((tb, heads * hd), lambda i: (i, 0)), kv_blk, kv_blk],
        out_specs=pl.BlockSpec((tb, heads, hd), lambda i: (i, 0, 0)),
        out_shape=jax.ShapeDtypeStruct((b, heads, hd), F32),
        compiler_params=_params(("arbitrary",)),
    )(q, mem_k, mem_v)


def _conv_block(x_ref, buf, w):
    cb = x_ref.shape[0]
    buf[8:8 + cb, :] = x_ref[...]
    y = buf[5:5 + cb, :] * w[0:1]
    y = y + buf[6:6 + cb, :] * w[1:2]
    y = y + buf[7:7 + cb, :] * w[2:3]
    y = y + buf[8:8 + cb, :] * w[3:4]
    buf[0:8, :] = buf[cb:cb + 8, :]
    return y


def _gdn_gates(ba, alog, dtb, h, heads):
    lane = lax.broadcasted_iota(jnp.int32, ba.shape, 1)
    beta_all = _sigmoid(ba)
    g_all = -jnp.exp(alog) * _softplus(ba + dtb)
    beta = jnp.sum(jnp.where(lane == h, beta_all, 0.0), axis=1, keepdims=True)
    g = jnp.sum(jnp.where(lane == heads + h, g_all, 0.0), axis=1, keepdims=True)
    return beta, g


def _gdn_prompt_kernel(q_ref, k_ref, v_ref, z_ref, ba_ref, cwq_ref, cwk_ref, cwv_ref, alog_ref, dtb_ref,
                       nw_ref, o_ref, s_ref, qbuf, kbuf, vbuf, s_scr, *, heads):
    hp = s_scr.shape[0]
    blk = pl.program_id(2)
    cb = q_ref.shape[0]
    c = CHUNK
    n_chunks = cb // c

    @pl.when(blk == 0)
    def _():
        s_scr[...] = jnp.zeros_like(s_scr)
        for buf in (qbuf, kbuf, vbuf):
            buf[0:8, :] = jnp.zeros((8, hp * LANES), F32)

    for x_ref, buf in ((q_ref, qbuf), (k_ref, kbuf), (v_ref, vbuf)):
        buf[8:8 + cb, :] = x_ref[...]
    conv_w = (cwq_ref[...], cwk_ref[...], cwv_ref[...])
    ba = ba_ref[...]
    gates = [_gdn_gates(ba, alog_ref[...], dtb_ref[...], pl.program_id(1) * hp + j, heads) for j in range(hp)]
    head_lanes = [slice(j * LANES, (j + 1) * LANES) for j in range(hp)]

    row = _iota2(c, 0)
    col = _iota2(c, 1)
    causal = col <= row
    strict = col < row
    tri = causal.astype(BF16)
    eye = (col == row).astype(F32)
    nw = nw_ref[...]

    def conv_rows(buf, w, r0):
        y = buf[r0 + 5:r0 + 5 + c, :] * w[0:1]
        y = y + buf[r0 + 6:r0 + 6 + c, :] * w[1:2]
        y = y + buf[r0 + 7:r0 + 7 + c, :] * w[2:3]
        return y + buf[r0 + 8:r0 + 8 + c, :] * w[3:4]

    conv = [[_silu(conv_rows(buf, w, ci * c)) for buf, w in zip((qbuf, kbuf, vbuf), conv_w)]
            for ci in range(n_chunks)]
    items = [(ci, j) for ci in range(n_chunks) for j in range(hp)]
    rows = lambda ci: slice(ci * c, (ci + 1) * c)
    qs = [_l2n(conv[ci][0][:, head_lanes[j]]) * (LANES ** -0.5) for ci, j in items]
    ks = [_l2n(conv[ci][1][:, head_lanes[j]]) for ci, j in items]
    vs = [conv[ci][2][:, head_lanes[j]] for ci, j in items]
    betas = [gates[j][0][rows(ci)] for ci, j in items]
    gcs = [_dot_exact_lhs(tri, jnp.broadcast_to(gates[j][1][rows(ci)], (c, c))) for ci, j in items]
    decays = [jnp.exp(jnp.where(causal, gc - gc.T, -jnp.inf)) for gc in gcs]
    kbs = [k * b for k, b in zip(ks, betas)]
    ms = [jnp.where(strict, _dot_nt(kb, k) * dec, 0.0) for kb, k, dec in zip(kbs, ks, decays)]
    xs = [_split2(-m) for m in ms]
    ts = [eye - m for m in ms]
    for _ in range(int(math.log2(c)) - 1):
        xs = [_split2(_dot_hp(xp, xp)) for xp in xs]
        ts = [t + _dot_hp(_split2(t), xp) for t, xp in zip(ts, xs)]
    egs = [jnp.exp(gc) for gc in gcs]
    uws = [_dot(t, jnp.concatenate([v * b, kb * eg], axis=1))
           for t, v, b, kb, eg in zip(ts, vs, betas, kbs, egs)]
    attns = [jnp.where(causal, _dot_nt(q, k) * dec, 0.0) for q, k, dec in zip(qs, ks, decays)]
    g_lasts = [gc[c - 1:c, :] for gc in gcs]
    kd_uws = [_dot_tn(k * jnp.exp(gl - gc), uw) for k, gl, gc, uw in zip(ks, g_lasts, gcs, uws)]
    at_uws = [_dot(attn, uw) for attn, uw in zip(attns, uws)]
    lhs = [jnp.concatenate([q * eg - at[:, LANES:], eye * jnp.exp(gl) - kd[:, LANES:]], axis=0)
           for q, eg, at, gl, kd in zip(qs, egs, at_uws, g_lasts, kd_uws)]

    states = [s_scr[j] for j in range(hp)]
    outs = []
    for it, (ci, j) in enumerate(items):
        both = _dot(lhs[it], states[j])
        outs.append(at_uws[it][:, :LANES] + both[:c])
        states[j] = both[c:] + kd_uws[it][:, :LANES]
    for j in range(hp):
        s_scr[j] = states[j]
    for buf in (qbuf, kbuf, vbuf):
        buf[0:8, :] = buf[cb:cb + 8, :]
    for (ci, j), o in zip(items, outs):
        o_ref[rows(ci), head_lanes[j]] = (_rms(o, nw) * _silu(z_ref[rows(ci), head_lanes[j]])).astype(BF16)

    @pl.when(blk == pl.num_programs(2) - 1)
    def _():
        s_ref[0] = s_scr[...]


def _gdn_prompt(proj, ba, conv_w, alog_pad, dtb_pad, norm_w, batch, seq, heads, cb):
    m = proj.shape[0]
    nblk = seq // cb
    hp = GDN_HEADS_PER_STEP
    assert heads % hp == 0
    groups = heads // hp
    rows = lambda b, h, c: b * nblk + c
    head_blk = lambda part: pl.BlockSpec((cb, hp * LANES), lambda b, h, c: (rows(b, h, c), part * groups + h))
    cw_blk = lambda part: pl.BlockSpec((CONV_WIDTH, hp * LANES), lambda b, h, c: (0, part * groups + h))
    row128 = pl.BlockSpec((1, LANES), lambda b, h, c: (0, 0))
    return pl.pallas_call(
        functools.partial(_gdn_prompt_kernel, heads=heads),
        grid=(batch, groups, nblk),
        in_specs=[head_blk(0), head_blk(1), head_blk(2), head_blk(3),
                  pl.BlockSpec((cb, LANES), lambda b, h, c: (rows(b, h, c), 0)),
                  cw_blk(0), cw_blk(1), cw_blk(2), row128, row128, row128],
        out_specs=[pl.BlockSpec((cb, hp * LANES), lambda b, h, c: (rows(b, h, c), h)),
                   pl.BlockSpec((1, hp, LANES, LANES), lambda b, h, c: (b, h, 0, 0))],
        out_shape=[jax.ShapeDtypeStruct((m, heads * LANES), BF16),
                   jax.ShapeDtypeStruct((batch, heads, LANES, LANES), F32)],
        scratch_shapes=[pltpu.VMEM((cb + 8, hp * LANES), F32)] * 3 + [pltpu.VMEM((hp, LANES, LANES), F32)],
        compiler_params=_params(("arbitrary", "arbitrary", "arbitrary")),
    )(proj, proj, proj, proj, ba, conv_w, conv_w, conv_w, alog_pad, dtb_pad, norm_w.reshape(1, LANES))


def _gdn_step_kernel(q_ref, k_ref, v_ref, z_ref, ba_ref, cq_ref, ck_ref, cv_ref, cwq_ref, cwk_ref, cwv_ref,
                     alog_ref, dtb_ref, nw_ref, s_ref, o_ref, so_ref, *, heads):
    h = pl.program_id(1)
    tb = q_ref.shape[0]

    def conv(x_ref, c_ref, w_ref):
        w = w_ref[...]
        y = c_ref[0] * w[0:1]
        y = y + c_ref[1] * w[1:2]
        y = y + c_ref[2] * w[2:3]
        return y + x_ref[...] * w[3:4]

    q = _l2n(_silu(conv(q_ref, cq_ref, cwq_ref))) * (LANES ** -0.5)
    k = _l2n(_silu(conv(k_ref, ck_ref, cwk_ref)))
    v = _silu(conv(v_ref, cv_ref, cwv_ref))
    beta, g = _gdn_gates(ba_ref[...], alog_ref[...], dtb_ref[...], h, heads)
    eg = jnp.exp(g)
    outs = []
    group = 4
    for b0 in range(0, tb, group):
        rows = range(b0, min(b0 + group, tb))
        kcols = [_col_from_row(k[b:b + 1]) for b in rows]
        for b, kcol in zip(rows, kcols):
            r = slice(b, b + 1)
            s = s_ref[b, 0] * eg[r]
            v_new = beta[r] * (v[r] - jnp.sum(kcol * s, axis=0, keepdims=True))
            s = s + kcol * v_new
            so_ref[b, 0] = s
            outs.append(_dot(q[r], s))
    o = jnp.concatenate(outs, axis=0)
    o_ref[...] = (_rms(o, nw_ref[...]) * _silu(z_ref[...])).astype(BF16)


def _gdn_step(proj, ba, conv_state_t, conv_w, alog_pad, dtb_pad, norm_w, state, layer, heads, tb):
    bsz = proj.shape[0]
    head_blk = lambda off: pl.BlockSpec((tb, LANES), lambda i, h: (i, off + h))
    cs_blk = lambda off: pl.BlockSpec((CONV_WIDTH - 1, tb, LANES), lambda i, h: (0, i, off + h))
    cw_blk = lambda off: pl.BlockSpec((CONV_WIDTH, LANES), lambda i, h: (0, off + h))
    row128 = pl.BlockSpec((1, LANES), lambda i, h: (0, 0))
    st_in = pl.BlockSpec((None, tb, 1, LANES, LANES), lambda i, h: (layer, i, h, 0, 0))
    st_out = pl.BlockSpec((tb, 1, LANES, LANES), lambda i, h: (i, h, 0, 0))
    return pl.pallas_call(
        functools.partial(_gdn_step_kernel, heads=heads),
        grid=(bsz // tb, heads),
        in_specs=[head_blk(0), head_blk(heads), head_blk(2 * heads), head_blk(3 * heads),
                  pl.BlockSpec((tb, LANES), lambda i, h: (i, 0)),
                  cs_blk(0), cs_blk(heads), cs_blk(2 * heads),
                  cw_blk(0), cw_blk(heads), cw_blk(2 * heads), row128, row128, row128, st_in],
        out_specs=[pl.BlockSpec((tb, LANES), lambda i, h: (i, h)), st_out],
        out_shape=[jax.ShapeDtypeStruct((bsz, heads * LANES), BF16),
                   jax.ShapeDtypeStruct(state.shape[1:], F32)],
        compiler_params=_params(("arbitrary", "arbitrary")),
    )(proj, proj, proj, proj, ba, conv_state_t, conv_state_t, conv_state_t, conv_w, conv_w, conv_w,
      alog_pad, dtb_pad, norm_w.reshape(1, LANES), state)


def _lru_gates(x, wa_ref, wi_ref, ba, bi, lam):
    nb = wa_ref.shape[0]
    ga, gi = [], []
    for s in range(nb):
        xs = x[:, s * LANES:(s + 1) * LANES].astype(BF16)
        ga.append(jnp.dot(xs, wa_ref[s], preferred_element_type=F32))
        gi.append(jnp.dot(xs, wi_ref[s], preferred_element_type=F32))
    gate_a = _sigmoid(jnp.concatenate(ga, axis=-1) + ba)
    gate_i = _sigmoid(jnp.concatenate(gi, axis=-1) + bi)
    log_a = -LRU_C * gate_a * _softplus(-lam)
    return log_a, gate_i


def _lru_prompt_kernel(xl_ref, yl_ref, cw_ref, cb_ref, wa_ref, wi_ref, ba_ref, bi_ref, lam_ref,
                       o_ref, hl_ref, xbuf, abuf, bbuf, h_scr):
    blk = pl.program_id(1)
    cb = xl_ref.shape[0]
    width = xl_ref.shape[1]

    @pl.when(blk == 0)
    def _():
        h_scr[...] = jnp.zeros_like(h_scr)
        xbuf[0:8, :] = jnp.zeros((8, width), F32)

    x = _conv_block(xl_ref, xbuf, cw_ref[...]) + cb_ref[...]
    log_a, gate_i = _lru_gates(x, wa_ref, wi_ref, ba_ref[...], bi_ref[...], lam_ref[...])
    mult = jnp.sqrt(_neg_expm1(2.0 * log_a))
    first = (lax.broadcasted_iota(jnp.int32, (cb, 1), 0) == 0) & (blk == 0)
    mult = jnp.where(first, 1.0, mult)
    abuf[...] = jnp.exp(log_a)
    bbuf[...] = mult * gate_i * x

    def step(t, h):
        h = abuf[pl.ds(t, 1), :] * h + bbuf[pl.ds(t, 1), :]
        bbuf[pl.ds(t, 1), :] = h
        return h

    h_last = lax.fori_loop(0, cb, step, h_scr[...], unroll=8)
    h_scr[...] = h_last
    o_ref[...] = (bbuf[...] * _gelu_tanh(yl_ref[...])).astype(BF16)

    @pl.when(blk == pl.num_programs(1) - 1)
    def _():
        hl_ref[0] = h_last


def _lru_prompt(xy, conv_w, conv_b, w_a, w_i, b_a, b_i, lam, batch, seq, cb):
    m = xy.shape[0]
    width = xy.shape[1] // 2
    nblk = seq // cb
    nb = w_a.shape[0]
    vec = pl.BlockSpec((1, width), lambda b, c: (0, 0))
    wblk = pl.BlockSpec((nb, LANES, LANES), lambda b, c: (0, 0, 0))
    out, h_last = pl.pallas_call(
        _lru_prompt_kernel,
        grid=(batch, nblk),
        in_specs=[pl.BlockSpec((cb, width), lambda b, c: (b * nblk + c, 0)),
                  pl.BlockSpec((cb, width), lambda b, c: (b * nblk + c, 1)),
                  pl.BlockSpec((CONV_WIDTH, width), lambda b, c: (0, 0)),
                  vec, wblk, wblk, vec, vec, vec],
        out_specs=[pl.BlockSpec((cb, width), lambda b, c: (b * nblk + c, 0)),
                   pl.BlockSpec((1, 1, width), lambda b, c: (b, 0, 0))],
        out_shape=[jax.ShapeDtypeStruct((m, width), BF16),
                   jax.ShapeDtypeStruct((batch, 1, width), F32)],
        scratch_shapes=[pltpu.VMEM((cb + 8, width), F32), pltpu.VMEM((cb, width), F32),
                        pltpu.VMEM((cb, width), F32), pltpu.VMEM((1, width), F32)],
        compiler_params=_params(("arbitrary", "arbitrary")),
    )(xy, xy, conv_w, conv_b.reshape(1, width), w_a, w_i, b_a.reshape(1, width), b_i.reshape(1, width),
      lam.reshape(1, width))
    return out, h_last.reshape(batch, width)


def _lru_step_kernel(xl_ref, yl_ref, cs_ref, h0_ref, cw_ref, cb_ref, wa_ref, wi_ref, ba_ref, bi_ref, lam_ref,
                     o_ref, h_ref, *, reset):
    w = cw_ref[...]
    x = cs_ref[0] * w[0:1]
    x = x + cs_ref[1] * w[1:2]
    x = x + cs_ref[2] * w[2:3]
    x = x + xl_ref[...] * w[3:4] + cb_ref[...]
    log_a, gate_i = _lru_gates(x, wa_ref, wi_ref, ba_ref[...], bi_ref[...], lam_ref[...])
    mult = 1.0 if reset else jnp.sqrt(_neg_expm1(2.0 * log_a))
    h = jnp.exp(log_a) * h0_ref[...] + mult * gate_i * x
    h_ref[...] = h
    o_ref[...] = (h * _gelu_tanh(yl_ref[...])).astype(BF16)


def _lru_step(xy, conv_state_t, h0, conv_w, conv_b, w_a, w_i, b_a, b_i, lam, reset):
    bsz = xy.shape[0]
    width = xy.shape[1] // 2
    nb = w_a.shape[0]
    vec = pl.BlockSpec((1, width), lambda i: (0, 0))
    wblk = pl.BlockSpec((nb, LANES, LANES), lambda i: (0, 0, 0))
    full = pl.BlockSpec((bsz, width), lambda i: (0, 0))
    return pl.pallas_call(
        functools.partial(_lru_step_kernel, reset=reset),
        grid=(1,),
        in_specs=[full, pl.BlockSpec((bsz, width), lambda i: (0, 1)),
                  pl.BlockSpec((CONV_WIDTH - 1, bsz, width), lambda i: (0, 0, 0)), full,
                  pl.BlockSpec((CONV_WIDTH, width), lambda i: (0, 0)), vec, wblk, wblk, vec, vec, vec],
        out_specs=[full, full],
        out_shape=[jax.ShapeDtypeStruct((bsz, width), BF16), jax.ShapeDtypeStruct((bsz, width), F32)],
        compiler_params=_params(("arbitrary",)),
    )(xy, xy, conv_state_t, h0, conv_w, conv_b.reshape(1, width), w_a, w_i, b_a.reshape(1, width),
      b_i.reshape(1, width), lam.reshape(1, width))


def _hgrn_lower_bound(lb_ref, layer, j=0):
    depth = lb_ref.shape[0]
    raw = [lb_ref[l, j] for l in range(depth)]
    mx = raw[0]
    for r in raw[1:]:
        mx = jnp.maximum(mx, r)
    ex = [jnp.exp(r - mx) for r in raw]
    tot = ex[0]
    for e in ex[1:]:
        tot = tot + e
    wts = [e / tot for e in ex]
    cum = wts[0]
    for w in wts[1:layer + 1]:
        cum = cum + w
    return cum - wts[0]


def _hgrn_inputs(q_raw, f_raw, lb):
    q = _silu(q_raw) * (LANES ** -0.5)
    f = lb + (1.0 - lb) * _sigmoid(f_raw)
    return q, f


def _level_ref(g, half):
    c = g.shape[0]
    sub = 8
    if 2 * half >= sub:
        g3 = g.reshape(c // (2 * half), 2 * half, LANES)
        return jnp.broadcast_to(g3[:, half - 1:half, :], g3.shape).reshape(c, LANES)
    g3 = g.reshape(c // sub, sub, LANES)
    rin = lax.broadcasted_iota(jnp.int32, g3.shape, 1)
    out = jnp.broadcast_to(g3[:, sub - half - 1:sub - half, :], g3.shape)
    for start in range(sub - 4 * half, -1, -2 * half):
        out = jnp.where(rin < start + 2 * half, jnp.broadcast_to(g3[:, start + half - 1:start + half, :], g3.shape),
                        out)
    return out.reshape(c, LANES)


def _hgrn_level_table(c):
    i = np.arange(c)[:, None]
    j = np.arange(c)[None, :]
    nlev = int(math.log2(c))
    top_bit = sum(((i ^ j) >> b > 0).astype(np.int32) for b in range(1, nlev))
    return jnp.asarray(np.where(j < i, nlev - 1 - top_bit, np.where(j == i, nlev, -1)), jnp.int32)


def _hgrn_prompt_kernel(q_ref, f_ref, i_ref, gz_ref, lb_ref, nw_ref, lvl_ref, o_ref, s_ref, s_scr, *, layer):
    hp = s_scr.shape[0]
    blk = pl.program_id(2)
    cb = q_ref.shape[0]
    c = CHUNK

    @pl.when(blk == 0)
    def _():
        s_scr[...] = jnp.zeros_like(s_scr)

    lbs = [_hgrn_lower_bound(lb_ref, layer, j) for j in range(hp)]
    head_lanes = [slice(j * LANES, (j + 1) * LANES) for j in range(hp)]
    tri = (_iota2(c, 1) <= _iota2(c, 0)).astype(BF16)
    nw = nw_ref[...]
    lvl = lvl_ref[...]
    halves = [c >> (i + 1) for i in range(int(math.log2(c)))]

    items = [(slice(ci * c, (ci + 1) * c), j) for ci in range(cb // c) for j in range(hp)]
    qf = [_hgrn_inputs(q_ref[sl, head_lanes[j]], f_ref[sl, head_lanes[j]], lbs[j]) for sl, j in items]
    qs = [q for q, _ in qf]
    ks = [1.0 - f for _, f in qf]
    vs = [i_ref[sl, head_lanes[j]] for sl, j in items]
    gs = [_dot_exact_lhs(tri, jnp.log(f)) for _, f in qf]

    amats = [_dot_nt(q, k) for q, k in zip(qs, ks)]
    for li, half in enumerate(halves):
        es = [jnp.exp(-jnp.abs(g - _level_ref(g, half))) for g in gs]
        ps = [_dot_nt(q * e, k * e) for q, k, e in zip(qs, ks, es)]
        amats = [jnp.where(lvl == li, p, a) for p, a in zip(ps, amats)]
    intra = [_dot(jnp.where(lvl >= 0, a, 0.0), v) for a, v in zip(amats, vs)]

    g_lasts = [g[c - 1:c, :] for g in gs]
    qdec = [q * jnp.exp(g) for q, g in zip(qs, gs)]
    sdec = [_col_from_row(jnp.exp(gl)) for gl in g_lasts]
    sadd = [_dot_tn(k * jnp.exp(gl - g), v) for k, gl, g, v in zip(ks, g_lasts, gs, vs)]

    states = [s_scr[j] for j in range(hp)]
    outs = []
    for (sl, j), o, qd, dec, add in zip(items, intra, qdec, sdec, sadd):
        outs.append(o + _dot(qd, states[j]))
        states[j] = states[j] * dec + add
    for j in range(hp):
        s_scr[j] = states[j]
    for (sl, j), o in zip(items, outs):
        o_ref[sl, head_lanes[j]] = (_rms(o, nw) * _silu(gz_ref[sl, head_lanes[j]])).astype(BF16)

    @pl.when(blk == pl.num_programs(2) - 1)
    def _():
        s_ref[0] = s_scr[...]


def _hgrn_prompt(proj, lb_raw, norm_w, layer, batch, seq, heads, cb):
    m = proj.shape[0]
    nblk = seq // cb
    depth = lb_raw.shape[0]
    hp = HGRN_HEADS_PER_STEP
    assert heads % hp == 0
    groups = heads // hp
    head_blk = lambda part: pl.BlockSpec((cb, hp * LANES), lambda b, h, c: (b * nblk + c, part * groups + h))
    return pl.pallas_call(
        functools.partial(_hgrn_prompt_kernel, layer=layer),
        grid=(batch, groups, nblk),
        in_specs=[head_blk(0), head_blk(1), head_blk(2), head_blk(3),
                  pl.BlockSpec((depth, hp, 1, LANES), lambda b, h, c: (0, h, 0, 0)),
                  pl.BlockSpec((1, LANES), lambda b, h, c: (0, 0)),
                  pl.BlockSpec((CHUNK, CHUNK), lambda b, h, c: (0, 0))],
        out_specs=[pl.BlockSpec((cb, hp * LANES), lambda b, h, c: (b * nblk + c, h)),
                   pl.BlockSpec((1, hp, LANES, LANES), lambda b, h, c: (b, h, 0, 0))],
        out_shape=[jax.ShapeDtypeStruct((m, heads * LANES), BF16),
                   jax.ShapeDtypeStruct((batch, heads, LANES, LANES), F32)],
        scratch_shapes=[pltpu.VMEM((hp, LANES, LANES), F32)],
        compiler_params=_params(("arbitrary", "arbitrary", "arbitrary")),
    )(proj, proj, proj, proj, lb_raw.reshape(depth, heads, 1, LANES), norm_w.reshape(1, LANES),
      _hgrn_level_table(CHUNK))


def _hgrn_step_kernel(q_ref, f_ref, i_ref, gz_ref, lb_ref, nw_ref, s_ref, o_ref, so_ref, *, layer):
    tb = q_ref.shape[0]
    lb = _hgrn_lower_bound(lb_ref, layer)
    q, f = _hgrn_inputs(q_ref[...], f_ref[...], lb)
    v = i_ref[...]
    outs = []
    group = 4
    for b0 in range(0, tb, group):
        rows = range(b0, min(b0 + group, tb))
        fcols = [_col_from_row(f[b:b + 1]) for b in rows]
        for b, fcol in zip(rows, fcols):
            s = fcol * (s_ref[b, 0] - v[b:b + 1]) + v[b:b + 1]
            so_ref[b, 0] = s
            outs.append(_dot(q[b:b + 1], s))
    o = jnp.concatenate(outs, axis=0)
    o_ref[...] = (_rms(o, nw_ref[...]) * _silu(gz_ref[...])).astype(BF16)


def _hgrn_step(proj, lb_raw, norm_w, state, state_idx, layer, heads, tb):
    bsz = proj.shape[0]
    depth = lb_raw.shape[0]
    head_blk = lambda off: pl.BlockSpec((tb, LANES), lambda i, h: (i, off + h))
    st_in = pl.BlockSpec((None, tb, 1, LANES, LANES), lambda i, h: (state_idx, i, h, 0, 0))
    st_out = pl.BlockSpec((tb, 1, LANES, LANES), lambda i, h: (i, h, 0, 0))
    return pl.pallas_call(
        functools.partial(_hgrn_step_kernel, layer=layer),
        grid=(bsz // tb, heads),
        in_specs=[head_blk(0), head_blk(heads), head_blk(2 * heads), head_blk(3 * heads),
                  pl.BlockSpec((depth, 1, 1, LANES), lambda i, h: (0, h, 0, 0)),
                  pl.BlockSpec((1, LANES), lambda i, h: (0, 0)), st_in],
        out_specs=[pl.BlockSpec((tb, LANES), lambda i, h: (i, h)), st_out],
        out_shape=[jax.ShapeDtypeStruct((bsz, heads * LANES), BF16),
                   jax.ShapeDtypeStruct(state.shape[1:], F32)],
        compiler_params=_params(("arbitrary", "arbitrary")),
    )(proj, proj, proj, proj, lb_raw.reshape(depth, heads, 1, LANES), norm_w.reshape(1, LANES), state)


def _trunk(x, mem_k, mem_v, states, wts, big, batch, seq, prompt):
    depth = wts["norm_mix"].shape[0]
    gdn_w = wts["gdn_conv_w"].shape[-1] // 3
    gdn_heads = gdn_w // LANES
    hgrn_heads = wts["hgrn_lb_raw"].shape[1] // LANES
    lru_w = wts["lru_conv_w"].shape[-1]
    bf = {name: list(handles) for name, handles in big.items()}
    new = {"gdn_conv": [], "gdn": [], "lru_conv": [], "lru": [], "hgrn": []}

    def keep(name, idx, w_used):
        w, li = bf[name][idx]
        bf[name][idx] = (w_used, li if w_used is w else 0)

    for l in range(depth):
        g_mix = wts["norm_mix"][l]
        if l % 2 == 0:
            e = l // 2
            if prompt:
                hn = _rms_cast(x, g_mix, TM)
                proj = _matmul(hn, wts["ab_head"], e, 0, 4 * gdn_w, TM_BF16, TN)
                ba = _matmul(hn, wts["ab_head"], e, 4 * gdn_w, LANES, TM_BF16, LANES)
                xy = _matmul(hn, wts["ab_lru"], e, 0, 2 * lru_w, TM_BF16, TN)
            else:
                proj, _ = _norm_matmul(x, g_mix, wts["ab_head"], e, 0, 4 * gdn_w, TM, TN)
                ba, _ = _norm_matmul(x, g_mix, wts["ab_head"], e, 4 * gdn_w, LANES, TM, LANES)
                xy, _ = _norm_matmul(x, g_mix, wts["ab_lru"], e, 0, 2 * lru_w, TM, TN)
            gargs = (wts["gdn_conv_w"][e], wts["alog_pad"][e], wts["dtb_pad"][e], wts["gdn_norm_w"][e])
            largs = (wts["lru_conv_w"][e], wts["lru_conv_b"][e], wts["lru_w_a"][e], wts["lru_w_i"][e],
                     wts["lru_b_a"][e], wts["lru_b_i"][e], wts["lru_lam"][e])
            w_out, li = bf["ab_w_out"][e]
            if prompt:
                o_a, s_new = _gdn_prompt(proj, ba, *gargs, batch, seq, gdn_heads, 512)
                o_b, h_new = _lru_prompt(xy, *largs, batch, seq, 256)
                tail = lambda t, w: t.reshape(batch, seq, -1)[:, seq - (CONV_WIDTH - 1):, :w]
                new["gdn_conv"].append(tail(proj, 3 * gdn_w))
                new["lru_conv"].append(tail(xy, lru_w))
                x = _matmul2_res(o_a, o_b, w_out, li, x, TM_BF16, TN)
            else:
                gc_state, lc_state = states["gdn_conv"][e], states["lru_conv"][e]
                o_a, s_new = _gdn_step(proj, ba, jnp.swapaxes(gc_state, 0, 1), *gargs, states["gdn"], e,
                                       gdn_heads, 128)
                o_b, h_new = _lru_step(xy, jnp.swapaxes(lc_state, 0, 1), states["lru"][e], *largs, reset=False)
                new["gdn_conv"].append(jnp.concatenate([gc_state[:, 1:], proj[:, None, :3 * gdn_w]], axis=1))
                new["lru_conv"].append(jnp.concatenate([lc_state[:, 1:], xy[:, None, :lru_w]], axis=1))
                x, w_used = _matmul_res(jnp.concatenate([o_a, o_b], axis=1), w_out, li, x, TM, TN)
                keep("ab_w_out", e, w_used)
            new["gdn"].append(s_new)
            new["lru"].append(h_new)
        else:
            o_idx = l // 2
            w_in, li = bf["c_w_in"][o_idx]
            proj, w_used = _norm_matmul(x, g_mix, w_in, li, 0, w_in.shape[2], TM, TN)
            keep("c_w_in", o_idx, w_used)
            hargs = (wts["hgrn_lb_raw"], wts["hgrn_norm_w"][o_idx])
            if prompt:
                o_c, s_new = _hgrn_prompt(proj, *hargs, l, batch, seq, hgrn_heads, 512)
            else:
                o_c, s_new = _hgrn_step(proj, *hargs, states["hgrn"], o_idx, l, hgrn_heads, 128)
            new["hgrn"].append(s_new)
            w_out, li = bf["c_w_out"][o_idx]
            x, w_used = _matmul_res(o_c, w_out, li, x, TM_BF16, TN)
            keep("c_w_out", o_idx, w_used)
        (w_q, lq), (w_o, lo) = bf["mem_w_q"][l], bf["mem_w_o"][l]
        if prompt:
            assert lq == lo
            x = _mem_attn_prompt(x, wts["norm_mem"][l], w_q, mem_k, mem_v, w_o, l, lq, seq, 512)
        else:
            q, w_used = _norm_matmul(x, wts["norm_mem"][l], w_q, lq, 0, w_q.shape[2], TM, TN)
            keep("mem_w_q", l, w_used)
            o = _mem_attn_sample(q, mem_k, mem_v, l, 8)
            x, w_used = _matmul_res(o.reshape(batch, -1).astype(BF16), w_o, lo, x, TM, TN)
            keep("mem_w_o", l, w_used)
        (w_up, lu), (w_dn, ld) = bf["ffn_w_up"][l], bf["ffn_w_down"][l]
        assert lu == ld
        x, up_used, dn_used = _ffn(x, wts["norm_ffn"][l], w_up, w_dn, lu, wts["norm_final"], l == depth - 1,
                                   TM, TF)
        keep("ffn_w_up", l, up_used)
        keep("ffn_w_down", l, dn_used)
    return x, {n: jnp.stack(v) for n, v in new.items()}, bf


def kernel(x_prompt, x_sample, cache_mem_k, cache_mem_v, state_gdn_conv, state_gdn, state_lru_conv, state_lru, state_hgrn, mem_prompt, norm_mix, norm_mem, norm_mem_kv, norm_ffn, norm_final, ab_w_in, ab_w_out, gdn_conv_w, gdn_a_log, gdn_dt_bias, gdn_norm_w, lru_conv_w, lru_conv_b, lru_w_a, lru_b_a, lru_w_i, lru_b_i, lru_lam, c_w_in, c_w_out, hgrn_lb_raw, hgrn_norm_w, mem_w_q, mem_w_k, mem_w_v, mem_w_o, ffn_w_up, ffn_w_down):
    bp, lp, d = x_prompt.shape
    bs, ls, _ = x_sample.shape
    assert ls == 1, "the sample group advances one token per call"
    depth = norm_mix.shape[0]
    gdn_heads = gdn_a_log.shape[1]
    gdn_w = gdn_heads * LANES
    mem_tokens, mem_heads, mem_hd = cache_mem_k.shape[2:]
    mem_w = mem_heads * mem_hd
    n_ba = 2 * gdn_heads
    assert n_ba <= LANES

    pad_ba = lambda a: jnp.pad(a, ((0, 0), (gdn_heads, LANES - n_ba)))[:, None, :]
    ab_bf = ab_w_in.astype(BF16)
    wts = dict(
        norm_mix=norm_mix, norm_mem=norm_mem, norm_ffn=norm_ffn, norm_final=norm_final,
        ab_head=ab_bf, ab_lru=ab_bf[:, :, 4 * gdn_w + n_ba:],
        gdn_conv_w=gdn_conv_w, alog_pad=pad_ba(gdn_a_log), dtb_pad=pad_ba(gdn_dt_bias), gdn_norm_w=gdn_norm_w,
        lru_conv_w=lru_conv_w, lru_conv_b=lru_conv_b, lru_w_a=lru_w_a.astype(BF16), lru_w_i=lru_w_i.astype(BF16),
        lru_b_a=lru_b_a, lru_b_i=lru_b_i, lru_lam=lru_lam,
        hgrn_lb_raw=hgrn_lb_raw, hgrn_norm_w=hgrn_norm_w,
    )
    big = dict(ab_w_out=ab_w_out, c_w_in=c_w_in, c_w_out=c_w_out, mem_w_q=mem_w_q, mem_w_o=mem_w_o,
               ffn_w_up=ffn_w_up, ffn_w_down=ffn_w_down)
    big = {name: [(w, l) for l in range(w.shape[0])] for name, w in big.items()}

    states = dict(gdn_conv=state_gdn_conv, gdn=state_gdn, lru_conv=state_lru_conv, lru=state_lru,
                  hgrn=state_hgrn)
    y_s, new_s, big_bf = _trunk(x_sample.reshape(bs * ls, d), cache_mem_k, cache_mem_v, states, wts, big,
                                bs, ls, False)

    mem_rows = mem_prompt.reshape(bp * mem_tokens, d)
    w_kv = jnp.concatenate([mem_w_k, mem_w_v], axis=-1).astype(BF16)
    kv = jnp.stack([_norm_matmul(mem_rows, norm_mem_kv[l], w_kv, l, 0, 2 * mem_w, TM, TN)[0] for l in range(depth)])
    p_mem_k = kv[:, :, :mem_w].reshape(depth, bp, mem_tokens, mem_w)
    p_mem_v = kv[:, :, mem_w:].reshape(depth, bp, mem_tokens, mem_w)

    y_p, new_p, _ = _trunk(x_prompt.reshape(bp * lp, d), p_mem_k, p_mem_v, None, wts, big_bf, bp, lp, True)

    order = ("gdn_conv", "gdn", "lru_conv", "lru", "hgrn")
    mem5 = lambda t: t.reshape(depth, bp, mem_tokens, mem_heads, mem_hd)
    return (y_p.reshape(bp, lp, d), y_s.reshape(bs, ls, d), mem5(p_mem_k), mem5(p_mem_v),
            *(new_p[n] for n in order), *(new_s[n] for n in order))
```

```python
import functools
import math

import jax
import jax.numpy as jnp
import numpy as np
from jax import lax
from jax.experimental import pallas as pl
from jax.experimental.pallas import tpu as pltpu

F32 = jnp.float32
BF16 = jnp.bfloat16
EPS = 1e-6
LANES = 128
CONV_WIDTH = 4
LRU_C = 8.0
VMEM_LIMIT_BYTES = 56 * 1024 * 1024
TM = 512
TM_BF16 = 1024
TN = 1024
TF = 512
TN_CAST = 512
TF_CAST = 256
ROW_GROUPS = 4
CHUNK = 128
GDN_HEADS_PER_STEP = 4
HGRN_HEADS_PER_STEP = 8


def _params(sem):
    return pltpu.CompilerParams(dimension_semantics=sem, vmem_limit_bytes=VMEM_LIMIT_BYTES)


def _sigmoid(x):
    return jax.nn.sigmoid(x)


def _silu(x):
    return x * _sigmoid(x)


def _softplus(x):
    return jnp.maximum(x, 0.0) + jnp.log1p(jnp.exp(-jnp.abs(x)))


def _neg_expm1(x):
    t = jnp.tanh(0.5 * x)
    return -2.0 * t / (1.0 - t)


def _gelu_tanh(x):
    c = math.sqrt(2.0 / math.pi)
    return 0.5 * x * (1.0 + jnp.tanh(c * (x + 0.044715 * (x * x * x))))


def _rms(x, w):
    return x * lax.rsqrt(jnp.mean(x * x, axis=-1, keepdims=True) + EPS) * w


def _l2n(x):
    return x * lax.rsqrt(jnp.sum(x * x, axis=-1, keepdims=True) + EPS)


def _dot(a, b):
    return jnp.dot(a.astype(BF16), b.astype(BF16), preferred_element_type=F32)


def _dot_nt(a, b):
    return lax.dot_general(a.astype(BF16), b.astype(BF16), (((1,), (1,)), ((), ())),
                           preferred_element_type=F32)


def _dot_tn(a, b):
    return lax.dot_general(a.astype(BF16), b.astype(BF16), (((0,), (0,)), ((), ())),
                           preferred_element_type=F32)


def _split2(a):
    hi = a.astype(BF16)
    lo = (a - hi.astype(F32)).astype(BF16)
    return hi, lo


def _dot_hp(a_parts, b_parts):
    ah, al = a_parts
    bh, bl = b_parts
    return jnp.dot(jnp.concatenate([ah, ah, al], axis=1), jnp.concatenate([bh, bl, bh], axis=0),
                   preferred_element_type=F32)


def _dot_exact_lhs(t, b):
    b1 = b.astype(BF16)
    r1 = b - b1.astype(F32)
    b2 = r1.astype(BF16)
    b3 = (r1 - b2.astype(F32)).astype(BF16)
    return jnp.dot(jnp.concatenate([t, t, t], axis=1), jnp.concatenate([b1, b2, b3], axis=0),
                   preferred_element_type=F32)


def _iota2(n, axis):
    return lax.broadcasted_iota(jnp.int32, (n, n), axis)


def _col_from_row(row):
    n = row.shape[-1]
    return jnp.broadcast_to(row, (n, n)).T


def _row_groups(rows):
    n = ROW_GROUPS if rows % (8 * ROW_GROUPS) == 0 else 1
    return [slice(r * rows // n, (r + 1) * rows // n) for r in range(n)]


def _norm_matmul_kernel(x_ref, g_ref, w_ref, o_ref, *rest):
    xn_ref = rest[-1]
    if len(rest) == 2:
        rest[0][...] = w_ref[...].astype(BF16)
        w_ref = rest[0]
    first = pl.program_id(1) == 0

    @pl.when(first)
    def _():
        for rows in _row_groups(x_ref.shape[0]):
            xn = _rms(x_ref[rows, :], g_ref[...]).astype(BF16)
            xn_ref[rows, :] = xn
            o_ref[rows, :] = jnp.dot(xn, w_ref[...], preferred_element_type=F32)

    @pl.when(jnp.logical_not(first))
    def _():
        o_ref[...] = jnp.dot(xn_ref[...], w_ref[...], preferred_element_type=F32)


def _norm_matmul(x, g, w, layer, n0, n, tm, tn):
    m, k = x.shape
    cast = w.dtype != BF16
    tm = min(tm, m)
    tn = min(tn, n, TN_CAST) if cast else min(tn, n)
    j0 = n0 // tn
    out_specs = [pl.BlockSpec((tm, tn), lambda i, j: (i, j))]
    out_shape = [jax.ShapeDtypeStruct((m, n), F32)]
    if cast:
        assert m == tm and n0 == 0 and n == w.shape[2], "each weight tile must be visited exactly once"
        out_specs.append(pl.BlockSpec((None, k, tn), lambda i, j: (0, 0, j)))
        out_shape.append(jax.ShapeDtypeStruct((1, k, n), BF16))
    res = pl.pallas_call(
        _norm_matmul_kernel,
        grid=(m // tm, n // tn),
        in_specs=[pl.BlockSpec((tm, k), lambda i, j: (i, 0)),
                  pl.BlockSpec((1, k), lambda i, j: (0, 0)),
                  pl.BlockSpec((None, k, tn), lambda i, j: (layer, 0, j0 + j))],
        out_specs=out_specs,
        out_shape=out_shape,
        scratch_shapes=[pltpu.VMEM((tm, k), BF16)],
        compiler_params=_params(("arbitrary", "arbitrary")),
    )(x, g.reshape(1, k), w)
    return (res[0], res[1]) if cast else (res[0], w)


def _rms_cast_kernel(x_ref, g_ref, o_ref):
    o_ref[...] = _rms(x_ref[...], g_ref[...]).astype(BF16)


def _rms_cast(x, g, tm):
    m, k = x.shape
    tm = min(tm, m)
    return pl.pallas_call(
        _rms_cast_kernel,
        grid=(m // tm,),
        in_specs=[pl.BlockSpec((tm, k), lambda i: (i, 0)), pl.BlockSpec((1, k), lambda i: (0, 0))],
        out_specs=pl.BlockSpec((tm, k), lambda i: (i, 0)),
        out_shape=jax.ShapeDtypeStruct((m, k), BF16),
        compiler_params=_params(("arbitrary",)),
    )(x, g.reshape(1, k))


def _matmul_kernel(a_ref, w_ref, o_ref):
    o_ref[...] = jnp.dot(a_ref[...], w_ref[...], preferred_element_type=F32)


def _matmul(a, w, layer, n0, n, tm, tn):
    m, k = a.shape
    tm = min(tm, m)
    tn = min(tn, n)
    j0 = n0 // tn
    return pl.pallas_call(
        _matmul_kernel,
        grid=(m // tm, n // tn),
        in_specs=[pl.BlockSpec((tm, k), lambda i, j: (i, 0)),
                  pl.BlockSpec((None, k, tn), lambda i, j: (layer, 0, j0 + j))],
        out_specs=pl.BlockSpec((tm, tn), lambda i, j: (i, j)),
        out_shape=jax.ShapeDtypeStruct((m, n), F32),
        compiler_params=_params(("arbitrary", "arbitrary")),
    )(a, w)


def _matmul2_res_kernel(a1_ref, a2_ref, w1_ref, w2_ref, r_ref, o_ref):
    acc = jnp.dot(a1_ref[...], w1_ref[...], preferred_element_type=F32)
    acc = acc + jnp.dot(a2_ref[...], w2_ref[...], preferred_element_type=F32)
    o_ref[...] = r_ref[...] + acc


def _matmul2_res(a1, a2, w, layer, res, tm, tn):
    m, kh = a1.shape
    n = w.shape[2]
    tm = min(tm, m)
    tn = min(tn, n)
    return pl.pallas_call(
        _matmul2_res_kernel,
        grid=(m // tm, n // tn),
        in_specs=[pl.BlockSpec((tm, kh), lambda i, j: (i, 0)),
                  pl.BlockSpec((tm, kh), lambda i, j: (i, 0)),
                  pl.BlockSpec((None, kh, tn), lambda i, j: (layer, 0, j)),
                  pl.BlockSpec((None, kh, tn), lambda i, j: (layer, 1, j)),
                  pl.BlockSpec((tm, tn), lambda i, j: (i, j))],
        out_specs=pl.BlockSpec((tm, tn), lambda i, j: (i, j)),
        out_shape=jax.ShapeDtypeStruct((m, n), F32),
        compiler_params=_params(("arbitrary", "arbitrary")),
    )(a1, a2, w, w, res)


def _matmul_res_kernel(a_ref, w_ref, r_ref, o_ref, *wb_ref):
    w = w_ref[...]
    if wb_ref:
        w = w.astype(BF16)
        wb_ref[0][...] = w
    o_ref[...] = r_ref[...] + jnp.dot(a_ref[...], w, preferred_element_type=F32)


def _matmul_res(a, w, layer, res, tm, tn):
    m, k = a.shape
    n = w.shape[2]
    cast = w.dtype != BF16
    tm = min(tm, m)
    tn = min(tn, n, TN_CAST) if cast else min(tn, n)
    out_specs = [pl.BlockSpec((tm, tn), lambda i, j: (i, j))]
    out_shape = [jax.ShapeDtypeStruct((m, n), F32)]
    if cast:
        assert m == tm, "each weight tile must be visited exactly once"
        out_specs.append(pl.BlockSpec((None, k, tn), lambda i, j: (0, 0, j)))
        out_shape.append(jax.ShapeDtypeStruct((1, k, n), BF16))
    out = pl.pallas_call(
        _matmul_res_kernel,
        grid=(m // tm, n // tn),
        in_specs=[pl.BlockSpec((tm, k), lambda i, j: (i, 0)),
                  pl.BlockSpec((None, k, tn), lambda i, j: (layer, 0, j)),
                  pl.BlockSpec((tm, tn), lambda i, j: (i, j))],
        out_specs=out_specs,
        out_shape=out_shape,
        compiler_params=_params(("arbitrary", "arbitrary")),
    )(a, w, res)
    return (out[0], out[1]) if cast else (out[0], w)


def _ffn_kernel(x_ref, g_ref, wu_ref, wd_ref, gf_ref, o_ref, *rest, final_norm):
    xn_ref = rest[-1]
    f = pl.program_id(1)
    if len(rest) == 3:
        rest[0][...] = wu_ref[...].astype(BF16)
        rest[1][...] = wd_ref[...].astype(BF16)
        wu_ref, wd_ref = rest[0], rest[1]

    def mlp(xn):
        h = jnp.dot(xn, wu_ref[...], preferred_element_type=F32)
        h = jnp.square(jnp.maximum(h, 0.0)).astype(BF16)
        return jnp.dot(h, wd_ref[...], preferred_element_type=F32)

    @pl.when(f == 0)
    def _():
        for rows in _row_groups(x_ref.shape[0]):
            x = x_ref[rows, :]
            xn = _rms(x, g_ref[...]).astype(BF16)
            xn_ref[rows, :] = xn
            o_ref[rows, :] = x + mlp(xn)

    @pl.when(f != 0)
    def _():
        o_ref[...] += mlp(xn_ref[...])

    if final_norm:
        @pl.when(f == pl.num_programs(1) - 1)
        def _():
            o_ref[...] = _rms(o_ref[...], gf_ref[...])


def _ffn(x, g, w_up, w_down, layer, g_final, final_norm, tm, tf):
    m, d = x.shape
    dff = w_up.shape[2]
    cast = w_up.dtype != BF16
    tm = min(tm, m)
    tf = min(tf, TF_CAST) if cast else tf
    out_specs = [pl.BlockSpec((tm, d), lambda i, f: (i, 0))]
    out_shape = [jax.ShapeDtypeStruct((m, d), F32)]
    if cast:
        assert m == tm, "each weight tile must be visited exactly once"
        out_specs += [pl.BlockSpec((None, d, tf), lambda i, f: (0, 0, f)),
                      pl.BlockSpec((None, tf, d), lambda i, f: (0, f, 0))]
        out_shape += [jax.ShapeDtypeStruct((1, d, dff), BF16), jax.ShapeDtypeStruct((1, dff, d), BF16)]
    out = pl.pallas_call(
        functools.partial(_ffn_kernel, final_norm=final_norm),
        grid=(m // tm, dff // tf),
        in_specs=[pl.BlockSpec((tm, d), lambda i, f: (i, 0), pipeline_mode=pl.Buffered(1)),
                  pl.BlockSpec((1, d), lambda i, f: (0, 0)),
                  pl.BlockSpec((None, d, tf), lambda i, f: (layer, 0, f)),
                  pl.BlockSpec((None, tf, d), lambda i, f: (layer, f, 0)),
                  pl.BlockSpec((1, d), lambda i, f: (0, 0))],
        out_specs=out_specs,
        out_shape=out_shape,
        scratch_shapes=[pltpu.VMEM((tm, d), BF16)],
        compiler_params=_params(("arbitrary", "arbitrary")),
    )(x, g.reshape(1, d), w_up, w_down, g_final.reshape(1, d))
    return (out[0], out[1], out[2]) if cast else (out[0], w_up, w_down)


def _mem_attn_prompt_kernel(x_ref, g_ref, wq_ref, k_ref, v_ref, wo_ref, o_ref, *, heads):
    x = x_ref[...]
    xn = _rms(x, g_ref[...]).astype(BF16)
    q = jnp.dot(xn, wq_ref[...], preferred_element_type=F32)
    k = k_ref[0].astype(BF16)
    v = v_ref[0].astype(BF16)
    scale = LANES ** -0.5
    outs = []
    for h in range(heads):
        sl = slice(h * LANES, (h + 1) * LANES)
        s = _dot_nt(q[:, sl], k[:, sl]) * scale
        e = jnp.exp(s - jnp.max(s, axis=-1, keepdims=True))
        p = e / jnp.sum(e, axis=-1, keepdims=True)
        outs.append(_dot(p, v[:, sl]))
    o = jnp.concatenate(outs, axis=-1).astype(BF16)
    o_ref[...] = x + jnp.dot(o, wo_ref[...], preferred_element_type=F32)


def _mem_attn_prompt(x, g, w_q, mem_k, mem_v, w_o, layer, w_layer, seq, tl):
    m, d = x.shape
    _, _, t, w = mem_k.shape
    nblk = seq // tl
    return pl.pallas_call(
        functools.partial(_mem_attn_prompt_kernel, heads=w // LANES),
        grid=(m // tl,),
        in_specs=[pl.BlockSpec((tl, d), lambda i: (i, 0)),
                  pl.BlockSpec((1, d), lambda i: (0, 0)),
                  pl.BlockSpec((None, d, w), lambda i: (w_layer, 0, 0), pipeline_mode=pl.Buffered(1)),
                  pl.BlockSpec((None, 1, t, w), lambda i: (layer, i // nblk, 0, 0)),
                  pl.BlockSpec((None, 1, t, w), lambda i: (layer, i // nblk, 0, 0)),
                  pl.BlockSpec((None, w, d), lambda i: (w_layer, 0, 0), pipeline_mode=pl.Buffered(1))],
        out_specs=pl.BlockSpec((tl, d), lambda i: (i, 0)),
        out_shape=jax.ShapeDtypeStruct((m, d), F32),
        compiler_params=_params(("arbitrary",)),
    )(x, g.reshape(1, d), w_q, mem_k, mem_v, w_o)


def _mem_attn_sample_kernel(q_ref, k_ref, v_ref, o_ref, *, tb, heads):
    scale = LANES ** -0.5
    for b in range(tb):
        q = jnp.concatenate([q_ref[b:b + 1, h * LANES:(h + 1) * LANES] for h in range(heads)], axis=0)
        s = jnp.sum(k_ref[b] * (q * scale)[None], axis=-1, keepdims=True)
        e = jnp.exp(s - jnp.max(s, axis=0, keepdims=True))
        o_ref[b] = jnp.sum(e * v_ref[b], axis=0) / jnp.sum(e, axis=0)


def _mem_attn_sample(q, mem_k, mem_v, layer, tb):
    b = q.shape[0]
    _, _, t, heads, hd = mem_k.shape
    kv_blk = pl.BlockSpec((None, tb, t, heads, hd), lambda i: (layer, i, 0, 0, 0))
    return pl.pallas_call(
        functools.partial(_mem_attn_sample_kernel, tb=tb, heads=heads),
        grid=(b // tb,),
        in_specs=[pl.BlockSpec((tb, heads * hd), lambda i: (i, 0)), kv_blk, kv_blk],
        out_specs=pl.BlockSpec((tb, heads, hd), lambda i: (i, 0, 0)),
        out_shape=jax.ShapeDtypeStruct((b, heads, hd), F32),
        compiler_params=_params(("arbitrary",)),
    )(q, mem_k, mem_v)


def _conv_block(x_ref, buf, w):
    cb = x_ref.shape[0]
    buf[8:8 + cb, :] = x_ref[...]
    y = buf[5:5 + cb, :] * w[0:1]
    y = y + buf[6:6 + cb, :] * w[1:2]
    y = y + buf[7:7 + cb, :] * w[2:3]
    y = y + buf[8:8 + cb, :] * w[3:4]
    buf[0:8, :] = buf[cb:cb + 8, :]
    return y


def _gdn_gates(ba, alog, dtb, h, heads):
    lane = lax.broadcasted_iota(jnp.int32, ba.shape, 1)
    beta_all = _sigmoid(ba)
    g_all = -jnp.exp(alog) * _softplus(ba + dtb)
    beta = jnp.sum(jnp.where(lane == h, beta_all, 0.0), axis=1, keepdims=True)
    g = jnp.sum(jnp.where(lane == heads + h, g_all, 0.0), axis=1, keepdims=True)
    return beta, g


def _gdn_prompt_kernel(q_ref, k_ref, v_ref, z_ref, ba_ref, cwq_ref, cwk_ref, cwv_ref, alog_ref, dtb_ref,
                       nw_ref, o_ref, s_ref, qbuf, kbuf, vbuf, s_scr, *, heads):
    hp = s_scr.shape[0]
    blk = pl.program_id(2)
    cb = q_ref.shape[0]
    c = CHUNK
    n_chunks = cb // c

    @pl.when(blk == 0)
    def _():
        s_scr[...] = jnp.zeros_like(s_scr)
        for buf in (qbuf, kbuf, vbuf):
            buf[0:8, :] = jnp.zeros((8, hp * LANES), F32)

    for x_ref, buf in ((q_ref, qbuf), (k_ref, kbuf), (v_ref, vbuf)):
        buf[8:8 + cb, :] = x_ref[...]
    conv_w = (cwq_ref[...], cwk_ref[...], cwv_ref[...])
    ba = ba_ref[...]
    gates = [_gdn_gates(ba, alog_ref[...], dtb_ref[...], pl.program_id(1) * hp + j, heads) for j in range(hp)]
    head_lanes = [slice(j * LANES, (j + 1) * LANES) for j in range(hp)]

    row = _iota2(c, 0)
    col = _iota2(c, 1)
    causal = col <= row
    strict = col < row
    tri = causal.astype(BF16)
    eye = (col == row).astype(F32)
    nw = nw_ref[...]

    def conv_rows(buf, w, r0):
        y = buf[r0 + 5:r0 + 5 + c, :] * w[0:1]
        y = y + buf[r0 + 6:r0 + 6 + c, :] * w[1:2]
        y = y + buf[r0 + 7:r0 + 7 + c, :] * w[2:3]
        return y + buf[r0 + 8:r0 + 8 + c, :] * w[3:4]

    conv = [[_silu(conv_rows(buf, w, ci * c)) for buf, w in zip((qbuf, kbuf, vbuf), conv_w)]
            for ci in range(n_chunks)]
    items = [(ci, j) for ci in range(n_chunks) for j in range(hp)]
    rows = lambda ci: slice(ci * c, (ci + 1) * c)
    qs = [_l2n(conv[ci][0][:, head_lanes[j]]) * (LANES ** -0.5) for ci, j in items]
    ks = [_l2n(conv[ci][1][:, head_lanes[j]]) for ci, j in items]
    vs = [conv[ci][2][:, head_lanes[j]] for ci, j in items]
    betas = [gates[j][0][rows(ci)] for ci, j in items]
    gcs = [_dot_exact_lhs(tri, jnp.broadcast_to(gates[j][1][rows(ci)], (c, c))) for ci, j in items]
    decays = [jnp.exp(jnp.where(causal, gc - gc.T, -jnp.inf)) for gc in gcs]
    kbs = [k * b for k, b in zip(ks, betas)]
    ms = [jnp.where(strict, _dot_nt(kb, k) * dec, 0.0) for kb, k, dec in zip(kbs, ks, decays)]
    xs = [_split2(-m) for m in ms]
    ts = [eye - m for m in ms]
    for _ in range(int(math.log2(c)) - 1):
        xs = [_split2(_dot_hp(xp, xp)) for xp in xs]
        ts = [t + _dot_hp(_split2(t), xp) for t, xp in zip(ts, xs)]
    egs = [jnp.exp(gc) for gc in gcs]
    uws = [_dot(t, jnp.concatenate([v * b, kb * eg], axis=1))
           for t, v, b, kb, eg in zip(ts, vs, betas, kbs, egs)]
    attns = [jnp.where(causal, _dot_nt(q, k) * dec, 0.0) for q, k, dec in zip(qs, ks, decays)]
    g_lasts = [gc[c - 1:c, :] for gc in gcs]
    kd_uws = [_dot_tn(k * jnp.exp(gl - gc), uw) for k, gl, gc, uw in zip(ks, g_lasts, gcs, uws)]
    at_uws = [_dot(attn, uw) for attn, uw in zip(attns, uws)]
    lhs = [jnp.concatenate([q * eg - at[:, LANES:], eye * jnp.exp(gl) - kd[:, LANES:]], axis=0)
           for q, eg, at, gl, kd in zip(qs, egs, at_uws, g_lasts, kd_uws)]

    states = [s_scr[j] for j in range(hp)]
    outs = []
    for it, (ci, j) in enumerate(items):
        both = _dot(lhs[it], states[j])
        outs.append(at_uws[it][:, :LANES] + both[:c])
        states[j] = both[c:] + kd_uws[it][:, :LANES]
    for j in range(hp):
        s_scr[j] = states[j]
    for buf in (qbuf, kbuf, vbuf):
        buf[0:8, :] = buf[cb:cb + 8, :]
    for (ci, j), o in zip(items, outs):
        o_ref[rows(ci), head_lanes[j]] = (_rms(o, nw) * _silu(z_ref[rows(ci), head_lanes[j]])).astype(BF16)

    @pl.when(blk == pl.num_programs(2) - 1)
    def _():
        s_ref[0] = s_scr[...]


def _gdn_prompt(proj, ba, conv_w, alog_pad, dtb_pad, norm_w, batch, seq, heads, cb):
    m = proj.shape[0]
    nblk = seq // cb
    hp = GDN_HEADS_PER_STEP
    assert heads % hp == 0
    groups = heads // hp
    rows = lambda b, h, c: b * nblk + c
    head_blk = lambda part: pl.BlockSpec((cb, hp * LANES), lambda b, h, c: (rows(b, h, c), part * groups + h))
    cw_blk = lambda part: pl.BlockSpec((CONV_WIDTH, hp * LANES), lambda b, h, c: (0, part * groups + h))
    row128 = pl.BlockSpec((1, LANES), lambda b, h, c: (0, 0))
    return pl.pallas_call(
        functools.partial(_gdn_prompt_kernel, heads=heads),
        grid=(batch, groups, nblk),
        in_specs=[head_blk(0), head_blk(1), head_blk(2), head_blk(3),
                  pl.BlockSpec((cb, LANES), lambda b, h, c: (rows(b, h, c), 0)),
                  cw_blk(0), cw_blk(1), cw_blk(2), row128, row128, row128],
        out_specs=[pl.BlockSpec((cb, hp * LANES), lambda b, h, c: (rows(b, h, c), h)),
                   pl.BlockSpec((1, hp, LANES, LANES), lambda b, h, c: (b, h, 0, 0))],
        out_shape=[jax.ShapeDtypeStruct((m, heads * LANES), BF16),
                   jax.ShapeDtypeStruct((batch, heads, LANES, LANES), F32)],
        scratch_shapes=[pltpu.VMEM((cb + 8, hp * LANES), F32)] * 3 + [pltpu.VMEM((hp, LANES, LANES), F32)],
        compiler_params=_params(("arbitrary", "arbitrary", "arbitrary")),
    )(proj, proj, proj, proj, ba, conv_w, conv_w, conv_w, alog_pad, dtb_pad, norm_w.reshape(1, LANES))


def _gdn_step_kernel(q_ref, k_ref, v_ref, z_ref, ba_ref, cq_ref, ck_ref, cv_ref, cwq_ref, cwk_ref, cwv_ref,
                     alog_ref, dtb_ref, nw_ref, s_ref, o_ref, so_ref, *, heads):
    h = pl.program_id(1)
    tb = q_ref.shape[0]

    def conv(x_ref, c_ref, w_ref):
        w = w_ref[...]
        y = c_ref[0] * w[0:1]
        y = y + c_ref[1] * w[1:2]
        y = y + c_ref[2] * w[2:3]
        return y + x_ref[...] * w[3:4]

    q = _l2n(_silu(conv(q_ref, cq_ref, cwq_ref))) * (LANES ** -0.5)
    k = _l2n(_silu(conv(k_ref, ck_ref, cwk_ref)))
    v = _silu(conv(v_ref, cv_ref, cwv_ref))
    beta, g = _gdn_gates(ba_ref[...], alog_ref[...], dtb_ref[...], h, heads)
    eg = jnp.exp(g)
    outs = []
    group = 4
    for b0 in range(0, tb, group):
        rows = range(b0, min(b0 + group, tb))
        kcols = [_col_from_row(k[b:b + 1]) for b in rows]
        for b, kcol in zip(rows, kcols):
            r = slice(b, b + 1)
            s = s_ref[b, 0] * eg[r]
            v_new = beta[r] * (v[r] - jnp.sum(kcol * s, axis=0, keepdims=True))
            s = s + kcol * v_new
            so_ref[b, 0] = s
            outs.append(_dot(q[r], s))
    o = jnp.concatenate(outs, axis=0)
    o_ref[...] = (_rms(o, nw_ref[...]) * _silu(z_ref[...])).astype(BF16)


def _gdn_step(proj, ba, conv_state_t, conv_w, alog_pad, dtb_pad, norm_w, state, layer, heads, tb):
    bsz = proj.shape[0]
    head_blk = lambda off: pl.BlockSpec((tb, LANES), lambda i, h: (i, off + h))
    cs_blk = lambda off: pl.BlockSpec((CONV_WIDTH - 1, tb, LANES), lambda i, h: (0, i, off + h))
    cw_blk = lambda off: pl.BlockSpec((CONV_WIDTH, LANES), lambda i, h: (0, off + h))
    row128 = pl.BlockSpec((1, LANES), lambda i, h: (0, 0))
    st_in = pl.BlockSpec((None, tb, 1, LANES, LANES), lambda i, h: (layer, i, h, 0, 0))
    st_out = pl.BlockSpec((tb, 1, LANES, LANES), lambda i, h: (i, h, 0, 0))
    return pl.pallas_call(
        functools.partial(_gdn_step_kernel, heads=heads),
        grid=(bsz // tb, heads),
        in_specs=[head_blk(0), head_blk(heads), head_blk(2 * heads), head_blk(3 * heads),
                  pl.BlockSpec((tb, LANES), lambda i, h: (i, 0)),
                  cs_blk(0), cs_blk(heads), cs_blk(2 * heads),
                  cw_blk(0), cw_blk(heads), cw_blk(2 * heads), row128, row128, row128, st_in],
        out_specs=[pl.BlockSpec((tb, LANES), lambda i, h: (i, h)), st_out],
        out_shape=[jax.ShapeDtypeStruct((bsz, heads * LANES), BF16),
                   jax.ShapeDtypeStruct(state.shape[1:], F32)],
        compiler_params=_params(("arbitrary", "arbitrary")),
    )(proj, proj, proj, proj, ba, conv_state_t, conv_state_t, conv_state_t, conv_w, conv_w, conv_w,
      alog_pad, dtb_pad, norm_w.reshape(1, LANES), state)


def _lru_gates(x, wa_ref, wi_ref, ba, bi, lam):
    nb = wa_ref.shape[0]
    ga, gi = [], []
    for s in range(nb):
        xs = x[:, s * LANES:(s + 1) * LANES].astype(BF16)
        ga.append(jnp.dot(xs, wa_ref[s], preferred_element_type=F32))
        gi.append(jnp.dot(xs, wi_ref[s], preferred_element_type=F32))
    gate_a = _sigmoid(jnp.concatenate(ga, axis=-1) + ba)
    gate_i = _sigmoid(jnp.concatenate(gi, axis=-1) + bi)
    log_a = -LRU_C * gate_a * _softplus(-lam)
    return log_a, gate_i


def _lru_prompt_kernel(xl_ref, yl_ref, cw_ref, cb_ref, wa_ref, wi_ref, ba_ref, bi_ref, lam_ref,
                       o_ref, hl_ref, xbuf, abuf, bbuf, h_scr):
    blk = pl.program_id(1)
    cb = xl_ref.shape[0]
    width = xl_ref.shape[1]

    @pl.when(blk == 0)
    def _():
        h_scr[...] = jnp.zeros_like(h_scr)
        xbuf[0:8, :] = jnp.zeros((8, width), F32)

    x = _conv_block(xl_ref, xbuf, cw_ref[...]) + cb_ref[...]
    log_a, gate_i = _lru_gates(x, wa_ref, wi_ref, ba_ref[...], bi_ref[...], lam_ref[...])
    mult = jnp.sqrt(_neg_expm1(2.0 * log_a))
    first = (lax.broadcasted_iota(jnp.int32, (cb, 1), 0) == 0) & (blk == 0)
    mult = jnp.where(first, 1.0, mult)
    abuf[...] = jnp.exp(log_a)
    bbuf[...] = mult * gate_i * x

    def step(t, h):
        h = abuf[pl.ds(t, 1), :] * h + bbuf[pl.ds(t, 1), :]
        bbuf[pl.ds(t, 1), :] = h
        return h

    h_last = lax.fori_loop(0, cb, step, h_scr[...], unroll=8)
    h_scr[...] = h_last
    o_ref[...] = (bbuf[...] * _gelu_tanh(yl_ref[...])).astype(BF16)

    @pl.when(blk == pl.num_programs(1) - 1)
    def _():
        hl_ref[0] = h_last


def _lru_prompt(xy, conv_w, conv_b, w_a, w_i, b_a, b_i, lam, batch, seq, cb):
    m = xy.shape[0]
    width = xy.shape[1] // 2
    nblk = seq // cb
    nb = w_a.shape[0]
    vec = pl.BlockSpec((1, width), lambda b, c: (0, 0))
    wblk = pl.BlockSpec((nb, LANES, LANES), lambda b, c: (0, 0, 0))
    out, h_last = pl.pallas_call(
        _lru_prompt_kernel,
        grid=(batch, nblk),
        in_specs=[pl.BlockSpec((cb, width), lambda b, c: (b * nblk + c, 0)),
                  pl.BlockSpec((cb, width), lambda b, c: (b * nblk + c, 1)),
                  pl.BlockSpec((CONV_WIDTH, width), lambda b, c: (0, 0)),
                  vec, wblk, wblk, vec, vec, vec],
        out_specs=[pl.BlockSpec((cb, width), lambda b, c: (b * nblk + c, 0)),
                   pl.BlockSpec((1, 1, width), lambda b, c: (b, 0, 0))],
        out_shape=[jax.ShapeDtypeStruct((m, width), BF16),
                   jax.ShapeDtypeStruct((batch, 1, width), F32)],
        scratch_shapes=[pltpu.VMEM((cb + 8, width), F32), pltpu.VMEM((cb, width), F32),
                        pltpu.VMEM((cb, width), F32), pltpu.VMEM((1, width), F32)],
        compiler_params=_params(("arbitrary", "arbitrary")),
    )(xy, xy, conv_w, conv_b.reshape(1, width), w_a, w_i, b_a.reshape(1, width), b_i.reshape(1, width),
      lam.reshape(1, width))
    return out, h_last.reshape(batch, width)


def _lru_step_kernel(xl_ref, yl_ref, cs_ref, h0_ref, cw_ref, cb_ref, wa_ref, wi_ref, ba_ref, bi_ref, lam_ref,
                     o_ref, h_ref, *, reset):
    w = cw_ref[...]
    x = cs_ref[0] * w[0:1]
    x = x + cs_ref[1] * w[1:2]
    x = x + cs_ref[2] * w[2:3]
    x = x + xl_ref[...] * w[3:4] + cb_ref[...]
    log_a, gate_i = _lru_gates(x, wa_ref, wi_ref, ba_ref[...], bi_ref[...], lam_ref[...])
    mult = 1.0 if reset else jnp.sqrt(_neg_expm1(2.0 * log_a))
    h = jnp.exp(log_a) * h0_ref[...] + mult * gate_i * x
    h_ref[...] = h
    o_ref[...] = (h * _gelu_tanh(yl_ref[...])).astype(BF16)


def _lru_step(xy, conv_state_t, h0, conv_w, conv_b, w_a, w_i, b_a, b_i, lam, reset):
    bsz = xy.shape[0]
    width = xy.shape[1] // 2
    nb = w_a.shape[0]
    vec = pl.BlockSpec((1, width), lambda i: (0, 0))
    wblk = pl.BlockSpec((nb, LANES, LANES), lambda i: (0, 0, 0))
    full = pl.BlockSpec((bsz, width), lambda i: (0, 0))
    return pl.pallas_call(
        functools.partial(_lru_step_kernel, reset=reset),
        grid=(1,),
        in_specs=[full, pl.BlockSpec((bsz, width), lambda i: (0, 1)),
                  pl.BlockSpec((CONV_WIDTH - 1, bsz, width), lambda i: (0, 0, 0)), full,
                  pl.BlockSpec((CONV_WIDTH, width), lambda i: (0, 0)), vec, wblk, wblk, vec, vec, vec],
        out_specs=[full, full],
        out_shape=[jax.ShapeDtypeStruct((bsz, width), BF16), jax.ShapeDtypeStruct((bsz, width), F32)],
        compiler_params=_params(("arbitrary",)),
    )(xy, xy, conv_state_t, h0, conv_w, conv_b.reshape(1, width), w_a, w_i, b_a.reshape(1, width),
      b_i.reshape(1, width), lam.reshape(1, width))


def _hgrn_lower_bound(lb_ref, layer, j=0):
    depth = lb_ref.shape[0]
    raw = [lb_ref[l, j] for l in range(depth)]
    mx = raw[0]
    for r in raw[1:]:
        mx = jnp.maximum(mx, r)
    ex = [jnp.exp(r - mx) for r in raw]
    tot = ex[0]
    for e in ex[1:]:
        tot = tot + e
    wts = [e / tot for e in ex]
    cum = wts[0]
    for w in wts[1:layer + 1]:
        cum = cum + w
    return cum - wts[0]


def _hgrn_inputs(q_raw, f_raw, lb):
    q = _silu(q_raw) * (LANES ** -0.5)
    f = lb + (1.0 - lb) * _sigmoid(f_raw)
    return q, f


def _level_ref(g, half):
    c = g.shape[0]
    sub = 8
    if 2 * half >= sub:
        g3 = g.reshape(c // (2 * half), 2 * half, LANES)
        return jnp.broadcast_to(g3[:, half - 1:half, :], g3.shape).reshape(c, LANES)
    g3 = g.reshape(c // sub, sub, LANES)
    rin = lax.broadcasted_iota(jnp.int32, g3.shape, 1)
    out = jnp.broadcast_to(g3[:, sub - half - 1:sub - half, :], g3.shape)
    for start in range(sub - 4 * half, -1, -2 * half):
        out = jnp.where(rin < start + 2 * half, jnp.broadcast_to(g3[:, start + half - 1:start + half, :], g3.shape),
                        out)
    return out.reshape(c, LANES)


def _hgrn_level_table(c):
    i = np.arange(c)[:, None]
    j = np.arange(c)[None, :]
    nlev = int(math.log2(c))
    top_bit = sum(((i ^ j) >> b > 0).astype(np.int32) for b in range(1, nlev))
    return jnp.asarray(np.where(j < i, nlev - 1 - top_bit, np.where(j == i, nlev, -1)), jnp.int32)


def _hgrn_prompt_kernel(q_ref, f_ref, i_ref, gz_ref, lb_ref, nw_ref, lvl_ref, o_ref, s_ref, s_scr, *, layer):
    hp = s_scr.shape[0]
    blk = pl.program_id(2)
    cb = q_ref.shape[0]
    c = CHUNK

    @pl.when(blk == 0)
    def _():
        s_scr[...] = jnp.zeros_like(s_scr)

    lbs = [_hgrn_lower_bound(lb_ref, layer, j) for j in range(hp)]
    head_lanes = [slice(j * LANES, (j + 1) * LANES) for j in range(hp)]
    tri = (_iota2(c, 1) <= _iota2(c, 0)).astype(BF16)
    nw = nw_ref[...]
    lvl = lvl_ref[...]
    halves = [c >> (i + 1) for i in range(int(math.log2(c)))]

    items = [(slice(ci * c, (ci + 1) * c), j) for ci in range(cb // c) for j in range(hp)]
    qf = [_hgrn_inputs(q_ref[sl, head_lanes[j]], f_ref[sl, head_lanes[j]], lbs[j]) for sl, j in items]
    qs = [q for q, _ in qf]
    ks = [1.0 - f for _, f in qf]
    vs = [i_ref[sl, head_lanes[j]] for sl, j in items]
    gs = [_dot_exact_lhs(tri, jnp.log(f)) for _, f in qf]

    amats = [_dot_nt(q, k) for q, k in zip(qs, ks)]
    for li, half in enumerate(halves):
        es = [jnp.exp(-jnp.abs(g - _level_ref(g, half))) for g in gs]
        ps = [_dot_nt(q * e, k * e) for q, k, e in zip(qs, ks, es)]
        amats = [jnp.where(lvl == li, p, a) for p, a in zip(ps, amats)]
    intra = [_dot(jnp.where(lvl >= 0, a, 0.0), v) for a, v in zip(amats, vs)]

    g_lasts = [g[c - 1:c, :] for g in gs]
    qdec = [q * jnp.exp(g) for q, g in zip(qs, gs)]
    sdec = [_col_from_row(jnp.exp(gl)) for gl in g_lasts]
    sadd = [_dot_tn(k * jnp.exp(gl - g), v) for k, gl, g, v in zip(ks, g_lasts, gs, vs)]

    states = [s_scr[j] for j in range(hp)]
    outs = []
    for (sl, j), o, qd, dec, add in zip(items, intra, qdec, sdec, sadd):
        outs.append(o + _dot(qd, states[j]))
        states[j] = states[j] * dec + add
    for j in range(hp):
        s_scr[j] = states[j]
    for (sl, j), o in zip(items, outs):
        o_ref[sl, head_lanes[j]] = (_rms(o, nw) * _silu(gz_ref[sl, head_lanes[j]])).astype(BF16)

    @pl.when(blk == pl.num_programs(2) - 1)
    def _():
        s_ref[0] = s_scr[...]


def _hgrn_prompt(proj, lb_raw, norm_w, layer, batch, seq, heads, cb):
    m = proj.shape[0]
    nblk = seq // cb
    depth = lb_raw.shape[0]
    hp = HGRN_HEADS_PER_STEP
    assert heads % hp == 0
    groups = heads // hp
    head_blk = lambda part: pl.BlockSpec((cb, hp * LANES), lambda b, h, c: (b * nblk + c, part * groups + h))
    return pl.pallas_call(
        functools.partial(_hgrn_prompt_kernel, layer=layer),
        grid=(batch, groups, nblk),
        in_specs=[head_blk(0), head_blk(1), head_blk(2), head_blk(3),
                  pl.BlockSpec((depth, hp, 1, LANES), lambda b, h, c: (0, h, 0, 0)),
                  pl.BlockSpec((1, LANES), lambda b, h, c: (0, 0)),
                  pl.BlockSpec((CHUNK, CHUNK), lambda b, h, c: (0, 0))],
        out_specs=[pl.BlockSpec((cb, hp * LANES), lambda b, h, c: (b * nblk + c, h)),
                   pl.BlockSpec((1, hp, LANES, LANES), lambda b, h, c: (b, h, 0, 0))],
        out_shape=[jax.ShapeDtypeStruct((m, heads * LANES), BF16),
                   jax.ShapeDtypeStruct((batch, heads, LANES, LANES), F32)],
        scratch_shapes=[pltpu.VMEM((hp, LANES, LANES), F32)],
        compiler_params=_params(("arbitrary", "arbitrary", "arbitrary")),
    )(proj, proj, proj, proj, lb_raw.reshape(depth, heads, 1, LANES), norm_w.reshape(1, LANES),
      _hgrn_level_table(CHUNK))


def _hgrn_step_kernel(q_ref, f_ref, i_ref, gz_ref, lb_ref, nw_ref, s_ref, o_ref, so_ref, *, layer):
    tb = q_ref.shape[0]
    lb = _hgrn_lower_bound(lb_ref, layer)
    q, f = _hgrn_inputs(q_ref[...], f_ref[...], lb)
    v = i_ref[...]
    outs = []
    group = 4
    for b0 in range(0, tb, group):
        rows = range(b0, min(b0 + group, tb))
        fcols = [_col_from_row(f[b:b + 1]) for b in rows]
        for b, fcol in zip(rows, fcols):
            s = fcol * (s_ref[b, 0] - v[b:b + 1]) + v[b:b + 1]
            so_ref[b, 0] = s
            outs.append(_dot(q[b:b + 1], s))
    o = jnp.concatenate(outs, axis=0)
    o_ref[...] = (_rms(o, nw_ref[...]) * _silu(gz_ref[...])).astype(BF16)


def _hgrn_step(proj, lb_raw, norm_w, state, state_idx, layer, heads, tb):
    bsz = proj.shape[0]
    depth = lb_raw.shape[0]
    head_blk = lambda off: pl.BlockSpec((tb, LANES), lambda i, h: (i, off + h))
    st_in = pl.BlockSpec((None, tb, 1, LANES, LANES), lambda i, h: (state_idx, i, h, 0, 0))
    st_out = pl.BlockSpec((tb, 1, LANES, LANES), lambda i, h: (i, h, 0, 0))
    return pl.pallas_call(
        functools.partial(_hgrn_step_kernel, layer=layer),
        grid=(bsz // tb, heads),
        in_specs=[head_blk(0), head_blk(heads), head_blk(2 * heads), head_blk(3 * heads),
                  pl.BlockSpec((depth, 1, 1, LANES), lambda i, h: (0, h, 0, 0)),
                  pl.BlockSpec((1, LANES), lambda i, h: (0, 0)), st_in],
        out_specs=[pl.BlockSpec((tb, LANES), lambda i, h: (i, h)), st_out],
        out_shape=[jax.ShapeDtypeStruct((bsz, heads * LANES), BF16),
                   jax.ShapeDtypeStruct(state.shape[1:], F32)],
        compiler_params=_params(("arbitrary", "arbitrary")),
    )(proj, proj, proj, proj, lb_raw.reshape(depth, heads, 1, LANES), norm_w.reshape(1, LANES), state)


def _trunk(x, mem_k, mem_v, states, wts, big, batch, seq, prompt):
    depth = wts["norm_mix"].shape[0]
    gdn_w = wts["gdn_conv_w"].shape[-1] // 3
    gdn_heads = gdn_w // LANES
    hgrn_heads = wts["hgrn_lb_raw"].shape[1] // LANES
    lru_w = wts["lru_conv_w"].shape[-1]
    bf = {name: list(handles) for name, handles in big.items()}
    new = {"gdn_conv": [], "gdn": [], "lru_conv": [], "lru": [], "hgrn": []}

    def keep(name, idx, w_used):
        w, li = bf[name][idx]
        bf[name][idx] = (w_used, li if w_used is w else 0)

    for l in range(depth):
        g_mix = wts["norm_mix"][l]
        if l % 2 == 0:
            e = l // 2
            if prompt:
                hn = _rms_cast(x, g_mix, TM)
                proj = _matmul(hn, wts["ab_head"], e, 0, 4 * gdn_w, TM_BF16, TN)
                ba = _matmul(hn, wts["ab_head"], e, 4 * gdn_w, LANES, TM_BF16, LANES)
                xy = _matmul(hn, wts["ab_lru"], e, 0, 2 * lru_w, TM_BF16, TN)
            else:
                proj, _ = _norm_matmul(x, g_mix, wts["ab_head"], e, 0, 4 * gdn_w, TM, TN)
                ba, _ = _norm_matmul(x, g_mix, wts["ab_head"], e, 4 * gdn_w, LANES, TM, LANES)
                xy, _ = _norm_matmul(x, g_mix, wts["ab_lru"], e, 0, 2 * lru_w, TM, TN)
            gargs = (wts["gdn_conv_w"][e], wts["alog_pad"][e], wts["dtb_pad"][e], wts["gdn_norm_w"][e])
            largs = (wts["lru_conv_w"][e], wts["lru_conv_b"][e], wts["lru_w_a"][e], wts["lru_w_i"][e],
                     wts["lru_b_a"][e], wts["lru_b_i"][e], wts["lru_lam"][e])
            w_out, li = bf["ab_w_out"][e]
            if prompt:
                o_a, s_new = _gdn_prompt(proj, ba, *gargs, batch, seq, gdn_heads, 512)
                o_b, h_new = _lru_prompt(xy, *largs, batch, seq, 256)
                tail = lambda t, w: t.reshape(batch, seq, -1)[:, seq - (CONV_WIDTH - 1):, :w]
                new["gdn_conv"].append(tail(proj, 3 * gdn_w))
                new["lru_conv"].append(tail(xy, lru_w))
                x = _matmul2_res(o_a, o_b, w_out, li, x, TM, TN)
            else:
                gc_state, lc_state = states["gdn_conv"][e], states["lru_conv"][e]
                o_a, s_new = _gdn_step(proj, ba, jnp.swapaxes(gc_state, 0, 1), *gargs, states["gdn"], e,
                                       gdn_heads, 128)
                o_b, h_new = _lru_step(xy, jnp.swapaxes(lc_state, 0, 1), states["lru"][e], *largs, reset=False)
                new["gdn_conv"].append(jnp.concatenate([gc_state[:, 1:], proj[:, None, :3 * gdn_w]], axis=1))
                new["lru_conv"].append(jnp.concatenate([lc_state[:, 1:], xy[:, None, :lru_w]], axis=1))
                x, w_used = _matmul_res(jnp.concatenate([o_a, o_b], axis=1), w_out, li, x, TM, TN)
                keep("ab_w_out", e, w_used)
            new["gdn"].append(s_new)
            new["lru"].append(h_new)
        else:
            o_idx = l // 2
            w_in, li = bf["c_w_in"][o_idx]
            if prompt:
                proj = _matmul(_rms_cast(x, g_mix, TM), w_in, li, 0, w_in.shape[2], TM_BF16, TN)
            else:
                proj, w_used = _norm_matmul(x, g_mix, w_in, li, 0, w_in.shape[2], TM, TN)
                keep("c_w_in", o_idx, w_used)
            hargs = (wts["hgrn_lb_raw"], wts["hgrn_norm_w"][o_idx])
            if prompt:
                o_c, s_new = _hgrn_prompt(proj, *hargs, l, batch, seq, hgrn_heads, 512)
            else:
                o_c, s_new = _hgrn_step(proj, *hargs, states["hgrn"], o_idx, l, hgrn_heads, 128)
            new["hgrn"].append(s_new)
            w_out, li = bf["c_w_out"][o_idx]
            x, w_used = _matmul_res(o_c, w_out, li, x, TM, TN)
            keep("c_w_out", o_idx, w_used)
        (w_q, lq), (w_o, lo) = bf["mem_w_q"][l], bf["mem_w_o"][l]
        if prompt:
            assert lq == lo
            x = _mem_attn_prompt(x, wts["norm_mem"][l], w_q, mem_k, mem_v, w_o, l, lq, seq, 512)
        else:
            q, w_used = _norm_matmul(x, wts["norm_mem"][l], w_q, lq, 0, w_q.shape[2], TM, TN)
            keep("mem_w_q", l, w_used)
            o = _mem_attn_sample(q, mem_k, mem_v, l, 8)
            x, w_used = _matmul_res(o.reshape(batch, -1).astype(BF16), w_o, lo, x, TM, TN)
            keep("mem_w_o", l, w_used)
        (w_up, lu), (w_dn, ld) = bf["ffn_w_up"][l], bf["ffn_w_down"][l]
        assert lu == ld
        x, up_used, dn_used = _ffn(x, wts["norm_ffn"][l], w_up, w_dn, lu, wts["norm_final"], l == depth - 1,
                                   TM, TF)
        keep("ffn_w_up", l, up_used)
        keep("ffn_w_down", l, dn_used)
    return x, {n: jnp.stack(v) for n, v in new.items()}, bf


def kernel(x_prompt, x_sample, cache_mem_k, cache_mem_v, state_gdn_conv, state_gdn, state_lru_conv, state_lru, state_hgrn, mem_prompt, norm_mix, norm_mem, norm_mem_kv, norm_ffn, norm_final, ab_w_in, ab_w_out, gdn_conv_w, gdn_a_log, gdn_dt_bias, gdn_norm_w, lru_conv_w, lru_conv_b, lru_w_a, lru_b_a, lru_w_i, lru_b_i, lru_lam, c_w_in, c_w_out, hgrn_lb_raw, hgrn_norm_w, mem_w_q, mem_w_k, mem_w_v, mem_w_o, ffn_w_up, ffn_w_down):
    bp, lp, d = x_prompt.shape
    bs, ls, _ = x_sample.shape
    assert ls == 1, "the sample group advances one token per call"
    depth = norm_mix.shape[0]
    gdn_heads = gdn_a_log.shape[1]
    gdn_w = gdn_heads * LANES
    mem_tokens, mem_heads, mem_hd = cache_mem_k.shape[2:]
    mem_w = mem_heads * mem_hd
    n_ba = 2 * gdn_heads
    assert n_ba <= LANES

    pad_ba = lambda a: jnp.pad(a, ((0, 0), (gdn_heads, LANES - n_ba)))[:, None, :]
    ab_bf = ab_w_in.astype(BF16)
    wts = dict(
        norm_mix=norm_mix, norm_mem=norm_mem, norm_ffn=norm_ffn, norm_final=norm_final,
        ab_head=ab_bf, ab_lru=ab_bf[:, :, 4 * gdn_w + n_ba:],
        gdn_conv_w=gdn_conv_w, alog_pad=pad_ba(gdn_a_log), dtb_pad=pad_ba(gdn_dt_bias), gdn_norm_w=gdn_norm_w,
        lru_conv_w=lru_conv_w, lru_conv_b=lru_conv_b, lru_w_a=lru_w_a.astype(BF16), lru_w_i=lru_w_i.astype(BF16),
        lru_b_a=lru_b_a, lru_b_i=lru_b_i, lru_lam=lru_lam,
        hgrn_lb_raw=hgrn_lb_raw, hgrn_norm_w=hgrn_norm_w,
    )
    big = dict(ab_w_out=ab_w_out, c_w_in=c_w_in, c_w_out=c_w_out, mem_w_q=mem_w_q, mem_w_o=mem_w_o,
               ffn_w_up=ffn_w_up, ffn_w_down=ffn_w_down)
    big = {name: [(w, l) for l in range(w.shape[0])] for name, w in big.items()}

    states = dict(gdn_conv=state_gdn_conv, gdn=state_gdn, lru_conv=state_lru_conv, lru=state_lru,
                  hgrn=state_hgrn)
    y_s, new_s, big_bf = _trunk(x_sample.reshape(bs * ls, d), cache_mem_k, cache_mem_v, states, wts, big,
                                bs, ls, False)

    mem_rows = mem_prompt.reshape(bp * mem_tokens, d)
    w_kv = jnp.concatenate([mem_w_k, mem_w_v], axis=-1).astype(BF16)
    kv = jnp.stack([_norm_matmul(mem_rows, norm_mem_kv[l], w_kv, l, 0, 2 * mem_w, TM, TN)[0] for l in range(depth)])
    p_mem_k = kv[:, :, :mem_w].reshape(depth, bp, mem_tokens, mem_w)
    p_mem_v = kv[:, :, mem_w:].reshape(depth, bp, mem_tokens, mem_w)

    y_p, new_p, _ = _trunk(x_prompt.reshape(bp * lp, d), p_mem_k, p_mem_v, None, wts, big_bf, bp, lp, True)

    order = ("gdn_conv", "gdn", "lru_conv", "lru", "hgrn")
    mem5 = lambda t: t.reshape(depth, bp, mem_tokens, mem_heads, mem_hd)
    return (y_p.reshape(bp, lp, d), y_s.reshape(bs, ls, d), mem5(p_mem_k), mem5(p_mem_v),
            *(new_p[n] for n in order), *(new_s[n] for n in order))
```
